```python
import math
import jax, jax.numpy as jnp
from jax import lax
import numpy as np


D_MODEL = 2048
BATCH = 4
SEQ = 4096
DEPTH = 4

GRID_W = 64
CTX_LEN = 256
N_MIXERS = 4
CTX_READERS = (0, 1, 2)
RMS_EPS = 1e-6
ROPE_THETA = 10000.0
D_FF = 4 * D_MODEL
N_MOD = 6

SWA_HEADS = 16
SWA_KV_HEADS = 4
SWA_GROUP = SWA_HEADS // SWA_KV_HEADS
SWA_HEAD_DIM = D_MODEL // SWA_HEADS
SWA_WINDOW = 128
SWA_BLOCK = 128

ML_HEADS = 8
ML_DV = D_MODEL // ML_HEADS
ML_DQK = ML_DV // 2
ML_CHUNK = 64
ML_F_BIAS_LO = 3.0
ML_F_BIAS_HI = 6.0

MLA_HEADS = 16
MLA_Q_RANK = 512
MLA_KV_RANK = 512
MLA_NOPE = 128
MLA_ROPE = 64
MLA_V = 128
MLA_BLOCK = 128

HY_ORDER = 2
HY_BANDS = 16
HY_EMB = 1 + 2 * HY_BANDS
HY_FILTER_W = 64
HY_SHORT = 3
HY_MIN_DECAY = -math.log(1e-2) / 1.5
HY_MAX_DECAY = -math.log(1e-2) / 0.3

kernel_name = 'hybrid_interleaved_dit_trunk'


def rmsnorm(x, g):
    xf = x.astype(jnp.float32)
    y = xf * lax.rsqrt(jnp.mean(xf * xf, axis=-1, keepdims=True) + RMS_EPS)
    return (y * g.astype(jnp.float32)).astype(x.dtype)


def modulate(x, shift, scale):
    return x * (1 + scale) + shift


def sqrelu_mlp(u, w1, w2):
    return jnp.square(jax.nn.relu(u @ w1)) @ w2


def axial_rope_tables(row, col, d_rot):
    n = d_rot // 4
    inv = ROPE_THETA ** (-jnp.arange(n, dtype=jnp.float32) / n)
    ang = jnp.concatenate([row.astype(jnp.float32)[:, None] * inv, col.astype(jnp.float32)[:, None] * inv], axis=-1)
    return jnp.cos(ang), jnp.sin(ang)


def apply_rope(x, cos, sin):
    half = x.shape[-1] // 2
    x1, x2 = x[..., :half], x[..., half:]
    c = cos[:, None, :].astype(x.dtype)
    s = sin[:, None, :].astype(x.dtype)
    return jnp.concatenate([x1 * c - x2 * s, x2 * c + x1 * s], axis=-1)


def dense_attention(q, k, v, scale):
    s = jnp.einsum('bqhd,bkhd->bhqk', q, k).astype(jnp.float32) * scale
    p = jax.nn.softmax(s, axis=-1).astype(v.dtype)
    return jnp.einsum('bhqk,bkhd->bqhd', p, v)


def window_gqa_mixer(u, uc, w_qkv, sink, w_o, rope_cs, need_ctx):
    B, S, _ = u.shape
    H, KV, G, HD, LB = SWA_HEADS, SWA_KV_HEADS, SWA_GROUP, SWA_HEAD_DIM, SWA_BLOCK
    scale = HD ** -0.5

    def project(a):
        T = a.shape[1]
        qkv = a @ w_qkv
        q = qkv[..., :H * HD].reshape(B, T, H, HD)
        k = qkv[..., H * HD:(H + KV) * HD].reshape(B, T, KV, HD)
        v = qkv[..., (H + KV) * HD:].reshape(B, T, KV, HD)
        return q, k, v

    q, k, v = project(u)
    q, k = apply_rope(q, *rope_cs), apply_rope(k, *rope_cs)
    qc, kc, vc = project(uc)
    t_ctx = uc.shape[1]
    sink_kg = sink.astype(jnp.float32).reshape(KV, G)

    nb = S // LB
    qb = q.reshape(B, nb, LB, KV, G, HD)

    def band(a):
        ab = jnp.pad(a.reshape(B, nb, LB, KV, HD), ((0, 0), (1, 1), (0, 0), (0, 0), (0, 0)))
        return jnp.concatenate([ab[:, :-2], ab[:, 1:-1], ab[:, 2:]], axis=2)

    kb, vb = band(k), band(v)
    s_loc = jnp.einsum('bnqkgd,bnjkd->bnkgqj', qb, kb).astype(jnp.float32) * scale
    blk = jnp.arange(nb)[:, None, None]
    qpos = blk * LB + jnp.arange(LB)[None, :, None]
    kpos = (blk - 1) * LB + jnp.arange(3 * LB)[None, None, :]
    valid = (jnp.abs(kpos - qpos) <= SWA_WINDOW) & (kpos >= 0) & (kpos < S)
    s_loc = jnp.where(valid[None, :, None, None], s_loc, -jnp.inf)
    s_ctx = jnp.einsum('bnqkgd,bjkd->bnkgqj', qb, kc).astype(jnp.float32) * scale
    s_snk = jnp.broadcast_to(sink_kg[None, None, :, :, None, None], s_loc.shape[:-1] + (1,))
    p = jax.nn.softmax(jnp.concatenate([s_loc, s_ctx, s_snk], axis=-1), axis=-1)
    p_loc = p[..., :3 * LB].astype(v.dtype)
    p_ctx = p[..., 3 * LB:3 * LB + t_ctx].astype(v.dtype)
    o = jnp.einsum('bnkgqj,bnjkd->bnqkgd', p_loc, vb) + jnp.einsum('bnkgqj,bjkd->bnqkgd', p_ctx, vc)
    y = o.reshape(B, S, H * HD) @ w_o
    yc = None
    if need_ctx:
        qcg = qc.reshape(B, t_ctx, KV, G, HD)
        sc = jnp.einsum('bqkgd,bjkd->bkgqj', qcg, kc).astype(jnp.float32) * scale
        sc_snk = jnp.broadcast_to(sink_kg[None, :, :, None, None], sc.shape[:-1] + (1,))
        pc = jax.nn.softmax(jnp.concatenate([sc, sc_snk], axis=-1), axis=-1)[..., :t_ctx]
        oc = jnp.einsum('bkgqj,bjkd->bqkgd', pc.astype(vc.dtype), vc)
        yc = oc.reshape(B, t_ctx, H * HD) @ w_o
    return y, yc


def mlstm_chunked(q, k, v, i_pre, log_f, state):
    B, H, T, _ = q.shape
    n_ch = T // ML_CHUNK
    tril = jnp.tril(jnp.ones((ML_CHUNK, ML_CHUNK), dtype=bool))

    def to_chunks(a):
        return jnp.moveaxis(a.reshape((B, H, n_ch, ML_CHUNK) + a.shape[3:]), 2, 0)

    def step(carry, chunk):
        C, n, m = carry
        qs, ks, vs, ic, fc = chunk
        b = jnp.cumsum(fc, axis=-1)
        dmat = jnp.where(tril, b[..., :, None] - b[..., None, :] + ic[..., None, :], -jnp.inf)
        inter = b + m[..., None]
        m_t = jnp.maximum(inter, jnp.max(dmat, axis=-1))
        w = jnp.exp(dmat - m_t[..., None])
        g = jnp.exp(inter - m_t)
        s = jnp.einsum('bhtd,bhsd->bhts', qs, ks) * w
        num = jnp.einsum('bhts,bhsv->bhtv', s, vs) + g[..., None] * jnp.einsum('bhvd,bhtd->bhtv', C, qs)
        den = jnp.sum(s, axis=-1) + g * jnp.einsum('bhd,bhtd->bht', n, qs)
        h = num / jnp.maximum(jnp.abs(den), jnp.exp(-m_t))[..., None]
        b_end = b[..., -1]
        dec = b_end[..., None] - b + ic
        m_new = jnp.maximum(b_end + m, jnp.max(dec, axis=-1))
        ws = jnp.exp(dec - m_new[..., None])
        gs = jnp.exp(b_end + m - m_new)
        C = gs[..., None, None] * C + jnp.einsum('bhs,bhsv,bhsd->bhvd', ws, vs, ks)
        n = gs[..., None] * n + jnp.einsum('bhs,bhsd->bhd', ws, ks)
        return (C, n, m_new), h

    state, h = lax.scan(step, state, tuple(to_chunks(a) for a in (q, k, v, i_pre, log_f)))
    h = jnp.moveaxis(h, 0, 2).reshape(B, H, T, v.shape[-1])
    return h, state


def mlstm_mixer(u, uc, w_in, gate_b, head_g, w_o, need_ctx):
    H, DQK, DV = ML_HEADS, ML_DQK, ML_DV
    f32 = jnp.float32
    o1, o2 = H * DQK, 2 * H * DQK
    o3, o4 = o2 + H * DV, o2 + 2 * H * DV

    def project(a):
        B, T, _ = a.shape
        z = a @ w_in
        heads = lambda t, d: jnp.moveaxis(t.reshape(B, T, H, d), 1, 2).astype(f32)
        q = heads(z[..., :o1], DQK) * DQK ** -0.5
        k = heads(z[..., o1:o2], DQK)
        v = heads(z[..., o2:o3], DV)
        og = jax.nn.sigmoid(z[..., o3:o4])
        g = z[..., o4:].astype(f32).reshape(B, T, 2, 2, H) + gate_b.astype(f32)
        i_pre = jnp.moveaxis(g[:, :, 0], 1, -1)
        log_f = jax.nn.log_sigmoid(jnp.moveaxis(g[:, :, 1], 1, -1))
        return (q, k, v, i_pre, log_f), og

    lat, og = project(u)
    ctx_seq, og_c = project(uc)
    B = u.shape[0]
    zero = (jnp.zeros((B, H, DV, DQK), f32), jnp.zeros((B, H, DQK), f32), jnp.zeros((B, H), f32))

    def run(seq, d, state, reverse):
        q, k, v, i_pre, log_f = seq
        parts = (q, k, v, i_pre[:, d], log_f[:, d])
        if reverse:
            parts = tuple(jnp.flip(a, axis=2) for a in parts)
        h, st = mlstm_chunked(*parts, state)
        return (jnp.flip(h, axis=2) if reverse else h), st

    hc_f, st_f = run(ctx_seq, 0, zero, False)
    hl_f, _ = run(lat, 0, st_f, False)
    hc_b, st_b = run(ctx_seq, 1, zero, True)
    hl_b, _ = run(lat, 1, st_b, True)

    def finish(hsum, gate):
        Bq, _, T, _ = hsum.shape
        hn = rmsnorm(jnp.moveaxis(hsum, 1, 2), head_g.reshape(H, DV))
        return (hn.reshape(Bq, T, H * DV).astype(gate.dtype) * gate) @ w_o

    y = finish(hl_f + hl_b, og)
    yc = finish(hc_f + hc_b, og_c) if need_ctx else None
    return y, yc


def mla_mixer(u, uc, w_in, q_g, kv_g, w_uq, w_ukv, w_o, rope_cs, need_ctx):
    H, NP, RP, VD = MLA_HEADS, MLA_NOPE, MLA_ROPE, MLA_V
    scale = (NP + RP) ** -0.5
    r1, r2 = MLA_Q_RANK, MLA_Q_RANK + MLA_KV_RANK

    def queries(z, cs):
        B, T, _ = z.shape
        q = (rmsnorm(z[..., :r1], q_g) @ w_uq).reshape(B, T, H, NP + RP)
        if cs is None:
            return q
        return jnp.concatenate([q[..., :NP], apply_rope(q[..., NP:], *cs)], axis=-1)

    def keys_values(z, cs):
        B, T, _ = z.shape
        kv = (rmsnorm(z[..., r1:r2], kv_g) @ w_ukv).reshape(B, T, H, NP + VD)
        k_rope = z[..., r2:][:, :, None, :]
        if cs is not None:
            k_rope = apply_rope(k_rope, *cs)
        k = jnp.concatenate([kv[..., :NP], jnp.broadcast_to(k_rope, (B, T, H, RP))], axis=-1)
        return k, kv[..., NP:]

    z, zc = u @ w_in, uc @ w_in
    q = queries(z, rope_cs)
    k, v = keys_values(z, rope_cs)
    kc, vc = keys_values(zc, None)
    k_all = jnp.concatenate([k, kc], axis=1)
    v_all = jnp.concatenate([v, vc], axis=1)
    B, S, _ = u.shape
    nb = S // MLA_BLOCK
    qb = jnp.moveaxis(q.reshape(B, nb, MLA_BLOCK, H, NP + RP), 1, 0)
    o = lax.map(lambda qblk: dense_attention(qblk, k_all, v_all, scale), qb)
    y = jnp.moveaxis(o, 0, 1).reshape(B, S, H * VD) @ w_o
    yc = None
    if need_ctx:
        oc = dense_attention(queries(zc, None), kc, vc, scale)
        yc = oc.reshape(B, uc.shape[1], H * VD) @ w_o
    return y, yc


def short_conv(a, w, b):
    T = a.shape[1]
    pad = HY_SHORT // 2
    ap = jnp.pad(a, ((0, 0), (pad, pad), (0, 0)))
    return sum(ap[:, tap:tap + T] * w[tap] for tap in range(HY_SHORT)) + b


def hyena_filter_spectra(L, w1, b1, w2, b2, w3, b3, w4, b4, freq, decay):
    f32 = jnp.float32
    D = w4.shape[-1] // (2 * HY_ORDER)
    t = jnp.linspace(0.0, 1.0, L, dtype=f32)[:, None]
    w = (2.0 * math.pi / L) * jnp.arange(L, dtype=f32)[:, None]
    bands = jnp.linspace(1e-4, HY_BANDS - 1, HY_BANDS, dtype=f32)[None, :]
    feats = jnp.concatenate([t, jnp.cos(bands * w), -jnp.sin(bands * w)], axis=-1)
    fr = freq.astype(f32)
    a = jnp.sin(fr * (feats @ w1.astype(f32) + b1.astype(f32)))
    a = jnp.sin(fr * (a @ w2.astype(f32) + b2.astype(f32)))
    a = jnp.sin(fr * (a @ w3.astype(f32) + b3.astype(f32)))
    filt = (a @ w4.astype(f32) + b4.astype(f32)) * jnp.exp(-t * jnp.abs(decay.astype(f32)))
    filt = filt.reshape(L, 2, HY_ORDER, D)
    circ = jnp.concatenate([filt[:, 0], jnp.zeros((1, HY_ORDER, D), f32), filt[:0:-1, 1]], axis=0)
    return jnp.fft.rfft(circ, axis=0)


def fft_long_conv(z, spec, skip):
    L = z.shape[1]
    zf32 = z.astype(jnp.float32)
    zf = jnp.fft.rfft(zf32, n=2 * L, axis=1)
    y = jnp.fft.irfft(zf * spec, n=2 * L, axis=1)[:, :L]
    return (y + zf32 * skip.astype(jnp.float32)).astype(z.dtype)


def hyena_mixer(u, uc, w_in, conv_w, conv_b, f_w1, f_b1, f_w2, f_b2, f_w3, f_b3, f_w4, f_b4, f_freq, decay, skip, w_o, need_ctx):
    def run(a):
        L = a.shape[1]
        spec = hyena_filter_spectra(L, f_w1, f_b1, f_w2, f_b2, f_w3, f_b3, f_w4, f_b4, f_freq, decay)
        x1, x2, v = jnp.split(short_conv(a @ w_in, conv_w, conv_b), 3, axis=-1)
        y = x1 * fft_long_conv(v, spec[:, 0], skip[0])
        y = x2 * fft_long_conv(y, spec[:, 1], skip[1])
        return y @ w_o
    return run(u), (run(uc) if need_ctx else None)


def setup_inputs(seed: int = 0) -> dict:
    key = jax.random.key(seed)
    keys = iter(jax.random.split(key, 48))
    f32 = jnp.float32
    D = D_MODEL

    def normal(shape, s=1.0):
        return s * jax.random.normal(next(keys), shape, f32)

    def dense(shape, gain=1.0):
        return normal(shape, gain * shape[-2] ** -0.5)

    nA, nB, nC, nD = (len(range(kind, DEPTH, N_MIXERS)) for kind in range(N_MIXERS))
    ml_i_b = normal((nB, 1, 2, ML_HEADS), 0.1)
    ml_f_b = jnp.linspace(ML_F_BIAS_LO, ML_F_BIAS_HI, ML_HEADS, dtype=f32) + normal((nB, 1, 2, ML_HEADS), 0.1)
    hy_decay = jnp.linspace(HY_MIN_DECAY, HY_MAX_DECAY, 2 * HY_ORDER * D, dtype=f32) + normal((nD, 2 * HY_ORDER * D), 0.1)
    return {
        'x': normal((BATCH, SEQ, D)),
        'c': normal((BATCH, D)),
        'ctx': normal((BATCH, CTX_LEN, D)),
        'c_ctx': normal((D,)),
        'ada_w': dense((DEPTH, D, N_MOD * D), 0.5),
        'ada_b': normal((DEPTH, N_MOD * D), 0.02),
        'norm_g': 1.0 + normal((DEPTH, 4, D), 0.05),
        'mlp_w1': dense((DEPTH, D, D_FF)),
        'mlp_w2': dense((DEPTH, D_FF, D)),
        'swa_w_qkv': dense((nA, D, (SWA_HEADS + 2 * SWA_KV_HEADS) * SWA_HEAD_DIM)),
        'swa_sink': normal((nA, SWA_HEADS), 0.5),
        'swa_w_o': dense((nA, SWA_HEADS * SWA_HEAD_DIM, D)),
        'ml_w_in': dense((nB, D, 2 * ML_HEADS * ML_DQK + 2 * ML_HEADS * ML_DV + 4 * ML_HEADS)),
        'ml_gate_b': jnp.concatenate([ml_i_b, ml_f_b], axis=1),
        'ml_head_g': 1.0 + normal((nB, ML_HEADS * ML_DV), 0.05),
        'ml_w_o': dense((nB, ML_HEADS * ML_DV, D)),
        'mla_w_in': dense((nC, D, MLA_Q_RANK + MLA_KV_RANK + MLA_ROPE)),
        'mla_q_g': 1.0 + normal((nC, MLA_Q_RANK), 0.05),
        'mla_kv_g': 1.0 + normal((nC, MLA_KV_RANK), 0.05),
        'mla_w_uq': dense((nC, MLA_Q_RANK, MLA_HEADS * (MLA_NOPE + MLA_ROPE))),
        'mla_w_ukv': dense((nC, MLA_KV_RANK, MLA_HEADS * (MLA_NOPE + MLA_V))),
        'mla_w_o': dense((nC, MLA_HEADS * MLA_V, D)),
        'hy_w_in': dense((nD, D, 3 * D)),
        'hy_conv_w': normal((nD, HY_SHORT, 3 * D), HY_SHORT ** -0.5),
        'hy_conv_b': normal((nD, 3 * D), 0.02),
        'hy_f_w1': dense((nD, HY_EMB, HY_FILTER_W)),
        'hy_f_b1': normal((nD, HY_FILTER_W), 0.1),
        'hy_f_w2': dense((nD, HY_FILTER_W, HY_FILTER_W)),
        'hy_f_b2': normal((nD, HY_FILTER_W), 0.1),
        'hy_f_w3': dense((nD, HY_FILTER_W, HY_FILTER_W)),
        'hy_f_b3': normal((nD, HY_FILTER_W), 0.1),
        'hy_f_w4': dense((nD, HY_FILTER_W, 2 * HY_ORDER * D), 0.1),
        'hy_f_b4': normal((nD, 2 * HY_ORDER * D), 0.02),
        'hy_f_freq': 1.0 + normal((nD, HY_FILTER_W), 0.05),
        'hy_decay': hy_decay,
        'hy_skip': normal((nD, HY_ORDER, D), 0.5),
        'hy_w_o': dense((nD, D, D)),
    }


def reference(x, c, ctx, c_ctx, ada_w, ada_b, norm_g, mlp_w1, mlp_w2,
              swa_w_qkv, swa_sink, swa_w_o,
              ml_w_in, ml_gate_b, ml_head_g, ml_w_o,
              mla_w_in, mla_q_g, mla_kv_g, mla_w_uq, mla_w_ukv, mla_w_o,
              hy_w_in, hy_conv_w, hy_conv_b, hy_f_w1, hy_f_b1, hy_f_w2, hy_f_b2,
              hy_f_w3, hy_f_b3, hy_f_w4, hy_f_b4, hy_f_freq, hy_decay, hy_skip, hy_w_o):
    S = x.shape[1]
    rows = S // GRID_W
    row = jnp.repeat(jnp.arange(rows, dtype=jnp.int32), GRID_W)
    col = jnp.broadcast_to(jnp.arange(GRID_W, dtype=jnp.int32)[None, :], (rows, GRID_W)).reshape(-1)
    rope_swa = axial_rope_tables(row, col, SWA_HEAD_DIM)
    rope_mla = axial_rope_tables(row, col, MLA_ROPE)
    s_lat = jax.nn.silu(c)
    s_ctx = jax.nn.silu(c_ctx)
    h, hc = x, ctx
    for i in range(DEPTH):
        kind, j = i % N_MIXERS, i // N_MIXERS
        ctx_after = any(l % N_MIXERS in CTX_READERS for l in range(i + 1, DEPTH))
        ctx_in = ctx_after or (kind in CTX_READERS)
        m_lat = [m[:, None, :] for m in jnp.split(s_lat @ ada_w[i] + ada_b[i], N_MOD, axis=-1)]
        u = modulate(rmsnorm(h, norm_g[i, 0]), m_lat[0], m_lat[1])
        if ctx_in:
            m_ctx = jnp.split(s_ctx @ ada_w[i] + ada_b[i], N_MOD, axis=-1)
            uc = modulate(rmsnorm(hc, norm_g[i, 0]), m_ctx[0], m_ctx[1])
        else:
            uc = None
        if kind == 0:
            y, yc = window_gqa_mixer(u, uc, swa_w_qkv[j], swa_sink[j], swa_w_o[j], rope_swa, ctx_after)
        elif kind == 1:
            y, yc = mlstm_mixer(u, uc, ml_w_in[j], ml_gate_b[j], ml_head_g[j], ml_w_o[j], ctx_after)
        elif kind == 2:
            y, yc = mla_mixer(u, uc, mla_w_in[j], mla_q_g[j], mla_kv_g[j], mla_w_uq[j], mla_w_ukv[j], mla_w_o[j], rope_mla, ctx_after)
        else:
            y, yc = hyena_mixer(u, uc, hy_w_in[j], hy_conv_w[j], hy_conv_b[j], hy_f_w1[j], hy_f_b1[j],
                                hy_f_w2[j], hy_f_b2[j], hy_f_w3[j], hy_f_b3[j], hy_f_w4[j], hy_f_b4[j],
                                hy_f_freq[j], hy_decay[j], hy_skip[j], hy_w_o[j], ctx_after)
        h = h + m_lat[2] * rmsnorm(y, norm_g[i, 1])
        v = modulate(rmsnorm(h, norm_g[i, 2]), m_lat[3], m_lat[4])
        h = h + m_lat[5] * rmsnorm(sqrelu_mlp(v, mlp_w1[i], mlp_w2[i]), norm_g[i, 3])
        if ctx_after:
            hc = hc + m_ctx[2] * rmsnorm(yc, norm_g[i, 1])
            vc = modulate(rmsnorm(hc, norm_g[i, 2]), m_ctx[3], m_ctx[4])
            hc = hc + m_ctx[5] * rmsnorm(sqrelu_mlp(vc, mlp_w1[i], mlp_w2[i]), norm_g[i, 3])
    return h
```

```python
import functools
import math

import numpy as np
import jax
import jax.numpy as jnp
from jax import lax
from jax.experimental import pallas as pl
from jax.experimental.pallas import tpu as pltpu

F32, BF16 = jnp.float32, jnp.bfloat16
HIGHEST = lax.Precision.HIGHEST

RMS_EPS = 1e-6
ROPE_THETA = 10000.0
GRID_W = 64
N_MOD = 6
LANE = 128

SWA_HEADS, SWA_KV_HEADS, SWA_HEAD_DIM, SWA_WINDOW, SWA_BLOCK = 16, 4, 128, 128, 128
ML_HEADS, ML_DQK, ML_DV = 8, 128, 256
ML_CHUNK = 256
MLA_HEADS, MLA_Q_RANK, MLA_KV_RANK, MLA_NOPE, MLA_ROPE, MLA_V = 16, 512, 512, 128, 64, 128
HY_ORDER, HY_BANDS, HY_FILTER_W, HY_SHORT = 2, 16, 64, 3
HY_P = 256

VMEM_LIMIT = 48 * 1024 * 1024


def _params(sem):
    return pltpu.CompilerParams(dimension_semantics=sem, vmem_limit_bytes=VMEM_LIMIT)


def _dot(a, b):
    return jnp.dot(a, b, preferred_element_type=F32)


def _dot_nt(a, b):
    return lax.dot_general(a, b, (((1,), (1,)), ((), ())), preferred_element_type=F32)


def _dot_tn(a, b):
    return lax.dot_general(a, b, (((0,), (0,)), ((), ())), preferred_element_type=F32)


def _rms(x, g):
    return x * lax.rsqrt(jnp.mean(x * x, axis=-1, keepdims=True) + RMS_EPS) * g


def _ada_kernel(s_ref, w_ref, b_ref, o_ref):
    s = s_ref[...]
    s = s * jax.nn.sigmoid(s)
    o_ref[0] = jnp.dot(s, w_ref[0], preferred_element_type=F32, precision=HIGHEST) + b_ref[0]


def ada_mods(cond, ada_w, ada_b, tn=1536):
    depth, d, n = ada_w.shape
    rows = cond.shape[0]
    return pl.pallas_call(
        _ada_kernel,
        grid=(depth, n // tn),
        in_specs=[
            pl.BlockSpec((rows, d), lambda l, j: (0, 0)),
            pl.BlockSpec((1, d, tn), lambda l, j: (l, 0, j)),
            pl.BlockSpec((1, 1, tn), lambda l, j: (l, 0, j)),
        ],
        out_specs=pl.BlockSpec((1, rows, tn), lambda l, j: (l, 0, j)),
        out_shape=jax.ShapeDtypeStruct((depth, rows, n), F32),
        compiler_params=_params(("parallel", "parallel")),
        name="ada_mods",
    )(cond, ada_w, ada_b.reshape(depth, 1, n))


class Rows:
    def __init__(self, batch, seq, ctx_len):
        self.batch, self.seq, self.ctx_len = batch, seq, ctx_len
        self.n_lat = batch * seq
        self.n_ctx = batch * ctx_len
        self.n_all = self.n_lat + self.n_ctx

    def mod_index(self, layer, k, tm):
        lat_blocks, per_batch = self.n_lat // tm, self.seq // tm

        def index(i, *_):
            b = jnp.where(i < lat_blocks, i // per_batch, self.batch)
            return (layer * 8 + b, 0, k)

        return index


def _linear_kernel(*refs, prologue, rope, n_lat_blocks, rope_pattern, rope_jmax):
    it = iter(refs)
    x_ref = next(it)
    g_ref = next(it) if prologue in ("norm", "norm_mod") else None
    sh_ref = next(it) if prologue == "norm_mod" else None
    sc_ref = next(it) if prologue == "norm_mod" else None
    w_ref = next(it)
    cc_ref = next(it) if rope else None
    ss_ref = next(it) if rope else None
    o_ref = next(it)
    u_ref = next(it)
    i, j = pl.program_id(0), pl.program_id(1)

    @pl.when(j == 0)
    def _():
        x = x_ref[...].astype(F32)
        if prologue in ("norm", "norm_mod"):
            x = _rms(x, g_ref[...])
        if prologue == "norm_mod":
            x = x * (1.0 + sc_ref[0]) + sh_ref[0]
        u_ref[...] = x.astype(BF16)

    acc = _dot(u_ref[...], w_ref[...])

    def plain():
        o_ref[...] = acc.astype(o_ref.dtype)

    def roped():
        cc, ss = cc_ref[...], ss_ref[...]
        segs = []
        for gi, on in enumerate(rope_pattern):
            seg = acc[:, gi * LANE:(gi + 1) * LANE]
            if on:
                seg = seg * cc + pltpu.roll(seg, LANE // 2, 1) * ss
            segs.append(seg)
        o_ref[...] = jnp.concatenate(segs, axis=1).astype(o_ref.dtype)

    if not rope:
        plain()
    else:
        cond = jnp.logical_and(i < n_lat_blocks, j < rope_jmax)
        pl.when(cond)(roped)
        pl.when(jnp.logical_not(cond))(plain)


def linear(x, w, *, rows, n_rows, tm, tn, out_dtype, x_cols=None, prologue="none", gain=None,
           mods=None, layer=0, mod_k=(0, 1), rope_tabs=None, rope_pattern=(), rope_jmax=1 << 30,
           name="linear"):
    k, n = w.shape
    xc = 0 if x_cols is None else x_cols
    rope = rope_tabs is not None
    in_specs = [pl.BlockSpec((tm, k), lambda i, j: (i, xc))]
    args = [x]
    if prologue in ("norm", "norm_mod"):
        in_specs.append(pl.BlockSpec((1, k), lambda i, j: (0, 0)))
        args.append(gain.reshape(1, k))
    if prologue == "norm_mod":
        for mk in mod_k:
            in_specs.append(pl.BlockSpec((1, 1, k), rows.mod_index(layer, mk, tm)))
            args.append(mods)
    in_specs.append(pl.BlockSpec((k, tn), lambda i, j: (0, j)))
    args.append(w)
    if rope:
        per_seq = rows.seq // tm
        for t in rope_tabs:
            in_specs.append(pl.BlockSpec((tm, LANE), lambda i, j: (i % per_seq, 0)))
            args.append(t)
    kern = functools.partial(_linear_kernel, prologue=prologue, rope=rope,
                             n_lat_blocks=rows.n_lat // tm, rope_pattern=tuple(rope_pattern),
                             rope_jmax=rope_jmax)
    return pl.pallas_call(
        kern,
        grid=(n_rows // tm, n // tn),
        in_specs=in_specs,
        out_specs=pl.BlockSpec((tm, tn), lambda i, j: (i, j)),
        out_shape=jax.ShapeDtypeStruct((n_rows, n), out_dtype),
        scratch_shapes=[pltpu.VMEM((tm, k), BF16)],
        compiler_params=_params(("parallel", "arbitrary")),
        name=name,
    )(*args)


def _outproj_kernel(*refs, mode):
    it = iter(refs)
    if mode == "mlstm":
        hf_ref, hb_ref, og_ref, hg_ref = next(it), next(it), next(it), next(it)
    else:
        o_ref_in = next(it)
    w_ref, h_ref, gate_ref, g_ref, out_ref = next(it), next(it), next(it), next(it), next(it)
    if mode == "mlstm":
        hs = hf_ref[0].astype(F32) + hb_ref[0].astype(F32)
        og = jax.nn.sigmoid(og_ref[...].astype(F32))
        hg = hg_ref[...]
        parts = []
        for h in range(ML_HEADS):
            sl = slice(h * ML_DV, (h + 1) * ML_DV)
            parts.append((_rms(hs[:, sl], hg[:, sl]) * og[:, sl]).astype(BF16))
        o = jnp.concatenate(parts, axis=1)
    else:
        o = o_ref_in[...]
    y = _dot(o, w_ref[...])
    out_ref[...] = h_ref[...] + gate_ref[0] * _rms(y, g_ref[...])


def outproj_residual(o_args, w, h, mods, gain, *, rows, n_rows, tm, layer, mode="plain", name="outproj"):
    k, d = w.shape
    if mode == "mlstm":
        hsum, z, head_g = o_args
        in_specs = [
            pl.BlockSpec((1, tm, k), lambda i: (0, i, 0)),
            pl.BlockSpec((1, tm, k), lambda i: (1, i, 0)),
            pl.BlockSpec((tm, k), lambda i: (i, 2)),
            pl.BlockSpec((1, k), lambda i: (0, 0)),
        ]
        args = [hsum, hsum, z, head_g.reshape(1, k)]
    else:
        in_specs = [pl.BlockSpec((tm, k), lambda i: (i, 0))]
        args = [o_args]
    in_specs += [
        pl.BlockSpec((k, d), lambda i: (0, 0)),
        pl.BlockSpec((tm, d), lambda i: (i, 0)),
        pl.BlockSpec((1, 1, d), rows.mod_index(layer, 2, tm)),
        pl.BlockSpec((1, d), lambda i: (0, 0)),
    ]
    args += [w, h, mods, gain.reshape(1, d)]
    return pl.pallas_call(
        functools.partial(_outproj_kernel, mode=mode),
        grid=(n_rows // tm,),
        in_specs=in_specs,
        out_specs=pl.BlockSpec((tm, d), lambda i: (i, 0)),
        out_shape=jax.ShapeDtypeStruct((n_rows, d), F32),
        compiler_params=_params(("parallel",)),
        name=name,
    )(*args)


def _mlp_kernel(h_ref, g2_ref, sh_ref, sc_ref, w1_ref, w2_ref, gate_ref, g3_ref, out_ref, v_ref, acc_ref):
    f = pl.program_id(1)

    @pl.when(f == 0)
    def _():
        v = _rms(h_ref[...], g2_ref[...]) * (1.0 + sc_ref[0]) + sh_ref[0]
        v_ref[...] = v.astype(BF16)
        acc_ref[...] = jnp.zeros_like(acc_ref)

    a = jnp.maximum(_dot(v_ref[...], w1_ref[...]), 0.0)
    acc_ref[...] += _dot((a * a).astype(BF16), w2_ref[...])

    @pl.when(f == pl.num_programs(1) - 1)
    def _():
        out_ref[...] = h_ref[...] + gate_ref[0] * _rms(acc_ref[...], g3_ref[...])


def mlp_residual(h, w1, w2, mods, g2, g3, *, rows, n_rows, tm, tf, layer):
    d, ff = w1.shape
    return pl.pallas_call(
        _mlp_kernel,
        grid=(n_rows // tm, ff // tf),
        in_specs=[
            pl.BlockSpec((tm, d), lambda i, f: (i, 0)),
            pl.BlockSpec((1, d), lambda i, f: (0, 0)),
            pl.BlockSpec((1, 1, d), rows.mod_index(layer, 3, tm)),
            pl.BlockSpec((1, 1, d), rows.mod_index(layer, 4, tm)),
            pl.BlockSpec((d, tf), lambda i, f: (0, f)),
            pl.BlockSpec((tf, d), lambda i, f: (f, 0)),
            pl.BlockSpec((1, 1, d), rows.mod_index(layer, 5, tm)),
            pl.BlockSpec((1, d), lambda i, f: (0, 0)),
        ],
        out_specs=pl.BlockSpec((tm, d), lambda i, f: (i, 0)),
        out_shape=jax.ShapeDtypeStruct((n_rows, d), F32),
        scratch_shapes=[pltpu.VMEM((tm, d), BF16), pltpu.VMEM((tm, d), F32)],
        compiler_params=_params(("parallel", "arbitrary")),
        name="mlp",
    )(h, g2.reshape(1, d), mods, mods, w1, w2, mods, g3.reshape(1, d))


def _rope_tables(seq, d_rot):
    pos = np.arange(seq)
    row, col = pos // GRID_W, pos % GRID_W
    n = d_rot // 4
    inv = ROPE_THETA ** (-np.arange(n, dtype=np.float64) / n)
    ang = np.concatenate([row[:, None] * inv, col[:, None] * inv], axis=-1)
    cos, sin = np.cos(ang), np.sin(ang)
    half = d_rot // 2
    cc = np.zeros((seq, LANE))
    ss = np.zeros((seq, LANE))
    cc[:, :half] = cos
    cc[:, LANE // 2:LANE // 2 + half] = cos
    ss[:, :half] = -sin
    ss[:, LANE // 2:LANE // 2 + half] = sin
    return jnp.asarray(cc, F32), jnp.asarray(ss, F32)


def _swa_kernel(sink_ref, q_ref, kp_ref, kc_ref, kn_ref, kx_ref, vp_ref, vc_ref, vn_ref, vx_ref, o_ref,
                *, nb, seq, ctx_len):
    n = pl.program_id(1)
    hd, grp, lb = SWA_HEAD_DIM, SWA_HEADS // SWA_KV_HEADS, SWA_BLOCK
    scale = hd ** -0.5
    m_rows, n_loc = grp * lb, 3 * lb
    r = lax.broadcasted_iota(jnp.int32, (m_rows, n_loc + ctx_len), 0)
    c = lax.broadcasted_iota(jnp.int32, (m_rows, n_loc + ctx_len), 1)
    qpos = n * lb + (r & (lb - 1))
    kpos = (n - 1) * lb + c
    local_ok = (jnp.abs(kpos - qpos) <= SWA_WINDOW) & (kpos >= 0) & (kpos < seq) & (n < nb)
    valid = local_ok | (c >= n_loc)
    outs = []
    for kv in range(SWA_KV_HEADS):
        ks = slice(kv * hd, (kv + 1) * hd)
        qg = jnp.concatenate([q_ref[:, (kv * grp + g) * hd:(kv * grp + g + 1) * hd] for g in range(grp)], axis=0)
        keys = jnp.concatenate([kp_ref[:, ks], kc_ref[:, ks], kn_ref[:, ks], kx_ref[:, ks]], axis=0)
        vals = jnp.concatenate([vp_ref[:, ks], vc_ref[:, ks], vn_ref[:, ks], vx_ref[:, ks]], axis=0)
        s = jnp.where(valid, _dot_nt(qg, keys) * scale, -1e30)
        for g in range(grp):
            sg = s[g * lb:(g + 1) * lb]
            snk = sink_ref[kv * grp + g]
            m = jnp.maximum(jnp.max(sg, axis=1, keepdims=True), snk)
            p = jnp.exp(sg - m)
            l = jnp.sum(p, axis=1, keepdims=True) + jnp.exp(snk - m)
            outs.append((_dot(p.astype(BF16), vals) / l).astype(BF16))
    o_ref[...] = jnp.concatenate(outs, axis=1)


def swa_attention(qkv, sink, *, rows):
    b, seq, ctx_len = rows.batch, rows.seq, rows.ctx_len
    lb = SWA_BLOCK
    nb, ncb = seq // lb, ctx_len // lb
    lat_blocks = rows.n_lat // lb
    qw = SWA_HEADS * SWA_HEAD_DIM
    kw = SWA_KV_HEADS * SWA_HEAD_DIM
    kcol, vcol = qw // kw, qw // kw + 1

    def qidx(bi, n):
        return (jnp.where(n < nb, bi * nb + n, lat_blocks + bi * ncb + (n - nb)), 0)

    def kidx(off, col):
        return lambda bi, n: (bi * nb + jnp.clip(n + off, 0, nb - 1), col)

    def xidx(col):
        return lambda bi, n: (rows.n_lat // ctx_len + bi, col)

    kern = functools.partial(_swa_kernel, nb=nb, seq=seq, ctx_len=ctx_len)
    return pl.pallas_call(
        kern,
        grid=(b, nb + ncb),
        in_specs=[
            pl.BlockSpec(memory_space=pltpu.SMEM),
            pl.BlockSpec((lb, qw), qidx),
            pl.BlockSpec((lb, kw), kidx(-1, kcol)),
            pl.BlockSpec((lb, kw), kidx(0, kcol)),
            pl.BlockSpec((lb, kw), kidx(1, kcol)),
            pl.BlockSpec((ctx_len, kw), xidx(kcol)),
            pl.BlockSpec((lb, kw), kidx(-1, vcol)),
            pl.BlockSpec((lb, kw), kidx(0, vcol)),
            pl.BlockSpec((lb, kw), kidx(1, vcol)),
            pl.BlockSpec((ctx_len, kw), xidx(vcol)),
        ],
        out_specs=pl.BlockSpec((lb, qw), qidx),
        out_shape=jax.ShapeDtypeStruct((rows.n_all, qw), BF16),
        compiler_params=_params(("parallel", "parallel")),
        name="swa_attention",
    )(sink, qkv, qkv, qkv, qkv, qkv, qkv, qkv, qkv, qkv)


def _mlstm_kernel(q_ref, k_ref, v_ref, gt_ref, gb_ref, o_ref, c_ref, n_ref, m_ref):
    d, c = pl.program_id(1), pl.program_id(2)
    L, H, dqk, dv = ML_CHUNK, ML_HEADS, ML_DQK, ML_DV
    scale = dqk ** -0.5

    @pl.when(c == 0)
    def _():
        c_ref[...] = jnp.zeros_like(c_ref)
        n_ref[...] = jnp.zeros_like(n_ref)
        m_ref[...] = jnp.zeros_like(m_ref)

    row = lax.broadcasted_iota(jnp.int32, (L, L), 0)
    col = lax.broadcasted_iota(jnp.int32, (L, L), 1)
    sign = 1 - 2 * d
    mask = (col - row) * sign <= 0
    mask_t = (row - col) * sign <= 0
    eye = row == col
    gates = gt_ref[...] + gb_ref[...]
    for h in range(H):
        qh = q_ref[:, h * dqk:(h + 1) * dqk]
        kh = k_ref[:, h * dqk:(h + 1) * dqk]
        vh = v_ref[:, h * dv:(h + 1) * dv]
        i_col = gates[:, h:h + 1]
        f_col = jax.nn.log_sigmoid(gates[:, H + h:H + h + 1])
        f_row = jnp.sum(jnp.where(eye, f_col, 0.0), axis=0, keepdims=True)
        i_row = jnp.sum(jnp.where(eye, i_col, 0.0), axis=0, keepdims=True)
        b_col = jnp.sum(jnp.where(mask, f_row, 0.0), axis=1, keepdims=True)
        b_row = jnp.sum(jnp.where(mask_t, f_col, 0.0), axis=0, keepdims=True)
        m_prev = m_ref[h]
        dmat = jnp.where(mask, b_col - b_row + i_row, -jnp.inf)
        inter = b_col + m_prev
        m_t = jnp.maximum(inter, jnp.max(dmat, axis=1, keepdims=True))
        w = jnp.exp(dmat - m_t)
        g = jnp.exp(inter - m_t) * scale
        s = _dot_nt(qh, kh) * (w * scale)
        c_prev = c_ref[h]
        n_prev = n_ref[h]
        num = _dot(s.astype(BF16), vh) + g * _dot_nt(qh, c_prev.astype(BF16))
        den = jnp.sum(s, axis=1, keepdims=True) + g * jnp.sum(qh.astype(F32) * n_prev, axis=1, keepdims=True)
        hout = num / jnp.maximum(jnp.abs(den), jnp.exp(-m_t))
        o_ref[0, :, h * dv:(h + 1) * dv] = hout.astype(o_ref.dtype)
        b_end = jnp.sum(f_col, axis=0, keepdims=True)
        dec = b_end - b_col + i_col
        m_new = jnp.maximum(b_end + m_prev, jnp.max(dec, axis=0, keepdims=True))
        ws = jnp.exp(dec - m_new)
        gs = jnp.exp(b_end + m_prev - m_new)
        c_ref[h] = gs * c_prev + _dot_tn((vh.astype(F32) * ws).astype(BF16), kh)
        n_ref[h] = gs * n_prev + jnp.sum(kh.astype(F32) * ws, axis=0, keepdims=True)
        m_ref[h] = m_new


def mlstm_scan(z, gates, gate_b, *, rows):
    b, seq, ctx_len = rows.batch, rows.seq, rows.ctx_len
    L = ML_CHUNK
    assert ctx_len == L and seq % L == 0
    nlc = seq // L
    hq, hv = ML_HEADS * ML_DQK, ML_HEADS * ML_DV

    def rb(bi, d, c):
        lat = bi * nlc + jnp.where(d == 0, c - 1, nlc - c)
        return jnp.where(c == 0, rows.n_lat // L + bi, lat)

    return pl.pallas_call(
        _mlstm_kernel,
        grid=(b, 2, nlc + 1),
        in_specs=[
            pl.BlockSpec((L, hq), lambda bi, d, c: (rb(bi, d, c), 0)),
            pl.BlockSpec((L, hq), lambda bi, d, c: (rb(bi, d, c), 1)),
            pl.BlockSpec((L, hv), lambda bi, d, c: (rb(bi, d, c), 1)),
            pl.BlockSpec((L, LANE), lambda bi, d, c: (rb(bi, d, c), d)),
            pl.BlockSpec((1, LANE), lambda bi, d, c: (0, d)),
        ],
        out_specs=pl.BlockSpec((1, L, hv), lambda bi, d, c: (d, rb(bi, d, c), 0)),
        out_shape=jax.ShapeDtypeStruct((2, rows.n_all, hv), BF16),
        scratch_shapes=[
            pltpu.VMEM((ML_HEADS, ML_DV, ML_DQK), F32),
            pltpu.VMEM((ML_HEADS, 1, ML_DQK), F32),
            pltpu.VMEM((ML_HEADS, 1, 1), F32),
        ],
        compiler_params=_params(("parallel", "parallel", "arbitrary")),
        name="mlstm_scan",
    )(z, z, z, gates, gate_b)


def _mla_kernel(q_ref, kvl_ref, kvx_ref, krl_ref, krx_ref, o_ref):
    scale = (MLA_NOPE + MLA_ROPE) ** -0.5
    qn, qr = q_ref[:, :LANE], q_ref[:, LANE:]
    s_l = (_dot_nt(qn, kvl_ref[:, :LANE]) + _dot_nt(qr, krl_ref[...])) * scale
    s_x = (_dot_nt(qn, kvx_ref[:, :LANE]) + _dot_nt(qr, krx_ref[...])) * scale
    m = jnp.maximum(jnp.max(s_l, axis=1, keepdims=True), jnp.max(s_x, axis=1, keepdims=True))
    p_l, p_x = jnp.exp(s_l - m), jnp.exp(s_x - m)
    l = jnp.sum(p_l, axis=1, keepdims=True) + jnp.sum(p_x, axis=1, keepdims=True)
    o = _dot(p_l.astype(BF16), kvl_ref[:, LANE:]) + _dot(p_x.astype(BF16), kvx_ref[:, LANE:])
    o_ref[...] = (o / l).astype(o_ref.dtype)


def mla_attention(q, kv, z, *, rows, tq=256):
    b, seq, ctx_len = rows.batch, rows.seq, rows.ctx_len
    nq = seq // tq
    kr_col = z.shape[1] // LANE - 1
    ctx0 = rows.n_lat // ctx_len
    return pl.pallas_call(
        _mla_kernel,
        grid=(b, MLA_HEADS, nq),
        in_specs=[
            pl.BlockSpec((tq, 2 * LANE), lambda bi, h, i: (bi * nq + i, h)),
            pl.BlockSpec((seq, 2 * LANE), lambda bi, h, i: (bi, h)),
            pl.BlockSpec((ctx_len, 2 * LANE), lambda bi, h, i: (ctx0 + bi, h)),
            pl.BlockSpec((seq, LANE), lambda bi, h, i: (bi, kr_col)),
            pl.BlockSpec((ctx_len, LANE), lambda bi, h, i: (ctx0 + bi, kr_col)),
        ],
        out_specs=pl.BlockSpec((tq, LANE), lambda bi, h, i: (bi * nq + i, h)),
        out_shape=jax.ShapeDtypeStruct((rows.n_lat, MLA_HEADS * MLA_V), BF16),
        compiler_params=_params(("parallel", "parallel", "parallel")),
        name="mla_attention",
    )(q, kv, kv, z, z)


def _hy_filter_kernel(ft_ref, w1_ref, b1_ref, w2_ref, b2_ref, w3_ref, b3_ref, fr_ref, w4_ref, b4_ref, dec_ref,
                      o_ref, a_ref, *, back_from):
    j = pl.program_id(0)
    hdot = functools.partial(jnp.dot, preferred_element_type=F32, precision=HIGHEST)

    @pl.when(j == 0)
    def _():
        fr = fr_ref[...]
        a = jnp.sin(fr * (hdot(ft_ref[...], w1_ref[...]) + b1_ref[...]))
        a = jnp.sin(fr * (hdot(a, w2_ref[...]) + b2_ref[...]))
        a_ref[...] = jnp.sin(fr * (hdot(a, w3_ref[...]) + b3_ref[...]))

    t = ft_ref[:, 0:1]
    filt = (hdot(a_ref[...], w4_ref[...]) + b4_ref[...]) * jnp.exp(-t * jnp.abs(dec_ref[...]))
    row = lax.broadcasted_iota(jnp.int32, filt.shape, 0)
    o_ref[...] = jnp.where(jnp.logical_and(row == 0, j >= back_from), 0.0, filt)


def hyena_filters(seq, w1, b1, w2, b2, w3, b3, w4, b4, freq, decay, tn=512):
    t = np.linspace(0.0, 1.0, seq)[:, None]
    w = (2.0 * math.pi / seq) * np.arange(seq)[:, None]
    bands = np.linspace(1e-4, HY_BANDS - 1, HY_BANDS)[None, :]
    feats = np.zeros((seq, LANE))
    feats[:, :1 + 2 * HY_BANDS] = np.concatenate([t, np.cos(bands * w), -np.sin(bands * w)], axis=-1)
    n = w4.shape[1]
    fw = HY_FILTER_W
    w1p = jnp.zeros((LANE, fw), F32).at[:w1.shape[0]].set(w1)
    row = lambda a: a.reshape(1, -1)
    full = lambda shape: pl.BlockSpec(shape, lambda j: (0, 0))
    return pl.pallas_call(
        functools.partial(_hy_filter_kernel, back_from=(n // 2) // tn),
        grid=(n // tn,),
        in_specs=[full((seq, LANE)), full((LANE, fw)), full((1, fw)), full((fw, fw)), full((1, fw)),
                  full((fw, fw)), full((1, fw)), full((1, fw)),
                  pl.BlockSpec((fw, tn), lambda j: (0, j)),
                  pl.BlockSpec((1, tn), lambda j: (0, j)),
                  pl.BlockSpec((1, tn), lambda j: (0, j))],
        out_specs=pl.BlockSpec((seq, tn), lambda j: (0, j)),
        out_shape=jax.ShapeDtypeStruct((seq, n), F32),
        scratch_shapes=[pltpu.VMEM((seq, fw), F32)],
        compiler_params=_params(("arbitrary",)),
        name="hyena_filters",
    )(jnp.asarray(feats, F32), w1p, row(b1), w2, row(b2), w3, row(b3), row(freq), w4, row(b4), row(decay))


def _dft_mats(seq, p):
    n = 2 * seq
    q = n // p
    k1n = q // 2 + 1
    hi, k1 = np.arange(q // 2), np.arange(k1n)
    th = 2 * np.pi * np.outer(k1, hi) / q
    ma = np.zeros((2 * k1n, q // 2))
    ma[0::2], ma[1::2] = np.cos(th), -np.sin(th)
    lo = np.arange(p)
    wf = np.zeros((k1n, 2 * p, 2 * p))
    wi = np.zeros((k1n, 2 * p, 2 * p))
    for k in k1:
        ph = -2 * np.pi * (np.outer(lo, lo) / p + k * lo[None, :] / n)
        er, ei = np.cos(ph), np.sin(ph)
        wf[k] = np.block([[er, -ei], [ei, er]])
        wi[k] = np.block([[er.T, ei.T], [-ei.T, er.T]])
    c = np.full(k1n, 2.0)
    c[0] = c[-1] = 1.0
    th2 = 2 * np.pi * np.outer(hi, k1) / q
    md = np.zeros((q // 2, 2 * k1n))
    md[:, 0::2], md[:, 1::2] = c * np.cos(th2) / n, -c * np.sin(th2) / n
    as_bf16 = lambda a: jnp.asarray(a, F32).astype(BF16)
    return as_bf16(ma), as_bf16(wf), as_bf16(wi), as_bf16(md)


def _hy_short_kernel(x1_ref, x2_ref, v_ref, w_ref, b_ref, o1_ref, o2_ref, o3_ref):
    for k, (x_ref, o_ref) in enumerate(((x1_ref, o1_ref), (x2_ref, o2_ref), (v_ref, o3_ref))):
        x = x_ref[...].astype(F32)
        n = x.shape[0]
        row = lax.broadcasted_iota(jnp.int32, x.shape, 0)
        prev = jnp.where(row == 0, 0.0, pltpu.roll(x, 1, 0))
        nxt = jnp.where(row == n - 1, 0.0, pltpu.roll(x, n - 1, 0))
        w = w_ref[k]
        o_ref[...] = (prev * w[0:1] + x * w[1:2] + nxt * w[2:3] + b_ref[k]).astype(o_ref.dtype)


def hyena_short_conv(x3, conv_w, conv_b, *, rows, cb=256):
    b, seq = rows.batch, rows.seq
    d = x3.shape[1] // 3
    nblk = d // cb
    w = jnp.transpose(conv_w.reshape(HY_SHORT, 3, d), (1, 0, 2))
    bias = conv_b.reshape(3, 1, d)
    in_x = [pl.BlockSpec((seq, cb), lambda bi, j, k=k: (bi, k * nblk + j)) for k in range(3)]
    out = pl.BlockSpec((seq, cb), lambda bi, j: (bi, j))
    shape = jax.ShapeDtypeStruct((rows.n_lat, d), BF16)
    return pl.pallas_call(
        _hy_short_kernel,
        grid=(b, nblk),
        in_specs=in_x + [pl.BlockSpec((3, HY_SHORT, cb), lambda bi, j: (0, 0, j)),
                         pl.BlockSpec((3, 1, cb), lambda bi, j: (0, 0, j))],
        out_specs=[out, out, out],
        out_shape=[shape, shape, shape],
        compiler_params=_params(("parallel", "parallel")),
        name="hyena_short_conv",
    )(x3, x3, x3, w, bias)


def _hy_stage_a_kernel(x_ref, m_ref, o_ref):
    o_ref[0] = _dot(m_ref[...], x_ref[0].astype(BF16)).astype(o_ref.dtype)


def hyena_stage_a(x, ma, *, width=8192):
    bx, qh, cols = x.shape
    mo = ma.shape[0]
    return pl.pallas_call(
        _hy_stage_a_kernel,
        grid=(bx, cols // width),
        in_specs=[pl.BlockSpec((1, qh, width), lambda bi, j: (bi, 0, j)),
                  pl.BlockSpec((mo, qh), lambda bi, j: (0, 0))],
        out_specs=pl.BlockSpec((1, mo, width), lambda bi, j: (bi, 0, j)),
        out_shape=jax.ShapeDtypeStruct((bx, mo, cols), BF16),
        compiler_params=_params(("parallel", "parallel")),
        name="hyena_stage_a",
    )(x, ma)


def _hy_spec_kernel(a0_ref, a1_ref, wf_ref, o_ref):
    p = o_ref.shape[2]
    cb = o_ref.shape[3]
    wf = wf_ref[0]
    x0 = _dot(wf, a0_ref[0].reshape(2 * p, cb))
    x1 = _dot(wf, a1_ref[0].reshape(2 * p, cb))
    o_ref[0, 0] = x0[:p] + x1[:p]
    o_ref[0, 1] = x0[p:] - x1[p:]


def hyena_spectrum(af, wf, *, d, cb=512):
    k1n, _, p, _ = af.shape
    nblk = (HY_ORDER * d) // cb
    return pl.pallas_call(
        _hy_spec_kernel,
        grid=(nblk, k1n),
        in_specs=[pl.BlockSpec((1, 2, p, cb), lambda j, k: (k, 0, 0, j)),
                  pl.BlockSpec((1, 2, p, cb), lambda j, k: (k, 0, 0, nblk + j)),
                  pl.BlockSpec((1, 2 * p, 2 * p), lambda j, k: (k, 0, 0))],
        out_specs=pl.BlockSpec((1, 2, p, cb), lambda j, k: (k, 0, 0, j)),
        out_shape=jax.ShapeDtypeStruct((k1n, 2, p, HY_ORDER * d), F32),
        compiler_params=_params(("parallel", "parallel")),
        name="hyena_spectrum",
    )(af, af, wf)


def _hy_mid_kernel(a_ref, wf_ref, wi_ref, h_ref, o_ref):
    p, cb = a_ref.shape[3], a_ref.shape[4]
    x = _dot(wf_ref[0], a_ref[0, 0].reshape(2 * p, cb))
    xr, xi = x[:p], x[p:]
    hr, hi = h_ref[0, 0], h_ref[0, 1]
    y = jnp.concatenate([xr * hr - xi * hi, xr * hi + xi * hr], axis=0).astype(BF16)
    o_ref[0, 0] = _dot(wi_ref[0], y).reshape(2, p, cb).astype(o_ref.dtype)


def hyena_mid(a, wf, wi, spec, *, order, d, cb=512):
    b, k1n, _, p, _ = a.shape
    nblk = d // cb
    return pl.pallas_call(
        _hy_mid_kernel,
        grid=(nblk, k1n, b),
        in_specs=[pl.BlockSpec((1, 1, 2, p, cb), lambda j, k, bi: (bi, k, 0, 0, j)),
                  pl.BlockSpec((1, 2 * p, 2 * p), lambda j, k, bi: (k, 0, 0)),
                  pl.BlockSpec((1, 2 * p, 2 * p), lambda j, k, bi: (k, 0, 0)),
                  pl.BlockSpec((1, 2, p, cb), lambda j, k, bi: (k, 0, 0, order * nblk + j))],
        out_specs=pl.BlockSpec((1, 1, 2, p, cb), lambda j, k, bi: (bi, k, 0, 0, j)),
        out_shape=jax.ShapeDtypeStruct(a.shape, BF16),
        compiler_params=_params(("parallel", "parallel", "parallel")),
        name="hyena_mid",
    )(a, wf, wi, spec)


def _hy_stage_d_kernel(c_ref, m_ref, x_ref, z_ref, skip_ref, o_ref):
    conv = _dot(m_ref[...], c_ref[0])
    z = z_ref[0].astype(F32)
    o_ref[0] = (x_ref[0].astype(F32) * (conv + z * skip_ref[...])).astype(o_ref.dtype)


def hyena_stage_d(c, md, xg, z, skip, *, width=8192):
    b, mo, cols = c.shape
    qh = md.shape[0]
    d = skip.shape[0]
    skip_t = jnp.tile(skip.reshape(1, d), (1, width // d))
    return pl.pallas_call(
        _hy_stage_d_kernel,
        grid=(b, cols // width),
        in_specs=[pl.BlockSpec((1, mo, width), lambda bi, j: (bi, 0, j)),
                  pl.BlockSpec((qh, mo), lambda bi, j: (0, 0)),
                  pl.BlockSpec((1, qh, width), lambda bi, j: (bi, 0, j)),
                  pl.BlockSpec((1, qh, width), lambda bi, j: (bi, 0, j)),
                  pl.BlockSpec((1, width), lambda bi, j: (0, 0))],
        out_specs=pl.BlockSpec((1, qh, width), lambda bi, j: (bi, 0, j)),
        out_shape=jax.ShapeDtypeStruct((b, qh, cols), BF16),
        compiler_params=_params(("parallel", "parallel")),
        name="hyena_stage_d",
    )(c, md, xg, z, skip_t)


def hyena_long_convs(x1, x2, v, filt, skip, *, rows):
    b, seq = rows.batch, rows.seq
    d = x1.shape[1]
    p = HY_P
    qh = (2 * seq // p) // 2
    k1n = qh + 1
    ma, wf, wi, md = _dft_mats(seq, p)
    af = hyena_stage_a(filt.reshape(1, qh, p * filt.shape[1]), ma)
    spec = hyena_spectrum(af.reshape(k1n, 2, p, filt.shape[1]), wf, d=d)
    view = lambda a: a.reshape(b, qh, p * d)
    y = view(v)
    for order, xg in enumerate((x1, x2)):
        a = hyena_stage_a(y, ma).reshape(b, k1n, 2, p, d)
        c = hyena_mid(a, wf, wi, spec, order=order, d=d)
        y = hyena_stage_d(c.reshape(b, 2 * k1n, p * d), md, view(xg), y, skip[order])
    return y.reshape(rows.n_lat, d)


def _mlstm_gate_weights(w_in, gate_b):
    h = ML_HEADS
    o4 = 2 * h * ML_DQK + 2 * h * ML_DV
    wg = w_in[:, o4:].reshape(-1, 2, 2, h)
    bg = gate_b.reshape(2, 2, h)
    w_out = jnp.zeros((w_in.shape[0], 2 * LANE), F32)
    b_out = jnp.zeros((1, 2 * LANE), F32)
    for d in range(2):
        for gate in range(2):
            lo = d * LANE + gate * h
            w_out = w_out.at[:, lo:lo + h].set(wg[:, gate, d])
            b_out = b_out.at[0, lo:lo + h].set(bg[gate, d])
    return w_out, b_out


def _mla_weights(w_in, w_uq):
    r2 = MLA_Q_RANK + MLA_KV_RANK
    half = MLA_ROPE // 2

    def spread(w):
        z = jnp.zeros(w.shape[:-1] + (half,), w.dtype)
        return jnp.concatenate([w[..., :half], z, w[..., half:], z], axis=-1)

    w_in_p = jnp.concatenate([w_in[:, :r2], spread(w_in[:, r2:])], axis=1)
    wq = w_uq.reshape(w_uq.shape[0], MLA_HEADS, MLA_NOPE + MLA_ROPE)
    wq_p = jnp.concatenate([wq[..., :MLA_NOPE], spread(wq[..., MLA_NOPE:])], axis=-1)
    return w_in_p, wq_p.reshape(w_uq.shape[0], MLA_HEADS * 2 * LANE)


def kernel(x, c, ctx, c_ctx, ada_w, ada_b, norm_g, mlp_w1, mlp_w2, swa_w_qkv, swa_sink, swa_w_o, ml_w_in, ml_gate_b, ml_head_g, ml_w_o, mla_w_in, mla_q_g, mla_kv_g, mla_w_uq, mla_w_ukv, mla_w_o, hy_w_in, hy_conv_w, hy_conv_b, hy_f_w1, hy_f_b1, hy_f_w2, hy_f_b2, hy_f_w3, hy_f_b3, hy_f_w4, hy_f_b4, hy_f_freq, hy_decay, hy_skip, hy_w_o):
    b, seq, d = x.shape
    ctx_len = ctx.shape[1]
    depth = ada_w.shape[0]
    assert depth == 4 and b < 8
    rows = Rows(b, seq, ctx_len)
    tm = 512
    bf = lambda w: w.astype(BF16)

    cond = jnp.zeros((8, d), F32).at[:b].set(c).at[b].set(c_ctx)
    mods = ada_mods(cond, ada_w, ada_b).reshape(depth * 8, 1, N_MOD * d)
    h = jnp.concatenate([x.reshape(rows.n_lat, d), ctx.reshape(rows.n_ctx, d)], axis=0)

    common = dict(rows=rows, tm=tm, mods=mods)

    def finish(h, o_args, w_o, layer, n_rows, mode="plain"):
        h = outproj_residual(o_args, bf(w_o), h, mods, norm_g[layer, 1], rows=rows, n_rows=n_rows, tm=tm,
                             layer=layer, mode=mode, name=f"outproj{layer}")
        return mlp_residual(h, bf(mlp_w1[layer]), bf(mlp_w2[layer]), mods, norm_g[layer, 2], norm_g[layer, 3],
                            rows=rows, n_rows=n_rows, tm=tm, tf=512, layer=layer)

    n_qk_groups = SWA_HEADS + SWA_KV_HEADS
    qkv = linear(h, bf(swa_w_qkv[0]), n_rows=rows.n_all, tn=512, out_dtype=BF16, prologue="norm_mod",
                 gain=norm_g[0, 0], layer=0, rope_tabs=_rope_tables(seq, SWA_HEAD_DIM), rope_pattern=(True,) * 4,
                 rope_jmax=n_qk_groups // 4, name="swa_qkv", **common)
    o = swa_attention(qkv, swa_sink[0], rows=rows)
    h = finish(h, o, swa_w_o[0], 0, rows.n_all)

    w_main = ml_w_in[0][:, :2 * ML_HEADS * ML_DQK + 2 * ML_HEADS * ML_DV]
    w_gate, b_gate = _mlstm_gate_weights(ml_w_in[0], ml_gate_b[0])
    z = linear(h, bf(w_main), n_rows=rows.n_all, tn=1024, out_dtype=BF16, prologue="norm_mod",
               gain=norm_g[1, 0], layer=1, name="mlstm_in", **common)
    gates = linear(h, bf(w_gate), n_rows=rows.n_all, tn=2 * LANE, out_dtype=F32, prologue="norm_mod",
                   gain=norm_g[1, 0], layer=1, name="mlstm_gates", **common)
    hsum = mlstm_scan(z, gates, b_gate, rows=rows)
    h = finish(h, (hsum, z, ml_head_g[0]), ml_w_o[0], 1, rows.n_all, mode="mlstm")

    w_in_p, w_uq_p = _mla_weights(mla_w_in[0], mla_w_uq[0])
    rope_mla = _rope_tables(seq, MLA_ROPE)
    zc = linear(h, bf(w_in_p), n_rows=rows.n_all, tn=w_in_p.shape[1], out_dtype=BF16, prologue="norm_mod",
                gain=norm_g[2, 0], layer=2, rope_tabs=rope_mla, rope_pattern=(False,) * 8 + (True,),
                name="mla_in", **common)
    q = linear(zc, bf(w_uq_p), n_rows=rows.n_lat, tn=512, out_dtype=BF16, x_cols=0, prologue="norm",
               gain=mla_q_g[0], rope_tabs=rope_mla, rope_pattern=(False, True) * 2, name="mla_q",
               rows=rows, tm=tm)
    kv = linear(zc, bf(mla_w_ukv[0]), n_rows=rows.n_all, tn=1024, out_dtype=BF16, x_cols=1, prologue="norm",
                gain=mla_kv_g[0], name="mla_kv", rows=rows, tm=tm)
    o = mla_attention(q, kv, zc, rows=rows)
    h = finish(h, o, mla_w_o[0], 2, rows.n_lat)

    x3 = linear(h, bf(hy_w_in[0]), n_rows=rows.n_lat, tn=1024, out_dtype=BF16, prologue="norm_mod",
                gain=norm_g[3, 0], layer=3, name="hyena_in", **common)
    x1, x2, v = hyena_short_conv(x3, hy_conv_w[0], hy_conv_b[0], rows=rows)
    filt = hyena_filters(seq, hy_f_w1[0], hy_f_b1[0], hy_f_w2[0], hy_f_b2[0], hy_f_w3[0], hy_f_b3[0],
                         hy_f_w4[0], hy_f_b4[0], hy_f_freq[0], hy_decay[0])
    y = hyena_long_convs(x1, x2, v, filt, hy_skip[0], rows=rows)
    h = finish(h, y, hy_w_o[0], 3, rows.n_lat)
    return h.reshape(b, seq, d)
```

```python
import functools
import math

import numpy as np
import jax
import jax.numpy as jnp
from jax import lax
from jax.experimental import pallas as pl
from jax.experimental.pallas import tpu as pltpu

F32, BF16 = jnp.float32, jnp.bfloat16
HIGHEST = lax.Precision.HIGHEST

RMS_EPS = 1e-6
ROPE_THETA = 10000.0
GRID_W = 64
N_MOD = 6
LANE = 128

SWA_HEADS, SWA_KV_HEADS, SWA_HEAD_DIM, SWA_WINDOW, SWA_BLOCK = 16, 4, 128, 128, 128
ML_HEADS, ML_DQK, ML_DV = 8, 128, 256
ML_CHUNK = 256
MLA_HEADS, MLA_Q_RANK, MLA_KV_RANK, MLA_NOPE, MLA_ROPE, MLA_V = 16, 512, 512, 128, 64, 128
HY_ORDER, HY_BANDS, HY_FILTER_W, HY_SHORT = 2, 16, 64, 3
HY_P = 256
HY_G = 16

VMEM_LIMIT = 48 * 1024 * 1024


def _params(sem):
    return pltpu.CompilerParams(dimension_semantics=sem, vmem_limit_bytes=VMEM_LIMIT)


def _dot(a, b):
    return jnp.dot(a, b, preferred_element_type=F32)


def _dot_nt(a, b):
    return lax.dot_general(a, b, (((1,), (1,)), ((), ())), preferred_element_type=F32)


def _dot_tn(a, b):
    return lax.dot_general(a, b, (((0,), (0,)), ((), ())), preferred_element_type=F32)


def _rms(x, g):
    return x * lax.rsqrt(jnp.mean(x * x, axis=-1, keepdims=True) + RMS_EPS) * g


def _ada_kernel(s_ref, w_ref, b_ref, o_ref):
    s = s_ref[...]
    s = s * jax.nn.sigmoid(s)
    o_ref[0] = jnp.dot(s, w_ref[0], preferred_element_type=F32, precision=HIGHEST) + b_ref[0]


def ada_mods(cond, ada_w, ada_b, tn=1536):
    depth, d, n = ada_w.shape
    rows = cond.shape[0]
    return pl.pallas_call(
        _ada_kernel,
        grid=(depth, n // tn),
        in_specs=[
            pl.BlockSpec((rows, d), lambda l, j: (0, 0)),
            pl.BlockSpec((1, d, tn), lambda l, j: (l, 0, j)),
            pl.BlockSpec((1, 1, tn), lambda l, j: (l, 0, j)),
        ],
        out_specs=pl.BlockSpec((1, rows, tn), lambda l, j: (l, 0, j)),
        out_shape=jax.ShapeDtypeStruct((depth, rows, n), F32),
        compiler_params=_params(("parallel", "parallel")),
        name="ada_mods",
    )(cond, ada_w, ada_b.reshape(depth, 1, n))


class Rows:
    def __init__(self, batch, seq, ctx_len):
        self.batch, self.seq, self.ctx_len = batch, seq, ctx_len
        self.n_lat = batch * seq
        self.n_ctx = batch * ctx_len
        self.n_all = self.n_lat + self.n_ctx

    def mod_index(self, layer, k, tm):
        lat_blocks, per_batch = self.n_lat // tm, self.seq // tm

        def index(i, *_):
            b = jnp.where(i < lat_blocks, i // per_batch, self.batch)
            return (layer * 8 + b, 0, k)

        return index


def _linear_kernel(*refs, prologue, rope, n_lat_blocks, rope_pattern, rope_jmax):
    it = iter(refs)
    x_ref = next(it)
    g_ref = next(it) if prologue in ("norm", "norm_mod") else None
    sh_ref = next(it) if prologue == "norm_mod" else None
    sc_ref = next(it) if prologue == "norm_mod" else None
    w_ref = next(it)
    cc_ref = next(it) if rope else None
    ss_ref = next(it) if rope else None
    o_ref = next(it)
    u_ref = next(it)
    i, j = pl.program_id(0), pl.program_id(1)

    @pl.when(j == 0)
    def _():
        x = x_ref[...].astype(F32)
        if prologue in ("norm", "norm_mod"):
            x = _rms(x, g_ref[...])
        if prologue == "norm_mod":
            x = x * (1.0 + sc_ref[0]) + sh_ref[0]
        u_ref[...] = x.astype(BF16)

    acc = _dot(u_ref[...], w_ref[...])

    def plain():
        o_ref[...] = acc.astype(o_ref.dtype)

    def roped():
        cc, ss = cc_ref[...], ss_ref[...]
        segs = []
        for gi, on in enumerate(rope_pattern):
            seg = acc[:, gi * LANE:(gi + 1) * LANE]
            if on:
                seg = seg * cc + pltpu.roll(seg, LANE // 2, 1) * ss
            segs.append(seg)
        o_ref[...] = jnp.concatenate(segs, axis=1).astype(o_ref.dtype)

    if not rope:
        plain()
    else:
        cond = jnp.logical_and(i < n_lat_blocks, j < rope_jmax)
        pl.when(cond)(roped)
        pl.when(jnp.logical_not(cond))(plain)


def linear(x, w, *, rows, n_rows, tm, tn, out_dtype, x_cols=None, prologue="none", gain=None,
           mods=None, layer=0, mod_k=(0, 1), rope_tabs=None, rope_pattern=(), rope_jmax=1 << 30,
           name="linear"):
    k, n = w.shape
    xc = 0 if x_cols is None else x_cols
    rope = rope_tabs is not None
    in_specs = [pl.BlockSpec((tm, k), lambda i, j: (i, xc))]
    args = [x]
    if prologue in ("norm", "norm_mod"):
        in_specs.append(pl.BlockSpec((1, k), lambda i, j: (0, 0)))
        args.append(gain.reshape(1, k))
    if prologue == "norm_mod":
        for mk in mod_k:
            in_specs.append(pl.BlockSpec((1, 1, k), rows.mod_index(layer, mk, tm)))
            args.append(mods)
    in_specs.append(pl.BlockSpec((k, tn), lambda i, j: (0, j)))
    args.append(w)
    if rope:
        per_seq = rows.seq // tm
        for t in rope_tabs:
            in_specs.append(pl.BlockSpec((tm, LANE), lambda i, j: (i % per_seq, 0)))
            args.append(t)
    kern = functools.partial(_linear_kernel, prologue=prologue, rope=rope,
                             n_lat_blocks=rows.n_lat // tm, rope_pattern=tuple(rope_pattern),
                             rope_jmax=rope_jmax)
    return pl.pallas_call(
        kern,
        grid=(n_rows // tm, n // tn),
        in_specs=in_specs,
        out_specs=pl.BlockSpec((tm, tn), lambda i, j: (i, j)),
        out_shape=jax.ShapeDtypeStruct((n_rows, n), out_dtype),
        scratch_shapes=[pltpu.VMEM((tm, k), BF16)],
        compiler_params=_params(("parallel", "arbitrary")),
        name=name,
    )(*args)


def _outproj_kernel(*refs, mode):
    it = iter(refs)
    if mode == "mlstm":
        hf_ref, hb_ref, og_ref, hg_ref = next(it), next(it), next(it), next(it)
    else:
        o_ref_in = next(it)
    w_ref, h_ref, gate_ref, g_ref, out_ref = next(it), next(it), next(it), next(it), next(it)
    if mode == "mlstm":
        hs = hf_ref[0].astype(F32) + hb_ref[0].astype(F32)
        og = jax.nn.sigmoid(og_ref[...].astype(F32))
        hg = hg_ref[...]
        parts = []
        for h in range(ML_HEADS):
            sl = slice(h * ML_DV, (h + 1) * ML_DV)
            parts.append((_rms(hs[:, sl], hg[:, sl]) * og[:, sl]).astype(BF16))
        o = jnp.concatenate(parts, axis=1)
    else:
        o = o_ref_in[...]
    y = _dot(o, w_ref[...])
    out_ref[...] = h_ref[...] + gate_ref[0] * _rms(y, g_ref[...])


def outproj_residual(o_args, w, h, mods, gain, *, rows, n_rows, tm, layer, mode="plain", name="outproj"):
    k, d = w.shape
    if mode == "mlstm":
        hsum, z, head_g = o_args
        in_specs = [
            pl.BlockSpec((1, tm, k), lambda i: (0, i, 0)),
            pl.BlockSpec((1, tm, k), lambda i: (1, i, 0)),
            pl.BlockSpec((tm, k), lambda i: (i, 2)),
            pl.BlockSpec((1, k), lambda i: (0, 0)),
        ]
        args = [hsum, hsum, z, head_g.reshape(1, k)]
    else:
        in_specs = [pl.BlockSpec((tm, k), lambda i: (i, 0))]
        args = [o_args]
    in_specs += [
        pl.BlockSpec((k, d), lambda i: (0, 0)),
        pl.BlockSpec((tm, d), lambda i: (i, 0)),
        pl.BlockSpec((1, 1, d), rows.mod_index(layer, 2, tm)),
        pl.BlockSpec((1, d), lambda i: (0, 0)),
    ]
    args += [w, h, mods, gain.reshape(1, d)]
    return pl.pallas_call(
        functools.partial(_outproj_kernel, mode=mode),
        grid=(n_rows // tm,),
        in_specs=in_specs,
        out_specs=pl.BlockSpec((tm, d), lambda i: (i, 0)),
        out_shape=jax.ShapeDtypeStruct((n_rows, d), F32),
        compiler_params=_params(("parallel",)),
        name=name,
    )(*args)


def _mlp_kernel(h_ref, g2_ref, sh_ref, sc_ref, w1_ref, w2_ref, gate_ref, g3_ref, out_ref, v_ref, acc_ref):
    f = pl.program_id(1)

    @pl.when(f == 0)
    def _():
        v = _rms(h_ref[...], g2_ref[...]) * (1.0 + sc_ref[0]) + sh_ref[0]
        v_ref[...] = v.astype(BF16)
        acc_ref[...] = jnp.zeros_like(acc_ref)

    a = jnp.maximum(_dot(v_ref[...], w1_ref[...]), 0.0)
    acc_ref[...] += _dot((a * a).astype(BF16), w2_ref[...])

    @pl.when(f == pl.num_programs(1) - 1)
    def _():
        out_ref[...] = h_ref[...] + gate_ref[0] * _rms(acc_ref[...], g3_ref[...])


def mlp_residual(h, w1, w2, mods, g2, g3, *, rows, n_rows, tm, tf, layer):
    d, ff = w1.shape
    return pl.pallas_call(
        _mlp_kernel,
        grid=(n_rows // tm, ff // tf),
        in_specs=[
            pl.BlockSpec((tm, d), lambda i, f: (i, 0)),
            pl.BlockSpec((1, d), lambda i, f: (0, 0)),
            pl.BlockSpec((1, 1, d), rows.mod_index(layer, 3, tm)),
            pl.BlockSpec((1, 1, d), rows.mod_index(layer, 4, tm)),
            pl.BlockSpec((d, tf), lambda i, f: (0, f)),
            pl.BlockSpec((tf, d), lambda i, f: (f, 0)),
            pl.BlockSpec((1, 1, d), rows.mod_index(layer, 5, tm)),
            pl.BlockSpec((1, d), lambda i, f: (0, 0)),
        ],
        out_specs=pl.BlockSpec((tm, d), lambda i, f: (i, 0)),
        out_shape=jax.ShapeDtypeStruct((n_rows, d), F32),
        scratch_shapes=[pltpu.VMEM((tm, d), BF16), pltpu.VMEM((tm, d), F32)],
        compiler_params=_params(("parallel", "arbitrary")),
        name="mlp",
    )(h, g2.reshape(1, d), mods, mods, w1, w2, mods, g3.reshape(1, d))


def _rope_tables(seq, d_rot):
    pos = np.arange(seq)
    row, col = pos // GRID_W, pos % GRID_W
    n = d_rot // 4
    inv = ROPE_THETA ** (-np.arange(n, dtype=np.float64) / n)
    ang = np.concatenate([row[:, None] * inv, col[:, None] * inv], axis=-1)
    cos, sin = np.cos(ang), np.sin(ang)
    half = d_rot // 2
    cc = np.zeros((seq, LANE))
    ss = np.zeros((seq, LANE))
    cc[:, :half] = cos
    cc[:, LANE // 2:LANE // 2 + half] = cos
    ss[:, :half] = -sin
    ss[:, LANE // 2:LANE // 2 + half] = sin
    return jnp.asarray(cc, F32), jnp.asarray(ss, F32)


def _swa_kernel(sink_ref, q_ref, kp_ref, kc_ref, kn_ref, kx_ref, vp_ref, vc_ref, vn_ref, vx_ref, o_ref,
                *, nb, seq, ctx_len):
    n = pl.program_id(1)
    hd, grp, lb = SWA_HEAD_DIM, SWA_HEADS // SWA_KV_HEADS, SWA_BLOCK
    scale = hd ** -0.5
    m_rows, n_loc = grp * lb, 3 * lb
    r = lax.broadcasted_iota(jnp.int32, (m_rows, n_loc + ctx_len), 0)
    c = lax.broadcasted_iota(jnp.int32, (m_rows, n_loc + ctx_len), 1)
    qpos = n * lb + (r & (lb - 1))
    kpos = (n - 1) * lb + c
    local_ok = (jnp.abs(kpos - qpos) <= SWA_WINDOW) & (kpos >= 0) & (kpos < seq) & (n < nb)
    valid = local_ok | (c >= n_loc)
    outs = []
    for kv in range(SWA_KV_HEADS):
        ks = slice(kv * hd, (kv + 1) * hd)
        qg = jnp.concatenate([q_ref[:, (kv * grp + g) * hd:(kv * grp + g + 1) * hd] for g in range(grp)], axis=0)
        keys = jnp.concatenate([kp_ref[:, ks], kc_ref[:, ks], kn_ref[:, ks], kx_ref[:, ks]], axis=0)
        vals = jnp.concatenate([vp_ref[:, ks], vc_ref[:, ks], vn_ref[:, ks], vx_ref[:, ks]], axis=0)
        s = jnp.where(valid, _dot_nt(qg, keys) * scale, -1e30)
        for g in range(grp):
            sg = s[g * lb:(g + 1) * lb]
            snk = sink_ref[kv * grp + g]
            m = jnp.maximum(jnp.max(sg, axis=1, keepdims=True), snk)
            p = jnp.exp(sg - m)
            l = jnp.sum(p, axis=1, keepdims=True) + jnp.exp(snk - m)
            outs.append((_dot(p.astype(BF16), vals) / l).astype(BF16))
    o_ref[...] = jnp.concatenate(outs, axis=1)


def swa_attention(qkv, sink, *, rows):
    b, seq, ctx_len = rows.batch, rows.seq, rows.ctx_len
    lb = SWA_BLOCK
    nb, ncb = seq // lb, ctx_len // lb
    lat_blocks = rows.n_lat // lb
    qw = SWA_HEADS * SWA_HEAD_DIM
    kw = SWA_KV_HEADS * SWA_HEAD_DIM
    kcol, vcol = qw // kw, qw // kw + 1

    def qidx(bi, n):
        return (jnp.where(n < nb, bi * nb + n, lat_blocks + bi * ncb + (n - nb)), 0)

    def kidx(off, col):
        return lambda bi, n: (bi * nb + jnp.clip(n + off, 0, nb - 1), col)

    def xidx(col):
        return lambda bi, n: (rows.n_lat // ctx_len + bi, col)

    kern = functools.partial(_swa_kernel, nb=nb, seq=seq, ctx_len=ctx_len)
    return pl.pallas_call(
        kern,
        grid=(b, nb + ncb),
        in_specs=[
            pl.BlockSpec(memory_space=pltpu.SMEM),
            pl.BlockSpec((lb, qw), qidx),
            pl.BlockSpec((lb, kw), kidx(-1, kcol)),
            pl.BlockSpec((lb, kw), kidx(0, kcol)),
            pl.BlockSpec((lb, kw), kidx(1, kcol)),
            pl.BlockSpec((ctx_len, kw), xidx(kcol)),
            pl.BlockSpec((lb, kw), kidx(-1, vcol)),
            pl.BlockSpec((lb, kw), kidx(0, vcol)),
            pl.BlockSpec((lb, kw), kidx(1, vcol)),
            pl.BlockSpec((ctx_len, kw), xidx(vcol)),
        ],
        out_specs=pl.BlockSpec((lb, qw), qidx),
        out_shape=jax.ShapeDtypeStruct((rows.n_all, qw), BF16),
        compiler_params=_params(("parallel", "parallel")),
        name="swa_attention",
    )(sink, qkv, qkv, qkv, qkv, qkv, qkv, qkv, qkv, qkv)


def _mlstm_kernel(q_ref, k_ref, v_ref, gt_ref, gb_ref, o_ref, c_ref, n_ref, m_ref):
    d, c = pl.program_id(1), pl.program_id(2)
    L, H, dqk, dv = ML_CHUNK, ML_HEADS, ML_DQK, ML_DV
    scale = dqk ** -0.5

    @pl.when(c == 0)
    def _():
        c_ref[...] = jnp.zeros_like(c_ref)
        n_ref[...] = jnp.zeros_like(n_ref)
        m_ref[...] = jnp.zeros_like(m_ref)

    row = lax.broadcasted_iota(jnp.int32, (L, L), 0)
    col = lax.broadcasted_iota(jnp.int32, (L, L), 1)
    sign = 1 - 2 * d
    mask = (col - row) * sign <= 0
    mask_t = (row - col) * sign <= 0
    eye = row == col
    gates = gt_ref[...] + gb_ref[...]
    for h in range(H):
        qh = q_ref[:, h * dqk:(h + 1) * dqk]
        kh = k_ref[:, h * dqk:(h + 1) * dqk]
        vh = v_ref[:, h * dv:(h + 1) * dv]
        i_col = gates[:, h:h + 1]
        f_col = jax.nn.log_sigmoid(gates[:, H + h:H + h + 1])
        f_row = jnp.sum(jnp.where(eye, f_col, 0.0), axis=0, keepdims=True)
        i_row = jnp.sum(jnp.where(eye, i_col, 0.0), axis=0, keepdims=True)
        b_col = jnp.sum(jnp.where(mask, f_row, 0.0), axis=1, keepdims=True)
        b_row = jnp.sum(jnp.where(mask_t, f_col, 0.0), axis=0, keepdims=True)
        m_prev = m_ref[h]
        dmat = jnp.where(mask, b_col - b_row + i_row, -jnp.inf)
        inter = b_col + m_prev
        m_t = jnp.maximum(inter, jnp.max(dmat, axis=1, keepdims=True))
        w = jnp.exp(dmat - m_t)
        g = jnp.exp(inter - m_t) * scale
        s = _dot_nt(qh, kh) * (w * scale)
        c_prev = c_ref[h]
        n_prev = n_ref[h]
        num = _dot(s.astype(BF16), vh) + g * _dot_nt(qh, c_prev.astype(BF16))
        den = jnp.sum(s, axis=1, keepdims=True) + g * jnp.sum(qh.astype(F32) * n_prev, axis=1, keepdims=True)
        hout = num / jnp.maximum(jnp.abs(den), jnp.exp(-m_t))
        o_ref[0, :, h * dv:(h + 1) * dv] = hout.astype(o_ref.dtype)
        b_end = jnp.sum(f_col, axis=0, keepdims=True)
        dec = b_end - b_col + i_col
        m_new = jnp.maximum(b_end + m_prev, jnp.max(dec, axis=0, keepdims=True))
        ws = jnp.exp(dec - m_new)
        gs = jnp.exp(b_end + m_prev - m_new)
        c_ref[h] = gs * c_prev + _dot_tn((vh.astype(F32) * ws).astype(BF16), kh)
        n_ref[h] = gs * n_prev + jnp.sum(kh.astype(F32) * ws, axis=0, keepdims=True)
        m_ref[h] = m_new


def mlstm_scan(z, gates, gate_b, *, rows):
    b, seq, ctx_len = rows.batch, rows.seq, rows.ctx_len
    L = ML_CHUNK
    assert ctx_len == L and seq % L == 0
    nlc = seq // L
    hq, hv = ML_HEADS * ML_DQK, ML_HEADS * ML_DV

    def rb(bi, d, c):
        lat = bi * nlc + jnp.where(d == 0, c - 1, nlc - c)
        return jnp.where(c == 0, rows.n_lat // L + bi, lat)

    return pl.pallas_call(
        _mlstm_kernel,
        grid=(b, 2, nlc + 1),
        in_specs=[
            pl.BlockSpec((L, hq), lambda bi, d, c: (rb(bi, d, c), 0)),
            pl.BlockSpec((L, hq), lambda bi, d, c: (rb(bi, d, c), 1)),
            pl.BlockSpec((L, hv), lambda bi, d, c: (rb(bi, d, c), 1)),
            pl.BlockSpec((L, LANE), lambda bi, d, c: (rb(bi, d, c), d)),
            pl.BlockSpec((1, LANE), lambda bi, d, c: (0, d)),
        ],
        out_specs=pl.BlockSpec((1, L, hv), lambda bi, d, c: (d, rb(bi, d, c), 0)),
        out_shape=jax.ShapeDtypeStruct((2, rows.n_all, hv), BF16),
        scratch_shapes=[
            pltpu.VMEM((ML_HEADS, ML_DV, ML_DQK), F32),
            pltpu.VMEM((ML_HEADS, 1, ML_DQK), F32),
            pltpu.VMEM((ML_HEADS, 1, 1), F32),
        ],
        compiler_params=_params(("parallel", "parallel", "arbitrary")),
        name="mlstm_scan",
    )(z, z, z, gates, gate_b)


def _mla_kernel(q_ref, kvl_ref, kvx_ref, krl_ref, krx_ref, o_ref):
    scale = (MLA_NOPE + MLA_ROPE) ** -0.5
    qn, qr = q_ref[:, :LANE], q_ref[:, LANE:]
    s_l = (_dot_nt(qn, kvl_ref[:, :LANE]) + _dot_nt(qr, krl_ref[...])) * scale
    s_x = (_dot_nt(qn, kvx_ref[:, :LANE]) + _dot_nt(qr, krx_ref[...])) * scale
    m = jnp.maximum(jnp.max(s_l, axis=1, keepdims=True), jnp.max(s_x, axis=1, keepdims=True))
    p_l, p_x = jnp.exp(s_l - m), jnp.exp(s_x - m)
    l = jnp.sum(p_l, axis=1, keepdims=True) + jnp.sum(p_x, axis=1, keepdims=True)
    o = _dot(p_l.astype(BF16), kvl_ref[:, LANE:]) + _dot(p_x.astype(BF16), kvx_ref[:, LANE:])
    o_ref[...] = (o / l).astype(o_ref.dtype)


def mla_attention(q, kv, z, *, rows, tq=256):
    b, seq, ctx_len = rows.batch, rows.seq, rows.ctx_len
    nq = seq // tq
    kr_col = z.shape[1] // LANE - 1
    ctx0 = rows.n_lat // ctx_len
    return pl.pallas_call(
        _mla_kernel,
        grid=(b, MLA_HEADS, nq),
        in_specs=[
            pl.BlockSpec((tq, 2 * LANE), lambda bi, h, i: (bi * nq + i, h)),
            pl.BlockSpec((seq, 2 * LANE), lambda bi, h, i: (bi, h)),
            pl.BlockSpec((ctx_len, 2 * LANE), lambda bi, h, i: (ctx0 + bi, h)),
            pl.BlockSpec((seq, LANE), lambda bi, h, i: (bi, kr_col)),
            pl.BlockSpec((ctx_len, LANE), lambda bi, h, i: (ctx0 + bi, kr_col)),
        ],
        out_specs=pl.BlockSpec((tq, LANE), lambda bi, h, i: (bi * nq + i, h)),
        out_shape=jax.ShapeDtypeStruct((rows.n_lat, MLA_HEADS * MLA_V), BF16),
        compiler_params=_params(("parallel", "parallel", "parallel")),
        name="mla_attention",
    )(q, kv, kv, z, z)


def _hy_filter_kernel(ft_ref, w1_ref, b1_ref, w2_ref, b2_ref, w3_ref, b3_ref, fr_ref, w4_ref, b4_ref, dec_ref,
                      o_ref, a_ref, *, back_from):
    j = pl.program_id(0)
    hdot = functools.partial(jnp.dot, preferred_element_type=F32, precision=HIGHEST)

    @pl.when(j == 0)
    def _():
        fr = fr_ref[...]
        a = jnp.sin(fr * (hdot(ft_ref[...], w1_ref[...]) + b1_ref[...]))
        a = jnp.sin(fr * (hdot(a, w2_ref[...]) + b2_ref[...]))
        a_ref[...] = jnp.sin(fr * (hdot(a, w3_ref[...]) + b3_ref[...]))

    t = ft_ref[:, 0:1]
    filt = (hdot(a_ref[...], w4_ref[...]) + b4_ref[...]) * jnp.exp(-t * jnp.abs(dec_ref[...]))
    row = lax.broadcasted_iota(jnp.int32, filt.shape, 0)
    o_ref[...] = jnp.where(jnp.logical_and(row == 0, j >= back_from), 0.0, filt)


def hyena_filters(seq, w1, b1, w2, b2, w3, b3, w4, b4, freq, decay, tn=512):
    t = np.linspace(0.0, 1.0, seq)[:, None]
    w = (2.0 * math.pi / seq) * np.arange(seq)[:, None]
    bands = np.linspace(1e-4, HY_BANDS - 1, HY_BANDS)[None, :]
    feats = np.zeros((seq, LANE))
    feats[:, :1 + 2 * HY_BANDS] = np.concatenate([t, np.cos(bands * w), -np.sin(bands * w)], axis=-1)
    n = w4.shape[1]
    fw = HY_FILTER_W
    w1p = jnp.zeros((LANE, fw), F32).at[:w1.shape[0]].set(w1)
    row = lambda a: a.reshape(1, -1)
    full = lambda shape: pl.BlockSpec(shape, lambda j: (0, 0))
    return pl.pallas_call(
        functools.partial(_hy_filter_kernel, back_from=(n // 2) // tn),
        grid=(n // tn,),
        in_specs=[full((seq, LANE)), full((LANE, fw)), full((1, fw)), full((fw, fw)), full((1, fw)),
                  full((fw, fw)), full((1, fw)), full((1, fw)),
                  pl.BlockSpec((fw, tn), lambda j: (0, j)),
                  pl.BlockSpec((1, tn), lambda j: (0, j)),
                  pl.BlockSpec((1, tn), lambda j: (0, j))],
        out_specs=pl.BlockSpec((seq, tn), lambda j: (0, j)),
        out_shape=jax.ShapeDtypeStruct((seq, n), F32),
        scratch_shapes=[pltpu.VMEM((seq, fw), F32)],
        compiler_params=_params(("arbitrary",)),
        name="hyena_filters",
    )(jnp.asarray(feats, F32), w1p, row(b1), w2, row(b2), w3, row(b3), row(freq), w4, row(b4), row(decay))


def _dft_mats(seq, p):
    n = 2 * seq
    q = n // p
    k1n = q // 2 + 1
    hi, k1 = np.arange(q // 2), np.arange(k1n)
    th = 2 * np.pi * np.outer(k1, hi) / q
    ma = np.zeros((2 * k1n, q // 2))
    ma[0::2], ma[1::2] = np.cos(th), -np.sin(th)
    lo = np.arange(p)
    wf = np.zeros((k1n, 2 * p, 2 * p))
    wi = np.zeros((k1n, 2 * p, 2 * p))
    for k in k1:
        ph = -2 * np.pi * (np.outer(lo, lo) / p + k * lo[None, :] / n)
        er, ei = np.cos(ph), np.sin(ph)
        wf[k] = np.block([[er, -ei], [ei, er]])
        wi[k] = np.block([[er.T, ei.T], [-ei.T, er.T]])
    c = np.full(k1n, 2.0)
    c[0] = c[-1] = 1.0
    th2 = 2 * np.pi * np.outer(hi, k1) / q
    md = np.zeros((q // 2, 2 * k1n))
    md[:, 0::2], md[:, 1::2] = c * np.cos(th2) / n, -c * np.sin(th2) / n
    as_bf16 = lambda a: jnp.asarray(a, F32).astype(BF16)
    return ma, as_bf16(wf), as_bf16(wi), md


def _hy_short_kernel(x1_ref, x2_ref, v_ref, w_ref, b_ref, o1_ref, o2_ref, o3_ref):
    for k, (x_ref, o_ref) in enumerate(((x1_ref, o1_ref), (x2_ref, o2_ref), (v_ref, o3_ref))):
        x = x_ref[...].astype(F32)
        n = x.shape[0]
        row = lax.broadcasted_iota(jnp.int32, x.shape, 0)
        prev = jnp.where(row == 0, 0.0, pltpu.roll(x, 1, 0))
        nxt = jnp.where(row == n - 1, 0.0, pltpu.roll(x, n - 1, 0))
        w = w_ref[k]
        o_ref[...] = (prev * w[0:1] + x * w[1:2] + nxt * w[2:3] + b_ref[k]).astype(o_ref.dtype)


def hyena_short_conv(x3, conv_w, conv_b, *, rows, cb=256):
    b, seq = rows.batch, rows.seq
    d = x3.shape[1] // 3
    nblk = d // cb
    w = jnp.transpose(conv_w.reshape(HY_SHORT, 3, d), (1, 0, 2))
    bias = conv_b.reshape(3, 1, d)
    in_x = [pl.BlockSpec((seq, cb), lambda bi, j, k=k: (bi, k * nblk + j)) for k in range(3)]
    out = pl.BlockSpec((seq, cb), lambda bi, j: (bi, j))
    shape = jax.ShapeDtypeStruct((rows.n_lat, d), BF16)
    return pl.pallas_call(
        _hy_short_kernel,
        grid=(b, nblk),
        in_specs=in_x + [pl.BlockSpec((3, HY_SHORT, cb), lambda bi, j: (0, 0, j)),
                         pl.BlockSpec((3, 1, cb), lambda bi, j: (0, 0, j))],
        out_specs=[out, out, out],
        out_shape=[shape, shape, shape],
        compiler_params=_params(("parallel", "parallel")),
        name="hyena_short_conv",
    )(x3, x3, x3, w, bias)


def _hy_stage_a_kernel(x_ref, m_ref, o_ref):
    qh, g, wc = x_ref.shape
    x = x_ref[...].reshape(qh * g, wc).astype(BF16)
    o_ref[...] = _dot(m_ref[...], x).reshape(o_ref.shape).astype(o_ref.dtype)


def hyena_stage_a(x, mak, *, p, width=2048):
    bx, seq, c = x.shape
    g = HY_G
    qh = seq // p
    mo = mak.shape[0] // g
    x5 = x.reshape(bx, qh, p // g, g, c)
    return pl.pallas_call(
        _hy_stage_a_kernel,
        grid=(bx, p // g, c // width),
        in_specs=[pl.BlockSpec((None, qh, None, g, width), lambda bi, l, j: (bi, 0, l, 0, j)),
                  pl.BlockSpec(mak.shape, lambda bi, l, j: (0, 0))],
        out_specs=pl.BlockSpec((None, mo, g, width), lambda bi, l, j: (bi, 0, l, j)),
        out_shape=jax.ShapeDtypeStruct((bx, mo, p, c), BF16),
        compiler_params=_params(("parallel", "parallel", "parallel")),
        name="hyena_stage_a",
    )(x5, mak)


def _hy_spec_kernel(a0_ref, a1_ref, wf_ref, o_ref):
    p = o_ref.shape[2]
    cb = o_ref.shape[3]
    wf = wf_ref[0]
    x0 = _dot(wf, a0_ref[0].reshape(2 * p, cb))
    x1 = _dot(wf, a1_ref[0].reshape(2 * p, cb))
    o_ref[0, 0] = x0[:p] + x1[:p]
    o_ref[0, 1] = x0[p:] - x1[p:]


def hyena_spectrum(af, wf, *, d, cb=512):
    k1n, _, p, _ = af.shape
    nblk = (HY_ORDER * d) // cb
    return pl.pallas_call(
        _hy_spec_kernel,
        grid=(nblk, k1n),
        in_specs=[pl.BlockSpec((1, 2, p, cb), lambda j, k: (k, 0, 0, j)),
                  pl.BlockSpec((1, 2, p, cb), lambda j, k: (k, 0, 0, nblk + j)),
                  pl.BlockSpec((1, 2 * p, 2 * p), lambda j, k: (k, 0, 0))],
        out_specs=pl.BlockSpec((1, 2, p, cb), lambda j, k: (k, 0, 0, j)),
        out_shape=jax.ShapeDtypeStruct((k1n, 2, p, HY_ORDER * d), F32),
        compiler_params=_params(("parallel", "parallel")),
        name="hyena_spectrum",
    )(af, af, wf)


def _hy_mid_kernel(a_ref, wf_ref, wi_ref, h_ref, o_ref):
    p, cb = a_ref.shape[3], a_ref.shape[4]
    x = _dot(wf_ref[0], a_ref[0, 0].reshape(2 * p, cb))
    xr, xi = x[:p], x[p:]
    hr, hi = h_ref[0, 0], h_ref[0, 1]
    y = jnp.concatenate([xr * hr - xi * hi, xr * hi + xi * hr], axis=0).astype(BF16)
    o_ref[0, 0] = _dot(wi_ref[0], y).reshape(2, p, cb).astype(o_ref.dtype)


def hyena_mid(a, wf, wi, spec, *, order, d, cb=512):
    b, k1n, _, p, _ = a.shape
    nblk = d // cb
    return pl.pallas_call(
        _hy_mid_kernel,
        grid=(nblk, k1n, b),
        in_specs=[pl.BlockSpec((1, 1, 2, p, cb), lambda j, k, bi: (bi, k, 0, 0, j)),
                  pl.BlockSpec((1, 2 * p, 2 * p), lambda j, k, bi: (k, 0, 0)),
                  pl.BlockSpec((1, 2 * p, 2 * p), lambda j, k, bi: (k, 0, 0)),
                  pl.BlockSpec((1, 2, p, cb), lambda j, k, bi: (k, 0, 0, order * nblk + j))],
        out_specs=pl.BlockSpec((1, 1, 2, p, cb), lambda j, k, bi: (bi, k, 0, 0, j)),
        out_shape=jax.ShapeDtypeStruct(a.shape, BF16),
        compiler_params=_params(("parallel", "parallel", "parallel")),
        name="hyena_mid",
    )(a, wf, wi, spec)


def _hy_stage_d_kernel(c_ref, m_ref, x_ref, z_ref, skip_ref, o_ref):
    mo, g, wc = c_ref.shape
    qh = x_ref.shape[0]
    conv = _dot(m_ref[...], c_ref[...].reshape(mo * g, wc))
    z = z_ref[...].reshape(qh * g, wc).astype(F32)
    x = x_ref[...].reshape(qh * g, wc).astype(F32)
    o_ref[...] = (x * (conv + z * skip_ref[...])).reshape(o_ref.shape).astype(o_ref.dtype)


def hyena_stage_d(c, mdk, xg, z, skip, *, p, width=2048):
    b, mo, _, d = c.shape
    g = HY_G
    seq = xg.shape[1]
    qh = seq // p
    v5 = lambda a: a.reshape(b, qh, p // g, g, d)
    tspec = pl.BlockSpec((None, qh, None, g, width), lambda bi, l, j: (bi, 0, l, 0, j))
    out = pl.pallas_call(
        _hy_stage_d_kernel,
        grid=(b, p // g, d // width),
        in_specs=[pl.BlockSpec((None, mo, g, width), lambda bi, l, j: (bi, 0, l, j)),
                  pl.BlockSpec(mdk.shape, lambda bi, l, j: (0, 0)),
                  tspec, tspec,
                  pl.BlockSpec((1, width), lambda bi, l, j: (0, j))],
        out_specs=tspec,
        out_shape=jax.ShapeDtypeStruct((b, qh, p // g, g, d), BF16),
        compiler_params=_params(("parallel", "parallel", "parallel")),
        name="hyena_stage_d",
    )(c, mdk, v5(xg), v5(z), skip.reshape(1, d))
    return out.reshape(b, seq, d)


def hyena_long_convs(x1, x2, v, filt, skip, *, rows):
    b, seq = rows.batch, rows.seq
    d = x1.shape[1]
    p = HY_P
    k1n = seq // p + 1
    ma, wf, wi, md = _dft_mats(seq, p)
    eye = np.eye(HY_G)
    as_bf16 = lambda a: jnp.asarray(a, F32).astype(BF16)
    mak, mdk = as_bf16(np.kron(ma, eye)), as_bf16(np.kron(md, eye))
    af = hyena_stage_a(filt.reshape(1, seq, filt.shape[1]), mak, p=p)
    spec = hyena_spectrum(af.reshape(k1n, 2, p, filt.shape[1]), wf, d=d)
    t3 = lambda a: a.reshape(b, seq, d)
    y = t3(v)
    for order, xg in enumerate((x1, x2)):
        a = hyena_stage_a(y, mak, p=p).reshape(b, k1n, 2, p, d)
        c = hyena_mid(a, wf, wi, spec, order=order, d=d)
        y = hyena_stage_d(c.reshape(b, 2 * k1n, p, d), mdk, t3(xg), y, skip[order], p=p)
    return y.reshape(rows.n_lat, d)


def _mlstm_gate_weights(w_in, gate_b):
    h = ML_HEADS
    o4 = 2 * h * ML_DQK + 2 * h * ML_DV
    wg = w_in[:, o4:].reshape(-1, 2, 2, h)
    bg = gate_b.reshape(2, 2, h)
    w_out = jnp.zeros((w_in.shape[0], 2 * LANE), F32)
    b_out = jnp.zeros((1, 2 * LANE), F32)
    for d in range(2):
        for gate in range(2):
            lo = d * LANE + gate * h
            w_out = w_out.at[:, lo:lo + h].set(wg[:, gate, d])
            b_out = b_out.at[0, lo:lo + h].set(bg[gate, d])
    return w_out, b_out


def _mla_weights(w_in, w_uq):
    r2 = MLA_Q_RANK + MLA_KV_RANK
    half = MLA_ROPE // 2

    def spread(w):
        z = jnp.zeros(w.shape[:-1] + (half,), w.dtype)
        return jnp.concatenate([w[..., :half], z, w[..., half:], z], axis=-1)

    w_in_p = jnp.concatenate([w_in[:, :r2], spread(w_in[:, r2:])], axis=1)
    wq = w_uq.reshape(w_uq.shape[0], MLA_HEADS, MLA_NOPE + MLA_ROPE)
    wq_p = jnp.concatenate([wq[..., :MLA_NOPE], spread(wq[..., MLA_NOPE:])], axis=-1)
    return w_in_p, wq_p.reshape(w_uq.shape[0], MLA_HEADS * 2 * LANE)


def kernel(x, c, ctx, c_ctx, ada_w, ada_b, norm_g, mlp_w1, mlp_w2, swa_w_qkv, swa_sink, swa_w_o, ml_w_in, ml_gate_b, ml_head_g, ml_w_o, mla_w_in, mla_q_g, mla_kv_g, mla_w_uq, mla_w_ukv, mla_w_o, hy_w_in, hy_conv_w, hy_conv_b, hy_f_w1, hy_f_b1, hy_f_w2, hy_f_b2, hy_f_w3, hy_f_b3, hy_f_w4, hy_f_b4, hy_f_freq, hy_decay, hy_skip, hy_w_o):
    b, seq, d = x.shape
    ctx_len = ctx.shape[1]
    depth = ada_w.shape[0]
    assert depth == 4 and b < 8
    rows = Rows(b, seq, ctx_len)
    tm = 512
    bf = lambda w: w.astype(BF16)

    cond = jnp.zeros((8, d), F32).at[:b].set(c).at[b].set(c_ctx)
    mods = ada_mods(cond, ada_w, ada_b).reshape(depth * 8, 1, N_MOD * d)
    h = jnp.concatenate([x.reshape(rows.n_lat, d), ctx.reshape(rows.n_ctx, d)], axis=0)

    common = dict(rows=rows, tm=tm, mods=mods)

    def finish(h, o_args, w_o, layer, n_rows, mode="plain"):
        h = outproj_residual(o_args, bf(w_o), h, mods, norm_g[layer, 1], rows=rows, n_rows=n_rows, tm=tm,
                             layer=layer, mode=mode, name=f"outproj{layer}")
        return mlp_residual(h, bf(mlp_w1[layer]), bf(mlp_w2[layer]), mods, norm_g[layer, 2], norm_g[layer, 3],
                            rows=rows, n_rows=n_rows, tm=tm, tf=512, layer=layer)

    n_qk_groups = SWA_HEADS + SWA_KV_HEADS
    qkv = linear(h, bf(swa_w_qkv[0]), n_rows=rows.n_all, tn=512, out_dtype=BF16, prologue="norm_mod",
                 gain=norm_g[0, 0], layer=0, rope_tabs=_rope_tables(seq, SWA_HEAD_DIM), rope_pattern=(True,) * 4,
                 rope_jmax=n_qk_groups // 4, name="swa_qkv", **common)
    o = swa_attention(qkv, swa_sink[0], rows=rows)
    h = finish(h, o, swa_w_o[0], 0, rows.n_all)

    w_main = ml_w_in[0][:, :2 * ML_HEADS * ML_DQK + 2 * ML_HEADS * ML_DV]
    w_gate, b_gate = _mlstm_gate_weights(ml_w_in[0], ml_gate_b[0])
    z = linear(h, bf(w_main), n_rows=rows.n_all, tn=1024, out_dtype=BF16, prologue="norm_mod",
               gain=norm_g[1, 0], layer=1, name="mlstm_in", **common)
    gates = linear(h, bf(w_gate), n_rows=rows.n_all, tn=2 * LANE, out_dtype=F32, prologue="norm_mod",
                   gain=norm_g[1, 0], layer=1, name="mlstm_gates", **common)
    hsum = mlstm_scan(z, gates, b_gate, rows=rows)
    h = finish(h, (hsum, z, ml_head_g[0]), ml_w_o[0], 1, rows.n_all, mode="mlstm")

    w_in_p, w_uq_p = _mla_weights(mla_w_in[0], mla_w_uq[0])
    rope_mla = _rope_tables(seq, MLA_ROPE)
    zc = linear(h, bf(w_in_p), n_rows=rows.n_all, tn=w_in_p.shape[1], out_dtype=BF16, prologue="norm_mod",
                gain=norm_g[2, 0], layer=2, rope_tabs=rope_mla, rope_pattern=(False,) * 8 + (True,),
                name="mla_in", **common)
    q = linear(zc, bf(w_uq_p), n_rows=rows.n_lat, tn=512, out_dtype=BF16, x_cols=0, prologue="norm",
               gain=mla_q_g[0], rope_tabs=rope_mla, rope_pattern=(False, True) * 2, name="mla_q",
               rows=rows, tm=tm)
    kv = linear(zc, bf(mla_w_ukv[0]), n_rows=rows.n_all, tn=1024, out_dtype=BF16, x_cols=1, prologue="norm",
                gain=mla_kv_g[0], name="mla_kv", rows=rows, tm=tm)
    o = mla_attention(q, kv, zc, rows=rows)
    h = finish(h, o, mla_w_o[0], 2, rows.n_lat)

    x3 = linear(h, bf(hy_w_in[0]), n_rows=rows.n_lat, tn=1024, out_dtype=BF16, prologue="norm_mod",
                gain=norm_g[3, 0], layer=3, name="hyena_in", **common)
    x1, x2, v = hyena_short_conv(x3, hy_conv_w[0], hy_conv_b[0], rows=rows)
    filt = hyena_filters(seq, hy_f_w1[0], hy_f_b1[0], hy_f_w2[0], hy_f_b2[0], hy_f_w3[0], hy_f_b3[0],
                         hy_f_w4[0], hy_f_b4[0], hy_f_freq[0], hy_decay[0])
    y = hyena_long_convs(x1, x2, v, filt, hy_skip[0], rows=rows)
    h = finish(h, y, hy_w_o[0], 3, rows.n_lat)
    return h.reshape(b, seq, d)
```

```python
import functools
import math

import numpy as np
import jax
import jax.numpy as jnp
from jax import lax
from jax.experimental import pallas as pl
from jax.experimental.pallas import tpu as pltpu

F32, BF16 = jnp.float32, jnp.bfloat16
HIGHEST = lax.Precision.HIGHEST

RMS_EPS = 1e-6
ROPE_THETA = 10000.0
GRID_W = 64
N_MOD = 6
LANE = 128

SWA_HEADS, SWA_KV_HEADS, SWA_HEAD_DIM, SWA_WINDOW, SWA_BLOCK = 16, 4, 128, 128, 128
ML_HEADS, ML_DQK, ML_DV = 8, 128, 256
ML_CHUNK = 256
MLA_HEADS, MLA_Q_RANK, MLA_KV_RANK, MLA_NOPE, MLA_ROPE, MLA_V = 16, 512, 512, 128, 64, 128
HY_ORDER, HY_BANDS, HY_FILTER_W, HY_SHORT = 2, 16, 64, 3
HY_P = 256
HY_G = 16

VMEM_LIMIT = 48 * 1024 * 1024


def _params(sem):
    return pltpu.CompilerParams(dimension_semantics=sem, vmem_limit_bytes=VMEM_LIMIT)


def _dot(a, b):
    return jnp.dot(a, b, preferred_element_type=F32)


def _dot_nt(a, b):
    return lax.dot_general(a, b, (((1,), (1,)), ((), ())), preferred_element_type=F32)


def _dot_tn(a, b):
    return lax.dot_general(a, b, (((0,), (0,)), ((), ())), preferred_element_type=F32)


def _rms(x, g):
    return x * lax.rsqrt(jnp.mean(x * x, axis=-1, keepdims=True) + RMS_EPS) * g


def _ada_kernel(s_ref, w_ref, b_ref, o_ref):
    s = s_ref[...]
    s = s * jax.nn.sigmoid(s)
    o_ref[0] = jnp.dot(s, w_ref[0], preferred_element_type=F32, precision=HIGHEST) + b_ref[0]


def ada_mods(cond, ada_w, ada_b, tn=1536):
    depth, d, n = ada_w.shape
    rows = cond.shape[0]
    return pl.pallas_call(
        _ada_kernel,
        grid=(depth, n // tn),
        in_specs=[
            pl.BlockSpec((rows, d), lambda l, j: (0, 0)),
            pl.BlockSpec((1, d, tn), lambda l, j: (l, 0, j)),
            pl.BlockSpec((1, 1, tn), lambda l, j: (l, 0, j)),
        ],
        out_specs=pl.BlockSpec((1, rows, tn), lambda l, j: (l, 0, j)),
        out_shape=jax.ShapeDtypeStruct((depth, rows, n), F32),
        compiler_params=_params(("parallel", "parallel")),
        name="ada_mods",
    )(cond, ada_w, ada_b.reshape(depth, 1, n))


class Rows:
    def __init__(self, batch, seq, ctx_len):
        self.batch, self.seq, self.ctx_len = batch, seq, ctx_len
        self.n_lat = batch * seq
        self.n_ctx = batch * ctx_len
        self.n_all = self.n_lat + self.n_ctx

    def mod_index(self, layer, k, tm):
        lat_blocks, per_batch = self.n_lat // tm, self.seq // tm

        def index(i, *_):
            b = jnp.where(i < lat_blocks, i // per_batch, self.batch)
            return (layer * 8 + b, 0, k)

        return index


def _linear_kernel(*refs, prologue, rope, n_lat_blocks, rope_pattern, rope_jmax):
    it = iter(refs)
    x_ref = next(it)
    g_ref = next(it) if prologue in ("norm", "norm_mod") else None
    sh_ref = next(it) if prologue == "norm_mod" else None
    sc_ref = next(it) if prologue == "norm_mod" else None
    w_ref = next(it)
    cc_ref = next(it) if rope else None
    ss_ref = next(it) if rope else None
    o_ref = next(it)
    u_ref = next(it)
    i, j = pl.program_id(0), pl.program_id(1)

    @pl.when(j == 0)
    def _():
        x = x_ref[...].astype(F32)
        if prologue in ("norm", "norm_mod"):
            x = _rms(x, g_ref[...])
        if prologue == "norm_mod":
            x = x * (1.0 + sc_ref[0]) + sh_ref[0]
        u_ref[...] = x.astype(BF16)

    acc = _dot(u_ref[...], w_ref[...])

    def plain():
        o_ref[...] = acc.astype(o_ref.dtype)

    def roped():
        cc, ss = cc_ref[...], ss_ref[...]
        segs = []
        for gi, on in enumerate(rope_pattern):
            seg = acc[:, gi * LANE:(gi + 1) * LANE]
            if on:
                seg = seg * cc + pltpu.roll(seg, LANE // 2, 1) * ss
            segs.append(seg)
        o_ref[...] = jnp.concatenate(segs, axis=1).astype(o_ref.dtype)

    if not rope:
        plain()
    else:
        cond = jnp.logical_and(i < n_lat_blocks, j < rope_jmax)
        pl.when(cond)(roped)
        pl.when(jnp.logical_not(cond))(plain)


def linear(x, w, *, rows, n_rows, tm, tn, out_dtype, x_cols=None, prologue="none", gain=None,
           mods=None, layer=0, mod_k=(0, 1), rope_tabs=None, rope_pattern=(), rope_jmax=1 << 30,
           n_out=None, name="linear"):
    k, n = w.shape
    n = n if n_out is None else n_out
    xc = 0 if x_cols is None else x_cols
    rope = rope_tabs is not None
    in_specs = [pl.BlockSpec((tm, k), lambda i, j: (i, xc))]
    args = [x]
    if prologue in ("norm", "norm_mod"):
        in_specs.append(pl.BlockSpec((1, k), lambda i, j: (0, 0)))
        args.append(gain.reshape(1, k))
    if prologue == "norm_mod":
        for mk in mod_k:
            in_specs.append(pl.BlockSpec((1, 1, k), rows.mod_index(layer, mk, tm)))
            args.append(mods)
    in_specs.append(pl.BlockSpec((k, tn), lambda i, j: (0, j)))
    args.append(w)
    if rope:
        per_seq = rows.seq // tm
        for t in rope_tabs:
            in_specs.append(pl.BlockSpec((tm, LANE), lambda i, j: (i % per_seq, 0)))
            args.append(t)
    kern = functools.partial(_linear_kernel, prologue=prologue, rope=rope,
                             n_lat_blocks=rows.n_lat // tm, rope_pattern=tuple(rope_pattern),
                             rope_jmax=rope_jmax)
    return pl.pallas_call(
        kern,
        grid=(n_rows // tm, n // tn),
        in_specs=in_specs,
        out_specs=pl.BlockSpec((tm, tn), lambda i, j: (i, j)),
        out_shape=jax.ShapeDtypeStruct((n_rows, n), out_dtype),
        scratch_shapes=[pltpu.VMEM((tm, k), BF16)],
        compiler_params=_params(("parallel", "arbitrary")),
        name=name,
    )(*args)


def _outproj_kernel(*refs, mode):
    it = iter(refs)
    if mode == "mlstm":
        hf_ref, hb_ref, og_ref, hg_ref = next(it), next(it), next(it), next(it)
    else:
        o_ref_in = next(it)
    w_ref, h_ref, gate_ref, g_ref, out_ref = next(it), next(it), next(it), next(it), next(it)
    if mode == "mlstm":
        hs = hf_ref[0].astype(F32) + hb_ref[0].astype(F32)
        og = jax.nn.sigmoid(og_ref[...].astype(F32))
        hg = hg_ref[...]
        parts = []
        for h in range(ML_HEADS):
            sl = slice(h * ML_DV, (h + 1) * ML_DV)
            parts.append((_rms(hs[:, sl], hg[:, sl]) * og[:, sl]).astype(BF16))
        o = jnp.concatenate(parts, axis=1)
    else:
        o = o_ref_in[...]
    y = _dot(o, w_ref[...])
    out_ref[...] = h_ref[...] + gate_ref[0] * _rms(y, g_ref[...])


def outproj_residual(o_args, w, h, mods, gain, *, rows, n_rows, tm, layer, mode="plain", name="outproj"):
    k, d = w.shape
    if mode == "mlstm":
        hsum, z, head_g = o_args
        in_specs = [
            pl.BlockSpec((1, tm, k), lambda i: (0, i, 0)),
            pl.BlockSpec((1, tm, k), lambda i: (1, i, 0)),
            pl.BlockSpec((tm, k), lambda i: (i, 2)),
            pl.BlockSpec((1, k), lambda i: (0, 0)),
        ]
        args = [hsum, hsum, z, head_g.reshape(1, k)]
    else:
        in_specs = [pl.BlockSpec((tm, k), lambda i: (i, 0))]
        args = [o_args]
    in_specs += [
        pl.BlockSpec((k, d), lambda i: (0, 0)),
        pl.BlockSpec((tm, d), lambda i: (i, 0)),
        pl.BlockSpec((1, 1, d), rows.mod_index(layer, 2, tm)),
        pl.BlockSpec((1, d), lambda i: (0, 0)),
    ]
    args += [w, h, mods, gain.reshape(1, d)]
    return pl.pallas_call(
        functools.partial(_outproj_kernel, mode=mode),
        grid=(n_rows // tm,),
        in_specs=in_specs,
        out_specs=pl.BlockSpec((tm, d), lambda i: (i, 0)),
        out_shape=jax.ShapeDtypeStruct((n_rows, d), F32),
        compiler_params=_params(("parallel",)),
        name=name,
    )(*args)


def _mlp_kernel(h_ref, g2_ref, sh_ref, sc_ref, w1_ref, w2_ref, gate_ref, g3_ref, out_ref, v_ref, acc_ref):
    f = pl.program_id(1)

    @pl.when(f == 0)
    def _():
        v = _rms(h_ref[...], g2_ref[...]) * (1.0 + sc_ref[0]) + sh_ref[0]
        v_ref[...] = v.astype(BF16)
        acc_ref[...] = jnp.zeros_like(acc_ref)

    a = jnp.maximum(_dot(v_ref[...], w1_ref[...]), 0.0)
    acc_ref[...] += _dot((a * a).astype(BF16), w2_ref[...])

    @pl.when(f == pl.num_programs(1) - 1)
    def _():
        out_ref[...] = h_ref[...] + gate_ref[0] * _rms(acc_ref[...], g3_ref[...])


def mlp_residual(h, w1, w2, mods, g2, g3, *, rows, n_rows, tm, tf, layer):
    _, d, ff = w1.shape
    return pl.pallas_call(
        _mlp_kernel,
        grid=(n_rows // tm, ff // tf),
        in_specs=[
            pl.BlockSpec((tm, d), lambda i, f: (i, 0)),
            pl.BlockSpec((1, d), lambda i, f: (0, 0)),
            pl.BlockSpec((1, 1, d), rows.mod_index(layer, 3, tm)),
            pl.BlockSpec((1, 1, d), rows.mod_index(layer, 4, tm)),
            pl.BlockSpec((None, d, tf), lambda i, f: (layer, 0, f)),
            pl.BlockSpec((None, tf, d), lambda i, f: (layer, f, 0)),
            pl.BlockSpec((1, 1, d), rows.mod_index(layer, 5, tm)),
            pl.BlockSpec((1, d), lambda i, f: (0, 0)),
        ],
        out_specs=pl.BlockSpec((tm, d), lambda i, f: (i, 0)),
        out_shape=jax.ShapeDtypeStruct((n_rows, d), F32),
        scratch_shapes=[pltpu.VMEM((tm, d), BF16), pltpu.VMEM((tm, d), F32)],
        compiler_params=_params(("parallel", "arbitrary")),
        name="mlp",
    )(h, g2.reshape(1, d), mods, mods, w1, w2, mods, g3.reshape(1, d))


def _rope_tables(seq, d_rot):
    pos = np.arange(seq)
    row, col = pos // GRID_W, pos % GRID_W
    n = d_rot // 4
    inv = ROPE_THETA ** (-np.arange(n, dtype=np.float64) / n)
    ang = np.concatenate([row[:, None] * inv, col[:, None] * inv], axis=-1)
    cos, sin = np.cos(ang), np.sin(ang)
    half = d_rot // 2
    cc = np.zeros((seq, LANE))
    ss = np.zeros((seq, LANE))
    cc[:, :half] = cos
    cc[:, LANE // 2:LANE // 2 + half] = cos
    ss[:, :half] = -sin
    ss[:, LANE // 2:LANE // 2 + half] = sin
    return jnp.asarray(cc, F32), jnp.asarray(ss, F32)


def _swa_kernel(sink_ref, q_ref, kp_ref, kc_ref, kn_ref, kx_ref, vp_ref, vc_ref, vn_ref, vx_ref, o_ref,
                *, nb, seq, ctx_len):
    n = pl.program_id(1)
    hd, grp, lb = SWA_HEAD_DIM, SWA_HEADS // SWA_KV_HEADS, SWA_BLOCK
    scale = hd ** -0.5
    m_rows, n_loc = grp * lb, 3 * lb
    r = lax.broadcasted_iota(jnp.int32, (m_rows, n_loc + ctx_len), 0)
    c = lax.broadcasted_iota(jnp.int32, (m_rows, n_loc + ctx_len), 1)
    qpos = n * lb + (r & (lb - 1))
    kpos = (n - 1) * lb + c
    local_ok = (jnp.abs(kpos - qpos) <= SWA_WINDOW) & (kpos >= 0) & (kpos < seq) & (n < nb)
    valid = local_ok | (c >= n_loc)
    outs = []
    for kv in range(SWA_KV_HEADS):
        ks = slice(kv * hd, (kv + 1) * hd)
        qg = jnp.concatenate([q_ref[:, (kv * grp + g) * hd:(kv * grp + g + 1) * hd] for g in range(grp)], axis=0)
        keys = jnp.concatenate([kp_ref[:, ks], kc_ref[:, ks], kn_ref[:, ks], kx_ref[:, ks]], axis=0)
        vals = jnp.concatenate([vp_ref[:, ks], vc_ref[:, ks], vn_ref[:, ks], vx_ref[:, ks]], axis=0)
        s = jnp.where(valid, _dot_nt(qg, keys) * scale, -1e30)
        for g in range(grp):
            sg = s[g * lb:(g + 1) * lb]
            snk = sink_ref[kv * grp + g]
            m = jnp.maximum(jnp.max(sg, axis=1, keepdims=True), snk)
            p = jnp.exp(sg - m)
            l = jnp.sum(p, axis=1, keepdims=True) + jnp.exp(snk - m)
            outs.append((_dot(p.astype(BF16), vals) / l).astype(BF16))
    o_ref[...] = jnp.concatenate(outs, axis=1)


def swa_attention(qkv, sink, *, rows):
    b, seq, ctx_len = rows.batch, rows.seq, rows.ctx_len
    lb = SWA_BLOCK
    nb, ncb = seq // lb, ctx_len // lb
    lat_blocks = rows.n_lat // lb
    qw = SWA_HEADS * SWA_HEAD_DIM
    kw = SWA_KV_HEADS * SWA_HEAD_DIM
    kcol, vcol = qw // kw, qw // kw + 1

    def qidx(bi, n):
        return (jnp.where(n < nb, bi * nb + n, lat_blocks + bi * ncb + (n - nb)), 0)

    def kidx(off, col):
        return lambda bi, n: (bi * nb + jnp.clip(n + off, 0, nb - 1), col)

    def xidx(col):
        return lambda bi, n: (rows.n_lat // ctx_len + bi, col)

    kern = functools.partial(_swa_kernel, nb=nb, seq=seq, ctx_len=ctx_len)
    return pl.pallas_call(
        kern,
        grid=(b, nb + ncb),
        in_specs=[
            pl.BlockSpec(memory_space=pltpu.SMEM),
            pl.BlockSpec((lb, qw), qidx),
            pl.BlockSpec((lb, kw), kidx(-1, kcol)),
            pl.BlockSpec((lb, kw), kidx(0, kcol)),
            pl.BlockSpec((lb, kw), kidx(1, kcol)),
            pl.BlockSpec((ctx_len, kw), xidx(kcol)),
            pl.BlockSpec((lb, kw), kidx(-1, vcol)),
            pl.BlockSpec((lb, kw), kidx(0, vcol)),
            pl.BlockSpec((lb, kw), kidx(1, vcol)),
            pl.BlockSpec((ctx_len, kw), xidx(vcol)),
        ],
        out_specs=pl.BlockSpec((lb, qw), qidx),
        out_shape=jax.ShapeDtypeStruct((rows.n_all, qw), BF16),
        compiler_params=_params(("parallel", "parallel")),
        name="swa_attention",
    )(sink, qkv, qkv, qkv, qkv, qkv, qkv, qkv, qkv, qkv)


def _mlstm_kernel(q_ref, k_ref, v_ref, gt_ref, gb_ref, o_ref, c_ref, n_ref, m_ref):
    d, c = pl.program_id(1), pl.program_id(2)
    L, H, dqk, dv = ML_CHUNK, ML_HEADS, ML_DQK, ML_DV
    scale = dqk ** -0.5

    @pl.when(c == 0)
    def _():
        c_ref[...] = jnp.zeros_like(c_ref)
        n_ref[...] = jnp.zeros_like(n_ref)
        m_ref[...] = jnp.zeros_like(m_ref)

    row = lax.broadcasted_iota(jnp.int32, (L, L), 0)
    col = lax.broadcasted_iota(jnp.int32, (L, L), 1)
    sign = 1 - 2 * d
    mask = (col - row) * sign <= 0
    mask_t = (row - col) * sign <= 0
    eye = row == col
    gates = gt_ref[...] + gb_ref[...]
    for h in range(H):
        qh = q_ref[:, h * dqk:(h + 1) * dqk]
        kh = k_ref[:, h * dqk:(h + 1) * dqk]
        vh = v_ref[:, h * dv:(h + 1) * dv]
        i_col = gates[:, h:h + 1]
        f_col = jax.nn.log_sigmoid(gates[:, H + h:H + h + 1])
        f_row = jnp.sum(jnp.where(eye, f_col, 0.0), axis=0, keepdims=True)
        i_row = jnp.sum(jnp.where(eye, i_col, 0.0), axis=0, keepdims=True)
        b_col = jnp.sum(jnp.where(mask, f_row, 0.0), axis=1, keepdims=True)
        b_row = jnp.sum(jnp.where(mask_t, f_col, 0.0), axis=0, keepdims=True)
        m_prev = m_ref[h]
        dmat = jnp.where(mask, b_col - b_row + i_row, -jnp.inf)
        inter = b_col + m_prev
        m_t = jnp.maximum(inter, jnp.max(dmat, axis=1, keepdims=True))
        w = jnp.exp(dmat - m_t)
        g = jnp.exp(inter - m_t) * scale
        s = _dot_nt(qh, kh) * (w * scale)
        c_prev = c_ref[h]
        n_prev = n_ref[h]
        num = _dot(s.astype(BF16), vh) + g * _dot_nt(qh, c_prev.astype(BF16))
        den = jnp.sum(s, axis=1, keepdims=True) + g * jnp.sum(qh.astype(F32) * n_prev, axis=1, keepdims=True)
        hout = num / jnp.maximum(jnp.abs(den), jnp.exp(-m_t))
        o_ref[0, :, h * dv:(h + 1) * dv] = hout.astype(o_ref.dtype)
        b_end = jnp.sum(f_col, axis=0, keepdims=True)
        dec = b_end - b_col + i_col
        m_new = jnp.maximum(b_end + m_prev, jnp.max(dec, axis=0, keepdims=True))
        ws = jnp.exp(dec - m_new)
        gs = jnp.exp(b_end + m_prev - m_new)
        c_ref[h] = gs * c_prev + _dot_tn((vh.astype(F32) * ws).astype(BF16), kh)
        n_ref[h] = gs * n_prev + jnp.sum(kh.astype(F32) * ws, axis=0, keepdims=True)
        m_ref[h] = m_new


def mlstm_scan(z, gates, gate_b, *, rows):
    b, seq, ctx_len = rows.batch, rows.seq, rows.ctx_len
    L = ML_CHUNK
    assert ctx_len == L and seq % L == 0
    nlc = seq // L
    hq, hv = ML_HEADS * ML_DQK, ML_HEADS * ML_DV

    def rb(bi, d, c):
        lat = bi * nlc + jnp.where(d == 0, c - 1, nlc - c)
        return jnp.where(c == 0, rows.n_lat // L + bi, lat)

    return pl.pallas_call(
        _mlstm_kernel,
        grid=(b, 2, nlc + 1),
        in_specs=[
            pl.BlockSpec((L, hq), lambda bi, d, c: (rb(bi, d, c), 0)),
            pl.BlockSpec((L, hq), lambda bi, d, c: (rb(bi, d, c), 1)),
            pl.BlockSpec((L, hv), lambda bi, d, c: (rb(bi, d, c), 1)),
            pl.BlockSpec((L, LANE), lambda bi, d, c: (rb(bi, d, c), d)),
            pl.BlockSpec((1, LANE), lambda bi, d, c: (0, d)),
        ],
        out_specs=pl.BlockSpec((1, L, hv), lambda bi, d, c: (d, rb(bi, d, c), 0)),
        out_shape=jax.ShapeDtypeStruct((2, rows.n_all, hv), BF16),
        scratch_shapes=[
            pltpu.VMEM((ML_HEADS, ML_DV, ML_DQK), F32),
            pltpu.VMEM((ML_HEADS, 1, ML_DQK), F32),
            pltpu.VMEM((ML_HEADS, 1, 1), F32),
        ],
        compiler_params=_params(("parallel", "parallel", "arbitrary")),
        name="mlstm_scan",
    )(z, z, z, gates, gate_b)


def _mla_kernel(q_ref, kvl_ref, kvx_ref, krl_ref, krx_ref, o_ref, kk_ref, vt_ref, *, seq, sub):
    @pl.when(pl.program_id(2) == 0)
    def _():
        kk_ref[:seq, :LANE] = kvl_ref[:, :LANE]
        kk_ref[:seq, LANE:] = krl_ref[...]
        kk_ref[seq:, :LANE] = kvx_ref[:, :LANE]
        kk_ref[seq:, LANE:] = krx_ref[...]
        vt_ref[:LANE, :seq] = kvl_ref[:, LANE:].astype(F32).T.astype(BF16)
        vt_ref[:LANE, seq:] = kvx_ref[:, LANE:].astype(F32).T.astype(BF16)
        vt_ref[LANE:, :] = jnp.ones((vt_ref.shape[0] - LANE, vt_ref.shape[1]), BF16)

    for s in range(q_ref.shape[0] // sub):
        rs = slice(s * sub, (s + 1) * sub)
        st = _dot_nt(kk_ref[...], q_ref[rs, :])
        pt = jnp.exp(st - jnp.max(st, axis=0, keepdims=True)).astype(BF16)
        oa = _dot(vt_ref[...], pt)
        o_ref[rs, :] = (oa[:LANE] / oa[LANE:LANE + 1]).T.astype(o_ref.dtype)


def mla_attention(q, kv, z, *, rows, tq=512, sub=256):
    b, seq, ctx_len = rows.batch, rows.seq, rows.ctx_len
    nq = seq // tq
    kr_col = z.shape[1] // LANE - 1
    ctx0 = rows.n_lat // ctx_len
    n_keys = seq + ctx_len
    ones_rows = 16
    return pl.pallas_call(
        functools.partial(_mla_kernel, seq=seq, sub=sub),
        grid=(b, MLA_HEADS, nq),
        in_specs=[
            pl.BlockSpec((tq, 2 * LANE), lambda bi, h, i: (bi * nq + i, h)),
            pl.BlockSpec((seq, 2 * LANE), lambda bi, h, i: (bi, h)),
            pl.BlockSpec((ctx_len, 2 * LANE), lambda bi, h, i: (ctx0 + bi, h)),
            pl.BlockSpec((seq, LANE), lambda bi, h, i: (bi, kr_col)),
            pl.BlockSpec((ctx_len, LANE), lambda bi, h, i: (ctx0 + bi, kr_col)),
        ],
        out_specs=pl.BlockSpec((tq, LANE), lambda bi, h, i: (bi * nq + i, h)),
        out_shape=jax.ShapeDtypeStruct((rows.n_lat, MLA_HEADS * MLA_V), BF16),
        scratch_shapes=[pltpu.VMEM((n_keys, 2 * LANE), BF16), pltpu.VMEM((LANE + ones_rows, n_keys), BF16)],
        compiler_params=_params(("parallel", "parallel", "arbitrary")),
        name="mla_attention",
    )(q, kv, kv, z, z)


def _hy_filter_kernel(ft_ref, w1_ref, b1_ref, w2_ref, b2_ref, w3_ref, b3_ref, fr_ref, w4_ref, b4_ref, dec_ref,
                      o_ref, a_ref, *, back_from):
    j = pl.program_id(0)
    hdot = functools.partial(jnp.dot, preferred_element_type=F32, precision=HIGHEST)

    @pl.when(j == 0)
    def _():
        fr = fr_ref[...]
        a = jnp.sin(fr * (hdot(ft_ref[...], w1_ref[...]) + b1_ref[...]))
        a = jnp.sin(fr * (hdot(a, w2_ref[...]) + b2_ref[...]))
        a_ref[...] = jnp.sin(fr * (hdot(a, w3_ref[...]) + b3_ref[...]))

    t = ft_ref[:, 0:1]
    filt = (hdot(a_ref[...], w4_ref[...]) + b4_ref[...]) * jnp.exp(-t * jnp.abs(dec_ref[...]))
    row = lax.broadcasted_iota(jnp.int32, filt.shape, 0)
    o_ref[...] = jnp.where(jnp.logical_and(row == 0, j >= back_from), 0.0, filt)


def hyena_filters(seq, w1, b1, w2, b2, w3, b3, w4, b4, freq, decay, tn=512):
    t = np.linspace(0.0, 1.0, seq)[:, None]
    w = (2.0 * math.pi / seq) * np.arange(seq)[:, None]
    bands = np.linspace(1e-4, HY_BANDS - 1, HY_BANDS)[None, :]
    feats = np.zeros((seq, LANE))
    feats[:, :1 + 2 * HY_BANDS] = np.concatenate([t, np.cos(bands * w), -np.sin(bands * w)], axis=-1)
    n = w4.shape[1]
    fw = HY_FILTER_W
    w1p = jnp.zeros((LANE, fw), F32).at[:w1.shape[0]].set(w1)
    row = lambda a: a.reshape(1, -1)
    full = lambda shape: pl.BlockSpec(shape, lambda j: (0, 0))
    return pl.pallas_call(
        functools.partial(_hy_filter_kernel, back_from=(n // 2) // tn),
        grid=(n // tn,),
        in_specs=[full((seq, LANE)), full((LANE, fw)), full((1, fw)), full((fw, fw)), full((1, fw)),
                  full((fw, fw)), full((1, fw)), full((1, fw)),
                  pl.BlockSpec((fw, tn), lambda j: (0, j)),
                  pl.BlockSpec((1, tn), lambda j: (0, j)),
                  pl.BlockSpec((1, tn), lambda j: (0, j))],
        out_specs=pl.BlockSpec((seq, tn), lambda j: (0, j)),
        out_shape=jax.ShapeDtypeStruct((seq, n), F32),
        scratch_shapes=[pltpu.VMEM((seq, fw), F32)],
        compiler_params=_params(("arbitrary",)),
        name="hyena_filters",
    )(jnp.asarray(feats, F32), w1p, row(b1), w2, row(b2), w3, row(b3), row(freq), w4, row(b4), row(decay))


def _dft_mats(seq, p):
    n = 2 * seq
    q = n // p
    k1n = q // 2 + 1
    hi, k1 = np.arange(q // 2), np.arange(k1n)
    th = 2 * np.pi * np.outer(k1, hi) / q
    ma = np.zeros((2 * k1n, q // 2))
    ma[0::2], ma[1::2] = np.cos(th), -np.sin(th)
    lo = np.arange(p)
    wf = np.zeros((k1n, 2 * p, 2 * p))
    wi = np.zeros((k1n, 2 * p, 2 * p))
    for k in k1:
        ph = -2 * np.pi * (np.outer(lo, lo) / p + k * lo[None, :] / n)
        er, ei = np.cos(ph), np.sin(ph)
        wf[k] = np.block([[er, -ei], [ei, er]])
        wi[k] = np.block([[er.T, ei.T], [-ei.T, er.T]])
    c = np.full(k1n, 2.0)
    c[0] = c[-1] = 1.0
    th2 = 2 * np.pi * np.outer(hi, k1) / q
    md = np.zeros((q // 2, 2 * k1n))
    md[:, 0::2], md[:, 1::2] = c * np.cos(th2) / n, -c * np.sin(th2) / n
    as_bf16 = lambda a: jnp.asarray(a, F32).astype(BF16)
    return ma, as_bf16(wf), as_bf16(wi), md


def _hy_short_kernel(x1_ref, x2_ref, v_ref, w_ref, b_ref, o1_ref, o2_ref, o3_ref):
    for k, (x_ref, o_ref) in enumerate(((x1_ref, o1_ref), (x2_ref, o2_ref), (v_ref, o3_ref))):
        x = x_ref[...].astype(F32)
        n = x.shape[0]
        row = lax.broadcasted_iota(jnp.int32, x.shape, 0)
        prev = jnp.where(row == 0, 0.0, pltpu.roll(x, 1, 0))
        nxt = jnp.where(row == n - 1, 0.0, pltpu.roll(x, n - 1, 0))
        w = w_ref[k]
        o_ref[...] = (prev * w[0:1] + x * w[1:2] + nxt * w[2:3] + b_ref[k]).astype(o_ref.dtype)


def hyena_short_conv(x3, conv_w, conv_b, *, rows, cb=256):
    b, seq = rows.batch, rows.seq
    d = x3.shape[1] // 3
    nblk = d // cb
    w = jnp.transpose(conv_w.reshape(HY_SHORT, 3, d), (1, 0, 2))
    bias = conv_b.reshape(3, 1, d)
    in_x = [pl.BlockSpec((seq, cb), lambda bi, j, k=k: (bi, k * nblk + j)) for k in range(3)]
    out = pl.BlockSpec((seq, cb), lambda bi, j: (bi, j))
    shape = jax.ShapeDtypeStruct((rows.n_lat, d), BF16)
    return pl.pallas_call(
        _hy_short_kernel,
        grid=(b, nblk),
        in_specs=in_x + [pl.BlockSpec((3, HY_SHORT, cb), lambda bi, j: (0, 0, j)),
                         pl.BlockSpec((3, 1, cb), lambda bi, j: (0, 0, j))],
        out_specs=[out, out, out],
        out_shape=[shape, shape, shape],
        compiler_params=_params(("parallel", "parallel")),
        name="hyena_short_conv",
    )(x3, x3, x3, w, bias)


def _hy_stage_a_kernel(x_ref, m_ref, o_ref):
    qh, g, wc = x_ref.shape
    x = x_ref[...].reshape(qh * g, wc).astype(BF16)
    o_ref[...] = _dot(m_ref[...], x).reshape(o_ref.shape).astype(o_ref.dtype)


def hyena_stage_a(x, mak, *, p, width=2048):
    bx, seq, c = x.shape
    g = HY_G
    qh = seq // p
    mo = mak.shape[0] // g
    x5 = x.reshape(bx, qh, p // g, g, c)
    return pl.pallas_call(
        _hy_stage_a_kernel,
        grid=(bx, p // g, c // width),
        in_specs=[pl.BlockSpec((None, qh, None, g, width), lambda bi, l, j: (bi, 0, l, 0, j)),
                  pl.BlockSpec(mak.shape, lambda bi, l, j: (0, 0))],
        out_specs=pl.BlockSpec((None, mo, g, width), lambda bi, l, j: (bi, 0, l, j)),
        out_shape=jax.ShapeDtypeStruct((bx, mo, p, c), BF16),
        compiler_params=_params(("parallel", "parallel", "parallel")),
        name="hyena_stage_a",
    )(x5, mak)


def _hy_spec_kernel(a0_ref, a1_ref, wf_ref, o_ref):
    p = o_ref.shape[2]
    cb = o_ref.shape[3]
    wf = wf_ref[0]
    x0 = _dot(wf, a0_ref[0].reshape(2 * p, cb))
    x1 = _dot(wf, a1_ref[0].reshape(2 * p, cb))
    o_ref[0, 0] = x0[:p] + x1[:p]
    o_ref[0, 1] = x0[p:] - x1[p:]


def hyena_spectrum(af, wf, *, d, cb=512):
    k1n, _, p, _ = af.shape
    nblk = (HY_ORDER * d) // cb
    return pl.pallas_call(
        _hy_spec_kernel,
        grid=(nblk, k1n),
        in_specs=[pl.BlockSpec((1, 2, p, cb), lambda j, k: (k, 0, 0, j)),
                  pl.BlockSpec((1, 2, p, cb), lambda j, k: (k, 0, 0, nblk + j)),
                  pl.BlockSpec((1, 2 * p, 2 * p), lambda j, k: (k, 0, 0))],
        out_specs=pl.BlockSpec((1, 2, p, cb), lambda j, k: (k, 0, 0, j)),
        out_shape=jax.ShapeDtypeStruct((k1n, 2, p, HY_ORDER * d), F32),
        compiler_params=_params(("parallel", "parallel")),
        name="hyena_spectrum",
    )(af, af, wf)


def _hy_mid_kernel(a_ref, wf_ref, wi_ref, h_ref, o_ref):
    b, _, p, cb = a_ref.shape
    hr, hi = h_ref[0], h_ref[1]
    for bi in range(b):
        x = _dot(wf_ref[...], a_ref[bi].reshape(2 * p, cb))
        xr, xi = x[:p], x[p:]
        y = jnp.concatenate([xr * hr - xi * hi, xr * hi + xi * hr], axis=0).astype(BF16)
        o_ref[bi] = _dot(wi_ref[...], y).reshape(2, p, cb).astype(o_ref.dtype)


def hyena_mid(a, wf, wi, spec, *, order, d, cb=512):
    b, k1n, _, p, _ = a.shape
    nblk = d // cb
    data = pl.BlockSpec((b, None, 2, p, cb), lambda j, k: (0, k, 0, 0, j))
    mat = pl.BlockSpec((None, 2 * p, 2 * p), lambda j, k: (k, 0, 0))
    return pl.pallas_call(
        _hy_mid_kernel,
        grid=(nblk, k1n),
        in_specs=[data, mat, mat,
                  pl.BlockSpec((None, 2, p, cb), lambda j, k: (k, 0, 0, order * nblk + j))],
        out_specs=data,
        out_shape=jax.ShapeDtypeStruct(a.shape, BF16),
        compiler_params=_params(("parallel", "parallel")),
        name="hyena_mid",
    )(a, wf, wi, spec)


def _hy_stage_d_kernel(c_ref, m_ref, x_ref, z_ref, skip_ref, o_ref):
    mo, g, wc = c_ref.shape
    qh = x_ref.shape[0]
    conv = _dot(m_ref[...], c_ref[...].reshape(mo * g, wc))
    z = z_ref[...].reshape(qh * g, wc).astype(F32)
    x = x_ref[...].reshape(qh * g, wc).astype(F32)
    o_ref[...] = (x * (conv + z * skip_ref[...])).reshape(o_ref.shape).astype(o_ref.dtype)


def hyena_stage_d(c, mdk, xg, z, skip, *, p, width=2048):
    b, mo, _, d = c.shape
    g = HY_G
    seq = xg.shape[1]
    qh = seq // p
    v5 = lambda a: a.reshape(b, qh, p // g, g, d)
    tspec = pl.BlockSpec((None, qh, None, g, width), lambda bi, l, j: (bi, 0, l, 0, j))
    out = pl.pallas_call(
        _hy_stage_d_kernel,
        grid=(b, p // g, d // width),
        in_specs=[pl.BlockSpec((None, mo, g, width), lambda bi, l, j: (bi, 0, l, j)),
                  pl.BlockSpec(mdk.shape, lambda bi, l, j: (0, 0)),
                  tspec, tspec,
                  pl.BlockSpec((1, width), lambda bi, l, j: (0, j))],
        out_specs=tspec,
        out_shape=jax.ShapeDtypeStruct((b, qh, p // g, g, d), BF16),
        compiler_params=_params(("parallel", "parallel", "parallel")),
        name="hyena_stage_d",
    )(c, mdk, v5(xg), v5(z), skip.reshape(1, d))
    return out.reshape(b, seq, d)


def hyena_long_convs(x1, x2, v, filt, skip, *, rows):
    b, seq = rows.batch, rows.seq
    d = x1.shape[1]
    p = HY_P
    k1n = seq // p + 1
    ma, wf, wi, md = _dft_mats(seq, p)
    eye = np.eye(HY_G)
    as_bf16 = lambda a: jnp.asarray(a, F32).astype(BF16)
    mak, mdk = as_bf16(np.kron(ma, eye)), as_bf16(np.kron(md, eye))
    af = hyena_stage_a(filt.reshape(1, seq, filt.shape[1]), mak, p=p)
    spec = hyena_spectrum(af.reshape(k1n, 2, p, filt.shape[1]), wf, d=d)
    t3 = lambda a: a.reshape(b, seq, d)
    y = t3(v)
    for order, xg in enumerate((x1, x2)):
        a = hyena_stage_a(y, mak, p=p).reshape(b, k1n, 2, p, d)
        c = hyena_mid(a, wf, wi, spec, order=order, d=d)
        y = hyena_stage_d(c.reshape(b, 2 * k1n, p, d), mdk, t3(xg), y, skip[order], p=p)
    return y.reshape(rows.n_lat, d)


def _mlstm_gate_weights(w_in, gate_b):
    h = ML_HEADS
    o4 = 2 * h * ML_DQK + 2 * h * ML_DV
    wg = w_in[:, o4:].reshape(-1, 2, 2, h)
    bg = gate_b.reshape(2, 2, h)
    w_out = jnp.zeros((w_in.shape[0], 2 * LANE), F32)
    b_out = jnp.zeros((1, 2 * LANE), F32)
    for d in range(2):
        for gate in range(2):
            lo = d * LANE + gate * h
            w_out = w_out.at[:, lo:lo + h].set(wg[:, gate, d])
            b_out = b_out.at[0, lo:lo + h].set(bg[gate, d])
    return w_out, b_out


def _mla_weights(w_in, w_uq):
    r2 = MLA_Q_RANK + MLA_KV_RANK
    half = MLA_ROPE // 2

    def spread(w):
        z = jnp.zeros(w.shape[:-1] + (half,), w.dtype)
        return jnp.concatenate([w[..., :half], z, w[..., half:], z], axis=-1)

    w_in_p = jnp.concatenate([w_in[:, :r2], spread(w_in[:, r2:])], axis=1)
    wq = w_uq.reshape(w_uq.shape[0], MLA_HEADS, MLA_NOPE + MLA_ROPE)
    wq_p = jnp.concatenate([wq[..., :MLA_NOPE], spread(wq[..., MLA_NOPE:])], axis=-1)
    wq_p = wq_p * (MLA_NOPE + MLA_ROPE) ** -0.5
    return w_in_p, wq_p.reshape(w_uq.shape[0], MLA_HEADS * 2 * LANE)


def kernel(x, c, ctx, c_ctx, ada_w, ada_b, norm_g, mlp_w1, mlp_w2, swa_w_qkv, swa_sink, swa_w_o, ml_w_in, ml_gate_b, ml_head_g, ml_w_o, mla_w_in, mla_q_g, mla_kv_g, mla_w_uq, mla_w_ukv, mla_w_o, hy_w_in, hy_conv_w, hy_conv_b, hy_f_w1, hy_f_b1, hy_f_w2, hy_f_b2, hy_f_w3, hy_f_b3, hy_f_w4, hy_f_b4, hy_f_freq, hy_decay, hy_skip, hy_w_o):
    b, seq, d = x.shape
    ctx_len = ctx.shape[1]
    depth = ada_w.shape[0]
    assert depth == 4 and b < 8
    rows = Rows(b, seq, ctx_len)
    tm = 512
    tm_lin = 1024 if rows.n_lat % 1024 == 0 and rows.n_ctx % 1024 == 0 else 512
    bf = lambda w: w.astype(BF16)
    mlp_w1b, mlp_w2b = bf(mlp_w1), bf(mlp_w2)

    cond = jnp.zeros((8, d), F32).at[:b].set(c).at[b].set(c_ctx)
    mods = ada_mods(cond, ada_w, ada_b).reshape(depth * 8, 1, N_MOD * d)
    h = jnp.concatenate([x.reshape(rows.n_lat, d), ctx.reshape(rows.n_ctx, d)], axis=0)

    common = dict(rows=rows, tm=tm_lin, mods=mods)

    def finish(h, o_args, w_o, layer, n_rows, mode="plain"):
        h = outproj_residual(o_args, bf(w_o), h, mods, norm_g[layer, 1], rows=rows, n_rows=n_rows, tm=tm,
                             layer=layer, mode=mode, name=f"outproj{layer}")
        return mlp_residual(h, mlp_w1b, mlp_w2b, mods, norm_g[layer, 2], norm_g[layer, 3],
                            rows=rows, n_rows=n_rows, tm=tm, tf=1024, layer=layer)

    n_qk_groups = SWA_HEADS + SWA_KV_HEADS
    qkv = linear(h, bf(swa_w_qkv[0]), n_rows=rows.n_all, tn=512, out_dtype=BF16, prologue="norm_mod",
                 gain=norm_g[0, 0], layer=0, rope_tabs=_rope_tables(seq, SWA_HEAD_DIM), rope_pattern=(True,) * 4,
                 rope_jmax=n_qk_groups // 4, name="swa_qkv", **common)
    o = swa_attention(qkv, swa_sink[0], rows=rows)
    h = finish(h, o, swa_w_o[0], 0, rows.n_all)

    w_gate, b_gate = _mlstm_gate_weights(ml_w_in[0], ml_gate_b[0])
    z = linear(h, bf(ml_w_in[0]), n_rows=rows.n_all, tn=1024, out_dtype=BF16, prologue="norm_mod",
               gain=norm_g[1, 0], layer=1, n_out=2 * ML_HEADS * (ML_DQK + ML_DV), name="mlstm_in", **common)
    gates = linear(h, bf(w_gate), n_rows=rows.n_all, tn=2 * LANE, out_dtype=F32, prologue="norm_mod",
                   gain=norm_g[1, 0], layer=1, name="mlstm_gates", **common)
    hsum = mlstm_scan(z, gates, b_gate, rows=rows)
    h = finish(h, (hsum, z, ml_head_g[0]), ml_w_o[0], 1, rows.n_all, mode="mlstm")

    w_in_p, w_uq_p = _mla_weights(mla_w_in[0], mla_w_uq[0])
    rope_mla = _rope_tables(seq, MLA_ROPE)
    zc = linear(h, bf(w_in_p), n_rows=rows.n_all, tn=w_in_p.shape[1], out_dtype=BF16, prologue="norm_mod",
                gain=norm_g[2, 0], layer=2, rope_tabs=rope_mla, rope_pattern=(False,) * 8 + (True,),
                name="mla_in", **common)
    q = linear(zc, bf(w_uq_p), n_rows=rows.n_lat, tn=512, out_dtype=BF16, x_cols=0, prologue="norm",
               gain=mla_q_g[0], rope_tabs=rope_mla, rope_pattern=(False, True) * 2, name="mla_q",
               rows=rows, tm=tm_lin)
    kv = linear(zc, bf(mla_w_ukv[0]), n_rows=rows.n_all, tn=1024, out_dtype=BF16, x_cols=1, prologue="norm",
                gain=mla_kv_g[0], name="mla_kv", rows=rows, tm=tm_lin)
    o = mla_attention(q, kv, zc, rows=rows)
    h = finish(h, o, mla_w_o[0], 2, rows.n_lat)

    x3 = linear(h, bf(hy_w_in[0]), n_rows=rows.n_lat, tn=1024, out_dtype=BF16, prologue="norm_mod",
                gain=norm_g[3, 0], layer=3, name="hyena_in", **common)
    x1, x2, v = hyena_short_conv(x3, hy_conv_w[0], hy_conv_b[0], rows=rows)
    filt = hyena_filters(seq, hy_f_w1[0], hy_f_b1[0], hy_f_w2[0], hy_f_b2[0], hy_f_w3[0], hy_f_b3[0],
                         hy_f_w4[0], hy_f_b4[0], hy_f_freq[0], hy_decay[0])
    y = hyena_long_convs(x1, x2, v, filt, hy_skip[0], rows=rows)
    h = finish(h, y, hy_w_o[0], 3, rows.n_lat)
    return h.reshape(b, seq, d)
```

```python
import functools
import math

import numpy as np
import jax
import jax.numpy as jnp
from jax import lax
from jax.experimental import pallas as pl
from jax.experimental.pallas import tpu as pltpu

F32, BF16 = jnp.float32, jnp.bfloat16
HIGHEST = lax.Precision.HIGHEST

RMS_EPS = 1e-6
ROPE_THETA = 10000.0
GRID_W = 64
N_MOD = 6
LANE = 128

SWA_HEADS, SWA_KV_HEADS, SWA_HEAD_DIM, SWA_WINDOW, SWA_BLOCK = 16, 4, 128, 128, 128
ML_HEADS, ML_DQK, ML_DV = 8, 128, 256
ML_CHUNK = 256
MLA_HEADS, MLA_Q_RANK, MLA_KV_RANK, MLA_NOPE, MLA_ROPE, MLA_V = 16, 512, 512, 128, 64, 128
HY_ORDER, HY_BANDS, HY_FILTER_W, HY_SHORT = 2, 16, 64, 3
HY_P = 256
HY_G = 16

VMEM_LIMIT = 48 * 1024 * 1024


def _params(sem):
    return pltpu.CompilerParams(dimension_semantics=sem, vmem_limit_bytes=VMEM_LIMIT)


def _dot(a, b):
    return jnp.dot(a, b, preferred_element_type=F32)


def _dot_nt(a, b):
    return lax.dot_general(a, b, (((1,), (1,)), ((), ())), preferred_element_type=F32)


def _dot_tn(a, b):
    return lax.dot_general(a, b, (((0,), (0,)), ((), ())), preferred_element_type=F32)


def _rms(x, g):
    return x * lax.rsqrt(jnp.mean(x * x, axis=-1, keepdims=True) + RMS_EPS) * g


def _ada_kernel(s_ref, w_ref, b_ref, o_ref):
    s = s_ref[...]
    s = s * jax.nn.sigmoid(s)
    o_ref[0] = jnp.dot(s, w_ref[0], preferred_element_type=F32, precision=HIGHEST) + b_ref[0]


def ada_mods(cond, ada_w, ada_b, tn=1536):
    depth, d, n = ada_w.shape
    rows = cond.shape[0]
    return pl.pallas_call(
        _ada_kernel,
        grid=(depth, n // tn),
        in_specs=[
            pl.BlockSpec((rows, d), lambda l, j: (0, 0)),
            pl.BlockSpec((1, d, tn), lambda l, j: (l, 0, j)),
            pl.BlockSpec((1, 1, tn), lambda l, j: (l, 0, j)),
        ],
        out_specs=pl.BlockSpec((1, rows, tn), lambda l, j: (l, 0, j)),
        out_shape=jax.ShapeDtypeStruct((depth, rows, n), F32),
        compiler_params=_params(("parallel", "parallel")),
        name="ada_mods",
    )(cond, ada_w, ada_b.reshape(depth, 1, n))


class Rows:
    def __init__(self, batch, seq, ctx_len):
        self.batch, self.seq, self.ctx_len = batch, seq, ctx_len
        self.n_lat = batch * seq
        self.n_ctx = batch * ctx_len
        self.n_all = self.n_lat + self.n_ctx

    def mod_index(self, layer, k, tm):
        lat_blocks, per_batch = self.n_lat // tm, self.seq // tm

        def index(i, *_):
            b = jnp.where(i < lat_blocks, i // per_batch, self.batch)
            return (layer * 8 + b, 0, k)

        return index


def _linear_kernel(*refs, prologue, rope, n_lat_blocks, rope_pattern, rope_jmax):
    it = iter(refs)
    x_ref = next(it)
    g_ref = next(it) if prologue in ("norm", "norm_mod") else None
    sh_ref = next(it) if prologue == "norm_mod" else None
    sc_ref = next(it) if prologue == "norm_mod" else None
    w_ref = next(it)
    cc_ref = next(it) if rope else None
    ss_ref = next(it) if rope else None
    o_ref = next(it)
    u_ref = next(it)
    i, j = pl.program_id(0), pl.program_id(1)

    @pl.when(j == 0)
    def _():
        x = x_ref[...].astype(F32)
        if prologue in ("norm", "norm_mod"):
            x = _rms(x, g_ref[...])
        if prologue == "norm_mod":
            x = x * (1.0 + sc_ref[0]) + sh_ref[0]
        u_ref[...] = x.astype(BF16)

    acc = _dot(u_ref[...], w_ref[...])

    def plain():
        o_ref[...] = acc.astype(o_ref.dtype)

    def roped():
        cc, ss = cc_ref[...], ss_ref[...]
        segs = []
        for gi, on in enumerate(rope_pattern):
            seg = acc[:, gi * LANE:(gi + 1) * LANE]
            if on:
                seg = seg * cc + pltpu.roll(seg, LANE // 2, 1) * ss
            segs.append(seg)
        o_ref[...] = jnp.concatenate(segs, axis=1).astype(o_ref.dtype)

    if not rope:
        plain()
    else:
        cond = jnp.logical_and(i < n_lat_blocks, j < rope_jmax)
        pl.when(cond)(roped)
        pl.when(jnp.logical_not(cond))(plain)


def linear(x, w, *, rows, n_rows, tm, tn, out_dtype, x_cols=None, prologue="none", gain=None,
           mods=None, layer=0, mod_k=(0, 1), rope_tabs=None, rope_pattern=(), rope_jmax=1 << 30,
           n_out=None, name="linear"):
    k, n = w.shape
    n = n if n_out is None else n_out
    xc = 0 if x_cols is None else x_cols
    rope = rope_tabs is not None
    in_specs = [pl.BlockSpec((tm, k), lambda i, j: (i, xc))]
    args = [x]
    if prologue in ("norm", "norm_mod"):
        in_specs.append(pl.BlockSpec((1, k), lambda i, j: (0, 0)))
        args.append(gain.reshape(1, k))
    if prologue == "norm_mod":
        for mk in mod_k:
            in_specs.append(pl.BlockSpec((1, 1, k), rows.mod_index(layer, mk, tm)))
            args.append(mods)
    in_specs.append(pl.BlockSpec((k, tn), lambda i, j: (0, j)))
    args.append(w)
    if rope:
        per_seq = rows.seq // tm
        for t in rope_tabs:
            in_specs.append(pl.BlockSpec((tm, LANE), lambda i, j: (i % per_seq, 0)))
            args.append(t)
    kern = functools.partial(_linear_kernel, prologue=prologue, rope=rope,
                             n_lat_blocks=rows.n_lat // tm, rope_pattern=tuple(rope_pattern),
                             rope_jmax=rope_jmax)
    return pl.pallas_call(
        kern,
        grid=(n_rows // tm, n // tn),
        in_specs=in_specs,
        out_specs=pl.BlockSpec((tm, tn), lambda i, j: (i, j)),
        out_shape=jax.ShapeDtypeStruct((n_rows, n), out_dtype),
        scratch_shapes=[pltpu.VMEM((tm, k), BF16)],
        compiler_params=_params(("parallel", "arbitrary")),
        name=name,
    )(*args)


def _outproj_kernel(*refs, mode):
    it = iter(refs)
    if mode == "mlstm":
        hf_ref, hb_ref, og_ref, hg_ref = next(it), next(it), next(it), next(it)
    else:
        o_ref_in = next(it)
    w_ref, h_ref, gate_ref, g_ref, out_ref = next(it), next(it), next(it), next(it), next(it)
    if mode == "mlstm":
        hs = hf_ref[0].astype(F32) + hb_ref[0].astype(F32)
        og = jax.nn.sigmoid(og_ref[...].astype(F32))
        hg = hg_ref[...]
        parts = []
        for h in range(ML_HEADS):
            sl = slice(h * ML_DV, (h + 1) * ML_DV)
            parts.append((_rms(hs[:, sl], hg[:, sl]) * og[:, sl]).astype(BF16))
        o = jnp.concatenate(parts, axis=1)
    else:
        o = o_ref_in[...]
    y = _dot(o, w_ref[...])
    out_ref[...] = h_ref[...] + gate_ref[0] * _rms(y, g_ref[...])


def outproj_residual(o_args, w, h, mods, gain, *, rows, n_rows, tm, layer, mode="plain", name="outproj"):
    k, d = w.shape
    if mode == "mlstm":
        hsum, z, head_g = o_args
        in_specs = [
            pl.BlockSpec((1, tm, k), lambda i: (0, i, 0)),
            pl.BlockSpec((1, tm, k), lambda i: (1, i, 0)),
            pl.BlockSpec((tm, k), lambda i: (i, 2)),
            pl.BlockSpec((1, k), lambda i: (0, 0)),
        ]
        args = [hsum, hsum, z, head_g.reshape(1, k)]
    else:
        in_specs = [pl.BlockSpec((tm, k), lambda i: (i, 0))]
        args = [o_args]
    in_specs += [
        pl.BlockSpec((k, d), lambda i: (0, 0)),
        pl.BlockSpec((tm, d), lambda i: (i, 0)),
        pl.BlockSpec((1, 1, d), rows.mod_index(layer, 2, tm)),
        pl.BlockSpec((1, d), lambda i: (0, 0)),
    ]
    args += [w, h, mods, gain.reshape(1, d)]
    return pl.pallas_call(
        functools.partial(_outproj_kernel, mode=mode),
        grid=(n_rows // tm,),
        in_specs=in_specs,
        out_specs=pl.BlockSpec((tm, d), lambda i: (i, 0)),
        out_shape=jax.ShapeDtypeStruct((n_rows, d), F32),
        compiler_params=_params(("parallel",)),
        name=name,
    )(*args)


def _mlp_kernel(h_ref, g2_ref, sh_ref, sc_ref, w1_ref, w2_ref, gate_ref, g3_ref, out_ref, v_ref, acc_ref):
    f = pl.program_id(1)

    @pl.when(f == 0)
    def _():
        v = _rms(h_ref[...], g2_ref[...]) * (1.0 + sc_ref[0]) + sh_ref[0]
        v_ref[...] = v.astype(BF16)
        acc_ref[...] = jnp.zeros_like(acc_ref)

    a = jnp.maximum(_dot(v_ref[...], w1_ref[...]), 0.0)
    acc_ref[...] += _dot((a * a).astype(BF16), w2_ref[...])

    @pl.when(f == pl.num_programs(1) - 1)
    def _():
        out_ref[...] = h_ref[...] + gate_ref[0] * _rms(acc_ref[...], g3_ref[...])


def mlp_residual(h, w1, w2, mods, g2, g3, *, rows, n_rows, tm, tf, layer):
    _, d, ff = w1.shape
    return pl.pallas_call(
        _mlp_kernel,
        grid=(n_rows // tm, ff // tf),
        in_specs=[
            pl.BlockSpec((tm, d), lambda i, f: (i, 0)),
            pl.BlockSpec((1, d), lambda i, f: (0, 0)),
            pl.BlockSpec((1, 1, d), rows.mod_index(layer, 3, tm)),
            pl.BlockSpec((1, 1, d), rows.mod_index(layer, 4, tm)),
            pl.BlockSpec((None, d, tf), lambda i, f: (layer, 0, f)),
            pl.BlockSpec((None, tf, d), lambda i, f: (layer, f, 0)),
            pl.BlockSpec((1, 1, d), rows.mod_index(layer, 5, tm)),
            pl.BlockSpec((1, d), lambda i, f: (0, 0)),
        ],
        out_specs=pl.BlockSpec((tm, d), lambda i, f: (i, 0)),
        out_shape=jax.ShapeDtypeStruct((n_rows, d), F32),
        scratch_shapes=[pltpu.VMEM((tm, d), BF16), pltpu.VMEM((tm, d), F32)],
        compiler_params=_params(("parallel", "arbitrary")),
        name="mlp",
    )(h, g2.reshape(1, d), mods, mods, w1, w2, mods, g3.reshape(1, d))


def _rope_tables(seq, d_rot):
    pos = np.arange(seq)
    row, col = pos // GRID_W, pos % GRID_W
    n = d_rot // 4
    inv = ROPE_THETA ** (-np.arange(n, dtype=np.float64) / n)
    ang = np.concatenate([row[:, None] * inv, col[:, None] * inv], axis=-1)
    cos, sin = np.cos(ang), np.sin(ang)
    half = d_rot // 2
    cc = np.zeros((seq, LANE))
    ss = np.zeros((seq, LANE))
    cc[:, :half] = cos
    cc[:, LANE // 2:LANE // 2 + half] = cos
    ss[:, :half] = -sin
    ss[:, LANE // 2:LANE // 2 + half] = sin
    return jnp.asarray(cc, F32), jnp.asarray(ss, F32)


def _swa_kernel(sink_ref, q_ref, kp_ref, kc_ref, kn_ref, kx_ref, vp_ref, vc_ref, vn_ref, vx_ref, o_ref,
                *, nb, seq, ctx_len):
    n = pl.program_id(1)
    hd, grp, lb = SWA_HEAD_DIM, SWA_HEADS // SWA_KV_HEADS, SWA_BLOCK
    scale = hd ** -0.5
    m_rows, n_loc = grp * lb, 3 * lb
    r = lax.broadcasted_iota(jnp.int32, (m_rows, n_loc + ctx_len), 0)
    c = lax.broadcasted_iota(jnp.int32, (m_rows, n_loc + ctx_len), 1)
    qpos = n * lb + (r & (lb - 1))
    kpos = (n - 1) * lb + c
    local_ok = (jnp.abs(kpos - qpos) <= SWA_WINDOW) & (kpos >= 0) & (kpos < seq) & (n < nb)
    valid = local_ok | (c >= n_loc)
    outs = []
    for kv in range(SWA_KV_HEADS):
        ks = slice(kv * hd, (kv + 1) * hd)
        qg = jnp.concatenate([q_ref[:, (kv * grp + g) * hd:(kv * grp + g + 1) * hd] for g in range(grp)], axis=0)
        keys = jnp.concatenate([kp_ref[:, ks], kc_ref[:, ks], kn_ref[:, ks], kx_ref[:, ks]], axis=0)
        vals = jnp.concatenate([vp_ref[:, ks], vc_ref[:, ks], vn_ref[:, ks], vx_ref[:, ks]], axis=0)
        s = jnp.where(valid, _dot_nt(qg, keys) * scale, -1e30)
        for g in range(grp):
            sg = s[g * lb:(g + 1) * lb]
            snk = sink_ref[kv * grp + g]
            m = jnp.maximum(jnp.max(sg, axis=1, keepdims=True), snk)
            p = jnp.exp(sg - m)
            l = jnp.sum(p, axis=1, keepdims=True) + jnp.exp(snk - m)
            outs.append((_dot(p.astype(BF16), vals) / l).astype(BF16))
    o_ref[...] = jnp.concatenate(outs, axis=1)


def swa_attention(qkv, sink, *, rows):
    b, seq, ctx_len = rows.batch, rows.seq, rows.ctx_len
    lb = SWA_BLOCK
    nb, ncb = seq // lb, ctx_len // lb
    lat_blocks = rows.n_lat // lb
    qw = SWA_HEADS * SWA_HEAD_DIM
    kw = SWA_KV_HEADS * SWA_HEAD_DIM
    kcol, vcol = qw // kw, qw // kw + 1

    def qidx(bi, n):
        return (jnp.where(n < nb, bi * nb + n, lat_blocks + bi * ncb + (n - nb)), 0)

    def kidx(off, col):
        return lambda bi, n: (bi * nb + jnp.clip(n + off, 0, nb - 1), col)

    def xidx(col):
        return lambda bi, n: (rows.n_lat // ctx_len + bi, col)

    kern = functools.partial(_swa_kernel, nb=nb, seq=seq, ctx_len=ctx_len)
    return pl.pallas_call(
        kern,
        grid=(b, nb + ncb),
        in_specs=[
            pl.BlockSpec(memory_space=pltpu.SMEM),
            pl.BlockSpec((lb, qw), qidx),
            pl.BlockSpec((lb, kw), kidx(-1, kcol)),
            pl.BlockSpec((lb, kw), kidx(0, kcol)),
            pl.BlockSpec((lb, kw), kidx(1, kcol)),
            pl.BlockSpec((ctx_len, kw), xidx(kcol)),
            pl.BlockSpec((lb, kw), kidx(-1, vcol)),
            pl.BlockSpec((lb, kw), kidx(0, vcol)),
            pl.BlockSpec((lb, kw), kidx(1, vcol)),
            pl.BlockSpec((ctx_len, kw), xidx(vcol)),
        ],
        out_specs=pl.BlockSpec((lb, qw), qidx),
        out_shape=jax.ShapeDtypeStruct((rows.n_all, qw), BF16),
        compiler_params=_params(("parallel", "parallel")),
        name="swa_attention",
    )(sink, qkv, qkv, qkv, qkv, qkv, qkv, qkv, qkv, qkv)


def _mlstm_kernel(q_ref, k_ref, v_ref, gi_ref, gf_ref, bi_ref, bf_ref, o_ref, c_ref, m_ref):
    d, c = pl.program_id(1), pl.program_id(2)
    L, H, dqk, dv = ML_CHUNK, ML_HEADS, ML_DQK, ML_DV
    scale = dqk ** -0.5

    @pl.when(c == 0)
    def _():
        c_ref[...] = jnp.zeros_like(c_ref)
        m_ref[...] = jnp.zeros_like(m_ref)

    row = lax.broadcasted_iota(jnp.int32, (L, L), 0)
    col = lax.broadcasted_iota(jnp.int32, (L, L), 1)
    sign = 1 - 2 * d
    mask = (col - row) * sign <= 0
    ones_mask = jnp.where(mask, 1.0, 0.0).astype(BF16)
    i_blk = gi_ref[...] + bi_ref[...]
    f_blk = jax.nn.log_sigmoid(gf_ref[...] + bf_ref[...])
    f_hi = f_blk.astype(BF16)
    r1 = f_blk - f_hi.astype(F32)
    f_mid = r1.astype(BF16)
    f_lo = (r1 - f_mid.astype(F32)).astype(BF16)
    b_blk = _dot(jnp.concatenate([ones_mask] * 3, axis=1), jnp.concatenate([f_hi, f_mid, f_lo], axis=0))
    b_end = jnp.sum(f_blk, axis=0, keepdims=True)
    e_rows = (i_blk - b_blk).T
    m_prev_blk = m_ref[...]
    dec = b_end - b_blk + i_blk
    m_new_blk = jnp.maximum(b_end + m_prev_blk, jnp.max(dec, axis=0, keepdims=True))
    ws_blk = jnp.exp(dec - m_new_blk)
    gs_blk = jnp.exp(b_end + m_prev_blk - m_new_blk)
    ones = jnp.ones((L, LANE), BF16)
    for h in range(H):
        qh = q_ref[:, h * dqk:(h + 1) * dqk]
        kh = k_ref[:, h * dqk:(h + 1) * dqk]
        v_aug = jnp.concatenate([v_ref[:, h * dv:(h + 1) * dv], ones], axis=1)
        e = jnp.where(mask, e_rows[h:h + 1, :], -jnp.inf)
        m_prev = m_prev_blk[:, h:h + 1]
        mm = jnp.maximum(m_prev, jnp.max(e, axis=1, keepdims=True))
        s = (_dot_nt(qh, kh) * jnp.exp(e - (mm - math.log(scale)))).astype(BF16)
        g = jnp.exp(m_prev - mm) * scale
        ct = c_ref[h]
        lhs = jnp.concatenate([s, (qh.astype(F32) * g).astype(BF16)], axis=1)
        res = _dot(lhs, jnp.concatenate([v_aug, ct.astype(BF16)], axis=0))
        floor = jnp.exp(-(b_blk[:, h:h + 1] + mm))
        hout = res[:, :dv] / jnp.maximum(jnp.abs(res[:, dv:dv + 1]), floor)
        o_ref[0, :, h * dv:(h + 1) * dv] = hout.astype(o_ref.dtype)
        kw = (kh.astype(F32) * ws_blk[:, h:h + 1]).astype(BF16)
        c_ref[h] = gs_blk[:, h:h + 1] * ct + _dot_tn(kw, v_aug)
    m_ref[...] = m_new_blk


def mlstm_scan(z, gates, gate_b, *, rows):
    b, seq, ctx_len = rows.batch, rows.seq, rows.ctx_len
    L = ML_CHUNK
    assert ctx_len == L and seq % L == 0
    nlc = seq // L
    hq, hv = ML_HEADS * ML_DQK, ML_HEADS * ML_DV

    def rb(bi, d, c):
        lat = bi * nlc + jnp.where(d == 0, c - 1, nlc - c)
        return jnp.where(c == 0, rows.n_lat // L + bi, lat)

    return pl.pallas_call(
        _mlstm_kernel,
        grid=(b, 2, nlc + 1),
        in_specs=[
            pl.BlockSpec((L, hq), lambda bi, d, c: (rb(bi, d, c), 0)),
            pl.BlockSpec((L, hq), lambda bi, d, c: (rb(bi, d, c), 1)),
            pl.BlockSpec((L, hv), lambda bi, d, c: (rb(bi, d, c), 1)),
            pl.BlockSpec((L, LANE), lambda bi, d, c: (rb(bi, d, c), 2 * d)),
            pl.BlockSpec((L, LANE), lambda bi, d, c: (rb(bi, d, c), 2 * d + 1)),
            pl.BlockSpec((1, LANE), lambda bi, d, c: (0, 2 * d)),
            pl.BlockSpec((1, LANE), lambda bi, d, c: (0, 2 * d + 1)),
        ],
        out_specs=pl.BlockSpec((1, L, hv), lambda bi, d, c: (d, rb(bi, d, c), 0)),
        out_shape=jax.ShapeDtypeStruct((2, rows.n_all, hv), BF16),
        scratch_shapes=[
            pltpu.VMEM((ML_HEADS, ML_DQK, ML_DV + LANE), F32),
            pltpu.VMEM((1, LANE), F32),
        ],
        compiler_params=_params(("parallel", "parallel", "arbitrary")),
        name="mlstm_scan",
    )(z, z, z, gates, gates, gate_b, gate_b)


def _mla_kernel(q_ref, kvl_ref, kvx_ref, krl_ref, krx_ref, o_ref, kk_ref, vt_ref, *, seq, sub, ck):
    @pl.when(pl.program_id(2) == 0)
    def _():
        kk_ref[:seq, :LANE] = kvl_ref[:, :LANE]
        kk_ref[:seq, LANE:] = krl_ref[...]
        kk_ref[seq:, :LANE] = kvx_ref[:, :LANE]
        kk_ref[seq:, LANE:] = krx_ref[...]
        vt_ref[:LANE, :seq] = kvl_ref[:, LANE:].astype(F32).T.astype(BF16)
        vt_ref[:LANE, seq:] = kvx_ref[:, LANE:].astype(F32).T.astype(BF16)
        vt_ref[LANE:, :] = jnp.ones((vt_ref.shape[0] - LANE, vt_ref.shape[1]), BF16)

    nsub = q_ref.shape[0] // sub
    n_keys = kk_ref.shape[0]
    bounds = [(k0, min(k0 + ck, n_keys)) for k0 in range(0, n_keys, ck)]
    nck = len(bounds)
    qs = [q_ref[s * sub:(s + 1) * sub, :] for s in range(nsub)]

    def scores(s, c):
        return _dot_nt(kk_ref[bounds[c][0]:bounds[c][1], :], qs[s])

    m = [None] * nsub
    acc = [None] * nsub
    st_next = [scores(s, 0) for s in range(nsub)]
    for c in range(nck):
        st = st_next
        if c + 1 < nck:
            st_next = [scores(s, c + 1) for s in range(nsub)]
        vt_c = vt_ref[:, bounds[c][0]:bounds[c][1]]
        for s in range(nsub):
            cm = jnp.max(st[s], axis=0, keepdims=True)
            m_new = cm if c == 0 else jnp.maximum(m[s], cm)
            pv = _dot(vt_c, jnp.exp(st[s] - m_new).astype(BF16))
            acc[s] = pv if c == 0 else acc[s] * jnp.exp(m[s] - m_new) + pv
            m[s] = m_new
    for s in range(nsub):
        o = acc[s][:LANE] / acc[s][LANE:LANE + 1]
        o_ref[s * sub:(s + 1) * sub, :] = o.T.astype(o_ref.dtype)


def mla_attention(q, kv, z, *, rows, tq=512, sub=256, ck=1024):
    b, seq, ctx_len = rows.batch, rows.seq, rows.ctx_len
    nq = seq // tq
    kr_col = z.shape[1] // LANE - 1
    ctx0 = rows.n_lat // ctx_len
    n_keys = seq + ctx_len
    ones_rows = 16
    return pl.pallas_call(
        functools.partial(_mla_kernel, seq=seq, sub=sub, ck=ck),
        grid=(b, MLA_HEADS, nq),
        in_specs=[
            pl.BlockSpec((tq, 2 * LANE), lambda bi, h, i: (bi * nq + i, h)),
            pl.BlockSpec((seq, 2 * LANE), lambda bi, h, i: (bi, h)),
            pl.BlockSpec((ctx_len, 2 * LANE), lambda bi, h, i: (ctx0 + bi, h)),
            pl.BlockSpec((seq, LANE), lambda bi, h, i: (bi, kr_col)),
            pl.BlockSpec((ctx_len, LANE), lambda bi, h, i: (ctx0 + bi, kr_col)),
        ],
        out_specs=pl.BlockSpec((tq, LANE), lambda bi, h, i: (bi * nq + i, h)),
        out_shape=jax.ShapeDtypeStruct((rows.n_lat, MLA_HEADS * MLA_V), BF16),
        scratch_shapes=[pltpu.VMEM((n_keys, 2 * LANE), BF16), pltpu.VMEM((LANE + ones_rows, n_keys), BF16)],
        compiler_params=_params(("parallel", "parallel", "arbitrary")),
        name="mla_attention",
    )(q, kv, kv, z, z)


def _hy_filter_kernel(ft_ref, w1_ref, b1_ref, w2_ref, b2_ref, w3_ref, b3_ref, fr_ref, w4_ref, b4_ref, dec_ref,
                      o_ref, a_ref, *, back_from):
    j = pl.program_id(0)
    hdot = functools.partial(jnp.dot, preferred_element_type=F32, precision=HIGHEST)

    @pl.when(j == 0)
    def _():
        fr = fr_ref[...]
        a = jnp.sin(fr * (hdot(ft_ref[...], w1_ref[...]) + b1_ref[...]))
        a = jnp.sin(fr * (hdot(a, w2_ref[...]) + b2_ref[...]))
        a_ref[...] = jnp.sin(fr * (hdot(a, w3_ref[...]) + b3_ref[...]))

    t = ft_ref[:, 0:1]
    filt = (hdot(a_ref[...], w4_ref[...]) + b4_ref[...]) * jnp.exp(-t * jnp.abs(dec_ref[...]))
    row = lax.broadcasted_iota(jnp.int32, filt.shape, 0)
    o_ref[...] = jnp.where(jnp.logical_and(row == 0, j >= back_from), 0.0, filt)


def hyena_filters(seq, w1, b1, w2, b2, w3, b3, w4, b4, freq, decay, tn=512):
    t = np.linspace(0.0, 1.0, seq)[:, None]
    w = (2.0 * math.pi / seq) * np.arange(seq)[:, None]
    bands = np.linspace(1e-4, HY_BANDS - 1, HY_BANDS)[None, :]
    feats = np.zeros((seq, LANE))
    feats[:, :1 + 2 * HY_BANDS] = np.concatenate([t, np.cos(bands * w), -np.sin(bands * w)], axis=-1)
    n = w4.shape[1]
    fw = HY_FILTER_W
    w1p = jnp.zeros((LANE, fw), F32).at[:w1.shape[0]].set(w1)
    row = lambda a: a.reshape(1, -1)
    full = lambda shape: pl.BlockSpec(shape, lambda j: (0, 0))
    return pl.pallas_call(
        functools.partial(_hy_filter_kernel, back_from=(n // 2) // tn),
        grid=(n // tn,),
        in_specs=[full((seq, LANE)), full((LANE, fw)), full((1, fw)), full((fw, fw)), full((1, fw)),
                  full((fw, fw)), full((1, fw)), full((1, fw)),
                  pl.BlockSpec((fw, tn), lambda j: (0, j)),
                  pl.BlockSpec((1, tn), lambda j: (0, j)),
                  pl.BlockSpec((1, tn), lambda j: (0, j))],
        out_specs=pl.BlockSpec((seq, tn), lambda j: (0, j)),
        out_shape=jax.ShapeDtypeStruct((seq, n), F32),
        scratch_shapes=[pltpu.VMEM((seq, fw), F32)],
        compiler_params=_params(("arbitrary",)),
        name="hyena_filters",
    )(jnp.asarray(feats, F32), w1p, row(b1), w2, row(b2), w3, row(b3), row(freq), w4, row(b4), row(decay))


def _dft_mats(seq, p):
    n = 2 * seq
    q = n // p
    k1n = q // 2 + 1
    hi, k1 = np.arange(q // 2), np.arange(k1n)
    th = 2 * np.pi * np.outer(k1, hi) / q
    ma = np.zeros((2 * k1n, q // 2))
    ma[0::2], ma[1::2] = np.cos(th), -np.sin(th)
    lo = np.arange(p)
    wf = np.zeros((k1n, 2 * p, 2 * p))
    wi = np.zeros((k1n, 2 * p, 2 * p))
    for k in k1:
        ph = -2 * np.pi * (np.outer(lo, lo) / p + k * lo[None, :] / n)
        er, ei = np.cos(ph), np.sin(ph)
        wf[k] = np.block([[er, -ei], [ei, er]])
        wi[k] = np.block([[er.T, ei.T], [-ei.T, er.T]])
    c = np.full(k1n, 2.0)
    c[0] = c[-1] = 1.0
    th2 = 2 * np.pi * np.outer(hi, k1) / q
    md = np.zeros((q // 2, 2 * k1n))
    md[:, 0::2], md[:, 1::2] = c * np.cos(th2) / n, -c * np.sin(th2) / n
    as_bf16 = lambda a: jnp.asarray(a, F32).astype(BF16)
    return ma, as_bf16(wf), as_bf16(wi), md


def _hy_short_kernel(x1_ref, x2_ref, v_ref, w_ref, b_ref, o1_ref, o2_ref, o3_ref):
    for k, (x_ref, o_ref) in enumerate(((x1_ref, o1_ref), (x2_ref, o2_ref), (v_ref, o3_ref))):
        x = x_ref[...].astype(F32)
        n = x.shape[0]
        row = lax.broadcasted_iota(jnp.int32, x.shape, 0)
        prev = jnp.where(row == 0, 0.0, pltpu.roll(x, 1, 0))
        nxt = jnp.where(row == n - 1, 0.0, pltpu.roll(x, n - 1, 0))
        w = w_ref[k]
        o_ref[...] = (prev * w[0:1] + x * w[1:2] + nxt * w[2:3] + b_ref[k]).astype(o_ref.dtype)


def hyena_short_conv(x3, conv_w, conv_b, *, rows, cb=256):
    b, seq = rows.batch, rows.seq
    d = x3.shape[1] // 3
    nblk = d // cb
    w = jnp.transpose(conv_w.reshape(HY_SHORT, 3, d), (1, 0, 2))
    bias = conv_b.reshape(3, 1, d)
    in_x = [pl.BlockSpec((seq, cb), lambda bi, j, k=k: (bi, k * nblk + j)) for k in range(3)]
    out = pl.BlockSpec((seq, cb), lambda bi, j: (bi, j))
    shape = jax.ShapeDtypeStruct((rows.n_lat, d), BF16)
    return pl.pallas_call(
        _hy_short_kernel,
        grid=(b, nblk),
        in_specs=in_x + [pl.BlockSpec((3, HY_SHORT, cb), lambda bi, j: (0, 0, j)),
                         pl.BlockSpec((3, 1, cb), lambda bi, j: (0, 0, j))],
        out_specs=[out, out, out],
        out_shape=[shape, shape, shape],
        compiler_params=_params(("parallel", "parallel")),
        name="hyena_short_conv",
    )(x3, x3, x3, w, bias)


def _hy_stage_a_kernel(x_ref, m_ref, o_ref):
    qh, g, wc = x_ref.shape
    x = x_ref[...].reshape(qh * g, wc).astype(BF16)
    o_ref[...] = _dot(m_ref[...], x).reshape(o_ref.shape).astype(o_ref.dtype)


def hyena_stage_a(x, mak, *, p, width=2048):
    bx, seq, c = x.shape
    g = HY_G
    qh = seq // p
    mo = mak.shape[0] // g
    x5 = x.reshape(bx, qh, p // g, g, c)
    return pl.pallas_call(
        _hy_stage_a_kernel,
        grid=(bx, p // g, c // width),
        in_specs=[pl.BlockSpec((None, qh, None, g, width), lambda bi, l, j: (bi, 0, l, 0, j)),
                  pl.BlockSpec(mak.shape, lambda bi, l, j: (0, 0))],
        out_specs=pl.BlockSpec((None, mo, g, width), lambda bi, l, j: (bi, 0, l, j)),
        out_shape=jax.ShapeDtypeStruct((bx, mo, p, c), BF16),
        compiler_params=_params(("parallel", "parallel", "parallel")),
        name="hyena_stage_a",
    )(x5, mak)


def _hy_spec_kernel(a0_ref, a1_ref, wf_ref, o_ref):
    p = o_ref.shape[2]
    cb = o_ref.shape[3]
    wf = wf_ref[0]
    x0 = _dot(wf, a0_ref[0].reshape(2 * p, cb))
    x1 = _dot(wf, a1_ref[0].reshape(2 * p, cb))
    o_ref[0, 0] = x0[:p] + x1[:p]
    o_ref[0, 1] = x0[p:] - x1[p:]


def hyena_spectrum(af, wf, *, d, cb=512):
    k1n, _, p, _ = af.shape
    nblk = (HY_ORDER * d) // cb
    return pl.pallas_call(
        _hy_spec_kernel,
        grid=(nblk, k1n),
        in_specs=[pl.BlockSpec((1, 2, p, cb), lambda j, k: (k, 0, 0, j)),
                  pl.BlockSpec((1, 2, p, cb), lambda j, k: (k, 0, 0, nblk + j)),
                  pl.BlockSpec((1, 2 * p, 2 * p), lambda j, k: (k, 0, 0))],
        out_specs=pl.BlockSpec((1, 2, p, cb), lambda j, k: (k, 0, 0, j)),
        out_shape=jax.ShapeDtypeStruct((k1n, 2, p, HY_ORDER * d), F32),
        compiler_params=_params(("parallel", "parallel")),
        name="hyena_spectrum",
    )(af, af, wf)


def _hy_mid_kernel(a_ref, wf_ref, wi_ref, h_ref, o_ref):
    b, _, p, cb = a_ref.shape
    hr, hi = h_ref[0], h_ref[1]
    for bi in range(b):
        x = _dot(wf_ref[...], a_ref[bi].reshape(2 * p, cb))
        xr, xi = x[:p], x[p:]
        y = jnp.concatenate([xr * hr - xi * hi, xr * hi + xi * hr], axis=0).astype(BF16)
        o_ref[bi] = _dot(wi_ref[...], y).reshape(2, p, cb).astype(o_ref.dtype)


def hyena_mid(a, wf, wi, spec, *, order, d, cb=512):
    b, k1n, _, p, _ = a.shape
    nblk = d // cb
    data = pl.BlockSpec((b, None, 2, p, cb), lambda j, k: (0, k, 0, 0, j))
    mat = pl.BlockSpec((None, 2 * p, 2 * p), lambda j, k: (k, 0, 0))
    return pl.pallas_call(
        _hy_mid_kernel,
        grid=(nblk, k1n),
        in_specs=[data, mat, mat,
                  pl.BlockSpec((None, 2, p, cb), lambda j, k: (k, 0, 0, order * nblk + j))],
        out_specs=data,
        out_shape=jax.ShapeDtypeStruct(a.shape, BF16),
        compiler_params=_params(("parallel", "parallel")),
        name="hyena_mid",
    )(a, wf, wi, spec)


def _hy_stage_d_kernel(c_ref, m_ref, x_ref, z_ref, skip_ref, o_ref):
    mo, g, wc = c_ref.shape
    qh = x_ref.shape[0]
    conv = _dot(m_ref[...], c_ref[...].reshape(mo * g, wc))
    z = z_ref[...].reshape(qh * g, wc).astype(F32)
    x = x_ref[...].reshape(qh * g, wc).astype(F32)
    o_ref[...] = (x * (conv + z * skip_ref[...])).reshape(o_ref.shape).astype(o_ref.dtype)


def hyena_stage_d(c, mdk, xg, z, skip, *, p, width=2048):
    b, mo, _, d = c.shape
    g = HY_G
    seq = xg.shape[1]
    qh = seq // p
    v5 = lambda a: a.reshape(b, qh, p // g, g, d)
    tspec = pl.BlockSpec((None, qh, None, g, width), lambda bi, l, j: (bi, 0, l, 0, j))
    out = pl.pallas_call(
        _hy_stage_d_kernel,
        grid=(b, p // g, d // width),
        in_specs=[pl.BlockSpec((None, mo, g, width), lambda bi, l, j: (bi, 0, l, j)),
                  pl.BlockSpec(mdk.shape, lambda bi, l, j: (0, 0)),
                  tspec, tspec,
                  pl.BlockSpec((1, width), lambda bi, l, j: (0, j))],
        out_specs=tspec,
        out_shape=jax.ShapeDtypeStruct((b, qh, p // g, g, d), BF16),
        compiler_params=_params(("parallel", "parallel", "parallel")),
        name="hyena_stage_d",
    )(c, mdk, v5(xg), v5(z), skip.reshape(1, d))
    return out.reshape(b, seq, d)


def hyena_long_convs(x1, x2, v, filt, skip, *, rows):
    b, seq = rows.batch, rows.seq
    d = x1.shape[1]
    p = HY_P
    k1n = seq // p + 1
    ma, wf, wi, md = _dft_mats(seq, p)
    eye = np.eye(HY_G)
    as_bf16 = lambda a: jnp.asarray(a, F32).astype(BF16)
    mak, mdk = as_bf16(np.kron(ma, eye)), as_bf16(np.kron(md, eye))
    af = hyena_stage_a(filt.reshape(1, seq, filt.shape[1]), mak, p=p)
    spec = hyena_spectrum(af.reshape(k1n, 2, p, filt.shape[1]), wf, d=d)
    t3 = lambda a: a.reshape(b, seq, d)
    y = t3(v)
    for order, xg in enumerate((x1, x2)):
        a = hyena_stage_a(y, mak, p=p).reshape(b, k1n, 2, p, d)
        c = hyena_mid(a, wf, wi, spec, order=order, d=d)
        y = hyena_stage_d(c.reshape(b, 2 * k1n, p, d), mdk, t3(xg), y, skip[order], p=p)
    return y.reshape(rows.n_lat, d)


def _mlstm_gate_weights(w_in, gate_b):
    h = ML_HEADS
    o4 = 2 * h * ML_DQK + 2 * h * ML_DV
    wg = w_in[:, o4:].reshape(-1, 2, 2, h)
    bg = gate_b.reshape(2, 2, h)
    w_out = jnp.zeros((w_in.shape[0], 4 * LANE), F32)
    b_out = jnp.zeros((1, 4 * LANE), F32)
    for d in range(2):
        for gate in range(2):
            lo = (2 * d + gate) * LANE
            w_out = w_out.at[:, lo:lo + h].set(wg[:, gate, d])
            b_out = b_out.at[0, lo:lo + h].set(bg[gate, d])
    return w_out, b_out


def _mla_weights(w_in, w_uq):
    r2 = MLA_Q_RANK + MLA_KV_RANK
    half = MLA_ROPE // 2

    def spread(w):
        z = jnp.zeros(w.shape[:-1] + (half,), w.dtype)
        return jnp.concatenate([w[..., :half], z, w[..., half:], z], axis=-1)

    w_in_p = jnp.concatenate([w_in[:, :r2], spread(w_in[:, r2:])], axis=1)
    wq = w_uq.reshape(w_uq.shape[0], MLA_HEADS, MLA_NOPE + MLA_ROPE)
    wq_p = jnp.concatenate([wq[..., :MLA_NOPE], spread(wq[..., MLA_NOPE:])], axis=-1)
    wq_p = wq_p * (MLA_NOPE + MLA_ROPE) ** -0.5
    return w_in_p, wq_p.reshape(w_uq.shape[0], MLA_HEADS * 2 * LANE)


def kernel(x, c, ctx, c_ctx, ada_w, ada_b, norm_g, mlp_w1, mlp_w2, swa_w_qkv, swa_sink, swa_w_o, ml_w_in, ml_gate_b, ml_head_g, ml_w_o, mla_w_in, mla_q_g, mla_kv_g, mla_w_uq, mla_w_ukv, mla_w_o, hy_w_in, hy_conv_w, hy_conv_b, hy_f_w1, hy_f_b1, hy_f_w2, hy_f_b2, hy_f_w3, hy_f_b3, hy_f_w4, hy_f_b4, hy_f_freq, hy_decay, hy_skip, hy_w_o):
    b, seq, d = x.shape
    ctx_len = ctx.shape[1]
    depth = ada_w.shape[0]
    assert depth == 4 and b < 8
    rows = Rows(b, seq, ctx_len)
    tm = 512
    tm_lin = 1024 if rows.n_lat % 1024 == 0 and rows.n_ctx % 1024 == 0 else 512
    bf = lambda w: w.astype(BF16)
    mlp_w1b, mlp_w2b = bf(mlp_w1), bf(mlp_w2)

    cond = jnp.zeros((8, d), F32).at[:b].set(c).at[b].set(c_ctx)
    mods = ada_mods(cond, ada_w, ada_b).reshape(depth * 8, 1, N_MOD * d)
    h = jnp.concatenate([x.reshape(rows.n_lat, d), ctx.reshape(rows.n_ctx, d)], axis=0)

    common = dict(rows=rows, tm=tm_lin, mods=mods)

    def finish(h, o_args, w_o, layer, n_rows, mode="plain"):
        h = outproj_residual(o_args, bf(w_o), h, mods, norm_g[layer, 1], rows=rows, n_rows=n_rows, tm=tm,
                             layer=layer, mode=mode, name=f"outproj{layer}")
        return mlp_residual(h, mlp_w1b, mlp_w2b, mods, norm_g[layer, 2], norm_g[layer, 3],
                            rows=rows, n_rows=n_rows, tm=tm, tf=1024, layer=layer)

    n_qk_groups = SWA_HEADS + SWA_KV_HEADS
    qkv = linear(h, bf(swa_w_qkv[0]), n_rows=rows.n_all, tn=512, out_dtype=BF16, prologue="norm_mod",
                 gain=norm_g[0, 0], layer=0, rope_tabs=_rope_tables(seq, SWA_HEAD_DIM), rope_pattern=(True,) * 4,
                 rope_jmax=n_qk_groups // 4, name="swa_qkv", **common)
    o = swa_attention(qkv, swa_sink[0], rows=rows)
    h = finish(h, o, swa_w_o[0], 0, rows.n_all)

    w_gate, b_gate = _mlstm_gate_weights(ml_w_in[0], ml_gate_b[0])
    z = linear(h, bf(ml_w_in[0]), n_rows=rows.n_all, tn=1024, out_dtype=BF16, prologue="norm_mod",
               gain=norm_g[1, 0], layer=1, n_out=2 * ML_HEADS * (ML_DQK + ML_DV), name="mlstm_in", **common)
    gates = linear(h, bf(w_gate), n_rows=rows.n_all, tn=4 * LANE, out_dtype=F32, prologue="norm_mod",
                   gain=norm_g[1, 0], layer=1, name="mlstm_gates", **common)
    hsum = mlstm_scan(z, gates, b_gate, rows=rows)
    h = finish(h, (hsum, z, ml_head_g[0]), ml_w_o[0], 1, rows.n_all, mode="mlstm")

    w_in_p, w_uq_p = _mla_weights(mla_w_in[0], mla_w_uq[0])
    rope_mla = _rope_tables(seq, MLA_ROPE)
    zc = linear(h, bf(w_in_p), n_rows=rows.n_all, tn=w_in_p.shape[1], out_dtype=BF16, prologue="norm_mod",
                gain=norm_g[2, 0], layer=2, rope_tabs=rope_mla, rope_pattern=(False,) * 8 + (True,),
                name="mla_in", **common)
    q = linear(zc, bf(w_uq_p), n_rows=rows.n_lat, tn=512, out_dtype=BF16, x_cols=0, prologue="norm",
               gain=mla_q_g[0], rope_tabs=rope_mla, rope_pattern=(False, True) * 2, name="mla_q",
               rows=rows, tm=tm_lin)
    kv = linear(zc, bf(mla_w_ukv[0]), n_rows=rows.n_all, tn=1024, out_dtype=BF16, x_cols=1, prologue="norm",
                gain=mla_kv_g[0], name="mla_kv", rows=rows, tm=tm_lin)
    o = mla_attention(q, kv, zc, rows=rows)
    h = finish(h, o, mla_w_o[0], 2, rows.n_lat)

    x3 = linear(h, bf(hy_w_in[0]), n_rows=rows.n_lat, tn=1024, out_dtype=BF16, prologue="norm_mod",
                gain=norm_g[3, 0], layer=3, name="hyena_in", **common)
    x1, x2, v = hyena_short_conv(x3, hy_conv_w[0], hy_conv_b[0], rows=rows)
    filt = hyena_filters(seq, hy_f_w1[0], hy_f_b1[0], hy_f_w2[0], hy_f_b2[0], hy_f_w3[0], hy_f_b3[0],
                         hy_f_w4[0], hy_f_b4[0], hy_f_freq[0], hy_decay[0])
    y = hyena_long_convs(x1, x2, v, filt, hy_skip[0], rows=rows)
    h = finish(h, y, hy_w_o[0], 3, rows.n_lat)
    return h.reshape(b, seq, d)
```

```python
import functools
import math

import numpy as np
import jax
import jax.numpy as jnp
from jax import lax
from jax.experimental import pallas as pl
from jax.experimental.pallas import tpu as pltpu

F32, BF16 = jnp.float32, jnp.bfloat16
HIGHEST = lax.Precision.HIGHEST

RMS_EPS = 1e-6
ROPE_THETA = 10000.0
GRID_W = 64
N_MOD = 6
LANE = 128

SWA_HEADS, SWA_KV_HEADS, SWA_HEAD_DIM, SWA_WINDOW, SWA_BLOCK = 16, 4, 128, 128, 128
ML_HEADS, ML_DQK, ML_DV = 8, 128, 256
ML_CHUNK = 256
MLA_HEADS, MLA_Q_RANK, MLA_KV_RANK, MLA_NOPE, MLA_ROPE, MLA_V = 16, 512, 512, 128, 64, 128
HY_ORDER, HY_BANDS, HY_FILTER_W, HY_SHORT = 2, 16, 64, 3
HY_P = 256
HY_G = 16

VMEM_LIMIT = 48 * 1024 * 1024


def _params(sem):
    return pltpu.CompilerParams(dimension_semantics=sem, vmem_limit_bytes=VMEM_LIMIT)


def _dot(a, b):
    return jnp.dot(a, b, preferred_element_type=F32)


def _dot_nt(a, b):
    return lax.dot_general(a, b, (((1,), (1,)), ((), ())), preferred_element_type=F32)


def _dot_tn(a, b):
    return lax.dot_general(a, b, (((0,), (0,)), ((), ())), preferred_element_type=F32)


def _rms(x, g):
    return x * lax.rsqrt(jnp.mean(x * x, axis=-1, keepdims=True) + RMS_EPS) * g


def _rms_ref(x_ref, inv_ref, g):
    x = x_ref[...].astype(F32)
    inv_ref[...] = lax.rsqrt(jnp.mean(x * x, axis=-1, keepdims=True) + RMS_EPS)
    return x_ref[...].astype(F32) * inv_ref[...] * g


def _ada_kernel(s_ref, w_ref, b_ref, o_ref):
    s = s_ref[...]
    s = s * jax.nn.sigmoid(s)
    o_ref[0] = jnp.dot(s, w_ref[0], preferred_element_type=F32, precision=HIGHEST) + b_ref[0]


def ada_mods(cond, ada_w, ada_b, tn=1536):
    depth, d, n = ada_w.shape
    rows = cond.shape[0]
    return pl.pallas_call(
        _ada_kernel,
        grid=(depth, n // tn),
        in_specs=[
            pl.BlockSpec((rows, d), lambda l, j: (0, 0)),
            pl.BlockSpec((1, d, tn), lambda l, j: (l, 0, j)),
            pl.BlockSpec((1, 1, tn), lambda l, j: (l, 0, j)),
        ],
        out_specs=pl.BlockSpec((1, rows, tn), lambda l, j: (l, 0, j)),
        out_shape=jax.ShapeDtypeStruct((depth, rows, n), F32),
        compiler_params=_params(("parallel", "parallel")),
        name="ada_mods",
    )(cond, ada_w, ada_b.reshape(depth, 1, n))


class Rows:
    def __init__(self, batch, seq, ctx_len):
        self.batch, self.seq, self.ctx_len = batch, seq, ctx_len
        self.n_lat = batch * seq
        self.n_ctx = batch * ctx_len
        self.n_all = self.n_lat + self.n_ctx

    def mod_index(self, layer, k, tm):
        lat_blocks, per_batch = self.n_lat // tm, self.seq // tm

        def index(i, *_):
            b = jnp.where(i < lat_blocks, i // per_batch, self.batch)
            return (layer * 8 + b, 0, k)

        return index


def _linear_kernel(*refs, prologue, rope, n_lat_blocks, rope_pattern, rope_jmax):
    it = iter(refs)
    x_ref = next(it)
    g_ref = next(it) if prologue in ("norm", "norm_mod") else None
    sh_ref = next(it) if prologue == "norm_mod" else None
    sc_ref = next(it) if prologue == "norm_mod" else None
    w_ref = next(it)
    cc_ref = next(it) if rope else None
    ss_ref = next(it) if rope else None
    o_ref = next(it)
    u_ref = next(it)
    inv_ref = next(it)
    i, j = pl.program_id(0), pl.program_id(1)

    @pl.when(j == 0)
    def _():
        if prologue in ("norm", "norm_mod"):
            x = _rms_ref(x_ref, inv_ref, g_ref[...])
        else:
            x = x_ref[...].astype(F32)
        if prologue == "norm_mod":
            x = x * (1.0 + sc_ref[0]) + sh_ref[0]
        u_ref[...] = x.astype(BF16)

    acc = _dot(u_ref[...], w_ref[...])

    def plain():
        o_ref[...] = acc.astype(o_ref.dtype)

    def roped():
        cc, ss = cc_ref[...], ss_ref[...]
        segs = []
        for gi, on in enumerate(rope_pattern):
            seg = acc[:, gi * LANE:(gi + 1) * LANE]
            if on:
                seg = seg * cc + pltpu.roll(seg, LANE // 2, 1) * ss
            segs.append(seg)
        o_ref[...] = jnp.concatenate(segs, axis=1).astype(o_ref.dtype)

    if not rope:
        plain()
    else:
        cond = jnp.logical_and(i < n_lat_blocks, j < rope_jmax)
        pl.when(cond)(roped)
        pl.when(jnp.logical_not(cond))(plain)


def linear(x, w, *, rows, n_rows, tm, tn, out_dtype, x_cols=None, prologue="none", gain=None,
           mods=None, layer=0, mod_k=(0, 1), rope_tabs=None, rope_pattern=(), rope_jmax=1 << 30,
           n_out=None, name="linear"):
    k, n = w.shape
    n = n if n_out is None else n_out
    xc = 0 if x_cols is None else x_cols
    rope = rope_tabs is not None
    in_specs = [pl.BlockSpec((tm, k), lambda i, j: (i, xc))]
    args = [x]
    if prologue in ("norm", "norm_mod"):
        in_specs.append(pl.BlockSpec((1, k), lambda i, j: (0, 0)))
        args.append(gain.reshape(1, k))
    if prologue == "norm_mod":
        for mk in mod_k:
            in_specs.append(pl.BlockSpec((1, 1, k), rows.mod_index(layer, mk, tm)))
            args.append(mods)
    in_specs.append(pl.BlockSpec((k, tn), lambda i, j: (0, j)))
    args.append(w)
    if rope:
        per_seq = rows.seq // tm
        for t in rope_tabs:
            in_specs.append(pl.BlockSpec((tm, LANE), lambda i, j: (i % per_seq, 0)))
            args.append(t)
    kern = functools.partial(_linear_kernel, prologue=prologue, rope=rope,
                             n_lat_blocks=rows.n_lat // tm, rope_pattern=tuple(rope_pattern),
                             rope_jmax=rope_jmax)
    return pl.pallas_call(
        kern,
        grid=(n_rows // tm, n // tn),
        in_specs=in_specs,
        out_specs=pl.BlockSpec((tm, tn), lambda i, j: (i, j)),
        out_shape=jax.ShapeDtypeStruct((n_rows, n), out_dtype),
        scratch_shapes=[pltpu.VMEM((tm, k), BF16), pltpu.VMEM((tm, 1), F32)],
        compiler_params=_params(("parallel", "arbitrary")),
        name=name,
    )(*args)


def _outproj_kernel(*refs, mode):
    it = iter(refs)
    if mode == "mlstm":
        hf_ref, hb_ref, og_ref, hg_ref = next(it), next(it), next(it), next(it)
    else:
        o_ref_in = next(it)
    w_ref, h_ref, gate_ref, g_ref, out_ref = next(it), next(it), next(it), next(it), next(it)
    y_ref, inv_ref = next(it), next(it)
    if mode == "mlstm":
        hs = hf_ref[0].astype(F32) + hb_ref[0].astype(F32)
        og = jax.nn.sigmoid(og_ref[...].astype(F32))
        hg = hg_ref[...]
        parts = []
        for h in range(ML_HEADS):
            sl = slice(h * ML_DV, (h + 1) * ML_DV)
            parts.append((_rms(hs[:, sl], hg[:, sl]) * og[:, sl]).astype(BF16))
        o = jnp.concatenate(parts, axis=1)
    else:
        o = o_ref_in[...]
    y_ref[...] = _dot(o, w_ref[...])
    out_ref[...] = h_ref[...] + gate_ref[0] * _rms_ref(y_ref, inv_ref, g_ref[...])


def outproj_residual(o_args, w, h, mods, gain, *, rows, n_rows, tm, layer, mode="plain", name="outproj"):
    k, d = w.shape
    if mode == "mlstm":
        hsum, z, head_g = o_args
        in_specs = [
            pl.BlockSpec((1, tm, k), lambda i: (0, i, 0)),
            pl.BlockSpec((1, tm, k), lambda i: (1, i, 0)),
            pl.BlockSpec((tm, k), lambda i: (i, 2)),
            pl.BlockSpec((1, k), lambda i: (0, 0)),
        ]
        args = [hsum, hsum, z, head_g.reshape(1, k)]
    else:
        in_specs = [pl.BlockSpec((tm, k), lambda i: (i, 0))]
        args = [o_args]
    in_specs += [
        pl.BlockSpec((k, d), lambda i: (0, 0)),
        pl.BlockSpec((tm, d), lambda i: (i, 0)),
        pl.BlockSpec((1, 1, d), rows.mod_index(layer, 2, tm)),
        pl.BlockSpec((1, d), lambda i: (0, 0)),
    ]
    args += [w, h, mods, gain.reshape(1, d)]
    return pl.pallas_call(
        functools.partial(_outproj_kernel, mode=mode),
        grid=(n_rows // tm,),
        in_specs=in_specs,
        out_specs=pl.BlockSpec((tm, d), lambda i: (i, 0)),
        out_shape=jax.ShapeDtypeStruct((n_rows, d), F32),
        scratch_shapes=[pltpu.VMEM((tm, d), F32), pltpu.VMEM((tm, 1), F32)],
        compiler_params=_params(("parallel",)),
        name=name,
    )(*args)


def _mlp_kernel(h_ref, g2_ref, sh_ref, sc_ref, w1_ref, w2_ref, gate_ref, g3_ref, out_ref, v_ref, acc_ref,
                inv_ref):
    f = pl.program_id(1)

    @pl.when(f == 0)
    def _():
        v = _rms_ref(h_ref, inv_ref, g2_ref[...]) * (1.0 + sc_ref[0]) + sh_ref[0]
        v_ref[...] = v.astype(BF16)
        acc_ref[...] = jnp.zeros_like(acc_ref)

    a = jnp.maximum(_dot(v_ref[...], w1_ref[...]), 0.0)
    acc_ref[...] += _dot((a * a).astype(BF16), w2_ref[...])

    @pl.when(f == pl.num_programs(1) - 1)
    def _():
        out_ref[...] = h_ref[...] + gate_ref[0] * _rms_ref(acc_ref, inv_ref, g3_ref[...])


def mlp_residual(h, w1, w2, mods, g2, g3, *, rows, n_rows, tm, tf, layer):
    _, d, ff = w1.shape
    return pl.pallas_call(
        _mlp_kernel,
        grid=(n_rows // tm, ff // tf),
        in_specs=[
            pl.BlockSpec((tm, d), lambda i, f: (i, 0)),
            pl.BlockSpec((1, d), lambda i, f: (0, 0)),
            pl.BlockSpec((1, 1, d), rows.mod_index(layer, 3, tm)),
            pl.BlockSpec((1, 1, d), rows.mod_index(layer, 4, tm)),
            pl.BlockSpec((None, d, tf), lambda i, f: (layer, 0, f)),
            pl.BlockSpec((None, tf, d), lambda i, f: (layer, f, 0)),
            pl.BlockSpec((1, 1, d), rows.mod_index(layer, 5, tm)),
            pl.BlockSpec((1, d), lambda i, f: (0, 0)),
        ],
        out_specs=pl.BlockSpec((tm, d), lambda i, f: (i, 0)),
        out_shape=jax.ShapeDtypeStruct((n_rows, d), F32),
        scratch_shapes=[pltpu.VMEM((tm, d), BF16), pltpu.VMEM((tm, d), F32), pltpu.VMEM((tm, 1), F32)],
        compiler_params=_params(("parallel", "arbitrary")),
        name="mlp",
    )(h, g2.reshape(1, d), mods, mods, w1, w2, mods, g3.reshape(1, d))


def _rope_tables(seq, d_rot):
    pos = np.arange(seq)
    row, col = pos // GRID_W, pos % GRID_W
    n = d_rot // 4
    inv = ROPE_THETA ** (-np.arange(n, dtype=np.float64) / n)
    ang = np.concatenate([row[:, None] * inv, col[:, None] * inv], axis=-1)
    cos, sin = np.cos(ang), np.sin(ang)
    half = d_rot // 2
    cc = np.zeros((seq, LANE))
    ss = np.zeros((seq, LANE))
    cc[:, :half] = cos
    cc[:, LANE // 2:LANE // 2 + half] = cos
    ss[:, :half] = -sin
    ss[:, LANE // 2:LANE // 2 + half] = sin
    return jnp.asarray(cc, F32), jnp.asarray(ss, F32)


def _swa_kernel(sink_ref, q_ref, kp_ref, kc_ref, kn_ref, kx_ref, vp_ref, vc_ref, vn_ref, vx_ref, o_ref,
                *, nb, seq, ctx_len):
    n = pl.program_id(1)
    hd, grp, lb = SWA_HEAD_DIM, SWA_HEADS // SWA_KV_HEADS, SWA_BLOCK
    m_rows, n_loc = grp * lb, 3 * lb
    r = lax.broadcasted_iota(jnp.int32, (m_rows, n_loc + ctx_len), 0)
    c = lax.broadcasted_iota(jnp.int32, (m_rows, n_loc + ctx_len), 1)
    qpos = n * lb + (r & (lb - 1))
    kpos = (n - 1) * lb + c
    local_ok = (jnp.abs(kpos - qpos) <= SWA_WINDOW) & (kpos >= 0) & (kpos < seq) & (n < nb)
    valid = local_ok | (c >= n_loc)
    ones = jnp.ones((n_loc + ctx_len, hd), BF16)
    outs = []
    for kv in range(SWA_KV_HEADS):
        ks = slice(kv * hd, (kv + 1) * hd)
        qg = jnp.concatenate([q_ref[:, (kv * grp + g) * hd:(kv * grp + g + 1) * hd] for g in range(grp)], axis=0)
        keys = jnp.concatenate([kp_ref[:, ks], kc_ref[:, ks], kn_ref[:, ks], kx_ref[:, ks]], axis=0)
        vals = jnp.concatenate([vp_ref[:, ks], vc_ref[:, ks], vn_ref[:, ks], vx_ref[:, ks]], axis=0)
        v_aug = jnp.concatenate([vals, ones], axis=1)
        s = jnp.where(valid, _dot_nt(qg, keys), -1e30)
        for g in range(grp):
            sg = s[g * lb:(g + 1) * lb]
            snk = sink_ref[kv * grp + g]
            m = jnp.maximum(jnp.max(sg, axis=1, keepdims=True), snk)
            ov = _dot(jnp.exp(sg - m).astype(BF16), v_aug)
            outs.append((ov[:, :hd] / (ov[:, hd:hd + 1] + jnp.exp(snk - m))).astype(BF16))
    o_ref[...] = jnp.concatenate(outs, axis=1)


def swa_attention(qkv, sink, *, rows):
    b, seq, ctx_len = rows.batch, rows.seq, rows.ctx_len
    lb = SWA_BLOCK
    nb, ncb = seq // lb, ctx_len // lb
    lat_blocks = rows.n_lat // lb
    qw = SWA_HEADS * SWA_HEAD_DIM
    kw = SWA_KV_HEADS * SWA_HEAD_DIM
    kcol, vcol = qw // kw, qw // kw + 1

    def qidx(bi, n):
        return (jnp.where(n < nb, bi * nb + n, lat_blocks + bi * ncb + (n - nb)), 0)

    def kidx(off, col):
        return lambda bi, n: (bi * nb + jnp.clip(n + off, 0, nb - 1), col)

    def xidx(col):
        return lambda bi, n: (rows.n_lat // ctx_len + bi, col)

    kern = functools.partial(_swa_kernel, nb=nb, seq=seq, ctx_len=ctx_len)
    return pl.pallas_call(
        kern,
        grid=(b, nb + ncb),
        in_specs=[
            pl.BlockSpec(memory_space=pltpu.SMEM),
            pl.BlockSpec((lb, qw), qidx),
            pl.BlockSpec((lb, kw), kidx(-1, kcol)),
            pl.BlockSpec((lb, kw), kidx(0, kcol)),
            pl.BlockSpec((lb, kw), kidx(1, kcol)),
            pl.BlockSpec((ctx_len, kw), xidx(kcol)),
            pl.BlockSpec((lb, kw), kidx(-1, vcol)),
            pl.BlockSpec((lb, kw), kidx(0, vcol)),
            pl.BlockSpec((lb, kw), kidx(1, vcol)),
            pl.BlockSpec((ctx_len, kw), xidx(vcol)),
        ],
        out_specs=pl.BlockSpec((lb, qw), qidx),
        out_shape=jax.ShapeDtypeStruct((rows.n_all, qw), BF16),
        compiler_params=_params(("parallel", "parallel")),
        name="swa_attention",
    )(sink, qkv, qkv, qkv, qkv, qkv, qkv, qkv, qkv, qkv)


def _mlstm_kernel(q_ref, k_ref, v_ref, gi_ref, gf_ref, bi_ref, bf_ref, o_ref, c_ref, m_ref):
    d, c = pl.program_id(1), pl.program_id(2)
    L, H, dqk, dv = ML_CHUNK, ML_HEADS, ML_DQK, ML_DV
    scale = dqk ** -0.5

    @pl.when(c == 0)
    def _():
        c_ref[...] = jnp.zeros_like(c_ref)
        m_ref[...] = jnp.zeros_like(m_ref)

    row = lax.broadcasted_iota(jnp.int32, (L, L), 0)
    col = lax.broadcasted_iota(jnp.int32, (L, L), 1)
    sign = 1 - 2 * d
    mask = (col - row) * sign <= 0
    ones_mask = jnp.where(mask, 1.0, 0.0).astype(BF16)
    i_blk = gi_ref[...] + bi_ref[...]
    f_blk = jax.nn.log_sigmoid(gf_ref[...] + bf_ref[...])
    f_hi = f_blk.astype(BF16)
    r1 = f_blk - f_hi.astype(F32)
    f_mid = r1.astype(BF16)
    f_lo = (r1 - f_mid.astype(F32)).astype(BF16)
    b_blk = _dot(jnp.concatenate([ones_mask] * 3, axis=1), jnp.concatenate([f_hi, f_mid, f_lo], axis=0))
    b_end = jnp.sum(f_blk, axis=0, keepdims=True)
    e_rows = (i_blk - b_blk).T
    m_prev_blk = m_ref[...]
    dec = b_end - b_blk + i_blk
    m_new_blk = jnp.maximum(b_end + m_prev_blk, jnp.max(dec, axis=0, keepdims=True))
    ws_blk = jnp.exp(dec - m_new_blk)
    gs_blk = jnp.exp(b_end + m_prev_blk - m_new_blk)
    ones = jnp.ones((L, LANE), BF16)
    for h in range(H):
        qh = q_ref[:, h * dqk:(h + 1) * dqk]
        kh = k_ref[:, h * dqk:(h + 1) * dqk]
        v_aug = jnp.concatenate([v_ref[:, h * dv:(h + 1) * dv], ones], axis=1)
        e = jnp.where(mask, e_rows[h:h + 1, :], -jnp.inf)
        m_prev = m_prev_blk[:, h:h + 1]
        mm = jnp.maximum(m_prev, jnp.max(e, axis=1, keepdims=True))
        s = (_dot_nt(qh, kh) * jnp.exp(e - (mm - math.log(scale)))).astype(BF16)
        g = jnp.exp(m_prev - mm) * scale
        ct = c_ref[h]
        lhs = jnp.concatenate([s, (qh.astype(F32) * g).astype(BF16)], axis=1)
        res = _dot(lhs, jnp.concatenate([v_aug, ct.astype(BF16)], axis=0))
        floor = jnp.exp(-(b_blk[:, h:h + 1] + mm))
        hout = res[:, :dv] / jnp.maximum(jnp.abs(res[:, dv:dv + 1]), floor)
        o_ref[0, :, h * dv:(h + 1) * dv] = hout.astype(o_ref.dtype)
        kw = (kh.astype(F32) * ws_blk[:, h:h + 1]).astype(BF16)
        c_ref[h] = gs_blk[:, h:h + 1] * ct + _dot_tn(kw, v_aug)
    m_ref[...] = m_new_blk


def mlstm_scan(z, gates, gate_b, *, rows):
    b, seq, ctx_len = rows.batch, rows.seq, rows.ctx_len
    L = ML_CHUNK
    assert ctx_len == L and seq % L == 0
    nlc = seq // L
    hq, hv = ML_HEADS * ML_DQK, ML_HEADS * ML_DV

    def rb(bi, d, c):
        lat = bi * nlc + jnp.where(d == 0, c - 1, nlc - c)
        return jnp.where(c == 0, rows.n_lat // L + bi, lat)

    return pl.pallas_call(
        _mlstm_kernel,
        grid=(b, 2, nlc + 1),
        in_specs=[
            pl.BlockSpec((L, hq), lambda bi, d, c: (rb(bi, d, c), 0)),
            pl.BlockSpec((L, hq), lambda bi, d, c: (rb(bi, d, c), 1)),
            pl.BlockSpec((L, hv), lambda bi, d, c: (rb(bi, d, c), 1)),
            pl.BlockSpec((L, LANE), lambda bi, d, c: (rb(bi, d, c), 2 * d)),
            pl.BlockSpec((L, LANE), lambda bi, d, c: (rb(bi, d, c), 2 * d + 1)),
            pl.BlockSpec((1, LANE), lambda bi, d, c: (0, 2 * d)),
            pl.BlockSpec((1, LANE), lambda bi, d, c: (0, 2 * d + 1)),
        ],
        out_specs=pl.BlockSpec((1, L, hv), lambda bi, d, c: (d, rb(bi, d, c), 0)),
        out_shape=jax.ShapeDtypeStruct((2, rows.n_all, hv), BF16),
        scratch_shapes=[
            pltpu.VMEM((ML_HEADS, ML_DQK, ML_DV + LANE), F32),
            pltpu.VMEM((1, LANE), F32),
        ],
        compiler_params=_params(("parallel", "parallel", "arbitrary")),
        name="mlstm_scan",
    )(z, z, z, gates, gates, gate_b, gate_b)


def _mla_kernel(q_ref, kvl_ref, kvx_ref, krl_ref, krx_ref, o_ref, kk_ref, vt_ref, *, seq, sub, ck):
    @pl.when(pl.program_id(2) == 0)
    def _():
        kk_ref[:seq, :LANE] = kvl_ref[:, :LANE]
        kk_ref[:seq, LANE:] = krl_ref[...]
        kk_ref[seq:, :LANE] = kvx_ref[:, :LANE]
        kk_ref[seq:, LANE:] = krx_ref[...]
        vt_ref[:LANE, :seq] = kvl_ref[:, LANE:].astype(F32).T.astype(BF16)
        vt_ref[:LANE, seq:] = kvx_ref[:, LANE:].astype(F32).T.astype(BF16)
        vt_ref[LANE:, :] = jnp.ones((vt_ref.shape[0] - LANE, vt_ref.shape[1]), BF16)

    nsub = q_ref.shape[0] // sub
    n_keys = kk_ref.shape[0]
    bounds = [(k0, min(k0 + ck, n_keys)) for k0 in range(0, n_keys, ck)]
    nck = len(bounds)
    qs = [q_ref[s * sub:(s + 1) * sub, :] for s in range(nsub)]

    def scores(s, c):
        return _dot_nt(kk_ref[bounds[c][0]:bounds[c][1], :], qs[s])

    m = [None] * nsub
    acc = [None] * nsub
    st_next = [scores(s, 0) for s in range(nsub)]
    for c in range(nck):
        st = st_next
        if c + 1 < nck:
            st_next = [scores(s, c + 1) for s in range(nsub)]
        vt_c = vt_ref[:, bounds[c][0]:bounds[c][1]]
        for s in range(nsub):
            cm = jnp.max(st[s], axis=0, keepdims=True)
            m_new = cm if c == 0 else jnp.maximum(m[s], cm)
            pv = _dot(vt_c, jnp.exp(st[s] - m_new).astype(BF16))
            acc[s] = pv if c == 0 else acc[s] * jnp.exp(m[s] - m_new) + pv
            m[s] = m_new
    for s in range(nsub):
        o = acc[s][:LANE] / acc[s][LANE:LANE + 1]
        o_ref[s * sub:(s + 1) * sub, :] = o.T.astype(o_ref.dtype)


def mla_attention(q, kv, z, *, rows, tq=512, sub=256, ck=1024):
    b, seq, ctx_len = rows.batch, rows.seq, rows.ctx_len
    nq = seq // tq
    kr_col = z.shape[1] // LANE - 1
    ctx0 = rows.n_lat // ctx_len
    n_keys = seq + ctx_len
    ones_rows = 16
    return pl.pallas_call(
        functools.partial(_mla_kernel, seq=seq, sub=sub, ck=ck),
        grid=(b, MLA_HEADS, nq),
        in_specs=[
            pl.BlockSpec((tq, 2 * LANE), lambda bi, h, i: (bi * nq + i, h)),
            pl.BlockSpec((seq, 2 * LANE), lambda bi, h, i: (bi, h)),
            pl.BlockSpec((ctx_len, 2 * LANE), lambda bi, h, i: (ctx0 + bi, h)),
            pl.BlockSpec((seq, LANE), lambda bi, h, i: (bi, kr_col)),
            pl.BlockSpec((ctx_len, LANE), lambda bi, h, i: (ctx0 + bi, kr_col)),
        ],
        out_specs=pl.BlockSpec((tq, LANE), lambda bi, h, i: (bi * nq + i, h)),
        out_shape=jax.ShapeDtypeStruct((rows.n_lat, MLA_HEADS * MLA_V), BF16),
        scratch_shapes=[pltpu.VMEM((n_keys, 2 * LANE), BF16), pltpu.VMEM((LANE + ones_rows, n_keys), BF16)],
        compiler_params=_params(("parallel", "parallel", "arbitrary")),
        name="mla_attention",
    )(q, kv, kv, z, z)


def _hy_filter_kernel(ft_ref, w1_ref, b1_ref, w2_ref, b2_ref, w3_ref, b3_ref, fr_ref, w4_ref, b4_ref, dec_ref,
                      o_ref, a_ref, *, back_from):
    j = pl.program_id(0)
    hdot = functools.partial(jnp.dot, preferred_element_type=F32, precision=HIGHEST)

    @pl.when(j == 0)
    def _():
        fr = fr_ref[...]
        a = jnp.sin(fr * (hdot(ft_ref[...], w1_ref[...]) + b1_ref[...]))
        a = jnp.sin(fr * (hdot(a, w2_ref[...]) + b2_ref[...]))
        a_ref[...] = jnp.sin(fr * (hdot(a, w3_ref[...]) + b3_ref[...]))

    t = ft_ref[:, 0:1]
    filt = (hdot(a_ref[...], w4_ref[...]) + b4_ref[...]) * jnp.exp(-t * jnp.abs(dec_ref[...]))
    row = lax.broadcasted_iota(jnp.int32, filt.shape, 0)
    o_ref[...] = jnp.where(jnp.logical_and(row == 0, j >= back_from), 0.0, filt)


def hyena_filters(seq, w1, b1, w2, b2, w3, b3, w4, b4, freq, decay, tn=512):
    t = np.linspace(0.0, 1.0, seq)[:, None]
    w = (2.0 * math.pi / seq) * np.arange(seq)[:, None]
    bands = np.linspace(1e-4, HY_BANDS - 1, HY_BANDS)[None, :]
    feats = np.zeros((seq, LANE))
    feats[:, :1 + 2 * HY_BANDS] = np.concatenate([t, np.cos(bands * w), -np.sin(bands * w)], axis=-1)
    n = w4.shape[1]
    fw = HY_FILTER_W
    w1p = jnp.zeros((LANE, fw), F32).at[:w1.shape[0]].set(w1)
    row = lambda a: a.reshape(1, -1)
    full = lambda shape: pl.BlockSpec(shape, lambda j: (0, 0))
    return pl.pallas_call(
        functools.partial(_hy_filter_kernel, back_from=(n // 2) // tn),
        grid=(n // tn,),
        in_specs=[full((seq, LANE)), full((LANE, fw)), full((1, fw)), full((fw, fw)), full((1, fw)),
                  full((fw, fw)), full((1, fw)), full((1, fw)),
                  pl.BlockSpec((fw, tn), lambda j: (0, j)),
                  pl.BlockSpec((1, tn), lambda j: (0, j)),
                  pl.BlockSpec((1, tn), lambda j: (0, j))],
        out_specs=pl.BlockSpec((seq, tn), lambda j: (0, j)),
        out_shape=jax.ShapeDtypeStruct((seq, n), F32),
        scratch_shapes=[pltpu.VMEM((seq, fw), F32)],
        compiler_params=_params(("arbitrary",)),
        name="hyena_filters",
    )(jnp.asarray(feats, F32), w1p, row(b1), w2, row(b2), w3, row(b3), row(freq), w4, row(b4), row(decay))


def _dft_mats(seq, p):
    n = 2 * seq
    q = n // p
    k1n = q // 2 + 1
    hi, k1 = np.arange(q // 2), np.arange(k1n)
    th = 2 * np.pi * np.outer(k1, hi) / q
    ma = np.zeros((2 * k1n, q // 2))
    ma[0::2], ma[1::2] = np.cos(th), -np.sin(th)
    lo = np.arange(p)
    wf = np.zeros((k1n, 2 * p, 2 * p))
    wi = np.zeros((k1n, 2 * p, 2 * p))
    for k in k1:
        ph = -2 * np.pi * (np.outer(lo, lo) / p + k * lo[None, :] / n)
        er, ei = np.cos(ph), np.sin(ph)
        wf[k] = np.block([[er, -ei], [ei, er]])
        wi[k] = np.block([[er.T, ei.T], [-ei.T, er.T]])
    c = np.full(k1n, 2.0)
    c[0] = c[-1] = 1.0
    th2 = 2 * np.pi * np.outer(hi, k1) / q
    md = np.zeros((q // 2, 2 * k1n))
    md[:, 0::2], md[:, 1::2] = c * np.cos(th2) / n, -c * np.sin(th2) / n
    as_bf16 = lambda a: jnp.asarray(a, F32).astype(BF16)
    return ma, as_bf16(wf), as_bf16(wi), md


def _hy_short_kernel(x1_ref, x2_ref, v_ref, w_ref, b_ref, o1_ref, o2_ref, o3_ref):
    for k, (x_ref, o_ref) in enumerate(((x1_ref, o1_ref), (x2_ref, o2_ref), (v_ref, o3_ref))):
        x = x_ref[...].astype(F32)
        n = x.shape[0]
        row = lax.broadcasted_iota(jnp.int32, x.shape, 0)
        prev = jnp.where(row == 0, 0.0, pltpu.roll(x, 1, 0))
        nxt = jnp.where(row == n - 1, 0.0, pltpu.roll(x, n - 1, 0))
        w = w_ref[k]
        o_ref[...] = (prev * w[0:1] + x * w[1:2] + nxt * w[2:3] + b_ref[k]).astype(o_ref.dtype)


def hyena_short_conv(x3, conv_w, conv_b, *, rows, cb=256):
    b, seq = rows.batch, rows.seq
    d = x3.shape[1] // 3
    nblk = d // cb
    w = jnp.transpose(conv_w.reshape(HY_SHORT, 3, d), (1, 0, 2))
    bias = conv_b.reshape(3, 1, d)
    in_x = [pl.BlockSpec((seq, cb), lambda bi, j, k=k: (bi, k * nblk + j)) for k in range(3)]
    out = pl.BlockSpec((seq, cb), lambda bi, j: (bi, j))
    shape = jax.ShapeDtypeStruct((rows.n_lat, d), BF16)
    return pl.pallas_call(
        _hy_short_kernel,
        grid=(b, nblk),
        in_specs=in_x + [pl.BlockSpec((3, HY_SHORT, cb), lambda bi, j: (0, 0, j)),
                         pl.BlockSpec((3, 1, cb), lambda bi, j: (0, 0, j))],
        out_specs=[out, out, out],
        out_shape=[shape, shape, shape],
        compiler_params=_params(("parallel", "parallel")),
        name="hyena_short_conv",
    )(x3, x3, x3, w, bias)


def _hy_stage_a_kernel(x_ref, m_ref, o_ref):
    qh, g, wc = x_ref.shape
    x = x_ref[...].reshape(qh * g, wc).astype(BF16)
    o_ref[...] = _dot(m_ref[...], x).reshape(o_ref.shape).astype(o_ref.dtype)


def hyena_stage_a(x, mak, *, p, width=2048):
    bx, seq, c = x.shape
    g = HY_G
    qh = seq // p
    mo = mak.shape[0] // g
    x5 = x.reshape(bx, qh, p // g, g, c)
    return pl.pallas_call(
        _hy_stage_a_kernel,
        grid=(bx, p // g, c // width),
        in_specs=[pl.BlockSpec((None, qh, None, g, width), lambda bi, l, j: (bi, 0, l, 0, j)),
                  pl.BlockSpec(mak.shape, lambda bi, l, j: (0, 0))],
        out_specs=pl.BlockSpec((None, mo, g, width), lambda bi, l, j: (bi, 0, l, j)),
        out_shape=jax.ShapeDtypeStruct((bx, mo, p, c), BF16),
        compiler_params=_params(("parallel", "parallel", "parallel")),
        name="hyena_stage_a",
    )(x5, mak)


def _hy_spec_kernel(a0_ref, a1_ref, wf_ref, o_ref):
    p = o_ref.shape[2]
    cb = o_ref.shape[3]
    wf = wf_ref[0]
    x0 = _dot(wf, a0_ref[0].reshape(2 * p, cb))
    x1 = _dot(wf, a1_ref[0].reshape(2 * p, cb))
    o_ref[0, 0] = x0[:p] + x1[:p]
    o_ref[0, 1] = x0[p:] - x1[p:]


def hyena_spectrum(af, wf, *, d, cb=1024):
    k1n, _, p, _ = af.shape
    nblk = (HY_ORDER * d) // cb
    return pl.pallas_call(
        _hy_spec_kernel,
        grid=(k1n, nblk),
        in_specs=[pl.BlockSpec((1, 2, p, cb), lambda k, j: (k, 0, 0, j)),
                  pl.BlockSpec((1, 2, p, cb), lambda k, j: (k, 0, 0, nblk + j)),
                  pl.BlockSpec((1, 2 * p, 2 * p), lambda k, j: (k, 0, 0))],
        out_specs=pl.BlockSpec((1, 2, p, cb), lambda k, j: (k, 0, 0, j)),
        out_shape=jax.ShapeDtypeStruct((k1n, 2, p, HY_ORDER * d), F32),
        compiler_params=_params(("parallel", "parallel")),
        name="hyena_spectrum",
    )(af, af, wf)


def _hy_mid_kernel(a_ref, wf_ref, wi_ref, h_ref, o_ref):
    b, _, p, cb = a_ref.shape
    hr, hi = h_ref[0], h_ref[1]
    for bi in range(b):
        x = _dot(wf_ref[...], a_ref[bi].reshape(2 * p, cb))
        xr, xi = x[:p], x[p:]
        y = jnp.concatenate([xr * hr - xi * hi, xr * hi + xi * hr], axis=0).astype(BF16)
        o_ref[bi] = _dot(wi_ref[...], y).reshape(2, p, cb).astype(o_ref.dtype)


def hyena_mid(a, wf, wi, spec, *, order, d, cb=512):
    b, k1n, _, p, _ = a.shape
    nblk = d // cb
    data = pl.BlockSpec((b, None, 2, p, cb), lambda k, j: (0, k, 0, 0, j))
    mat = pl.BlockSpec((None, 2 * p, 2 * p), lambda k, j: (k, 0, 0))
    return pl.pallas_call(
        _hy_mid_kernel,
        grid=(k1n, nblk),
        in_specs=[data, mat, mat,
                  pl.BlockSpec((None, 2, p, cb), lambda k, j: (k, 0, 0, order * nblk + j))],
        out_specs=data,
        out_shape=jax.ShapeDtypeStruct(a.shape, BF16),
        compiler_params=_params(("parallel", "parallel")),
        name="hyena_mid",
    )(a, wf, wi, spec)


def _hy_stage_d_kernel(c_ref, m_ref, x_ref, z_ref, skip_ref, *rest, fused):
    ma_ref, o_ref, a_ref = rest if fused else (None, rest[0], None)
    mo, g, wc = c_ref.shape
    qh = x_ref.shape[0]
    conv = _dot(m_ref[...], c_ref[...].reshape(mo * g, wc))
    z = z_ref[...].reshape(qh * g, wc).astype(F32)
    x = x_ref[...].reshape(qh * g, wc).astype(F32)
    y = (x * (conv + z * skip_ref[...])).astype(o_ref.dtype)
    o_ref[...] = y.reshape(o_ref.shape)
    if fused:
        a_ref[...] = _dot(ma_ref[...], y).reshape(a_ref.shape).astype(a_ref.dtype)


def hyena_stage_d(c, mdk, xg, z, skip, *, p, mak=None, width=2048):
    b, mo, _, d = c.shape
    g = HY_G
    seq = xg.shape[1]
    qh = seq // p
    fused = mak is not None
    v5 = lambda a: a.reshape(b, qh, p // g, g, d)
    tspec = pl.BlockSpec((None, qh, None, g, width), lambda bi, l, j: (bi, 0, l, 0, j))
    fspec = pl.BlockSpec((None, mo, g, width), lambda bi, l, j: (bi, 0, l, j))
    in_specs = [fspec, pl.BlockSpec(mdk.shape, lambda bi, l, j: (0, 0)), tspec, tspec,
                pl.BlockSpec((1, width), lambda bi, l, j: (0, j))]
    args = [c, mdk, v5(xg), v5(z), skip.reshape(1, d)]
    out_specs, out_shape = [tspec], [jax.ShapeDtypeStruct((b, qh, p // g, g, d), BF16)]
    if fused:
        in_specs.append(pl.BlockSpec(mak.shape, lambda bi, l, j: (0, 0)))
        args.append(mak)
        out_specs.append(fspec)
        out_shape.append(jax.ShapeDtypeStruct(c.shape, BF16))
    outs = pl.pallas_call(
        functools.partial(_hy_stage_d_kernel, fused=fused),
        grid=(b, p // g, d // width),
        in_specs=in_specs,
        out_specs=out_specs,
        out_shape=out_shape,
        compiler_params=_params(("parallel", "parallel", "parallel")),
        name="hyena_stage_d",
    )(*args)
    y = outs[0].reshape(b, seq, d)
    return (y, outs[1]) if fused else y


def hyena_long_convs(x1, x2, v, filt, skip, *, rows):
    b, seq = rows.batch, rows.seq
    d = x1.shape[1]
    p = HY_P
    k1n = seq // p + 1
    ma, wf, wi, md = _dft_mats(seq, p)
    eye = np.eye(HY_G)
    as_bf16 = lambda a: jnp.asarray(a, F32).astype(BF16)
    mak, mdk = as_bf16(np.kron(ma, eye)), as_bf16(np.kron(md, eye))
    af = hyena_stage_a(filt.reshape(1, seq, filt.shape[1]), mak, p=p)
    spec = hyena_spectrum(af.reshape(k1n, 2, p, filt.shape[1]), wf, d=d)
    t3 = lambda a: a.reshape(b, seq, d)
    y = t3(v)
    a = hyena_stage_a(y, mak, p=p)
    c = hyena_mid(a.reshape(b, k1n, 2, p, d), wf, wi, spec, order=0, d=d)
    y, a = hyena_stage_d(c.reshape(a.shape), mdk, t3(x1), y, skip[0], p=p, mak=mak)
    c = hyena_mid(a.reshape(b, k1n, 2, p, d), wf, wi, spec, order=1, d=d)
    y = hyena_stage_d(c.reshape(a.shape), mdk, t3(x2), y, skip[1], p=p)
    return y.reshape(rows.n_lat, d)


def _mlstm_gate_weights(w_in, gate_b):
    h = ML_HEADS
    o4 = 2 * h * ML_DQK + 2 * h * ML_DV
    wg = w_in[:, o4:].reshape(-1, 2, 2, h)
    bg = gate_b.reshape(2, 2, h)
    w_out = jnp.zeros((w_in.shape[0], 4 * LANE), F32)
    b_out = jnp.zeros((1, 4 * LANE), F32)
    for d in range(2):
        for gate in range(2):
            lo = (2 * d + gate) * LANE
            w_out = w_out.at[:, lo:lo + h].set(wg[:, gate, d])
            b_out = b_out.at[0, lo:lo + h].set(bg[gate, d])
    return w_out, b_out


def _mla_weights(w_in, w_uq):
    r2 = MLA_Q_RANK + MLA_KV_RANK
    half = MLA_ROPE // 2

    def spread(w):
        z = jnp.zeros(w.shape[:-1] + (half,), w.dtype)
        return jnp.concatenate([w[..., :half], z, w[..., half:], z], axis=-1)

    w_in_p = jnp.concatenate([w_in[:, :r2], spread(w_in[:, r2:])], axis=1)
    wq = w_uq.reshape(w_uq.shape[0], MLA_HEADS, MLA_NOPE + MLA_ROPE)
    wq_p = jnp.concatenate([wq[..., :MLA_NOPE], spread(wq[..., MLA_NOPE:])], axis=-1)
    wq_p = wq_p * (MLA_NOPE + MLA_ROPE) ** -0.5
    return w_in_p, wq_p.reshape(w_uq.shape[0], MLA_HEADS * 2 * LANE)


def kernel(x, c, ctx, c_ctx, ada_w, ada_b, norm_g, mlp_w1, mlp_w2, swa_w_qkv, swa_sink, swa_w_o, ml_w_in, ml_gate_b, ml_head_g, ml_w_o, mla_w_in, mla_q_g, mla_kv_g, mla_w_uq, mla_w_ukv, mla_w_o, hy_w_in, hy_conv_w, hy_conv_b, hy_f_w1, hy_f_b1, hy_f_w2, hy_f_b2, hy_f_w3, hy_f_b3, hy_f_w4, hy_f_b4, hy_f_freq, hy_decay, hy_skip, hy_w_o):
    b, seq, d = x.shape
    ctx_len = ctx.shape[1]
    depth = ada_w.shape[0]
    assert depth == 4 and b < 8
    rows = Rows(b, seq, ctx_len)
    tm = 512
    tm_lin = 1024 if rows.n_lat % 1024 == 0 and rows.n_ctx % 1024 == 0 else 512
    bf = lambda w: w.astype(BF16)
    mlp_w1b, mlp_w2b = bf(mlp_w1), bf(mlp_w2)

    cond = jnp.zeros((8, d), F32).at[:b].set(c).at[b].set(c_ctx)
    mods = ada_mods(cond, ada_w, ada_b).reshape(depth * 8, 1, N_MOD * d)
    h = jnp.concatenate([x.reshape(rows.n_lat, d), ctx.reshape(rows.n_ctx, d)], axis=0)

    common = dict(rows=rows, tm=tm_lin, mods=mods)

    def finish(h, o_args, w_o, layer, n_rows, mode="plain"):
        h = outproj_residual(o_args, bf(w_o), h, mods, norm_g[layer, 1], rows=rows, n_rows=n_rows, tm=tm,
                             layer=layer, mode=mode, name=f"outproj{layer}")
        return mlp_residual(h, mlp_w1b, mlp_w2b, mods, norm_g[layer, 2], norm_g[layer, 3],
                            rows=rows, n_rows=n_rows, tm=tm, tf=1024, layer=layer)

    n_qk_groups = SWA_HEADS + SWA_KV_HEADS
    q_cols = SWA_HEADS * SWA_HEAD_DIM
    w_qkv = jnp.concatenate([swa_w_qkv[0][:, :q_cols] * SWA_HEAD_DIM ** -0.5, swa_w_qkv[0][:, q_cols:]], axis=1)
    qkv = linear(h, bf(w_qkv), n_rows=rows.n_all, tn=512, out_dtype=BF16, prologue="norm_mod",
                 gain=norm_g[0, 0], layer=0, rope_tabs=_rope_tables(seq, SWA_HEAD_DIM), rope_pattern=(True,) * 4,
                 rope_jmax=n_qk_groups // 4, name="swa_qkv", **common)
    o = swa_attention(qkv, swa_sink[0], rows=rows)
    h = finish(h, o, swa_w_o[0], 0, rows.n_all)

    w_gate, b_gate = _mlstm_gate_weights(ml_w_in[0], ml_gate_b[0])
    z = linear(h, bf(ml_w_in[0]), n_rows=rows.n_all, tn=1024, out_dtype=BF16, prologue="norm_mod",
               gain=norm_g[1, 0], layer=1, n_out=2 * ML_HEADS * (ML_DQK + ML_DV), name="mlstm_in", **common)
    gates = linear(h, bf(w_gate), n_rows=rows.n_all, tn=4 * LANE, out_dtype=F32, prologue="norm_mod",
                   gain=norm_g[1, 0], layer=1, name="mlstm_gates", **common)
    hsum = mlstm_scan(z, gates, b_gate, rows=rows)
    h = finish(h, (hsum, z, ml_head_g[0]), ml_w_o[0], 1, rows.n_all, mode="mlstm")

    w_in_p, w_uq_p = _mla_weights(mla_w_in[0], mla_w_uq[0])
    rope_mla = _rope_tables(seq, MLA_ROPE)
    zc = linear(h, bf(w_in_p), n_rows=rows.n_all, tn=w_in_p.shape[1], out_dtype=BF16, prologue="norm_mod",
                gain=norm_g[2, 0], layer=2, rope_tabs=rope_mla, rope_pattern=(False,) * 8 + (True,),
                name="mla_in", **common)
    q = linear(zc, bf(w_uq_p), n_rows=rows.n_lat, tn=512, out_dtype=BF16, x_cols=0, prologue="norm",
               gain=mla_q_g[0], rope_tabs=rope_mla, rope_pattern=(False, True) * 2, name="mla_q",
               rows=rows, tm=tm_lin)
    kv = linear(zc, bf(mla_w_ukv[0]), n_rows=rows.n_all, tn=1024, out_dtype=BF16, x_cols=1, prologue="norm",
                gain=mla_kv_g[0], name="mla_kv", rows=rows, tm=tm_lin)
    o = mla_attention(q, kv, zc, rows=rows)
    h = finish(h, o, mla_w_o[0], 2, rows.n_lat)

    x3 = linear(h, bf(hy_w_in[0]), n_rows=rows.n_lat, tn=1024, out_dtype=BF16, prologue="norm_mod",
                gain=norm_g[3, 0], layer=3, name="hyena_in", **common)
    x1, x2, v = hyena_short_conv(x3, hy_conv_w[0], hy_conv_b[0], rows=rows)
    filt = hyena_filters(seq, hy_f_w1[0], hy_f_b1[0], hy_f_w2[0], hy_f_b2[0], hy_f_w3[0], hy_f_b3[0],
                         hy_f_w4[0], hy_f_b4[0], hy_f_freq[0], hy_decay[0])
    y = hyena_long_convs(x1, x2, v, filt, hy_skip[0], rows=rows)
    h = finish(h, y, hy_w_o[0], 3, rows.n_lat)
    return h.reshape(b, seq, d)
```

```python
import functools
import math

import numpy as np
import jax
import jax.numpy as jnp
from jax import lax
from jax.experimental import pallas as pl
from jax.experimental.pallas import tpu as pltpu

F32, BF16 = jnp.float32, jnp.bfloat16
HIGHEST = lax.Precision.HIGHEST

RMS_EPS = 1e-6
ROPE_THETA = 10000.0
GRID_W = 64
N_MOD = 6
LANE = 128

SWA_HEADS, SWA_KV_HEADS, SWA_HEAD_DIM, SWA_WINDOW, SWA_BLOCK = 16, 4, 128, 128, 128
ML_HEADS, ML_DQK, ML_DV = 8, 128, 256
ML_CHUNK = 256
MLA_HEADS, MLA_Q_RANK, MLA_KV_RANK, MLA_NOPE, MLA_ROPE, MLA_V = 16, 512, 512, 128, 64, 128
HY_ORDER, HY_BANDS, HY_FILTER_W, HY_SHORT = 2, 16, 64, 3
HY_P = 256
HY_G = 16

VMEM_LIMIT = 48 * 1024 * 1024


def _params(sem):
    return pltpu.CompilerParams(dimension_semantics=sem, vmem_limit_bytes=VMEM_LIMIT)


def _dot(a, b):
    return jnp.dot(a, b, preferred_element_type=F32)


def _dot_nt(a, b):
    return lax.dot_general(a, b, (((1,), (1,)), ((), ())), preferred_element_type=F32)


def _dot_tn(a, b):
    return lax.dot_general(a, b, (((0,), (0,)), ((), ())), preferred_element_type=F32)


def _rms(x, g):
    return x * lax.rsqrt(jnp.mean(x * x, axis=-1, keepdims=True) + RMS_EPS) * g


def _rms_ref(x_ref, inv_ref, g):
    x = x_ref[...].astype(F32)
    inv_ref[...] = lax.rsqrt(jnp.mean(x * x, axis=-1, keepdims=True) + RMS_EPS)
    return x_ref[...].astype(F32) * inv_ref[...] * g


def _ada_kernel(s_ref, w_ref, b_ref, o_ref):
    s = s_ref[...]
    s = s * jax.nn.sigmoid(s)
    hi = s.astype(BF16)
    lo = (s - hi.astype(F32)).astype(BF16)
    w = w_ref[0]
    w_hi = w.astype(BF16)
    w_lo = (w - w_hi.astype(F32)).astype(BF16)
    r = _dot(jnp.concatenate([hi, lo], axis=0), w_hi)
    n = s.shape[0]
    o_ref[0] = r[:n] + r[n:] + _dot(hi, w_lo) + b_ref[0]


def ada_mods(cond, ada_w, ada_b, tn=1536):
    depth, d, n = ada_w.shape
    rows = cond.shape[0]
    return pl.pallas_call(
        _ada_kernel,
        grid=(depth, n // tn),
        in_specs=[
            pl.BlockSpec((rows, d), lambda l, j: (0, 0)),
            pl.BlockSpec((1, d, tn), lambda l, j: (l, 0, j)),
            pl.BlockSpec((1, 1, tn), lambda l, j: (l, 0, j)),
        ],
        out_specs=pl.BlockSpec((1, rows, tn), lambda l, j: (l, 0, j)),
        out_shape=jax.ShapeDtypeStruct((depth, rows, n), F32),
        compiler_params=_params(("parallel", "parallel")),
        name="ada_mods",
    )(cond, ada_w, ada_b.reshape(depth, 1, n))


class Rows:
    def __init__(self, batch, seq, ctx_len):
        self.batch, self.seq, self.ctx_len = batch, seq, ctx_len
        self.n_lat = batch * seq
        self.n_ctx = batch * ctx_len
        self.n_all = self.n_lat + self.n_ctx

    def mod_index(self, layer, k, tm):
        lat_blocks, per_batch = self.n_lat // tm, self.seq // tm

        def index(i, *_):
            b = jnp.where(i < lat_blocks, i // per_batch, self.batch)
            return (layer * 8 + b, 0, k)

        return index


def _linear_kernel(*refs, prologue, rope, n_lat_blocks, rope_pattern, rope_jmax):
    it = iter(refs)
    x_ref = next(it)
    g_ref = next(it) if prologue in ("norm", "norm_mod") else None
    sh_ref = next(it) if prologue == "norm_mod" else None
    sc_ref = next(it) if prologue == "norm_mod" else None
    w_ref = next(it)
    cc_ref = next(it) if rope else None
    ss_ref = next(it) if rope else None
    o_ref = next(it)
    u_ref = next(it)
    inv_ref = next(it)
    i, j = pl.program_id(0), pl.program_id(1)

    @pl.when(j == 0)
    def _():
        if prologue in ("norm", "norm_mod"):
            x = _rms_ref(x_ref, inv_ref, g_ref[...])
        else:
            x = x_ref[...].astype(F32)
        if prologue == "norm_mod":
            x = x * (1.0 + sc_ref[0]) + sh_ref[0]
        u_ref[...] = x.astype(BF16)

    acc = _dot(u_ref[...], w_ref[...])

    def plain():
        o_ref[...] = acc.astype(o_ref.dtype)

    def roped():
        cc, ss = cc_ref[...], ss_ref[...]
        segs = []
        for gi, on in enumerate(rope_pattern):
            seg = acc[:, gi * LANE:(gi + 1) * LANE]
            if on:
                seg = seg * cc + pltpu.roll(seg, LANE // 2, 1) * ss
            segs.append(seg)
        o_ref[...] = jnp.concatenate(segs, axis=1).astype(o_ref.dtype)

    if not rope:
        plain()
    else:
        cond = jnp.logical_and(i < n_lat_blocks, j < rope_jmax)
        pl.when(cond)(roped)
        pl.when(jnp.logical_not(cond))(plain)


def linear(x, w, *, rows, n_rows, tm, tn, out_dtype, x_cols=None, prologue="none", gain=None,
           mods=None, layer=0, mod_k=(0, 1), rope_tabs=None, rope_pattern=(), rope_jmax=1 << 30,
           n_out=None, name="linear"):
    k, n = w.shape
    n = n if n_out is None else n_out
    xc = 0 if x_cols is None else x_cols
    rope = rope_tabs is not None
    in_specs = [pl.BlockSpec((tm, k), lambda i, j: (i, xc))]
    args = [x]
    if prologue in ("norm", "norm_mod"):
        in_specs.append(pl.BlockSpec((1, k), lambda i, j: (0, 0)))
        args.append(gain.reshape(1, k))
    if prologue == "norm_mod":
        for mk in mod_k:
            in_specs.append(pl.BlockSpec((1, 1, k), rows.mod_index(layer, mk, tm)))
            args.append(mods)
    in_specs.append(pl.BlockSpec((k, tn), lambda i, j: (0, j)))
    args.append(w)
    if rope:
        per_seq = rows.seq // tm
        for t in rope_tabs:
            in_specs.append(pl.BlockSpec((tm, LANE), lambda i, j: (i % per_seq, 0)))
            args.append(t)
    kern = functools.partial(_linear_kernel, prologue=prologue, rope=rope,
                             n_lat_blocks=rows.n_lat // tm, rope_pattern=tuple(rope_pattern),
                             rope_jmax=rope_jmax)
    return pl.pallas_call(
        kern,
        grid=(n_rows // tm, n // tn),
        in_specs=in_specs,
        out_specs=pl.BlockSpec((tm, tn), lambda i, j: (i, j)),
        out_shape=jax.ShapeDtypeStruct((n_rows, n), out_dtype),
        scratch_shapes=[pltpu.VMEM((tm, k), BF16), pltpu.VMEM((tm, 1), F32)],
        compiler_params=_params(("parallel", "arbitrary")),
        name=name,
    )(*args)


def _outproj_kernel(*refs, mode):
    it = iter(refs)
    if mode == "mlstm":
        hf_ref, hb_ref, og_ref, hg_ref = next(it), next(it), next(it), next(it)
    else:
        o_ref_in = next(it)
    w_ref, h_ref, gate_ref, g_ref, out_ref = next(it), next(it), next(it), next(it), next(it)
    y_ref, inv_ref = next(it), next(it)
    if mode == "mlstm":
        hs = hf_ref[0].astype(F32) + hb_ref[0].astype(F32)
        og = jax.nn.sigmoid(og_ref[...].astype(F32))
        hg = hg_ref[...]
        parts = []
        for h in range(ML_HEADS):
            sl = slice(h * ML_DV, (h + 1) * ML_DV)
            parts.append((_rms(hs[:, sl], hg[:, sl]) * og[:, sl]).astype(BF16))
        o = jnp.concatenate(parts, axis=1)
    else:
        o = o_ref_in[...]
    y_ref[...] = _dot(o, w_ref[...])
    out_ref[...] = h_ref[...] + gate_ref[0] * _rms_ref(y_ref, inv_ref, g_ref[...])


def outproj_residual(o_args, w, h, mods, gain, *, rows, n_rows, tm, layer, mode="plain", name="outproj"):
    k, d = w.shape
    if mode == "mlstm":
        hsum, z, head_g = o_args
        in_specs = [
            pl.BlockSpec((1, tm, k), lambda i: (0, i, 0)),
            pl.BlockSpec((1, tm, k), lambda i: (1, i, 0)),
            pl.BlockSpec((tm, k), lambda i: (i, 2)),
            pl.BlockSpec((1, k), lambda i: (0, 0)),
        ]
        args = [hsum, hsum, z, head_g.reshape(1, k)]
    else:
        in_specs = [pl.BlockSpec((tm, k), lambda i: (i, 0))]
        args = [o_args]
    in_specs += [
        pl.BlockSpec((k, d), lambda i: (0, 0)),
        pl.BlockSpec((tm, d), lambda i: (i, 0)),
        pl.BlockSpec((1, 1, d), rows.mod_index(layer, 2, tm)),
        pl.BlockSpec((1, d), lambda i: (0, 0)),
    ]
    args += [w, h, mods, gain.reshape(1, d)]
    return pl.pallas_call(
        functools.partial(_outproj_kernel, mode=mode),
        grid=(n_rows // tm,),
        in_specs=in_specs,
        out_specs=pl.BlockSpec((tm, d), lambda i: (i, 0)),
        out_shape=jax.ShapeDtypeStruct((n_rows, d), F32),
        scratch_shapes=[pltpu.VMEM((tm, d), F32), pltpu.VMEM((tm, 1), F32)],
        compiler_params=_params(("parallel",)),
        name=name,
    )(*args)


def _mlp_kernel(h_ref, g2_ref, sh_ref, sc_ref, w1_ref, w2_ref, gate_ref, g3_ref, out_ref, v_ref, acc_ref,
                inv_ref):
    f = pl.program_id(1)

    @pl.when(f == 0)
    def _():
        v = _rms_ref(h_ref, inv_ref, g2_ref[...]) * (1.0 + sc_ref[0]) + sh_ref[0]
        v_ref[...] = v.astype(BF16)
        acc_ref[...] = jnp.zeros_like(acc_ref)

    a = jnp.maximum(_dot(v_ref[...], w1_ref[...]), 0.0)
    acc_ref[...] += _dot((a * a).astype(BF16), w2_ref[...])

    @pl.when(f == pl.num_programs(1) - 1)
    def _():
        out_ref[...] = h_ref[...] + gate_ref[0] * _rms_ref(acc_ref, inv_ref, g3_ref[...])


def mlp_residual(h, w1, w2, mods, g2, g3, *, rows, n_rows, tm, tf, layer):
    _, d, ff = w1.shape
    return pl.pallas_call(
        _mlp_kernel,
        grid=(n_rows // tm, ff // tf),
        in_specs=[
            pl.BlockSpec((tm, d), lambda i, f: (i, 0)),
            pl.BlockSpec((1, d), lambda i, f: (0, 0)),
            pl.BlockSpec((1, 1, d), rows.mod_index(layer, 3, tm)),
            pl.BlockSpec((1, 1, d), rows.mod_index(layer, 4, tm)),
            pl.BlockSpec((None, d, tf), lambda i, f: (layer, 0, f)),
            pl.BlockSpec((None, tf, d), lambda i, f: (layer, f, 0)),
            pl.BlockSpec((1, 1, d), rows.mod_index(layer, 5, tm)),
            pl.BlockSpec((1, d), lambda i, f: (0, 0)),
        ],
        out_specs=pl.BlockSpec((tm, d), lambda i, f: (i, 0)),
        out_shape=jax.ShapeDtypeStruct((n_rows, d), F32),
        scratch_shapes=[pltpu.VMEM((tm, d), BF16), pltpu.VMEM((tm, d), F32), pltpu.VMEM((tm, 1), F32)],
        compiler_params=_params(("parallel", "arbitrary")),
        name="mlp",
    )(h, g2.reshape(1, d), mods, mods, w1, w2, mods, g3.reshape(1, d))


def _rope_tables(seq, d_rot):
    pos = np.arange(seq)
    row, col = pos // GRID_W, pos % GRID_W
    n = d_rot // 4
    inv = ROPE_THETA ** (-np.arange(n, dtype=np.float64) / n)
    ang = np.concatenate([row[:, None] * inv, col[:, None] * inv], axis=-1)
    cos, sin = np.cos(ang), np.sin(ang)
    half = d_rot // 2
    cc = np.zeros((seq, LANE))
    ss = np.zeros((seq, LANE))
    cc[:, :half] = cos
    cc[:, LANE // 2:LANE // 2 + half] = cos
    ss[:, :half] = -sin
    ss[:, LANE // 2:LANE // 2 + half] = sin
    return jnp.asarray(cc, F32), jnp.asarray(ss, F32)


def _swa_kernel(sink_ref, q_ref, kp_ref, kc_ref, kn_ref, kx_ref, vp_ref, vc_ref, vn_ref, vx_ref, o_ref,
                *, nb, seq, ctx_len):
    n = pl.program_id(1)
    hd, grp, lb = SWA_HEAD_DIM, SWA_HEADS // SWA_KV_HEADS, SWA_BLOCK
    m_rows, n_loc = grp * lb, 3 * lb
    r = lax.broadcasted_iota(jnp.int32, (m_rows, n_loc + ctx_len), 0)
    c = lax.broadcasted_iota(jnp.int32, (m_rows, n_loc + ctx_len), 1)
    qpos = n * lb + (r & (lb - 1))
    kpos = (n - 1) * lb + c
    local_ok = (jnp.abs(kpos - qpos) <= SWA_WINDOW) & (kpos >= 0) & (kpos < seq) & (n < nb)
    valid = local_ok | (c >= n_loc)
    ones = jnp.ones((n_loc + ctx_len, hd), BF16)
    outs = []
    for kv in range(SWA_KV_HEADS):
        ks = slice(kv * hd, (kv + 1) * hd)
        qg = jnp.concatenate([q_ref[:, (kv * grp + g) * hd:(kv * grp + g + 1) * hd] for g in range(grp)], axis=0)
        keys = jnp.concatenate([kp_ref[:, ks], kc_ref[:, ks], kn_ref[:, ks], kx_ref[:, ks]], axis=0)
        vals = jnp.concatenate([vp_ref[:, ks], vc_ref[:, ks], vn_ref[:, ks], vx_ref[:, ks]], axis=0)
        v_aug = jnp.concatenate([vals, ones], axis=1)
        s = jnp.where(valid, _dot_nt(qg, keys), -1e30)
        for g in range(grp):
            sg = s[g * lb:(g + 1) * lb]
            snk = sink_ref[kv * grp + g]
            m = jnp.maximum(jnp.max(sg, axis=1, keepdims=True), snk)
            ov = _dot(jnp.exp(sg - m).astype(BF16), v_aug)
            outs.append((ov[:, :hd] / (ov[:, hd:hd + 1] + jnp.exp(snk - m))).astype(BF16))
    o_ref[...] = jnp.concatenate(outs, axis=1)


def swa_attention(qkv, sink, *, rows):
    b, seq, ctx_len = rows.batch, rows.seq, rows.ctx_len
    lb = SWA_BLOCK
    nb, ncb = seq // lb, ctx_len // lb
    lat_blocks = rows.n_lat // lb
    qw = SWA_HEADS * SWA_HEAD_DIM
    kw = SWA_KV_HEADS * SWA_HEAD_DIM
    kcol, vcol = qw // kw, qw // kw + 1

    def qidx(bi, n):
        return (jnp.where(n < nb, bi * nb + n, lat_blocks + bi * ncb + (n - nb)), 0)

    def kidx(off, col):
        return lambda bi, n: (bi * nb + jnp.clip(n + off, 0, nb - 1), col)

    def xidx(col):
        return lambda bi, n: (rows.n_lat // ctx_len + bi, col)

    kern = functools.partial(_swa_kernel, nb=nb, seq=seq, ctx_len=ctx_len)
    return pl.pallas_call(
        kern,
        grid=(b, nb + ncb),
        in_specs=[
            pl.BlockSpec(memory_space=pltpu.SMEM),
            pl.BlockSpec((lb, qw), qidx),
            pl.BlockSpec((lb, kw), kidx(-1, kcol)),
            pl.BlockSpec((lb, kw), kidx(0, kcol)),
            pl.BlockSpec((lb, kw), kidx(1, kcol)),
            pl.BlockSpec((ctx_len, kw), xidx(kcol)),
            pl.BlockSpec((lb, kw), kidx(-1, vcol)),
            pl.BlockSpec((lb, kw), kidx(0, vcol)),
            pl.BlockSpec((lb, kw), kidx(1, vcol)),
            pl.BlockSpec((ctx_len, kw), xidx(vcol)),
        ],
        out_specs=pl.BlockSpec((lb, qw), qidx),
        out_shape=jax.ShapeDtypeStruct((rows.n_all, qw), BF16),
        compiler_params=_params(("parallel", "parallel")),
        name="swa_attention",
    )(sink, qkv, qkv, qkv, qkv, qkv, qkv, qkv, qkv, qkv)


def _mlstm_kernel(q_ref, k_ref, v_ref, gi_ref, gf_ref, bi_ref, bf_ref, o_ref, c_ref, m_ref):
    d, c = pl.program_id(1), pl.program_id(2)
    L, H, dqk, dv = ML_CHUNK, ML_HEADS, ML_DQK, ML_DV
    scale = dqk ** -0.5

    @pl.when(c == 0)
    def _():
        c_ref[...] = jnp.zeros_like(c_ref)
        m_ref[...] = jnp.zeros_like(m_ref)

    row = lax.broadcasted_iota(jnp.int32, (L, L), 0)
    col = lax.broadcasted_iota(jnp.int32, (L, L), 1)
    sign = 1 - 2 * d
    mask = (col - row) * sign <= 0
    ones_mask = jnp.where(mask, 1.0, 0.0).astype(BF16)
    i_blk = gi_ref[...] + bi_ref[...]
    f_blk = jax.nn.log_sigmoid(gf_ref[...] + bf_ref[...])
    f_hi = f_blk.astype(BF16)
    r1 = f_blk - f_hi.astype(F32)
    f_mid = r1.astype(BF16)
    f_lo = (r1 - f_mid.astype(F32)).astype(BF16)
    b_blk = _dot(jnp.concatenate([ones_mask] * 3, axis=1), jnp.concatenate([f_hi, f_mid, f_lo], axis=0))
    b_end = jnp.sum(f_blk, axis=0, keepdims=True)
    e_rows = (i_blk - b_blk).T
    m_prev_blk = m_ref[...]
    dec = b_end - b_blk + i_blk
    m_new_blk = jnp.maximum(b_end + m_prev_blk, jnp.max(dec, axis=0, keepdims=True))
    ws_blk = jnp.exp(dec - m_new_blk)
    gs_blk = jnp.exp(b_end + m_prev_blk - m_new_blk)
    ones = jnp.ones((L, LANE), BF16)
    for h in range(H):
        qh = q_ref[:, h * dqk:(h + 1) * dqk]
        kh = k_ref[:, h * dqk:(h + 1) * dqk]
        v_aug = jnp.concatenate([v_ref[:, h * dv:(h + 1) * dv], ones], axis=1)
        e = jnp.where(mask, e_rows[h:h + 1, :], -jnp.inf)
        m_prev = m_prev_blk[:, h:h + 1]
        mm = jnp.maximum(m_prev, jnp.max(e, axis=1, keepdims=True))
        s = (_dot_nt(qh, kh) * jnp.exp(e - (mm - math.log(scale)))).astype(BF16)
        g = jnp.exp(m_prev - mm) * scale
        ct = c_ref[h]
        lhs = jnp.concatenate([s, (qh.astype(F32) * g).astype(BF16)], axis=1)
        res = _dot(lhs, jnp.concatenate([v_aug, ct.astype(BF16)], axis=0))
        floor = jnp.exp(-(b_blk[:, h:h + 1] + mm))
        hout = res[:, :dv] / jnp.maximum(jnp.abs(res[:, dv:dv + 1]), floor)
        o_ref[0, :, h * dv:(h + 1) * dv] = hout.astype(o_ref.dtype)
        kw = (kh.astype(F32) * ws_blk[:, h:h + 1]).astype(BF16)
        c_ref[h] = gs_blk[:, h:h + 1] * ct + _dot_tn(kw, v_aug)
    m_ref[...] = m_new_blk


def mlstm_scan(z, gates, gate_b, *, rows):
    b, seq, ctx_len = rows.batch, rows.seq, rows.ctx_len
    L = ML_CHUNK
    assert ctx_len == L and seq % L == 0
    nlc = seq // L
    hq, hv = ML_HEADS * ML_DQK, ML_HEADS * ML_DV

    def rb(bi, d, c):
        lat = bi * nlc + jnp.where(d == 0, c - 1, nlc - c)
        return jnp.where(c == 0, rows.n_lat // L + bi, lat)

    return pl.pallas_call(
        _mlstm_kernel,
        grid=(b, 2, nlc + 1),
        in_specs=[
            pl.BlockSpec((L, hq), lambda bi, d, c: (rb(bi, d, c), 0)),
            pl.BlockSpec((L, hq), lambda bi, d, c: (rb(bi, d, c), 1)),
            pl.BlockSpec((L, hv), lambda bi, d, c: (rb(bi, d, c), 1)),
            pl.BlockSpec((L, LANE), lambda bi, d, c: (rb(bi, d, c), 2 * d)),
            pl.BlockSpec((L, LANE), lambda bi, d, c: (rb(bi, d, c), 2 * d + 1)),
            pl.BlockSpec((1, LANE), lambda bi, d, c: (0, 2 * d)),
            pl.BlockSpec((1, LANE), lambda bi, d, c: (0, 2 * d + 1)),
        ],
        out_specs=pl.BlockSpec((1, L, hv), lambda bi, d, c: (d, rb(bi, d, c), 0)),
        out_shape=jax.ShapeDtypeStruct((2, rows.n_all, hv), BF16),
        scratch_shapes=[
            pltpu.VMEM((ML_HEADS, ML_DQK, ML_DV + LANE), F32),
            pltpu.VMEM((1, LANE), F32),
        ],
        compiler_params=_params(("parallel", "parallel", "arbitrary")),
        name="mlstm_scan",
    )(z, z, z, gates, gates, gate_b, gate_b)


def _mla_kernel(q_ref, kvl_ref, kvx_ref, krl_ref, krx_ref, o_ref, kk_ref, vt_ref, *, seq, sub, ck):
    @pl.when(pl.program_id(2) == 0)
    def _():
        kk_ref[:seq, :LANE] = kvl_ref[:, :LANE]
        kk_ref[:seq, LANE:] = krl_ref[...]
        kk_ref[seq:, :LANE] = kvx_ref[:, :LANE]
        kk_ref[seq:, LANE:] = krx_ref[...]
        vt_ref[:LANE, :seq] = kvl_ref[:, LANE:].astype(F32).T.astype(BF16)
        vt_ref[:LANE, seq:] = kvx_ref[:, LANE:].astype(F32).T.astype(BF16)
        vt_ref[LANE:, :] = jnp.ones((vt_ref.shape[0] - LANE, vt_ref.shape[1]), BF16)

    nsub = q_ref.shape[0] // sub
    n_keys = kk_ref.shape[0]
    bounds = [(k0, min(k0 + ck, n_keys)) for k0 in range(0, n_keys, ck)]
    nck = len(bounds)
    qs = [q_ref[s * sub:(s + 1) * sub, :] for s in range(nsub)]

    def scores(s, c):
        return _dot_nt(kk_ref[bounds[c][0]:bounds[c][1], :], qs[s])

    m = [None] * nsub
    acc = [None] * nsub
    st_next = [scores(s, 0) for s in range(nsub)]
    for c in range(nck):
        st = st_next
        if c + 1 < nck:
            st_next = [scores(s, c + 1) for s in range(nsub)]
        vt_c = vt_ref[:, bounds[c][0]:bounds[c][1]]
        for s in range(nsub):
            cm = jnp.max(st[s], axis=0, keepdims=True)
            m_new = cm if c == 0 else jnp.maximum(m[s], cm)
            pv = _dot(vt_c, jnp.exp(st[s] - m_new).astype(BF16))
            acc[s] = pv if c == 0 else acc[s] * jnp.exp(m[s] - m_new) + pv
            m[s] = m_new
    for s in range(nsub):
        o = acc[s][:LANE] / acc[s][LANE:LANE + 1]
        o_ref[s * sub:(s + 1) * sub, :] = o.T.astype(o_ref.dtype)


def mla_attention(q, kv, z, *, rows, tq=1024, sub=256, ck=2048):
    b, seq, ctx_len = rows.batch, rows.seq, rows.ctx_len
    tq = min(tq, seq)
    nq = seq // tq
    kr_col = z.shape[1] // LANE - 1
    ctx0 = rows.n_lat // ctx_len
    n_keys = seq + ctx_len
    ones_rows = 16
    return pl.pallas_call(
        functools.partial(_mla_kernel, seq=seq, sub=sub, ck=ck),
        grid=(b, MLA_HEADS, nq),
        in_specs=[
            pl.BlockSpec((tq, 2 * LANE), lambda bi, h, i: (bi * nq + i, h)),
            pl.BlockSpec((seq, 2 * LANE), lambda bi, h, i: (bi, h)),
            pl.BlockSpec((ctx_len, 2 * LANE), lambda bi, h, i: (ctx0 + bi, h)),
            pl.BlockSpec((seq, LANE), lambda bi, h, i: (bi, kr_col)),
            pl.BlockSpec((ctx_len, LANE), lambda bi, h, i: (ctx0 + bi, kr_col)),
        ],
        out_specs=pl.BlockSpec((tq, LANE), lambda bi, h, i: (bi * nq + i, h)),
        out_shape=jax.ShapeDtypeStruct((rows.n_lat, MLA_HEADS * MLA_V), BF16),
        scratch_shapes=[pltpu.VMEM((n_keys, 2 * LANE), BF16), pltpu.VMEM((LANE + ones_rows, n_keys), BF16)],
        compiler_params=_params(("parallel", "parallel", "arbitrary")),
        name="mla_attention",
    )(q, kv, kv, z, z)


def _hy_filter_kernel(ft_ref, w1_ref, b1_ref, w2_ref, b2_ref, w3_ref, b3_ref, fr_ref, fr3_ref, w4_ref, b4_ref,
                      dec_ref, o_ref, a_ref, *, back_from):
    j = pl.program_id(0)
    fw = HY_FILTER_W
    hdot = functools.partial(jnp.dot, preferred_element_type=F32, precision=HIGHEST)

    @pl.when(j == 0)
    def _():
        fr = fr_ref[...]
        a = jnp.sin(fr * (hdot(ft_ref[...], w1_ref[...]) + b1_ref[...]))
        a = jnp.sin(fr * (hdot(a, w2_ref[...]) + b2_ref[...]))
        a3 = jnp.sin(fr3_ref[...] * (hdot(a, w3_ref[...]) + b3_ref[...]))
        hi = a3.astype(BF16)
        lo = (a3 - hi.astype(F32)).astype(BF16)
        lane = lax.broadcasted_iota(jnp.int32, a3.shape, 1)
        a_ref[...] = jnp.where((lane >= fw) & (lane < 2 * fw), lo, hi)

    w4 = w4_ref[...]
    w_hi = w4.astype(BF16)
    w_lo = (w4 - w_hi.astype(F32)).astype(BF16)
    rhs = jnp.concatenate([w_hi, w_hi, w_lo, jnp.zeros_like(w_hi)], axis=0)
    t = ft_ref[:, 0:1]
    filt = (_dot(a_ref[...], rhs) + b4_ref[...]) * jnp.exp(-t * jnp.abs(dec_ref[...]))
    row = lax.broadcasted_iota(jnp.int32, filt.shape, 0)
    o_ref[...] = jnp.where(jnp.logical_and(row == 0, j >= back_from), 0.0, filt)


def hyena_filters(seq, w1, b1, w2, b2, w3, b3, w4, b4, freq, decay, tn=512):
    t = np.linspace(0.0, 1.0, seq)[:, None]
    w = (2.0 * math.pi / seq) * np.arange(seq)[:, None]
    bands = np.linspace(1e-4, HY_BANDS - 1, HY_BANDS)[None, :]
    feats = np.zeros((seq, LANE))
    feats[:, :1 + 2 * HY_BANDS] = np.concatenate([t, np.cos(bands * w), -np.sin(bands * w)], axis=-1)
    n = w4.shape[1]
    fw = HY_FILTER_W
    w1p = jnp.zeros((LANE, fw), F32).at[:w1.shape[0]].set(w1)
    row = lambda a: a.reshape(1, -1)
    tile3 = lambda a: jnp.concatenate([a, a, a, jnp.zeros_like(a)], axis=-1)
    full = lambda shape: pl.BlockSpec(shape, lambda j: (0, 0))
    return pl.pallas_call(
        functools.partial(_hy_filter_kernel, back_from=(n // 2) // tn),
        grid=(n // tn,),
        in_specs=[full((seq, LANE)), full((LANE, fw)), full((1, fw)), full((fw, fw)), full((1, fw)),
                  full((fw, 4 * fw)), full((1, 4 * fw)), full((1, fw)), full((1, 4 * fw)),
                  pl.BlockSpec((fw, tn), lambda j: (0, j)),
                  pl.BlockSpec((1, tn), lambda j: (0, j)),
                  pl.BlockSpec((1, tn), lambda j: (0, j))],
        out_specs=pl.BlockSpec((seq, tn), lambda j: (0, j)),
        out_shape=jax.ShapeDtypeStruct((seq, n), F32),
        scratch_shapes=[pltpu.VMEM((seq, 4 * fw), BF16)],
        compiler_params=_params(("arbitrary",)),
        name="hyena_filters",
    )(jnp.asarray(feats, F32), w1p, row(b1), w2, row(b2), tile3(w3), tile3(row(b3)), row(freq), tile3(row(freq)),
      w4, row(b4), row(decay))


def _dft_mats(seq, p):
    n = 2 * seq
    q = n // p
    k1n = q // 2 + 1
    hi, k1 = np.arange(q // 2), np.arange(k1n)
    th = 2 * np.pi * np.outer(k1, hi) / q
    ma = np.zeros((2 * k1n, q // 2))
    ma[0::2], ma[1::2] = np.cos(th), -np.sin(th)
    lo = np.arange(p)
    wf = np.zeros((k1n, 2 * p, 2 * p))
    wi = np.zeros((k1n, 2 * p, 2 * p))
    for k in k1:
        ph = -2 * np.pi * (np.outer(lo, lo) / p + k * lo[None, :] / n)
        er, ei = np.cos(ph), np.sin(ph)
        wf[k] = np.block([[er, -ei], [ei, er]])
        wi[k] = np.block([[er.T, ei.T], [-ei.T, er.T]])
    c = np.full(k1n, 2.0)
    c[0] = c[-1] = 1.0
    th2 = 2 * np.pi * np.outer(hi, k1) / q
    md = np.zeros((q // 2, 2 * k1n))
    md[:, 0::2], md[:, 1::2] = c * np.cos(th2) / n, -c * np.sin(th2) / n
    as_bf16 = lambda a: jnp.asarray(a, F32).astype(BF16)
    return ma, as_bf16(wf), as_bf16(wi), md


def _hy_short_kernel(x1_ref, x2_ref, v_ref, w_ref, b_ref, o1_ref, o2_ref, o3_ref):
    for k, (x_ref, o_ref) in enumerate(((x1_ref, o1_ref), (x2_ref, o2_ref), (v_ref, o3_ref))):
        x = x_ref[...].astype(F32)
        n = x.shape[0]
        row = lax.broadcasted_iota(jnp.int32, x.shape, 0)
        prev = jnp.where(row == 0, 0.0, pltpu.roll(x, 1, 0))
        nxt = jnp.where(row == n - 1, 0.0, pltpu.roll(x, n - 1, 0))
        w = w_ref[k]
        o_ref[...] = (prev * w[0:1] + x * w[1:2] + nxt * w[2:3] + b_ref[k]).astype(o_ref.dtype)


def hyena_short_conv(x3, conv_w, conv_b, *, rows, cb=256):
    b, seq = rows.batch, rows.seq
    d = x3.shape[1] // 3
    nblk = d // cb
    w = jnp.transpose(conv_w.reshape(HY_SHORT, 3, d), (1, 0, 2))
    bias = conv_b.reshape(3, 1, d)
    in_x = [pl.BlockSpec((seq, cb), lambda bi, j, k=k: (bi, k * nblk + j)) for k in range(3)]
    out = pl.BlockSpec((seq, cb), lambda bi, j: (bi, j))
    shape = jax.ShapeDtypeStruct((rows.n_lat, d), BF16)
    return pl.pallas_call(
        _hy_short_kernel,
        grid=(b, nblk),
        in_specs=in_x + [pl.BlockSpec((3, HY_SHORT, cb), lambda bi, j: (0, 0, j)),
                         pl.BlockSpec((3, 1, cb), lambda bi, j: (0, 0, j))],
        out_specs=[out, out, out],
        out_shape=[shape, shape, shape],
        compiler_params=_params(("parallel", "parallel")),
        name="hyena_short_conv",
    )(x3, x3, x3, w, bias)


def _hy_stage_a_kernel(x_ref, m_ref, o_ref):
    qh, g, wc = x_ref.shape
    x = x_ref[...].reshape(qh * g, wc).astype(BF16)
    o_ref[...] = _dot(m_ref[...], x).reshape(o_ref.shape).astype(o_ref.dtype)


def hyena_stage_a(x, mak, *, p, width=2048):
    bx, seq, c = x.shape
    g = HY_G
    qh = seq // p
    mo = mak.shape[0] // g
    x5 = x.reshape(bx, qh, p // g, g, c)
    return pl.pallas_call(
        _hy_stage_a_kernel,
        grid=(bx, p // g, c // width),
        in_specs=[pl.BlockSpec((None, qh, None, g, width), lambda bi, l, j: (bi, 0, l, 0, j)),
                  pl.BlockSpec(mak.shape, lambda bi, l, j: (0, 0))],
        out_specs=pl.BlockSpec((None, mo, g, width), lambda bi, l, j: (bi, 0, l, j)),
        out_shape=jax.ShapeDtypeStruct((bx, mo, p, c), BF16),
        compiler_params=_params(("parallel", "parallel", "parallel")),
        name="hyena_stage_a",
    )(x5, mak)


def _hy_spec_kernel(a0_ref, a1_ref, wf_ref, o_ref):
    p = o_ref.shape[2]
    cb = o_ref.shape[3]
    wf = wf_ref[0]
    x0 = _dot(wf, a0_ref[0].reshape(2 * p, cb))
    x1 = _dot(wf, a1_ref[0].reshape(2 * p, cb))
    o_ref[0, 0] = x0[:p] + x1[:p]
    o_ref[0, 1] = x0[p:] - x1[p:]


def hyena_spectrum(af, wf, *, d, cb=1024):
    k1n, _, p, _ = af.shape
    nblk = (HY_ORDER * d) // cb
    return pl.pallas_call(
        _hy_spec_kernel,
        grid=(k1n, nblk),
        in_specs=[pl.BlockSpec((1, 2, p, cb), lambda k, j: (k, 0, 0, j)),
                  pl.BlockSpec((1, 2, p, cb), lambda k, j: (k, 0, 0, nblk + j)),
                  pl.BlockSpec((1, 2 * p, 2 * p), lambda k, j: (k, 0, 0))],
        out_specs=pl.BlockSpec((1, 2, p, cb), lambda k, j: (k, 0, 0, j)),
        out_shape=jax.ShapeDtypeStruct((k1n, 2, p, HY_ORDER * d), F32),
        compiler_params=_params(("parallel", "parallel")),
        name="hyena_spectrum",
    )(af, af, wf)


def _hy_mid_kernel(a_ref, wf_ref, wi_ref, h_ref, o_ref):
    b, _, p, cb = a_ref.shape
    hr, hi = h_ref[0], h_ref[1]
    for bi in range(b):
        x = _dot(wf_ref[...], a_ref[bi].reshape(2 * p, cb))
        xr, xi = x[:p], x[p:]
        y = jnp.concatenate([xr * hr - xi * hi, xr * hi + xi * hr], axis=0).astype(BF16)
        o_ref[bi] = _dot(wi_ref[...], y).reshape(2, p, cb).astype(o_ref.dtype)


def hyena_mid(a, wf, wi, spec, *, order, d, cb=512):
    b, k1n, _, p, _ = a.shape
    nblk = d // cb
    data = pl.BlockSpec((b, None, 2, p, cb), lambda k, j: (0, k, 0, 0, j))
    mat = pl.BlockSpec((None, 2 * p, 2 * p), lambda k, j: (k, 0, 0))
    return pl.pallas_call(
        _hy_mid_kernel,
        grid=(k1n, nblk),
        in_specs=[data, mat, mat,
                  pl.BlockSpec((None, 2, p, cb), lambda k, j: (k, 0, 0, order * nblk + j))],
        out_specs=data,
        out_shape=jax.ShapeDtypeStruct(a.shape, BF16),
        compiler_params=_params(("parallel", "parallel")),
        name="hyena_mid",
    )(a, wf, wi, spec)


def _hy_stage_d_kernel(c_ref, m_ref, x_ref, z_ref, skip_ref, *rest, fused):
    ma_ref, o_ref, a_ref = rest if fused else (None, rest[0], None)
    mo, g, wc = c_ref.shape
    qh = x_ref.shape[0]
    conv = _dot(m_ref[...], c_ref[...].reshape(mo * g, wc))
    z = z_ref[...].reshape(qh * g, wc).astype(F32)
    x = x_ref[...].reshape(qh * g, wc).astype(F32)
    y = (x * (conv + z * skip_ref[...])).astype(o_ref.dtype)
    o_ref[...] = y.reshape(o_ref.shape)
    if fused:
        a_ref[...] = _dot(ma_ref[...], y).reshape(a_ref.shape).astype(a_ref.dtype)


def hyena_stage_d(c, mdk, xg, z, skip, *, p, mak=None, width=2048):
    b, mo, _, d = c.shape
    g = HY_G
    seq = xg.shape[1]
    qh = seq // p
    fused = mak is not None
    v5 = lambda a: a.reshape(b, qh, p // g, g, d)
    tspec = pl.BlockSpec((None, qh, None, g, width), lambda bi, l, j: (bi, 0, l, 0, j))
    fspec = pl.BlockSpec((None, mo, g, width), lambda bi, l, j: (bi, 0, l, j))
    in_specs = [fspec, pl.BlockSpec(mdk.shape, lambda bi, l, j: (0, 0)), tspec, tspec,
                pl.BlockSpec((1, width), lambda bi, l, j: (0, j))]
    args = [c, mdk, v5(xg), v5(z), skip.reshape(1, d)]
    out_specs, out_shape = [tspec], [jax.ShapeDtypeStruct((b, qh, p // g, g, d), BF16)]
    if fused:
        in_specs.append(pl.BlockSpec(mak.shape, lambda bi, l, j: (0, 0)))
        args.append(mak)
        out_specs.append(fspec)
        out_shape.append(jax.ShapeDtypeStruct(c.shape, BF16))
    outs = pl.pallas_call(
        functools.partial(_hy_stage_d_kernel, fused=fused),
        grid=(b, p // g, d // width),
        in_specs=in_specs,
        out_specs=out_specs,
        out_shape=out_shape,
        compiler_params=_params(("parallel", "parallel", "parallel")),
        name="hyena_stage_d",
    )(*args)
    y = outs[0].reshape(b, seq, d)
    return (y, outs[1]) if fused else y


def hyena_long_convs(x1, x2, v, filt, skip, *, rows):
    b, seq = rows.batch, rows.seq
    d = x1.shape[1]
    p = HY_P
    k1n = seq // p + 1
    ma, wf, wi, md = _dft_mats(seq, p)
    eye = np.eye(HY_G)
    as_bf16 = lambda a: jnp.asarray(a, F32).astype(BF16)
    mak, mdk = as_bf16(np.kron(ma, eye)), as_bf16(np.kron(md, eye))
    af = hyena_stage_a(filt.reshape(1, seq, filt.shape[1]), mak, p=p)
    spec = hyena_spectrum(af.reshape(k1n, 2, p, filt.shape[1]), wf, d=d)
    t3 = lambda a: a.reshape(b, seq, d)
    y = t3(v)
    a = hyena_stage_a(y, mak, p=p)
    c = hyena_mid(a.reshape(b, k1n, 2, p, d), wf, wi, spec, order=0, d=d)
    y, a = hyena_stage_d(c.reshape(a.shape), mdk, t3(x1), y, skip[0], p=p, mak=mak)
    c = hyena_mid(a.reshape(b, k1n, 2, p, d), wf, wi, spec, order=1, d=d)
    y = hyena_stage_d(c.reshape(a.shape), mdk, t3(x2), y, skip[1], p=p)
    return y.reshape(rows.n_lat, d)


def _mlstm_gate_weights(w_in, gate_b):
    h = ML_HEADS
    o4 = 2 * h * ML_DQK + 2 * h * ML_DV
    wg = w_in[:, o4:].reshape(-1, 2, 2, h)
    bg = gate_b.reshape(2, 2, h)
    w_out = jnp.zeros((w_in.shape[0], 4 * LANE), F32)
    b_out = jnp.zeros((1, 4 * LANE), F32)
    for d in range(2):
        for gate in range(2):
            lo = (2 * d + gate) * LANE
            w_out = w_out.at[:, lo:lo + h].set(wg[:, gate, d])
            b_out = b_out.at[0, lo:lo + h].set(bg[gate, d])
    return w_out, b_out


def _mla_weights(w_in, w_uq):
    r2 = MLA_Q_RANK + MLA_KV_RANK
    half = MLA_ROPE // 2

    def spread(w):
        z = jnp.zeros(w.shape[:-1] + (half,), w.dtype)
        return jnp.concatenate([w[..., :half], z, w[..., half:], z], axis=-1)

    w_in_p = jnp.concatenate([w_in[:, :r2], spread(w_in[:, r2:])], axis=1)
    wq = w_uq.reshape(w_uq.shape[0], MLA_HEADS, MLA_NOPE + MLA_ROPE)
    wq_p = jnp.concatenate([wq[..., :MLA_NOPE], spread(wq[..., MLA_NOPE:])], axis=-1)
    wq_p = wq_p * (MLA_NOPE + MLA_ROPE) ** -0.5
    return w_in_p, wq_p.reshape(w_uq.shape[0], MLA_HEADS * 2 * LANE)


def kernel(x, c, ctx, c_ctx, ada_w, ada_b, norm_g, mlp_w1, mlp_w2, swa_w_qkv, swa_sink, swa_w_o, ml_w_in, ml_gate_b, ml_head_g, ml_w_o, mla_w_in, mla_q_g, mla_kv_g, mla_w_uq, mla_w_ukv, mla_w_o, hy_w_in, hy_conv_w, hy_conv_b, hy_f_w1, hy_f_b1, hy_f_w2, hy_f_b2, hy_f_w3, hy_f_b3, hy_f_w4, hy_f_b4, hy_f_freq, hy_decay, hy_skip, hy_w_o):
    b, seq, d = x.shape
    ctx_len = ctx.shape[1]
    depth = ada_w.shape[0]
    assert depth == 4 and b < 8
    rows = Rows(b, seq, ctx_len)
    tm = 512
    tm_lin = 1024 if rows.n_lat % 1024 == 0 and rows.n_ctx % 1024 == 0 else 512
    bf = lambda w: w.astype(BF16)
    mlp_w1b, mlp_w2b = bf(mlp_w1), bf(mlp_w2)

    cond = jnp.zeros((8, d), F32).at[:b].set(c).at[b].set(c_ctx)
    mods = ada_mods(cond, ada_w, ada_b).reshape(depth * 8, 1, N_MOD * d)
    h = jnp.concatenate([x.reshape(rows.n_lat, d), ctx.reshape(rows.n_ctx, d)], axis=0)

    common = dict(rows=rows, tm=tm_lin, mods=mods)

    def finish(h, o_args, w_o, layer, n_rows, mode="plain"):
        h = outproj_residual(o_args, bf(w_o), h, mods, norm_g[layer, 1], rows=rows, n_rows=n_rows, tm=tm,
                             layer=layer, mode=mode, name=f"outproj{layer}")
        return mlp_residual(h, mlp_w1b, mlp_w2b, mods, norm_g[layer, 2], norm_g[layer, 3],
                            rows=rows, n_rows=n_rows, tm=tm, tf=1024, layer=layer)

    n_qk_groups = SWA_HEADS + SWA_KV_HEADS
    q_cols = SWA_HEADS * SWA_HEAD_DIM
    w_qkv = jnp.concatenate([swa_w_qkv[0][:, :q_cols] * SWA_HEAD_DIM ** -0.5, swa_w_qkv[0][:, q_cols:]], axis=1)
    qkv = linear(h, bf(w_qkv), n_rows=rows.n_all, tn=512, out_dtype=BF16, prologue="norm_mod",
                 gain=norm_g[0, 0], layer=0, rope_tabs=_rope_tables(seq, SWA_HEAD_DIM), rope_pattern=(True,) * 4,
                 rope_jmax=n_qk_groups // 4, name="swa_qkv", **common)
    o = swa_attention(qkv, swa_sink[0], rows=rows)
    h = finish(h, o, swa_w_o[0], 0, rows.n_all)

    w_gate, b_gate = _mlstm_gate_weights(ml_w_in[0], ml_gate_b[0])
    z = linear(h, bf(ml_w_in[0]), n_rows=rows.n_all, tn=1024, out_dtype=BF16, prologue="norm_mod",
               gain=norm_g[1, 0], layer=1, n_out=2 * ML_HEADS * (ML_DQK + ML_DV), name="mlstm_in", **common)
    gates = linear(h, bf(w_gate), n_rows=rows.n_all, tn=4 * LANE, out_dtype=F32, prologue="norm_mod",
                   gain=norm_g[1, 0], layer=1, name="mlstm_gates", **common)
    hsum = mlstm_scan(z, gates, b_gate, rows=rows)
    h = finish(h, (hsum, z, ml_head_g[0]), ml_w_o[0], 1, rows.n_all, mode="mlstm")

    w_in_p, w_uq_p = _mla_weights(mla_w_in[0], mla_w_uq[0])
    rope_mla = _rope_tables(seq, MLA_ROPE)
    zc = linear(h, bf(w_in_p), n_rows=rows.n_all, tn=w_in_p.shape[1], out_dtype=BF16, prologue="norm_mod",
                gain=norm_g[2, 0], layer=2, rope_tabs=rope_mla, rope_pattern=(False,) * 8 + (True,),
                name="mla_in", **common)
    q = linear(zc, bf(w_uq_p), n_rows=rows.n_lat, tn=2048, out_dtype=BF16, x_cols=0, prologue="norm",
               gain=mla_q_g[0], rope_tabs=rope_mla, rope_pattern=(False, True) * 8, name="mla_q",
               rows=rows, tm=tm_lin)
    kv = linear(zc, bf(mla_w_ukv[0]), n_rows=rows.n_all, tn=2048, out_dtype=BF16, x_cols=1, prologue="norm",
                gain=mla_kv_g[0], name="mla_kv", rows=rows, tm=tm_lin)
    o = mla_attention(q, kv, zc, rows=rows)
    h = finish(h, o, mla_w_o[0], 2, rows.n_lat)

    x3 = linear(h, bf(hy_w_in[0]), n_rows=rows.n_lat, tn=1024, out_dtype=BF16, prologue="norm_mod",
                gain=norm_g[3, 0], layer=3, name="hyena_in", **common)
    x1, x2, v = hyena_short_conv(x3, hy_conv_w[0], hy_conv_b[0], rows=rows)
    filt = hyena_filters(seq, hy_f_w1[0], hy_f_b1[0], hy_f_w2[0], hy_f_b2[0], hy_f_w3[0], hy_f_b3[0],
                         hy_f_w4[0], hy_f_b4[0], hy_f_freq[0], hy_decay[0])
    y = hyena_long_convs(x1, x2, v, filt, hy_skip[0], rows=rows)
    h = finish(h, y, hy_w_o[0], 3, rows.n_lat)
    return h.reshape(b, seq, d)
```

```python
import functools
import math

import numpy as np
import jax
import jax.numpy as jnp
from jax import lax
from jax.experimental import pallas as pl
from jax.experimental.pallas import tpu as pltpu

F32, BF16 = jnp.float32, jnp.bfloat16
HIGHEST = lax.Precision.HIGHEST

RMS_EPS = 1e-6
ROPE_THETA = 10000.0
GRID_W = 64
N_MOD = 6
LANE = 128

SWA_HEADS, SWA_KV_HEADS, SWA_HEAD_DIM, SWA_WINDOW, SWA_BLOCK = 16, 4, 128, 128, 128
ML_HEADS, ML_DQK, ML_DV = 8, 128, 256
ML_CHUNK = 256
MLA_HEADS, MLA_Q_RANK, MLA_KV_RANK, MLA_NOPE, MLA_ROPE, MLA_V = 16, 512, 512, 128, 64, 128
HY_ORDER, HY_BANDS, HY_FILTER_W, HY_SHORT = 2, 16, 64, 3
HY_P = 256
HY_G = 16

VMEM_LIMIT = 48 * 1024 * 1024


def _params(sem):
    return pltpu.CompilerParams(dimension_semantics=sem, vmem_limit_bytes=VMEM_LIMIT)


def _dot(a, b):
    return jnp.dot(a, b, preferred_element_type=F32)


def _dot_nt(a, b):
    return lax.dot_general(a, b, (((1,), (1,)), ((), ())), preferred_element_type=F32)


def _dot_tn(a, b):
    return lax.dot_general(a, b, (((0,), (0,)), ((), ())), preferred_element_type=F32)


def _rms(x, g):
    return x * lax.rsqrt(jnp.mean(x * x, axis=-1, keepdims=True) + RMS_EPS) * g


def _rms_ref(x_ref, inv_ref, g):
    x = x_ref[...].astype(F32)
    inv_ref[...] = lax.rsqrt(jnp.mean(x * x, axis=-1, keepdims=True) + RMS_EPS)
    return x_ref[...].astype(F32) * inv_ref[...] * g


def _ada_kernel(s_ref, w_ref, b_ref, o_ref):
    s = s_ref[...]
    s = s * jax.nn.sigmoid(s)
    hi = s.astype(BF16)
    lo = (s - hi.astype(F32)).astype(BF16)
    w = w_ref[0]
    w_hi = w.astype(BF16)
    w_lo = (w - w_hi.astype(F32)).astype(BF16)
    r = _dot(jnp.concatenate([hi, lo], axis=0), w_hi)
    n = s.shape[0]
    o_ref[0] = r[:n] + r[n:] + _dot(hi, w_lo) + b_ref[0]


def ada_mods(cond, ada_w, ada_b, tn=1536):
    depth, d, n = ada_w.shape
    rows = cond.shape[0]
    return pl.pallas_call(
        _ada_kernel,
        grid=(depth, n // tn),
        in_specs=[
            pl.BlockSpec((rows, d), lambda l, j: (0, 0)),
            pl.BlockSpec((1, d, tn), lambda l, j: (l, 0, j)),
            pl.BlockSpec((1, 1, tn), lambda l, j: (l, 0, j)),
        ],
        out_specs=pl.BlockSpec((1, rows, tn), lambda l, j: (l, 0, j)),
        out_shape=jax.ShapeDtypeStruct((depth, rows, n), F32),
        compiler_params=_params(("parallel", "parallel")),
        name="ada_mods",
    )(cond, ada_w, ada_b.reshape(depth, 1, n))


class Rows:
    def __init__(self, batch, seq, ctx_len):
        self.batch, self.seq, self.ctx_len = batch, seq, ctx_len
        self.n_lat = batch * seq
        self.n_ctx = batch * ctx_len
        self.n_all = self.n_lat + self.n_ctx

    def mod_index(self, layer, k, tm):
        lat_blocks, per_batch = self.n_lat // tm, self.seq // tm

        def index(i, *_):
            b = jnp.where(i < lat_blocks, i // per_batch, self.batch)
            return (layer * 8 + b, 0, k)

        return index


def _linear_kernel(*refs, prologue, rope, rope_pattern, n_lat_blocks, dual):
    it = iter(refs)
    x_ref = next(it)
    x2_ref = next(it) if dual else None
    g_ref = next(it) if prologue in ("norm", "norm_mod") else None
    sh_ref = next(it) if prologue == "norm_mod" else None
    sc_ref = next(it) if prologue == "norm_mod" else None
    w_ref = next(it)
    cc_ref = next(it) if rope else None
    ss_ref = next(it) if rope else None
    o_ref = next(it)
    rows_ref = next(it) if dual else None
    u_ref = next(it)
    inv_ref = next(it)
    first = pl.program_id(1) == 0

    def prologue_from(src_ref):
        if prologue in ("norm", "norm_mod"):
            x = _rms_ref(src_ref, inv_ref, g_ref[...])
        else:
            x = src_ref[...].astype(F32)
        if prologue == "norm_mod":
            x = x * (1.0 + sc_ref[0]) + sh_ref[0]
        u_ref[...] = x.astype(BF16)
        if dual:
            rows_ref[...] = src_ref[...]

    if dual:
        is_lat = pl.program_id(0) < n_lat_blocks
        pl.when(jnp.logical_and(first, is_lat))(functools.partial(prologue_from, x_ref))
        pl.when(jnp.logical_and(first, jnp.logical_not(is_lat)))(functools.partial(prologue_from, x2_ref))
    else:
        pl.when(first)(functools.partial(prologue_from, x_ref))

    acc = _dot(u_ref[...], w_ref[...])

    def plain():
        o_ref[...] = acc.astype(o_ref.dtype)

    def roped():
        cc, ss = cc_ref[...], ss_ref[...]
        segs = []
        for gi, on in enumerate(rope_pattern):
            seg = acc[:, gi * LANE:(gi + 1) * LANE]
            if on:
                seg = seg * cc + pltpu.roll(seg, LANE // 2, 1) * ss
            segs.append(seg)
        o_ref[...] = jnp.concatenate(segs, axis=1).astype(o_ref.dtype)

    if rope:
        roped()
    else:
        plain()


def linear(x, w, *, rows, n_rows, tm, tn, out_dtype, x_cols=None, prologue="none", gain=None,
           mods=None, layer=0, mod_k=(0, 1), rope_tabs=None, rope_pattern=(), rope_jmax=1 << 30,
           n_out=None, x2=None, name="linear"):
    k, n = w.shape
    n = n if n_out is None else n_out
    xc = 0 if x_cols is None else x_cols
    rope = rope_tabs is not None
    dual = x2 is not None
    n_lat_blocks = rows.n_lat // tm
    if dual:
        in_specs = [pl.BlockSpec((tm, k), lambda i, j: (jnp.minimum(i, n_lat_blocks - 1), 0)),
                    pl.BlockSpec((tm, k), lambda i, j: (jnp.maximum(i - n_lat_blocks, 0), 0))]
        args = [x, x2]
    else:
        in_specs = [pl.BlockSpec((tm, k), lambda i, j: (i, xc))]
        args = [x]
    if prologue in ("norm", "norm_mod"):
        in_specs.append(pl.BlockSpec((1, k), lambda i, j: (0, 0)))
        args.append(gain.reshape(1, k))
    if prologue == "norm_mod":
        for mk in mod_k:
            in_specs.append(pl.BlockSpec((1, 1, k), rows.mod_index(layer, mk, tm)))
            args.append(mods)
    in_specs.append(pl.BlockSpec((k, tn), lambda i, j: (0, j)))
    args.append(w)
    if rope:
        per_seq = rows.seq // tm

        def tab_index(i, j):
            rotate = jnp.logical_and(i < n_lat_blocks, j < rope_jmax)
            return (jnp.where(rotate, 0, 1), i % per_seq, 0)

        for t, fill in zip(rope_tabs, (1.0, 0.0)):
            in_specs.append(pl.BlockSpec((None, tm, LANE), tab_index))
            args.append(jnp.stack([t, jnp.full_like(t, fill)]))
    kern = functools.partial(_linear_kernel, prologue=prologue, rope=rope, rope_pattern=tuple(rope_pattern),
                             n_lat_blocks=n_lat_blocks, dual=dual)
    out_specs = pl.BlockSpec((tm, tn), lambda i, j: (i, j))
    out_shape = jax.ShapeDtypeStruct((n_rows, n), out_dtype)
    if dual:
        out_specs = [out_specs, pl.BlockSpec((tm, k), lambda i, j: (i, 0))]
        out_shape = [out_shape, jax.ShapeDtypeStruct((n_rows, k), x.dtype)]
    return pl.pallas_call(
        kern,
        grid=(n_rows // tm, n // tn),
        in_specs=in_specs,
        out_specs=out_specs,
        out_shape=out_shape,
        scratch_shapes=[pltpu.VMEM((tm, k), BF16), pltpu.VMEM((tm, 1), F32)],
        compiler_params=_params(("parallel", "arbitrary")),
        name=name,
    )(*args)


def _outproj_kernel(*refs, mode):
    it = iter(refs)
    if mode == "mlstm":
        hf_ref, hb_ref, og_ref, hg_ref = next(it), next(it), next(it), next(it)
    else:
        o_ref_in = next(it)
    w_ref, h_ref, gate_ref, g_ref, out_ref = next(it), next(it), next(it), next(it), next(it)
    y_ref, inv_ref = next(it), next(it)
    if mode == "mlstm":
        hs = hf_ref[0].astype(F32) + hb_ref[0].astype(F32)
        og = jax.nn.sigmoid(og_ref[...].astype(F32))
        hg = hg_ref[...]
        parts = []
        for h in range(ML_HEADS):
            sl = slice(h * ML_DV, (h + 1) * ML_DV)
            parts.append((_rms(hs[:, sl], hg[:, sl]) * og[:, sl]).astype(BF16))
        o = jnp.concatenate(parts, axis=1)
    else:
        o = o_ref_in[...]
    y_ref[...] = _dot(o, w_ref[...])
    out_ref[...] = h_ref[...] + gate_ref[0] * _rms_ref(y_ref, inv_ref, g_ref[...])


def outproj_residual(o_args, w, h, mods, gain, *, rows, n_rows, tm, layer, mode="plain", name="outproj"):
    k, d = w.shape
    if mode == "mlstm":
        hsum, z, head_g = o_args
        in_specs = [
            pl.BlockSpec((1, tm, k), lambda i: (0, i, 0)),
            pl.BlockSpec((1, tm, k), lambda i: (1, i, 0)),
            pl.BlockSpec((tm, k), lambda i: (i, 2)),
            pl.BlockSpec((1, k), lambda i: (0, 0)),
        ]
        args = [hsum, hsum, z, head_g.reshape(1, k)]
    else:
        in_specs = [pl.BlockSpec((tm, k), lambda i: (i, 0))]
        args = [o_args]
    in_specs += [
        pl.BlockSpec((k, d), lambda i: (0, 0)),
        pl.BlockSpec((tm, d), lambda i: (i, 0)),
        pl.BlockSpec((1, 1, d), rows.mod_index(layer, 2, tm)),
        pl.BlockSpec((1, d), lambda i: (0, 0)),
    ]
    args += [w, h, mods, gain.reshape(1, d)]
    return pl.pallas_call(
        functools.partial(_outproj_kernel, mode=mode),
        grid=(n_rows // tm,),
        in_specs=in_specs,
        out_specs=pl.BlockSpec((tm, d), lambda i: (i, 0)),
        out_shape=jax.ShapeDtypeStruct((n_rows, d), F32),
        scratch_shapes=[pltpu.VMEM((tm, d), F32), pltpu.VMEM((tm, 1), F32)],
        compiler_params=_params(("parallel",)),
        name=name,
    )(*args)


def _mlp_kernel(h_ref, g2_ref, sh_ref, sc_ref, w1_ref, w2_ref, gate_ref, g3_ref, out_ref, v_ref, acc_ref,
                inv_ref):
    f = pl.program_id(1)

    @pl.when(f == 0)
    def _():
        v = _rms_ref(h_ref, inv_ref, g2_ref[...]) * (1.0 + sc_ref[0]) + sh_ref[0]
        v_ref[...] = v.astype(BF16)
        acc_ref[...] = jnp.zeros_like(acc_ref)

    a = jnp.maximum(_dot(v_ref[...], w1_ref[...]), 0.0)
    acc_ref[...] += _dot((a * a).astype(BF16), w2_ref[...])

    @pl.when(f == pl.num_programs(1) - 1)
    def _():
        out_ref[...] = h_ref[...] + gate_ref[0] * _rms_ref(acc_ref, inv_ref, g3_ref[...])


def mlp_residual(h, w1, w2, mods, g2, g3, *, rows, n_rows, tm, tf, layer):
    _, d, ff = w1.shape
    return pl.pallas_call(
        _mlp_kernel,
        grid=(n_rows // tm, ff // tf),
        in_specs=[
            pl.BlockSpec((tm, d), lambda i, f: (i, 0)),
            pl.BlockSpec((1, d), lambda i, f: (0, 0)),
            pl.BlockSpec((1, 1, d), rows.mod_index(layer, 3, tm)),
            pl.BlockSpec((1, 1, d), rows.mod_index(layer, 4, tm)),
            pl.BlockSpec((None, d, tf), lambda i, f: (layer, 0, f)),
            pl.BlockSpec((None, tf, d), lambda i, f: (layer, f, 0)),
            pl.BlockSpec((1, 1, d), rows.mod_index(layer, 5, tm)),
            pl.BlockSpec((1, d), lambda i, f: (0, 0)),
        ],
        out_specs=pl.BlockSpec((tm, d), lambda i, f: (i, 0)),
        out_shape=jax.ShapeDtypeStruct((n_rows, d), F32),
        scratch_shapes=[pltpu.VMEM((tm, d), BF16), pltpu.VMEM((tm, d), F32), pltpu.VMEM((tm, 1), F32)],
        compiler_params=_params(("parallel", "arbitrary")),
        name="mlp",
    )(h, g2.reshape(1, d), mods, mods, w1, w2, mods, g3.reshape(1, d))


def _rope_tables(seq, d_rot):
    pos = np.arange(seq)
    row, col = pos // GRID_W, pos % GRID_W
    n = d_rot // 4
    inv = ROPE_THETA ** (-np.arange(n, dtype=np.float64) / n)
    ang = np.concatenate([row[:, None] * inv, col[:, None] * inv], axis=-1)
    cos, sin = np.cos(ang), np.sin(ang)
    half = d_rot // 2
    cc = np.zeros((seq, LANE))
    ss = np.zeros((seq, LANE))
    cc[:, :half] = cos
    cc[:, LANE // 2:LANE // 2 + half] = cos
    ss[:, :half] = -sin
    ss[:, LANE // 2:LANE // 2 + half] = sin
    return jnp.asarray(cc, F32), jnp.asarray(ss, F32)


def _swa_kernel(sink_ref, q_ref, kp_ref, kc_ref, kn_ref, kx_ref, vp_ref, vc_ref, vn_ref, vx_ref, o_ref,
                *, nb, seq, ctx_len):
    n = pl.program_id(1)
    hd, grp, lb = SWA_HEAD_DIM, SWA_HEADS // SWA_KV_HEADS, SWA_BLOCK
    m_rows, n_loc = grp * lb, 3 * lb
    r = lax.broadcasted_iota(jnp.int32, (m_rows, n_loc + ctx_len), 0)
    c = lax.broadcasted_iota(jnp.int32, (m_rows, n_loc + ctx_len), 1)
    qpos = n * lb + (r & (lb - 1))
    kpos = (n - 1) * lb + c
    local_ok = (jnp.abs(kpos - qpos) <= SWA_WINDOW) & (kpos >= 0) & (kpos < seq) & (n < nb)
    valid = local_ok | (c >= n_loc)
    ones = jnp.ones((n_loc + ctx_len, hd), BF16)
    outs = []
    for kv in range(SWA_KV_HEADS):
        ks = slice(kv * hd, (kv + 1) * hd)
        qg = jnp.concatenate([q_ref[:, (kv * grp + g) * hd:(kv * grp + g + 1) * hd] for g in range(grp)], axis=0)
        keys = jnp.concatenate([kp_ref[:, ks], kc_ref[:, ks], kn_ref[:, ks], kx_ref[:, ks]], axis=0)
        vals = jnp.concatenate([vp_ref[:, ks], vc_ref[:, ks], vn_ref[:, ks], vx_ref[:, ks]], axis=0)
        v_aug = jnp.concatenate([vals, ones], axis=1)
        s = jnp.where(valid, _dot_nt(qg, keys), -1e30)
        for g in range(grp):
            sg = s[g * lb:(g + 1) * lb]
            snk = sink_ref[kv * grp + g]
            m = jnp.maximum(jnp.max(sg, axis=1, keepdims=True), snk)
            ov = _dot(jnp.exp(sg - m).astype(BF16), v_aug)
            outs.append((ov[:, :hd] / (ov[:, hd:hd + 1] + jnp.exp(snk - m))).astype(BF16))
    o_ref[...] = jnp.concatenate(outs, axis=1)


def swa_attention(qkv, sink, *, rows):
    b, seq, ctx_len = rows.batch, rows.seq, rows.ctx_len
    lb = SWA_BLOCK
    nb, ncb = seq // lb, ctx_len // lb
    lat_blocks = rows.n_lat // lb
    qw = SWA_HEADS * SWA_HEAD_DIM
    kw = SWA_KV_HEADS * SWA_HEAD_DIM
    kcol, vcol = qw // kw, qw // kw + 1

    def qidx(bi, n):
        return (jnp.where(n < nb, bi * nb + n, lat_blocks + bi * ncb + (n - nb)), 0)

    def kidx(off, col):
        return lambda bi, n: (bi * nb + jnp.clip(n + off, 0, nb - 1), col)

    def xidx(col):
        return lambda bi, n: (rows.n_lat // ctx_len + bi, col)

    kern = functools.partial(_swa_kernel, nb=nb, seq=seq, ctx_len=ctx_len)
    return pl.pallas_call(
        kern,
        grid=(b, nb + ncb),
        in_specs=[
            pl.BlockSpec(memory_space=pltpu.SMEM),
            pl.BlockSpec((lb, qw), qidx),
            pl.BlockSpec((lb, kw), kidx(-1, kcol)),
            pl.BlockSpec((lb, kw), kidx(0, kcol)),
            pl.BlockSpec((lb, kw), kidx(1, kcol)),
            pl.BlockSpec((ctx_len, kw), xidx(kcol)),
            pl.BlockSpec((lb, kw), kidx(-1, vcol)),
            pl.BlockSpec((lb, kw), kidx(0, vcol)),
            pl.BlockSpec((lb, kw), kidx(1, vcol)),
            pl.BlockSpec((ctx_len, kw), xidx(vcol)),
        ],
        out_specs=pl.BlockSpec((lb, qw), qidx),
        out_shape=jax.ShapeDtypeStruct((rows.n_all, qw), BF16),
        compiler_params=_params(("parallel", "parallel")),
        name="swa_attention",
    )(sink, qkv, qkv, qkv, qkv, qkv, qkv, qkv, qkv, qkv)


def _mlstm_kernel(q_ref, k_ref, v_ref, gi_ref, gf_ref, bi_ref, bf_ref, o_ref, c_ref, m_ref):
    d, c = pl.program_id(1), pl.program_id(2)
    L, H, dqk, dv = ML_CHUNK, ML_HEADS, ML_DQK, ML_DV
    scale = dqk ** -0.5

    @pl.when(c == 0)
    def _():
        c_ref[...] = jnp.zeros_like(c_ref)
        m_ref[...] = jnp.zeros_like(m_ref)

    row = lax.broadcasted_iota(jnp.int32, (L, L), 0)
    col = lax.broadcasted_iota(jnp.int32, (L, L), 1)
    sign = 1 - 2 * d
    mask = (col - row) * sign <= 0
    ones_mask = jnp.where(mask, 1.0, 0.0).astype(BF16)
    i_blk = gi_ref[...] + bi_ref[...]
    f_blk = jax.nn.log_sigmoid(gf_ref[...] + bf_ref[...])
    f_hi = f_blk.astype(BF16)
    r1 = f_blk - f_hi.astype(F32)
    f_mid = r1.astype(BF16)
    f_lo = (r1 - f_mid.astype(F32)).astype(BF16)
    b_blk = _dot(jnp.concatenate([ones_mask] * 3, axis=1), jnp.concatenate([f_hi, f_mid, f_lo], axis=0))
    b_end = jnp.sum(f_blk, axis=0, keepdims=True)
    e_rows = (i_blk - b_blk).T
    m_prev_blk = m_ref[...]
    dec = b_end - b_blk + i_blk
    m_new_blk = jnp.maximum(b_end + m_prev_blk, jnp.max(dec, axis=0, keepdims=True))
    ws_blk = jnp.exp(dec - m_new_blk)
    gs_blk = jnp.exp(b_end + m_prev_blk - m_new_blk)
    ones = jnp.ones((L, LANE), BF16)
    for h in range(H):
        qh = q_ref[:, h * dqk:(h + 1) * dqk]
        kh = k_ref[:, h * dqk:(h + 1) * dqk]
        v_aug = jnp.concatenate([v_ref[:, h * dv:(h + 1) * dv], ones], axis=1)
        e = jnp.where(mask, e_rows[h:h + 1, :], -jnp.inf)
        m_prev = m_prev_blk[:, h:h + 1]
        mm = jnp.maximum(m_prev, jnp.max(e, axis=1, keepdims=True))
        s = (_dot_nt(qh, kh) * jnp.exp(e - (mm - math.log(scale)))).astype(BF16)
        g = jnp.exp(m_prev - mm) * scale
        ct = c_ref[h]
        lhs = jnp.concatenate([s, (qh.astype(F32) * g).astype(BF16)], axis=1)
        res = _dot(lhs, jnp.concatenate([v_aug, ct.astype(BF16)], axis=0))
        floor = jnp.exp(-(b_blk[:, h:h + 1] + mm))
        hout = res[:, :dv] / jnp.maximum(jnp.abs(res[:, dv:dv + 1]), floor)
        o_ref[0, :, h * dv:(h + 1) * dv] = hout.astype(o_ref.dtype)
        kw = (kh.astype(F32) * ws_blk[:, h:h + 1]).astype(BF16)
        c_ref[h] = gs_blk[:, h:h + 1] * ct + _dot_tn(kw, v_aug)
    m_ref[...] = m_new_blk


def mlstm_scan(z, gates, gate_b, *, rows):
    b, seq, ctx_len = rows.batch, rows.seq, rows.ctx_len
    L = ML_CHUNK
    assert ctx_len == L and seq % L == 0
    nlc = seq // L
    hq, hv = ML_HEADS * ML_DQK, ML_HEADS * ML_DV

    def rb(bi, d, c):
        lat = bi * nlc + jnp.where(d == 0, c - 1, nlc - c)
        return jnp.where(c == 0, rows.n_lat // L + bi, lat)

    return pl.pallas_call(
        _mlstm_kernel,
        grid=(b, 2, nlc + 1),
        in_specs=[
            pl.BlockSpec((L, hq), lambda bi, d, c: (rb(bi, d, c), 0)),
            pl.BlockSpec((L, hq), lambda bi, d, c: (rb(bi, d, c), 1)),
            pl.BlockSpec((L, hv), lambda bi, d, c: (rb(bi, d, c), 1)),
            pl.BlockSpec((L, LANE), lambda bi, d, c: (rb(bi, d, c), 2 * d)),
            pl.BlockSpec((L, LANE), lambda bi, d, c: (rb(bi, d, c), 2 * d + 1)),
            pl.BlockSpec((1, LANE), lambda bi, d, c: (0, 2 * d)),
            pl.BlockSpec((1, LANE), lambda bi, d, c: (0, 2 * d + 1)),
        ],
        out_specs=pl.BlockSpec((1, L, hv), lambda bi, d, c: (d, rb(bi, d, c), 0)),
        out_shape=jax.ShapeDtypeStruct((2, rows.n_all, hv), BF16),
        scratch_shapes=[
            pltpu.VMEM((ML_HEADS, ML_DQK, ML_DV + LANE), F32),
            pltpu.VMEM((1, LANE), F32),
        ],
        compiler_params=_params(("parallel", "parallel", "arbitrary")),
        name="mlstm_scan",
    )(z, z, z, gates, gates, gate_b, gate_b)


def _mla_kernel(q_ref, kvl_ref, kvx_ref, krl_ref, krx_ref, o_ref, kk_ref, vt_ref, *, seq, sub, ck):
    @pl.when(pl.program_id(2) == 0)
    def _():
        kk_ref[:seq, :LANE] = kvl_ref[:, :LANE]
        kk_ref[:seq, LANE:] = krl_ref[...]
        kk_ref[seq:, :LANE] = kvx_ref[:, :LANE]
        kk_ref[seq:, LANE:] = krx_ref[...]
        vt_ref[:LANE, :seq] = kvl_ref[:, LANE:].astype(F32).T.astype(BF16)
        vt_ref[:LANE, seq:] = kvx_ref[:, LANE:].astype(F32).T.astype(BF16)
        vt_ref[LANE:, :] = jnp.ones((vt_ref.shape[0] - LANE, vt_ref.shape[1]), BF16)

    nsub = q_ref.shape[0] // sub
    n_keys = kk_ref.shape[0]
    bounds = [(k0, min(k0 + ck, n_keys)) for k0 in range(0, n_keys, ck)]
    nck = len(bounds)
    qs = [q_ref[s * sub:(s + 1) * sub, :] for s in range(nsub)]

    def scores(s, c):
        return _dot_nt(kk_ref[bounds[c][0]:bounds[c][1], :], qs[s])

    m = [None] * nsub
    acc = [None] * nsub
    st_next = [scores(s, 0) for s in range(nsub)]
    for c in range(nck):
        st = st_next
        if c + 1 < nck:
            st_next = [scores(s, c + 1) for s in range(nsub)]
        vt_c = vt_ref[:, bounds[c][0]:bounds[c][1]]
        for s in range(nsub):
            cm = jnp.max(st[s], axis=0, keepdims=True)
            m_new = cm if c == 0 else jnp.maximum(m[s], cm)
            pv = _dot(vt_c, jnp.exp(st[s] - m_new).astype(BF16))
            acc[s] = pv if c == 0 else acc[s] * jnp.exp(m[s] - m_new) + pv
            m[s] = m_new
    for s in range(nsub):
        o = acc[s][:LANE] / acc[s][LANE:LANE + 1]
        o_ref[s * sub:(s + 1) * sub, :] = o.T.astype(o_ref.dtype)


def mla_attention(q, kv, z, *, rows, tq=2048, sub=256, ck=2048):
    b, seq, ctx_len = rows.batch, rows.seq, rows.ctx_len
    tq = min(tq, seq)
    nq = seq // tq
    kr_col = z.shape[1] // LANE - 1
    ctx0 = rows.n_lat // ctx_len
    n_keys = seq + ctx_len
    ones_rows = 16
    return pl.pallas_call(
        functools.partial(_mla_kernel, seq=seq, sub=sub, ck=ck),
        grid=(b, MLA_HEADS, nq),
        in_specs=[
            pl.BlockSpec((tq, 2 * LANE), lambda bi, h, i: (bi * nq + i, h)),
            pl.BlockSpec((seq, 2 * LANE), lambda bi, h, i: (bi, h)),
            pl.BlockSpec((ctx_len, 2 * LANE), lambda bi, h, i: (ctx0 + bi, h)),
            pl.BlockSpec((seq, LANE), lambda bi, h, i: (bi, kr_col)),
            pl.BlockSpec((ctx_len, LANE), lambda bi, h, i: (ctx0 + bi, kr_col)),
        ],
        out_specs=pl.BlockSpec((tq, LANE), lambda bi, h, i: (bi * nq + i, h)),
        out_shape=jax.ShapeDtypeStruct((rows.n_lat, MLA_HEADS * MLA_V), BF16),
        scratch_shapes=[pltpu.VMEM((n_keys, 2 * LANE), BF16), pltpu.VMEM((LANE + ones_rows, n_keys), BF16)],
        compiler_params=_params(("parallel", "parallel", "arbitrary")),
        name="mla_attention",
    )(q, kv, kv, z, z)


def _hy_filter_kernel(ft_ref, w1_ref, b1_ref, w2_ref, b2_ref, w3_ref, b3_ref, fr_ref, fr3_ref, w4_ref, b4_ref,
                      dec_ref, o_ref, a_ref, *, back_from):
    j = pl.program_id(0)
    fw = HY_FILTER_W
    hdot = functools.partial(jnp.dot, preferred_element_type=F32, precision=HIGHEST)

    @pl.when(j == 0)
    def _():
        fr = fr_ref[...]
        a = jnp.sin(fr * (hdot(ft_ref[...], w1_ref[...]) + b1_ref[...]))
        a = jnp.sin(fr * (hdot(a, w2_ref[...]) + b2_ref[...]))
        a3 = jnp.sin(fr3_ref[...] * (hdot(a, w3_ref[...]) + b3_ref[...]))
        hi = a3.astype(BF16)
        lo = (a3 - hi.astype(F32)).astype(BF16)
        lane = lax.broadcasted_iota(jnp.int32, a3.shape, 1)
        a_ref[...] = jnp.where((lane >= fw) & (lane < 2 * fw), lo, hi)

    w4 = w4_ref[...]
    w_hi = w4.astype(BF16)
    w_lo = (w4 - w_hi.astype(F32)).astype(BF16)
    rhs = jnp.concatenate([w_hi, w_hi, w_lo, jnp.zeros_like(w_hi)], axis=0)
    t = ft_ref[:, 0:1]
    filt = (_dot(a_ref[...], rhs) + b4_ref[...]) * jnp.exp(-t * jnp.abs(dec_ref[...]))
    row = lax.broadcasted_iota(jnp.int32, filt.shape, 0)
    o_ref[...] = jnp.where(jnp.logical_and(row == 0, j >= back_from), 0.0, filt)


def hyena_filters(seq, w1, b1, w2, b2, w3, b3, w4, b4, freq, decay, tn=512):
    t = np.linspace(0.0, 1.0, seq)[:, None]
    w = (2.0 * math.pi / seq) * np.arange(seq)[:, None]
    bands = np.linspace(1e-4, HY_BANDS - 1, HY_BANDS)[None, :]
    feats = np.zeros((seq, LANE))
    feats[:, :1 + 2 * HY_BANDS] = np.concatenate([t, np.cos(bands * w), -np.sin(bands * w)], axis=-1)
    n = w4.shape[1]
    fw = HY_FILTER_W
    w1p = jnp.zeros((LANE, fw), F32).at[:w1.shape[0]].set(w1)
    row = lambda a: a.reshape(1, -1)
    tile3 = lambda a: jnp.concatenate([a, a, a, jnp.zeros_like(a)], axis=-1)
    full = lambda shape: pl.BlockSpec(shape, lambda j: (0, 0))
    return pl.pallas_call(
        functools.partial(_hy_filter_kernel, back_from=(n // 2) // tn),
        grid=(n // tn,),
        in_specs=[full((seq, LANE)), full((LANE, fw)), full((1, fw)), full((fw, fw)), full((1, fw)),
                  full((fw, 4 * fw)), full((1, 4 * fw)), full((1, fw)), full((1, 4 * fw)),
                  pl.BlockSpec((fw, tn), lambda j: (0, j)),
                  pl.BlockSpec((1, tn), lambda j: (0, j)),
                  pl.BlockSpec((1, tn), lambda j: (0, j))],
        out_specs=pl.BlockSpec((seq, tn), lambda j: (0, j)),
        out_shape=jax.ShapeDtypeStruct((seq, n), F32),
        scratch_shapes=[pltpu.VMEM((seq, 4 * fw), BF16)],
        compiler_params=_params(("arbitrary",)),
        name="hyena_filters",
    )(jnp.asarray(feats, F32), w1p, row(b1), w2, row(b2), tile3(w3), tile3(row(b3)), row(freq), tile3(row(freq)),
      w4, row(b4), row(decay))


def _dft_mats(seq, p):
    n = 2 * seq
    q = n // p
    k1n = q // 2 + 1
    hi, k1 = np.arange(q // 2), np.arange(k1n)
    th = 2 * np.pi * np.outer(k1, hi) / q
    ma = np.zeros((2 * k1n, q // 2))
    ma[0::2], ma[1::2] = np.cos(th), -np.sin(th)
    lo = np.arange(p)
    wf = np.zeros((k1n, 2 * p, 2 * p))
    wi = np.zeros((k1n, 2 * p, 2 * p))
    for k in k1:
        ph = -2 * np.pi * (np.outer(lo, lo) / p + k * lo[None, :] / n)
        er, ei = np.cos(ph), np.sin(ph)
        wf[k] = np.block([[er, -ei], [ei, er]])
        wi[k] = np.block([[er.T, ei.T], [-ei.T, er.T]])
    c = np.full(k1n, 2.0)
    c[0] = c[-1] = 1.0
    th2 = 2 * np.pi * np.outer(hi, k1) / q
    md = np.zeros((q // 2, 2 * k1n))
    md[:, 0::2], md[:, 1::2] = c * np.cos(th2) / n, -c * np.sin(th2) / n
    as_bf16 = lambda a: jnp.asarray(a, F32).astype(BF16)
    return ma, as_bf16(wf), as_bf16(wi), md


def _hy_short_kernel(x1_ref, x2_ref, v_ref, w_ref, b_ref, o1_ref, o2_ref, o3_ref):
    for k, (x_ref, o_ref) in enumerate(((x1_ref, o1_ref), (x2_ref, o2_ref), (v_ref, o3_ref))):
        x = x_ref[...].astype(F32)
        n = x.shape[0]
        row = lax.broadcasted_iota(jnp.int32, x.shape, 0)
        prev = jnp.where(row == 0, 0.0, pltpu.roll(x, 1, 0))
        nxt = jnp.where(row == n - 1, 0.0, pltpu.roll(x, n - 1, 0))
        w = w_ref[k]
        o_ref[...] = (prev * w[0:1] + x * w[1:2] + nxt * w[2:3] + b_ref[k]).astype(o_ref.dtype)


def hyena_short_conv(x3, conv_w, conv_b, *, rows, cb=256):
    b, seq = rows.batch, rows.seq
    d = x3.shape[1] // 3
    nblk = d // cb
    w = jnp.transpose(conv_w.reshape(HY_SHORT, 3, d), (1, 0, 2))
    bias = conv_b.reshape(3, 1, d)
    in_x = [pl.BlockSpec((seq, cb), lambda bi, j, k=k: (bi, k * nblk + j)) for k in range(3)]
    out = pl.BlockSpec((seq, cb), lambda bi, j: (bi, j))
    shape = jax.ShapeDtypeStruct((rows.n_lat, d), BF16)
    return pl.pallas_call(
        _hy_short_kernel,
        grid=(b, nblk),
        in_specs=in_x + [pl.BlockSpec((3, HY_SHORT, cb), lambda bi, j: (0, 0, j)),
                         pl.BlockSpec((3, 1, cb), lambda bi, j: (0, 0, j))],
        out_specs=[out, out, out],
        out_shape=[shape, shape, shape],
        compiler_params=_params(("parallel", "parallel")),
        name="hyena_short_conv",
    )(x3, x3, x3, w, bias)


def _hy_stage_a_kernel(x_ref, m_ref, o_ref):
    qh, g, wc = x_ref.shape
    x = x_ref[...].reshape(qh * g, wc).astype(BF16)
    o_ref[...] = _dot(m_ref[...], x).reshape(o_ref.shape).astype(o_ref.dtype)


def hyena_stage_a(x, mak, *, p, width=2048):
    bx, seq, c = x.shape
    g = HY_G
    qh = seq // p
    mo = mak.shape[0] // g
    x5 = x.reshape(bx, qh, p // g, g, c)
    return pl.pallas_call(
        _hy_stage_a_kernel,
        grid=(bx, p // g, c // width),
        in_specs=[pl.BlockSpec((None, qh, None, g, width), lambda bi, l, j: (bi, 0, l, 0, j)),
                  pl.BlockSpec(mak.shape, lambda bi, l, j: (0, 0))],
        out_specs=pl.BlockSpec((None, mo, g, width), lambda bi, l, j: (bi, 0, l, j)),
        out_shape=jax.ShapeDtypeStruct((bx, mo, p, c), BF16),
        compiler_params=_params(("parallel", "parallel", "parallel")),
        name="hyena_stage_a",
    )(x5, mak)


def _hy_spec_kernel(a0_ref, a1_ref, wf_ref, o_ref):
    p = o_ref.shape[2]
    cb = o_ref.shape[3]
    wf = wf_ref[0]
    x0 = _dot(wf, a0_ref[0].reshape(2 * p, cb))
    x1 = _dot(wf, a1_ref[0].reshape(2 * p, cb))
    o_ref[0, 0] = x0[:p] + x1[:p]
    o_ref[0, 1] = x0[p:] - x1[p:]


def hyena_spectrum(af, wf, *, d, cb=1024):
    k1n, _, p, _ = af.shape
    nblk = (HY_ORDER * d) // cb
    return pl.pallas_call(
        _hy_spec_kernel,
        grid=(k1n, nblk),
        in_specs=[pl.BlockSpec((1, 2, p, cb), lambda k, j: (k, 0, 0, j)),
                  pl.BlockSpec((1, 2, p, cb), lambda k, j: (k, 0, 0, nblk + j)),
                  pl.BlockSpec((1, 2 * p, 2 * p), lambda k, j: (k, 0, 0))],
        out_specs=pl.BlockSpec((1, 2, p, cb), lambda k, j: (k, 0, 0, j)),
        out_shape=jax.ShapeDtypeStruct((k1n, 2, p, HY_ORDER * d), F32),
        compiler_params=_params(("parallel", "parallel")),
        name="hyena_spectrum",
    )(af, af, wf)


def _hy_mid_kernel(a_ref, wf_ref, wi_ref, h_ref, o_ref):
    b, _, p, cb = a_ref.shape
    hr, hi = h_ref[0], h_ref[1]
    for bi in range(b):
        x = _dot(wf_ref[...], a_ref[bi].reshape(2 * p, cb))
        xr, xi = x[:p], x[p:]
        y = jnp.concatenate([xr * hr - xi * hi, xr * hi + xi * hr], axis=0).astype(BF16)
        o_ref[bi] = _dot(wi_ref[...], y).reshape(2, p, cb).astype(o_ref.dtype)


def hyena_mid(a, wf, wi, spec, *, order, d, cb=512):
    b, k1n, _, p, _ = a.shape
    nblk = d // cb
    data = pl.BlockSpec((b, None, 2, p, cb), lambda k, j: (0, k, 0, 0, j))
    mat = pl.BlockSpec((None, 2 * p, 2 * p), lambda k, j: (k, 0, 0))
    return pl.pallas_call(
        _hy_mid_kernel,
        grid=(k1n, nblk),
        in_specs=[data, mat, mat,
                  pl.BlockSpec((None, 2, p, cb), lambda k, j: (k, 0, 0, order * nblk + j))],
        out_specs=data,
        out_shape=jax.ShapeDtypeStruct(a.shape, BF16),
        compiler_params=_params(("parallel", "parallel")),
        name="hyena_mid",
    )(a, wf, wi, spec)


def _hy_stage_d_kernel(c_ref, m_ref, x_ref, z_ref, skip_ref, *rest, fused):
    ma_ref, o_ref, a_ref = rest if fused else (None, rest[0], None)
    mo, g, wc = c_ref.shape
    qh = x_ref.shape[0]
    conv = _dot(m_ref[...], c_ref[...].reshape(mo * g, wc))
    z = z_ref[...].reshape(qh * g, wc).astype(F32)
    x = x_ref[...].reshape(qh * g, wc).astype(F32)
    y = (x * (conv + z * skip_ref[...])).astype(o_ref.dtype)
    o_ref[...] = y.reshape(o_ref.shape)
    if fused:
        a_ref[...] = _dot(ma_ref[...], y).reshape(a_ref.shape).astype(a_ref.dtype)


def hyena_stage_d(c, mdk, xg, z, skip, *, p, mak=None, width=2048):
    b, mo, _, d = c.shape
    g = HY_G
    seq = xg.shape[1]
    qh = seq // p
    fused = mak is not None
    v5 = lambda a: a.reshape(b, qh, p // g, g, d)
    tspec = pl.BlockSpec((None, qh, None, g, width), lambda bi, l, j: (bi, 0, l, 0, j))
    fspec = pl.BlockSpec((None, mo, g, width), lambda bi, l, j: (bi, 0, l, j))
    in_specs = [fspec, pl.BlockSpec(mdk.shape, lambda bi, l, j: (0, 0)), tspec, tspec,
                pl.BlockSpec((1, width), lambda bi, l, j: (0, j))]
    args = [c, mdk, v5(xg), v5(z), skip.reshape(1, d)]
    out_specs, out_shape = [tspec], [jax.ShapeDtypeStruct((b, qh, p // g, g, d), BF16)]
    if fused:
        in_specs.append(pl.BlockSpec(mak.shape, lambda bi, l, j: (0, 0)))
        args.append(mak)
        out_specs.append(fspec)
        out_shape.append(jax.ShapeDtypeStruct(c.shape, BF16))
    outs = pl.pallas_call(
        functools.partial(_hy_stage_d_kernel, fused=fused),
        grid=(b, p // g, d // width),
        in_specs=in_specs,
        out_specs=out_specs,
        out_shape=out_shape,
        compiler_params=_params(("parallel", "parallel", "parallel")),
        name="hyena_stage_d",
    )(*args)
    y = outs[0].reshape(b, seq, d)
    return (y, outs[1]) if fused else y


def hyena_long_convs(x1, x2, v, filt, skip, *, rows):
    b, seq = rows.batch, rows.seq
    d = x1.shape[1]
    p = HY_P
    k1n = seq // p + 1
    ma, wf, wi, md = _dft_mats(seq, p)
    eye = np.eye(HY_G)
    as_bf16 = lambda a: jnp.asarray(a, F32).astype(BF16)
    mak, mdk = as_bf16(np.kron(ma, eye)), as_bf16(np.kron(md, eye))
    af = hyena_stage_a(filt.reshape(1, seq, filt.shape[1]), mak, p=p)
    spec = hyena_spectrum(af.reshape(k1n, 2, p, filt.shape[1]), wf, d=d)
    t3 = lambda a: a.reshape(b, seq, d)
    y = t3(v)
    a = hyena_stage_a(y, mak, p=p)
    c = hyena_mid(a.reshape(b, k1n, 2, p, d), wf, wi, spec, order=0, d=d)
    y, a = hyena_stage_d(c.reshape(a.shape), mdk, t3(x1), y, skip[0], p=p, mak=mak)
    c = hyena_mid(a.reshape(b, k1n, 2, p, d), wf, wi, spec, order=1, d=d)
    y = hyena_stage_d(c.reshape(a.shape), mdk, t3(x2), y, skip[1], p=p)
    return y.reshape(rows.n_lat, d)


def _mlstm_gate_weights(w_in, gate_b):
    h = ML_HEADS
    o4 = 2 * h * ML_DQK + 2 * h * ML_DV
    wg = w_in[:, o4:].reshape(-1, 2, 2, h)
    bg = gate_b.reshape(2, 2, h)
    w_out = jnp.zeros((w_in.shape[0], 4 * LANE), F32)
    b_out = jnp.zeros((1, 4 * LANE), F32)
    for d in range(2):
        for gate in range(2):
            lo = (2 * d + gate) * LANE
            w_out = w_out.at[:, lo:lo + h].set(wg[:, gate, d])
            b_out = b_out.at[0, lo:lo + h].set(bg[gate, d])
    return w_out, b_out


def _mla_weights(w_in, w_uq):
    r2 = MLA_Q_RANK + MLA_KV_RANK
    half = MLA_ROPE // 2

    def spread(w):
        z = jnp.zeros(w.shape[:-1] + (half,), w.dtype)
        return jnp.concatenate([w[..., :half], z, w[..., half:], z], axis=-1)

    w_in_p = jnp.concatenate([w_in[:, :r2], spread(w_in[:, r2:])], axis=1)
    wq = w_uq.reshape(w_uq.shape[0], MLA_HEADS, MLA_NOPE + MLA_ROPE)
    wq_p = jnp.concatenate([wq[..., :MLA_NOPE], spread(wq[..., MLA_NOPE:])], axis=-1)
    wq_p = wq_p * (MLA_NOPE + MLA_ROPE) ** -0.5
    return w_in_p, wq_p.reshape(w_uq.shape[0], MLA_HEADS * 2 * LANE)


def kernel(x, c, ctx, c_ctx, ada_w, ada_b, norm_g, mlp_w1, mlp_w2, swa_w_qkv, swa_sink, swa_w_o, ml_w_in, ml_gate_b, ml_head_g, ml_w_o, mla_w_in, mla_q_g, mla_kv_g, mla_w_uq, mla_w_ukv, mla_w_o, hy_w_in, hy_conv_w, hy_conv_b, hy_f_w1, hy_f_b1, hy_f_w2, hy_f_b2, hy_f_w3, hy_f_b3, hy_f_w4, hy_f_b4, hy_f_freq, hy_decay, hy_skip, hy_w_o):
    b, seq, d = x.shape
    ctx_len = ctx.shape[1]
    depth = ada_w.shape[0]
    assert depth == 4 and b < 8
    rows = Rows(b, seq, ctx_len)
    tm = 512
    tm_lin = 1024 if rows.n_lat % 1024 == 0 and rows.n_ctx % 1024 == 0 else 512
    bf = lambda w: w.astype(BF16)
    mlp_w1b, mlp_w2b = bf(mlp_w1), bf(mlp_w2)

    cond = jnp.zeros((8, d), F32).at[:b].set(c).at[b].set(c_ctx)
    mods = ada_mods(cond, ada_w, ada_b).reshape(depth * 8, 1, N_MOD * d)

    common = dict(rows=rows, tm=tm_lin, mods=mods)

    def finish(h, o_args, w_o, layer, n_rows, mode="plain"):
        h = outproj_residual(o_args, bf(w_o), h, mods, norm_g[layer, 1], rows=rows, n_rows=n_rows, tm=tm,
                             layer=layer, mode=mode, name=f"outproj{layer}")
        return mlp_residual(h, mlp_w1b, mlp_w2b, mods, norm_g[layer, 2], norm_g[layer, 3],
                            rows=rows, n_rows=n_rows, tm=tm, tf=1024, layer=layer)

    n_qk_groups = SWA_HEADS + SWA_KV_HEADS
    q_cols = SWA_HEADS * SWA_HEAD_DIM
    w_qkv = jnp.concatenate([swa_w_qkv[0][:, :q_cols] * SWA_HEAD_DIM ** -0.5, swa_w_qkv[0][:, q_cols:]], axis=1)
    qkv, h = linear(x.reshape(rows.n_lat, d), bf(w_qkv), x2=ctx.reshape(rows.n_ctx, d), n_rows=rows.n_all, tn=512,
                    out_dtype=BF16, prologue="norm_mod",
                 gain=norm_g[0, 0], layer=0, rope_tabs=_rope_tables(seq, SWA_HEAD_DIM), rope_pattern=(True,) * 4,
                 rope_jmax=n_qk_groups // 4, name="swa_qkv", rows=rows, tm=tm, mods=mods)
    o = swa_attention(qkv, swa_sink[0], rows=rows)
    h = finish(h, o, swa_w_o[0], 0, rows.n_all)

    w_gate, b_gate = _mlstm_gate_weights(ml_w_in[0], ml_gate_b[0])
    z = linear(h, bf(ml_w_in[0]), n_rows=rows.n_all, tn=1024, out_dtype=BF16, prologue="norm_mod",
               gain=norm_g[1, 0], layer=1, n_out=2 * ML_HEADS * (ML_DQK + ML_DV), name="mlstm_in", **common)
    gates = linear(h, bf(w_gate), n_rows=rows.n_all, tn=4 * LANE, out_dtype=F32, prologue="norm_mod",
                   gain=norm_g[1, 0], layer=1, name="mlstm_gates", **common)
    hsum = mlstm_scan(z, gates, b_gate, rows=rows)
    h = finish(h, (hsum, z, ml_head_g[0]), ml_w_o[0], 1, rows.n_all, mode="mlstm")

    w_in_p, w_uq_p = _mla_weights(mla_w_in[0], mla_w_uq[0])
    rope_mla = _rope_tables(seq, MLA_ROPE)
    zc = linear(h, bf(w_in_p), n_rows=rows.n_all, tn=w_in_p.shape[1], out_dtype=BF16, prologue="norm_mod",
                gain=norm_g[2, 0], layer=2, rope_tabs=rope_mla, rope_pattern=(False,) * 8 + (True,),
                name="mla_in", **common)
    q = linear(zc, bf(w_uq_p), n_rows=rows.n_lat, tn=2048, out_dtype=BF16, x_cols=0, prologue="norm",
               gain=mla_q_g[0], rope_tabs=rope_mla, rope_pattern=(False, True) * 8, name="mla_q",
               rows=rows, tm=tm_lin)
    kv = linear(zc, bf(mla_w_ukv[0]), n_rows=rows.n_all, tn=2048, out_dtype=BF16, x_cols=1, prologue="norm",
                gain=mla_kv_g[0], name="mla_kv", rows=rows, tm=tm_lin)
    o = mla_attention(q, kv, zc, rows=rows)
    h = finish(h, o, mla_w_o[0], 2, rows.n_lat)

    x3 = linear(h, bf(hy_w_in[0]), n_rows=rows.n_lat, tn=1024, out_dtype=BF16, prologue="norm_mod",
                gain=norm_g[3, 0], layer=3, name="hyena_in", **common)
    x1, x2, v = hyena_short_conv(x3, hy_conv_w[0], hy_conv_b[0], rows=rows)
    filt = hyena_filters(seq, hy_f_w1[0], hy_f_b1[0], hy_f_w2[0], hy_f_b2[0], hy_f_w3[0], hy_f_b3[0],
                         hy_f_w4[0], hy_f_b4[0], hy_f_freq[0], hy_decay[0])
    y = hyena_long_convs(x1, x2, v, filt, hy_skip[0], rows=rows)
    h = finish(h, y, hy_w_o[0], 3, rows.n_lat)
    return h.reshape(b, seq, d)
```

```python
import functools
import math

import numpy as np
import jax
import jax.numpy as jnp
from jax import lax
from jax.experimental import pallas as pl
from jax.experimental.pallas import tpu as pltpu

F32, BF16 = jnp.float32, jnp.bfloat16
HIGHEST = lax.Precision.HIGHEST

RMS_EPS = 1e-6
ROPE_THETA = 10000.0
GRID_W = 64
N_MOD = 6
LANE = 128

SWA_HEADS, SWA_KV_HEADS, SWA_HEAD_DIM, SWA_WINDOW, SWA_BLOCK = 16, 4, 128, 128, 128
ML_HEADS, ML_DQK, ML_DV = 8, 128, 256
ML_CHUNK = 256
MLA_HEADS, MLA_Q_RANK, MLA_KV_RANK, MLA_NOPE, MLA_ROPE, MLA_V = 16, 512, 512, 128, 64, 128
HY_ORDER, HY_BANDS, HY_FILTER_W, HY_SHORT = 2, 16, 64, 3
HY_P = 256
HY_G = 16

VMEM_LIMIT = 48 * 1024 * 1024


def _params(sem):
    return pltpu.CompilerParams(dimension_semantics=sem, vmem_limit_bytes=VMEM_LIMIT)


def _dot(a, b):
    return jnp.dot(a, b, preferred_element_type=F32)


def _dot_nt(a, b):
    return lax.dot_general(a, b, (((1,), (1,)), ((), ())), preferred_element_type=F32)


def _dot_tn(a, b):
    return lax.dot_general(a, b, (((0,), (0,)), ((), ())), preferred_element_type=F32)


def _rms(x, g):
    return x * lax.rsqrt(jnp.mean(x * x, axis=-1, keepdims=True) + RMS_EPS) * g


def _rms_ref(x_ref, inv_ref, g):
    x = x_ref[...].astype(F32)
    inv_ref[...] = lax.rsqrt(jnp.mean(x * x, axis=-1, keepdims=True) + RMS_EPS)
    return x_ref[...].astype(F32) * inv_ref[...] * g


def _ada_kernel(s_ref, w_ref, b_ref, o_ref):
    s = s_ref[...]
    s = s * jax.nn.sigmoid(s)
    hi = s.astype(BF16)
    lo = (s - hi.astype(F32)).astype(BF16)
    w = w_ref[0]
    w_hi = w.astype(BF16)
    w_lo = (w - w_hi.astype(F32)).astype(BF16)
    r = _dot(jnp.concatenate([hi, lo], axis=0), w_hi)
    n = s.shape[0]
    o_ref[0] = r[:n] + r[n:] + _dot(hi, w_lo) + b_ref[0]


def ada_mods(cond, ada_w, ada_b, tn=1536):
    depth, d, n = ada_w.shape
    rows = cond.shape[0]
    return pl.pallas_call(
        _ada_kernel,
        grid=(depth, n // tn),
        in_specs=[
            pl.BlockSpec((rows, d), lambda l, j: (0, 0)),
            pl.BlockSpec((1, d, tn), lambda l, j: (l, 0, j)),
            pl.BlockSpec((1, 1, tn), lambda l, j: (l, 0, j)),
        ],
        out_specs=pl.BlockSpec((1, rows, tn), lambda l, j: (l, 0, j)),
        out_shape=jax.ShapeDtypeStruct((depth, rows, n), F32),
        compiler_params=_params(("parallel", "parallel")),
        name="ada_mods",
    )(cond, ada_w, ada_b.reshape(depth, 1, n))


class Rows:
    def __init__(self, batch, seq, ctx_len):
        self.batch, self.seq, self.ctx_len = batch, seq, ctx_len
        self.n_lat = batch * seq
        self.n_ctx = batch * ctx_len
        self.n_all = self.n_lat + self.n_ctx

    def mod_index(self, layer, k, tm):
        lat_blocks, per_batch = self.n_lat // tm, self.seq // tm

        def index(i, *_):
            b = jnp.where(i < lat_blocks, i // per_batch, self.batch)
            return (layer * 8 + b, 0, k)

        return index


def _linear_kernel(*refs, prologue, rope, rope_pattern, n_lat_blocks, dual):
    it = iter(refs)
    x_ref = next(it)
    x2_ref = next(it) if dual else None
    g_ref = next(it) if prologue in ("norm", "norm_mod") else None
    sh_ref = next(it) if prologue == "norm_mod" else None
    sc_ref = next(it) if prologue == "norm_mod" else None
    w_ref = next(it)
    cc_ref = next(it) if rope else None
    ss_ref = next(it) if rope else None
    o_ref = next(it)
    rows_ref = next(it) if dual else None
    u_ref = next(it)
    inv_ref = next(it)
    first = pl.program_id(1) == 0

    def prologue_from(src_ref):
        if prologue in ("norm", "norm_mod"):
            x = _rms_ref(src_ref, inv_ref, g_ref[...])
        else:
            x = src_ref[...].astype(F32)
        if prologue == "norm_mod":
            x = x * (1.0 + sc_ref[0]) + sh_ref[0]
        u_ref[...] = x.astype(BF16)
        if dual:
            rows_ref[...] = src_ref[...]

    if dual:
        is_lat = pl.program_id(0) < n_lat_blocks
        pl.when(jnp.logical_and(first, is_lat))(functools.partial(prologue_from, x_ref))
        pl.when(jnp.logical_and(first, jnp.logical_not(is_lat)))(functools.partial(prologue_from, x2_ref))
    else:
        pl.when(first)(functools.partial(prologue_from, x_ref))

    acc = _dot(u_ref[...], w_ref[...])

    def plain():
        o_ref[...] = acc.astype(o_ref.dtype)

    def roped():
        cc, ss = cc_ref[...], ss_ref[...]
        segs = []
        for gi, on in enumerate(rope_pattern):
            seg = acc[:, gi * LANE:(gi + 1) * LANE]
            if on:
                seg = seg * cc + pltpu.roll(seg, LANE // 2, 1) * ss
            segs.append(seg)
        o_ref[...] = jnp.concatenate(segs, axis=1).astype(o_ref.dtype)

    if rope:
        roped()
    else:
        plain()


def linear(x, w, *, rows, n_rows, tm, tn, out_dtype, x_cols=None, prologue="none", gain=None,
           mods=None, layer=0, mod_k=(0, 1), rope_tabs=None, rope_pattern=(), rope_jmax=1 << 30,
           n_out=None, x2=None, name="linear"):
    k, n = w.shape
    n = n if n_out is None else n_out
    xc = 0 if x_cols is None else x_cols
    rope = rope_tabs is not None
    dual = x2 is not None
    n_lat_blocks = rows.n_lat // tm
    if dual:
        in_specs = [pl.BlockSpec((tm, k), lambda i, j: (jnp.minimum(i, n_lat_blocks - 1), 0)),
                    pl.BlockSpec((tm, k), lambda i, j: (jnp.maximum(i - n_lat_blocks, 0), 0))]
        args = [x, x2]
    else:
        in_specs = [pl.BlockSpec((tm, k), lambda i, j: (i, xc))]
        args = [x]
    if prologue in ("norm", "norm_mod"):
        in_specs.append(pl.BlockSpec((1, k), lambda i, j: (0, 0)))
        args.append(gain.reshape(1, k))
    if prologue == "norm_mod":
        for mk in mod_k:
            in_specs.append(pl.BlockSpec((1, 1, k), rows.mod_index(layer, mk, tm)))
            args.append(mods)
    in_specs.append(pl.BlockSpec((k, tn), lambda i, j: (0, j)))
    args.append(w)
    if rope:
        per_seq = rows.seq // tm

        def tab_index(i, j):
            rotate = jnp.logical_and(i < n_lat_blocks, j < rope_jmax)
            return (jnp.where(rotate, 0, 1), i % per_seq, 0)

        for t, fill in zip(rope_tabs, (1.0, 0.0)):
            in_specs.append(pl.BlockSpec((None, tm, LANE), tab_index))
            args.append(jnp.stack([t, jnp.full_like(t, fill)]))
    kern = functools.partial(_linear_kernel, prologue=prologue, rope=rope, rope_pattern=tuple(rope_pattern),
                             n_lat_blocks=n_lat_blocks, dual=dual)
    out_specs = pl.BlockSpec((tm, tn), lambda i, j: (i, j))
    out_shape = jax.ShapeDtypeStruct((n_rows, n), out_dtype)
    if dual:
        out_specs = [out_specs, pl.BlockSpec((tm, k), lambda i, j: (i, 0))]
        out_shape = [out_shape, jax.ShapeDtypeStruct((n_rows, k), x.dtype)]
    return pl.pallas_call(
        kern,
        grid=(n_rows // tm, n // tn),
        in_specs=in_specs,
        out_specs=out_specs,
        out_shape=out_shape,
        scratch_shapes=[pltpu.VMEM((tm, k), BF16), pltpu.VMEM((tm, 1), F32)],
        compiler_params=_params(("parallel", "arbitrary")),
        name=name,
    )(*args)


def _outproj_kernel(*refs, mode):
    it = iter(refs)
    if mode == "mlstm":
        hf_ref, hb_ref, og_ref, hg_ref = next(it), next(it), next(it), next(it)
    else:
        o_ref_in = next(it)
    w_ref, h_ref, gate_ref, g_ref, out_ref = next(it), next(it), next(it), next(it), next(it)
    y_ref, inv_ref = next(it), next(it)
    if mode == "mlstm":
        hs = hf_ref[...].astype(F32) + hb_ref[...].astype(F32)
        og = jax.nn.sigmoid(og_ref[...].astype(F32))
        hg = hg_ref[...]
        parts = []
        for h in range(ML_HEADS):
            sl = slice(h * ML_DV, (h + 1) * ML_DV)
            parts.append((_rms(hs[:, sl], hg[:, sl]) * og[:, sl]).astype(BF16))
        o = jnp.concatenate(parts, axis=1)
    else:
        o = o_ref_in[...]
    y_ref[...] = _dot(o, w_ref[...])
    out_ref[...] = h_ref[...] + gate_ref[0] * _rms_ref(y_ref, inv_ref, g_ref[...])


def outproj_residual(o_args, w, h, mods, gain, *, rows, n_rows, tm, layer, mode="plain", name="outproj"):
    k, d = w.shape
    if mode == "mlstm":
        h_fwd, h_bwd, z, head_g = o_args
        in_specs = [
            pl.BlockSpec((tm, k), lambda i: (i, 0)),
            pl.BlockSpec((tm, k), lambda i: (i, 0)),
            pl.BlockSpec((tm, k), lambda i: (i, 2)),
            pl.BlockSpec((1, k), lambda i: (0, 0)),
        ]
        args = [h_fwd, h_bwd, z, head_g.reshape(1, k)]
    else:
        in_specs = [pl.BlockSpec((tm, k), lambda i: (i, 0))]
        args = [o_args]
    in_specs += [
        pl.BlockSpec((k, d), lambda i: (0, 0)),
        pl.BlockSpec((tm, d), lambda i: (i, 0)),
        pl.BlockSpec((1, 1, d), rows.mod_index(layer, 2, tm)),
        pl.BlockSpec((1, d), lambda i: (0, 0)),
    ]
    args += [w, h, mods, gain.reshape(1, d)]
    return pl.pallas_call(
        functools.partial(_outproj_kernel, mode=mode),
        grid=(n_rows // tm,),
        in_specs=in_specs,
        out_specs=pl.BlockSpec((tm, d), lambda i: (i, 0)),
        out_shape=jax.ShapeDtypeStruct((n_rows, d), F32),
        scratch_shapes=[pltpu.VMEM((tm, d), F32), pltpu.VMEM((tm, 1), F32)],
        compiler_params=_params(("parallel",)),
        name=name,
    )(*args)


def _mlp_kernel(h_ref, g2_ref, sh_ref, sc_ref, w1_ref, w2_ref, gate_ref, g3_ref, out_ref, v_ref, acc_ref,
                inv_ref):
    f = pl.program_id(1)

    @pl.when(f == 0)
    def _():
        v = _rms_ref(h_ref, inv_ref, g2_ref[...]) * (1.0 + sc_ref[0]) + sh_ref[0]
        v_ref[...] = v.astype(BF16)
        acc_ref[...] = jnp.zeros_like(acc_ref)

    a = jnp.maximum(_dot(v_ref[...], w1_ref[...]), 0.0)
    acc_ref[...] += _dot((a * a).astype(BF16), w2_ref[...])

    @pl.when(f == pl.num_programs(1) - 1)
    def _():
        out_ref[...] = h_ref[...] + gate_ref[0] * _rms_ref(acc_ref, inv_ref, g3_ref[...])


def mlp_residual(h, w1, w2, mods, g2, g3, *, rows, n_rows, tm, tf, layer):
    _, d, ff = w1.shape
    return pl.pallas_call(
        _mlp_kernel,
        grid=(n_rows // tm, ff // tf),
        in_specs=[
            pl.BlockSpec((tm, d), lambda i, f: (i, 0)),
            pl.BlockSpec((1, d), lambda i, f: (0, 0)),
            pl.BlockSpec((1, 1, d), rows.mod_index(layer, 3, tm)),
            pl.BlockSpec((1, 1, d), rows.mod_index(layer, 4, tm)),
            pl.BlockSpec((None, d, tf), lambda i, f: (layer, 0, f)),
            pl.BlockSpec((None, tf, d), lambda i, f: (layer, f, 0)),
            pl.BlockSpec((1, 1, d), rows.mod_index(layer, 5, tm)),
            pl.BlockSpec((1, d), lambda i, f: (0, 0)),
        ],
        out_specs=pl.BlockSpec((tm, d), lambda i, f: (i, 0)),
        out_shape=jax.ShapeDtypeStruct((n_rows, d), F32),
        scratch_shapes=[pltpu.VMEM((tm, d), BF16), pltpu.VMEM((tm, d), F32), pltpu.VMEM((tm, 1), F32)],
        compiler_params=_params(("parallel", "arbitrary")),
        name="mlp",
    )(h, g2.reshape(1, d), mods, mods, w1, w2, mods, g3.reshape(1, d))


def _rope_tables(seq, d_rot):
    pos = np.arange(seq)
    row, col = pos // GRID_W, pos % GRID_W
    n = d_rot // 4
    inv = ROPE_THETA ** (-np.arange(n, dtype=np.float64) / n)
    ang = np.concatenate([row[:, None] * inv, col[:, None] * inv], axis=-1)
    cos, sin = np.cos(ang), np.sin(ang)
    half = d_rot // 2
    cc = np.zeros((seq, LANE))
    ss = np.zeros((seq, LANE))
    cc[:, :half] = cos
    cc[:, LANE // 2:LANE // 2 + half] = cos
    ss[:, :half] = -sin
    ss[:, LANE // 2:LANE // 2 + half] = sin
    return jnp.asarray(cc, F32), jnp.asarray(ss, F32)


def _swa_kernel(sink_ref, q_ref, kp_ref, kc_ref, kn_ref, kx_ref, vp_ref, vc_ref, vn_ref, vx_ref, o_ref,
                *, nb, seq, ctx_len):
    n = pl.program_id(1)
    hd, grp, lb = SWA_HEAD_DIM, SWA_HEADS // SWA_KV_HEADS, SWA_BLOCK
    m_rows, n_loc = grp * lb, 3 * lb
    r = lax.broadcasted_iota(jnp.int32, (m_rows, n_loc + ctx_len), 0)
    c = lax.broadcasted_iota(jnp.int32, (m_rows, n_loc + ctx_len), 1)
    qpos = n * lb + (r & (lb - 1))
    kpos = (n - 1) * lb + c
    local_ok = (jnp.abs(kpos - qpos) <= SWA_WINDOW) & (kpos >= 0) & (kpos < seq) & (n < nb)
    valid = local_ok | (c >= n_loc)
    ones = jnp.ones((n_loc + ctx_len, hd), BF16)
    outs = []
    for kv in range(SWA_KV_HEADS):
        ks = slice(kv * hd, (kv + 1) * hd)
        qg = jnp.concatenate([q_ref[:, (kv * grp + g) * hd:(kv * grp + g + 1) * hd] for g in range(grp)], axis=0)
        keys = jnp.concatenate([kp_ref[:, ks], kc_ref[:, ks], kn_ref[:, ks], kx_ref[:, ks]], axis=0)
        vals = jnp.concatenate([vp_ref[:, ks], vc_ref[:, ks], vn_ref[:, ks], vx_ref[:, ks]], axis=0)
        v_aug = jnp.concatenate([vals, ones], axis=1)
        s = jnp.where(valid, _dot_nt(qg, keys), -1e30)
        for g in range(grp):
            sg = s[g * lb:(g + 1) * lb]
            snk = sink_ref[kv * grp + g] * math.log2(math.e)
            m = jnp.maximum(jnp.max(sg, axis=1, keepdims=True), snk)
            ov = _dot(jnp.exp2(sg - m).astype(BF16), v_aug)
            outs.append((ov[:, :hd] / (ov[:, hd:hd + 1] + jnp.exp2(snk - m))).astype(BF16))
    o_ref[...] = jnp.concatenate(outs, axis=1)


def swa_attention(qkv, sink, *, rows):
    b, seq, ctx_len = rows.batch, rows.seq, rows.ctx_len
    lb = SWA_BLOCK
    nb, ncb = seq // lb, ctx_len // lb
    lat_blocks = rows.n_lat // lb
    qw = SWA_HEADS * SWA_HEAD_DIM
    kw = SWA_KV_HEADS * SWA_HEAD_DIM
    kcol, vcol = qw // kw, qw // kw + 1

    def qidx(bi, n):
        return (jnp.where(n < nb, bi * nb + n, lat_blocks + bi * ncb + (n - nb)), 0)

    def kidx(off, col):
        return lambda bi, n: (bi * nb + jnp.clip(n + off, 0, nb - 1), col)

    def xidx(col):
        return lambda bi, n: (rows.n_lat // ctx_len + bi, col)

    kern = functools.partial(_swa_kernel, nb=nb, seq=seq, ctx_len=ctx_len)
    return pl.pallas_call(
        kern,
        grid=(b, nb + ncb),
        in_specs=[
            pl.BlockSpec(memory_space=pltpu.SMEM),
            pl.BlockSpec((lb, qw), qidx),
            pl.BlockSpec((lb, kw), kidx(-1, kcol)),
            pl.BlockSpec((lb, kw), kidx(0, kcol)),
            pl.BlockSpec((lb, kw), kidx(1, kcol)),
            pl.BlockSpec((ctx_len, kw), xidx(kcol)),
            pl.BlockSpec((lb, kw), kidx(-1, vcol)),
            pl.BlockSpec((lb, kw), kidx(0, vcol)),
            pl.BlockSpec((lb, kw), kidx(1, vcol)),
            pl.BlockSpec((ctx_len, kw), xidx(vcol)),
        ],
        out_specs=pl.BlockSpec((lb, qw), qidx),
        out_shape=jax.ShapeDtypeStruct((rows.n_all, qw), BF16),
        compiler_params=_params(("parallel", "parallel")),
        name="swa_attention",
    )(sink, qkv, qkv, qkv, qkv, qkv, qkv, qkv, qkv, qkv)


def _mlstm_kernel(*refs):
    ins, (of_ref, ob_ref, c_ref, m_ref) = refs[:14], refs[14:]
    c = pl.program_id(1)
    L, H, dqk, dv = ML_CHUNK, ML_HEADS, ML_DQK, ML_DV
    scale = dqk ** -0.5

    @pl.when(c == 0)
    def _():
        c_ref[...] = jnp.zeros_like(c_ref)
        m_ref[...] = jnp.zeros_like(m_ref)

    row = lax.broadcasted_iota(jnp.int32, (L, L), 0)
    col = lax.broadcasted_iota(jnp.int32, (L, L), 1)
    ones = jnp.ones((L, LANE), BF16)
    for d, o_ref in enumerate((of_ref, ob_ref)):
        q_ref, k_ref, v_ref, gi_ref, gf_ref, bi_ref, bf_ref = ins[7 * d:7 * d + 7]
        mask = (col <= row) if d == 0 else (col >= row)
        ones_mask = jnp.where(mask, 1.0, 0.0).astype(BF16)
        i_blk = gi_ref[...] + bi_ref[...]
        f_blk = jax.nn.log_sigmoid(gf_ref[...] + bf_ref[...])
        f_hi = f_blk.astype(BF16)
        r1 = f_blk - f_hi.astype(F32)
        f_mid = r1.astype(BF16)
        f_lo = (r1 - f_mid.astype(F32)).astype(BF16)
        b_blk = _dot(jnp.concatenate([ones_mask] * 3, axis=1), jnp.concatenate([f_hi, f_mid, f_lo], axis=0))
        b_end = jnp.sum(f_blk, axis=0, keepdims=True)
        e_rows = (i_blk - b_blk).T
        m_prev_blk = m_ref[d]
        dec = b_end - b_blk + i_blk
        m_new_blk = jnp.maximum(b_end + m_prev_blk, jnp.max(dec, axis=0, keepdims=True))
        ws_blk = jnp.exp(dec - m_new_blk)
        gs_blk = jnp.exp(b_end + m_prev_blk - m_new_blk)
        for h in range(H):
            qh = q_ref[:, h * dqk:(h + 1) * dqk]
            kh = k_ref[:, h * dqk:(h + 1) * dqk]
            v_aug = jnp.concatenate([v_ref[:, h * dv:(h + 1) * dv], ones], axis=1)
            e = jnp.where(mask, e_rows[h:h + 1, :], -jnp.inf)
            m_prev = m_prev_blk[:, h:h + 1]
            mm = jnp.maximum(m_prev, jnp.max(e, axis=1, keepdims=True))
            s = (_dot_nt(qh, kh) * jnp.exp(e - (mm - math.log(scale)))).astype(BF16)
            g = jnp.exp(m_prev - mm) * scale
            ct = c_ref[d, h]
            lhs = jnp.concatenate([s, (qh.astype(F32) * g).astype(BF16)], axis=1)
            res = _dot(lhs, jnp.concatenate([v_aug, ct.astype(BF16)], axis=0))
            floor = jnp.exp(-(b_blk[:, h:h + 1] + mm))
            hout = res[:, :dv] / jnp.maximum(jnp.abs(res[:, dv:dv + 1]), floor)
            o_ref[:, h * dv:(h + 1) * dv] = hout.astype(o_ref.dtype)
            kw = (kh.astype(F32) * ws_blk[:, h:h + 1]).astype(BF16)
            c_ref[d, h] = gs_blk[:, h:h + 1] * ct + _dot_tn(kw, v_aug)
        m_ref[d] = m_new_blk


def mlstm_scan(z, gates, gate_b, *, rows):
    b, seq, ctx_len = rows.batch, rows.seq, rows.ctx_len
    L = ML_CHUNK
    assert ctx_len == L and seq % L == 0
    nlc = seq // L
    hq, hv = ML_HEADS * ML_DQK, ML_HEADS * ML_DV

    def rb(d):
        def index(bi, c):
            lat = bi * nlc + (c - 1 if d == 0 else nlc - c)
            return jnp.where(c == 0, rows.n_lat // L + bi, lat)
        return index

    in_specs, args = [], []
    for d in range(2):
        r = rb(d)
        in_specs += [
            pl.BlockSpec((L, hq), lambda bi, c, r=r: (r(bi, c), 0)),
            pl.BlockSpec((L, hq), lambda bi, c, r=r: (r(bi, c), 1)),
            pl.BlockSpec((L, hv), lambda bi, c, r=r: (r(bi, c), 1)),
            pl.BlockSpec((L, LANE), lambda bi, c, r=r, d=d: (r(bi, c), 2 * d)),
            pl.BlockSpec((L, LANE), lambda bi, c, r=r, d=d: (r(bi, c), 2 * d + 1)),
            pl.BlockSpec((1, LANE), lambda bi, c, d=d: (0, 2 * d)),
            pl.BlockSpec((1, LANE), lambda bi, c, d=d: (0, 2 * d + 1)),
        ]
        args += [z, z, z, gates, gates, gate_b, gate_b]
    return pl.pallas_call(
        _mlstm_kernel,
        grid=(b, nlc + 1),
        in_specs=in_specs,
        out_specs=[pl.BlockSpec((L, hv), lambda bi, c, r=rb(d): (r(bi, c), 0)) for d in range(2)],
        out_shape=[jax.ShapeDtypeStruct((rows.n_all, hv), BF16)] * 2,
        scratch_shapes=[
            pltpu.VMEM((2, ML_HEADS, ML_DQK, ML_DV + LANE), F32),
            pltpu.VMEM((2, 1, LANE), F32),
        ],
        compiler_params=_params(("parallel", "arbitrary")),
        name="mlstm_scan",
    )(*args)


def _mla_kernel(q_ref, kvl_ref, kvx_ref, krl_ref, krx_ref, o_ref, kk_ref, vt_ref, *, seq, sub, ck):
    @pl.when(pl.program_id(2) == 0)
    def _():
        kk_ref[:seq, :LANE] = kvl_ref[:, :LANE]
        kk_ref[:seq, LANE:] = krl_ref[...]
        kk_ref[seq:, :LANE] = kvx_ref[:, :LANE]
        kk_ref[seq:, LANE:] = krx_ref[...]
        vt_ref[:LANE, :seq] = kvl_ref[:, LANE:].astype(F32).T.astype(BF16)
        vt_ref[:LANE, seq:] = kvx_ref[:, LANE:].astype(F32).T.astype(BF16)
        vt_ref[LANE:, :] = jnp.ones((vt_ref.shape[0] - LANE, vt_ref.shape[1]), BF16)

    nsub = q_ref.shape[0] // sub
    n_keys = kk_ref.shape[0]
    bounds = [(k0, min(k0 + ck, n_keys)) for k0 in range(0, n_keys, ck)]
    nck = len(bounds)
    qs = [q_ref[s * sub:(s + 1) * sub, :] for s in range(nsub)]

    def scores(s, c):
        return _dot_nt(kk_ref[bounds[c][0]:bounds[c][1], :], qs[s])

    m = [None] * nsub
    acc = [None] * nsub
    st_next = [scores(s, 0) for s in range(nsub)]
    for c in range(nck):
        st = st_next
        if c + 1 < nck:
            st_next = [scores(s, c + 1) for s in range(nsub)]
        vt_c = vt_ref[:, bounds[c][0]:bounds[c][1]]
        for s in range(nsub):
            cm = jnp.max(st[s], axis=0, keepdims=True)
            m_new = cm if c == 0 else jnp.maximum(m[s], cm)
            pv = _dot(vt_c, jnp.exp2(st[s] - m_new).astype(BF16))
            acc[s] = pv if c == 0 else acc[s] * jnp.exp2(m[s] - m_new) + pv
            m[s] = m_new
    for s in range(nsub):
        o = acc[s][:LANE] / acc[s][LANE:LANE + 1]
        o_ref[s * sub:(s + 1) * sub, :] = o.T.astype(o_ref.dtype)


def mla_attention(q, kv, z, *, rows, tq=2048, sub=256, ck=2048):
    b, seq, ctx_len = rows.batch, rows.seq, rows.ctx_len
    tq = min(tq, seq)
    nq = seq // tq
    kr_col = z.shape[1] // LANE - 1
    ctx0 = rows.n_lat // ctx_len
    n_keys = seq + ctx_len
    ones_rows = 16
    return pl.pallas_call(
        functools.partial(_mla_kernel, seq=seq, sub=sub, ck=ck),
        grid=(b, MLA_HEADS, nq),
        in_specs=[
            pl.BlockSpec((tq, 2 * LANE), lambda bi, h, i: (bi * nq + i, h)),
            pl.BlockSpec((seq, 2 * LANE), lambda bi, h, i: (bi, h)),
            pl.BlockSpec((ctx_len, 2 * LANE), lambda bi, h, i: (ctx0 + bi, h)),
            pl.BlockSpec((seq, LANE), lambda bi, h, i: (bi, kr_col)),
            pl.BlockSpec((ctx_len, LANE), lambda bi, h, i: (ctx0 + bi, kr_col)),
        ],
        out_specs=pl.BlockSpec((tq, LANE), lambda bi, h, i: (bi * nq + i, h)),
        out_shape=jax.ShapeDtypeStruct((rows.n_lat, MLA_HEADS * MLA_V), BF16),
        scratch_shapes=[pltpu.VMEM((n_keys, 2 * LANE), BF16), pltpu.VMEM((LANE + ones_rows, n_keys), BF16)],
        compiler_params=_params(("parallel", "parallel", "arbitrary")),
        name="mla_attention",
    )(q, kv, kv, z, z)


def _hy_filter_kernel(ft_ref, w1_ref, b1_ref, w2_ref, b2_ref, w3_ref, b3_ref, fr_ref, fr3_ref, w4_ref, b4_ref,
                      dec_ref, o_ref, a_ref, *, back_from):
    j = pl.program_id(0)
    fw = HY_FILTER_W
    hdot = functools.partial(jnp.dot, preferred_element_type=F32, precision=HIGHEST)

    @pl.when(j == 0)
    def _():
        fr = fr_ref[...]
        a = jnp.sin(fr * (hdot(ft_ref[...], w1_ref[...]) + b1_ref[...]))
        a = jnp.sin(fr * (hdot(a, w2_ref[...]) + b2_ref[...]))
        a3 = jnp.sin(fr3_ref[...] * (hdot(a, w3_ref[...]) + b3_ref[...]))
        hi = a3.astype(BF16)
        lo = (a3 - hi.astype(F32)).astype(BF16)
        lane = lax.broadcasted_iota(jnp.int32, a3.shape, 1)
        a_ref[...] = jnp.where((lane >= fw) & (lane < 2 * fw), lo, hi)

    w4 = w4_ref[...]
    w_hi = w4.astype(BF16)
    w_lo = (w4 - w_hi.astype(F32)).astype(BF16)
    rhs = jnp.concatenate([w_hi, w_hi, w_lo, jnp.zeros_like(w_hi)], axis=0)
    t = ft_ref[:, 0:1]
    filt = (_dot(a_ref[...], rhs) + b4_ref[...]) * jnp.exp(-t * jnp.abs(dec_ref[...]))
    row = lax.broadcasted_iota(jnp.int32, filt.shape, 0)
    o_ref[...] = jnp.where(jnp.logical_and(row == 0, j >= back_from), 0.0, filt)


def hyena_filters(seq, w1, b1, w2, b2, w3, b3, w4, b4, freq, decay, tn=512):
    t = np.linspace(0.0, 1.0, seq)[:, None]
    w = (2.0 * math.pi / seq) * np.arange(seq)[:, None]
    bands = np.linspace(1e-4, HY_BANDS - 1, HY_BANDS)[None, :]
    feats = np.zeros((seq, LANE))
    feats[:, :1 + 2 * HY_BANDS] = np.concatenate([t, np.cos(bands * w), -np.sin(bands * w)], axis=-1)
    n = w4.shape[1]
    fw = HY_FILTER_W
    w1p = jnp.zeros((LANE, fw), F32).at[:w1.shape[0]].set(w1)
    row = lambda a: a.reshape(1, -1)
    tile3 = lambda a: jnp.concatenate([a, a, a, jnp.zeros_like(a)], axis=-1)
    full = lambda shape: pl.BlockSpec(shape, lambda j: (0, 0))
    return pl.pallas_call(
        functools.partial(_hy_filter_kernel, back_from=(n // 2) // tn),
        grid=(n // tn,),
        in_specs=[full((seq, LANE)), full((LANE, fw)), full((1, fw)), full((fw, fw)), full((1, fw)),
                  full((fw, 4 * fw)), full((1, 4 * fw)), full((1, fw)), full((1, 4 * fw)),
                  pl.BlockSpec((fw, tn), lambda j: (0, j)),
                  pl.BlockSpec((1, tn), lambda j: (0, j)),
                  pl.BlockSpec((1, tn), lambda j: (0, j))],
        out_specs=pl.BlockSpec((seq, tn), lambda j: (0, j)),
        out_shape=jax.ShapeDtypeStruct((seq, n), F32),
        scratch_shapes=[pltpu.VMEM((seq, 4 * fw), BF16)],
        compiler_params=_params(("arbitrary",)),
        name="hyena_filters",
    )(jnp.asarray(feats, F32), w1p, row(b1), w2, row(b2), tile3(w3), tile3(row(b3)), row(freq), tile3(row(freq)),
      w4, row(b4), row(decay))


def _dft_mats(seq, p):
    n = 2 * seq
    q = n // p
    k1n = q // 2 + 1
    hi, k1 = np.arange(q // 2), np.arange(k1n)
    th = 2 * np.pi * np.outer(k1, hi) / q
    ma = np.zeros((2 * k1n, q // 2))
    ma[0::2], ma[1::2] = np.cos(th), -np.sin(th)
    lo = np.arange(p)
    wf = np.zeros((k1n, 2 * p, 2 * p))
    wi = np.zeros((k1n, 2 * p, 2 * p))
    for k in k1:
        ph = -2 * np.pi * (np.outer(lo, lo) / p + k * lo[None, :] / n)
        er, ei = np.cos(ph), np.sin(ph)
        wf[k] = np.block([[er, -ei], [ei, er]])
        wi[k] = np.block([[er.T, ei.T], [-ei.T, er.T]])
    c = np.full(k1n, 2.0)
    c[0] = c[-1] = 1.0
    th2 = 2 * np.pi * np.outer(hi, k1) / q
    md = np.zeros((q // 2, 2 * k1n))
    md[:, 0::2], md[:, 1::2] = c * np.cos(th2) / n, -c * np.sin(th2) / n
    as_bf16 = lambda a: jnp.asarray(a, F32).astype(BF16)
    return ma, as_bf16(wf), as_bf16(wi), md


def _hy_short_kernel(x1_ref, x2_ref, v_ref, w_ref, b_ref, o1_ref, o2_ref, o3_ref):
    for k, (x_ref, o_ref) in enumerate(((x1_ref, o1_ref), (x2_ref, o2_ref), (v_ref, o3_ref))):
        x = x_ref[...].astype(F32)
        n = x.shape[0]
        row = lax.broadcasted_iota(jnp.int32, x.shape, 0)
        prev = jnp.where(row == 0, 0.0, pltpu.roll(x, 1, 0))
        nxt = jnp.where(row == n - 1, 0.0, pltpu.roll(x, n - 1, 0))
        w = w_ref[k]
        o_ref[...] = (prev * w[0:1] + x * w[1:2] + nxt * w[2:3] + b_ref[k]).astype(o_ref.dtype)


def hyena_short_conv(x3, conv_w, conv_b, *, rows, cb=256):
    b, seq = rows.batch, rows.seq
    d = x3.shape[1] // 3
    nblk = d // cb
    w = jnp.transpose(conv_w.reshape(HY_SHORT, 3, d), (1, 0, 2))
    bias = conv_b.reshape(3, 1, d)
    in_x = [pl.BlockSpec((seq, cb), lambda bi, j, k=k: (bi, k * nblk + j)) for k in range(3)]
    out = pl.BlockSpec((seq, cb), lambda bi, j: (bi, j))
    shape = jax.ShapeDtypeStruct((rows.n_lat, d), BF16)
    return pl.pallas_call(
        _hy_short_kernel,
        grid=(b, nblk),
        in_specs=in_x + [pl.BlockSpec((3, HY_SHORT, cb), lambda bi, j: (0, 0, j)),
                         pl.BlockSpec((3, 1, cb), lambda bi, j: (0, 0, j))],
        out_specs=[out, out, out],
        out_shape=[shape, shape, shape],
        compiler_params=_params(("parallel", "parallel")),
        name="hyena_short_conv",
    )(x3, x3, x3, w, bias)


def _hy_stage_a_kernel(x_ref, m_ref, o_ref):
    qh, g, wc = x_ref.shape
    x = x_ref[...].reshape(qh * g, wc).astype(BF16)
    o_ref[...] = _dot(m_ref[...], x).reshape(o_ref.shape).astype(o_ref.dtype)


def hyena_stage_a(x, mak, *, p, width=2048):
    bx, seq, c = x.shape
    g = HY_G
    qh = seq // p
    mo = mak.shape[0] // g
    x5 = x.reshape(bx, qh, p // g, g, c)
    return pl.pallas_call(
        _hy_stage_a_kernel,
        grid=(bx, p // g, c // width),
        in_specs=[pl.BlockSpec((None, qh, None, g, width), lambda bi, l, j: (bi, 0, l, 0, j)),
                  pl.BlockSpec(mak.shape, lambda bi, l, j: (0, 0))],
        out_specs=pl.BlockSpec((None, mo, g, width), lambda bi, l, j: (bi, 0, l, j)),
        out_shape=jax.ShapeDtypeStruct((bx, mo, p, c), BF16),
        compiler_params=_params(("parallel", "parallel", "parallel")),
        name="hyena_stage_a",
    )(x5, mak)


def _hy_spec_kernel(a0_ref, a1_ref, wf_ref, o_ref):
    p = o_ref.shape[2]
    cb = o_ref.shape[3]
    wf = wf_ref[0]
    x0 = _dot(wf, a0_ref[0].reshape(2 * p, cb))
    x1 = _dot(wf, a1_ref[0].reshape(2 * p, cb))
    o_ref[0, 0] = x0[:p] + x1[:p]
    o_ref[0, 1] = x0[p:] - x1[p:]


def hyena_spectrum(af, wf, *, d, cb=2048):
    k1n, _, p, _ = af.shape
    nblk = (HY_ORDER * d) // cb
    return pl.pallas_call(
        _hy_spec_kernel,
        grid=(k1n, nblk),
        in_specs=[pl.BlockSpec((1, 2, p, cb), lambda k, j: (k, 0, 0, j)),
                  pl.BlockSpec((1, 2, p, cb), lambda k, j: (k, 0, 0, nblk + j)),
                  pl.BlockSpec((1, 2 * p, 2 * p), lambda k, j: (k, 0, 0))],
        out_specs=pl.BlockSpec((1, 2, p, cb), lambda k, j: (k, 0, 0, j)),
        out_shape=jax.ShapeDtypeStruct((k1n, 2, p, HY_ORDER * d), F32),
        compiler_params=_params(("parallel", "parallel")),
        name="hyena_spectrum",
    )(af, af, wf)


def _hy_mid_kernel(a_ref, wf_ref, wi_ref, h_ref, o_ref):
    b, _, p, cb = a_ref.shape
    hr, hi = h_ref[0], h_ref[1]
    for bi in range(b):
        x = _dot(wf_ref[...], a_ref[bi].reshape(2 * p, cb))
        xr, xi = x[:p], x[p:]
        y = jnp.concatenate([xr * hr - xi * hi, xr * hi + xi * hr], axis=0).astype(BF16)
        o_ref[bi] = _dot(wi_ref[...], y).reshape(2, p, cb).astype(o_ref.dtype)


def hyena_mid(a, wf, wi, spec, *, order, d, cb=512):
    b, k1n, _, p, _ = a.shape
    nblk = d // cb
    data = pl.BlockSpec((b, None, 2, p, cb), lambda k, j: (0, k, 0, 0, j))
    mat = pl.BlockSpec((None, 2 * p, 2 * p), lambda k, j: (k, 0, 0))
    return pl.pallas_call(
        _hy_mid_kernel,
        grid=(k1n, nblk),
        in_specs=[data, mat, mat,
                  pl.BlockSpec((None, 2, p, cb), lambda k, j: (k, 0, 0, order * nblk + j))],
        out_specs=data,
        out_shape=jax.ShapeDtypeStruct(a.shape, BF16),
        compiler_params=_params(("parallel", "parallel")),
        name="hyena_mid",
    )(a, wf, wi, spec)


def _hy_stage_d_kernel(c_ref, m_ref, x_ref, z_ref, skip_ref, *rest, fused):
    ma_ref, o_ref, a_ref = rest if fused else (None, rest[0], None)
    mo, g, wc = c_ref.shape
    qh = x_ref.shape[0]
    conv = _dot(m_ref[...], c_ref[...].reshape(mo * g, wc))
    z = z_ref[...].reshape(qh * g, wc).astype(F32)
    x = x_ref[...].reshape(qh * g, wc).astype(F32)
    y = (x * (conv + z * skip_ref[...])).astype(o_ref.dtype)
    o_ref[...] = y.reshape(o_ref.shape)
    if fused:
        a_ref[...] = _dot(ma_ref[...], y).reshape(a_ref.shape).astype(a_ref.dtype)


def hyena_stage_d(c, mdk, xg, z, skip, *, p, mak=None, width=2048):
    b, mo, _, d = c.shape
    g = HY_G
    seq = xg.shape[1]
    qh = seq // p
    fused = mak is not None
    v5 = lambda a: a.reshape(b, qh, p // g, g, d)
    tspec = pl.BlockSpec((None, qh, None, g, width), lambda bi, l, j: (bi, 0, l, 0, j))
    fspec = pl.BlockSpec((None, mo, g, width), lambda bi, l, j: (bi, 0, l, j))
    in_specs = [fspec, pl.BlockSpec(mdk.shape, lambda bi, l, j: (0, 0)), tspec, tspec,
                pl.BlockSpec((1, width), lambda bi, l, j: (0, j))]
    args = [c, mdk, v5(xg), v5(z), skip.reshape(1, d)]
    out_specs, out_shape = [tspec], [jax.ShapeDtypeStruct((b, qh, p // g, g, d), BF16)]
    if fused:
        in_specs.append(pl.BlockSpec(mak.shape, lambda bi, l, j: (0, 0)))
        args.append(mak)
        out_specs.append(fspec)
        out_shape.append(jax.ShapeDtypeStruct(c.shape, BF16))
    outs = pl.pallas_call(
        functools.partial(_hy_stage_d_kernel, fused=fused),
        grid=(b, p // g, d // width),
        in_specs=in_specs,
        out_specs=out_specs,
        out_shape=out_shape,
        compiler_params=_params(("parallel", "parallel", "parallel")),
        name="hyena_stage_d",
    )(*args)
    y = outs[0].reshape(b, seq, d)
    return (y, outs[1]) if fused else y


def hyena_long_convs(x1, x2, v, filt, skip, *, rows):
    b, seq = rows.batch, rows.seq
    d = x1.shape[1]
    p = HY_P
    k1n = seq // p + 1
    ma, wf, wi, md = _dft_mats(seq, p)
    eye = np.eye(HY_G)
    as_bf16 = lambda a: jnp.asarray(a, F32).astype(BF16)
    mak, mdk = as_bf16(np.kron(ma, eye)), as_bf16(np.kron(md, eye))
    af = hyena_stage_a(filt.reshape(1, seq, filt.shape[1]), mak, p=p)
    spec = hyena_spectrum(af.reshape(k1n, 2, p, filt.shape[1]), wf, d=d)
    t3 = lambda a: a.reshape(b, seq, d)
    y = t3(v)
    a = hyena_stage_a(y, mak, p=p)
    c = hyena_mid(a.reshape(b, k1n, 2, p, d), wf, wi, spec, order=0, d=d)
    y, a = hyena_stage_d(c.reshape(a.shape), mdk, t3(x1), y, skip[0], p=p, mak=mak)
    c = hyena_mid(a.reshape(b, k1n, 2, p, d), wf, wi, spec, order=1, d=d)
    y = hyena_stage_d(c.reshape(a.shape), mdk, t3(x2), y, skip[1], p=p)
    return y.reshape(rows.n_lat, d)


def _mlstm_gate_weights(w_in, gate_b):
    h = ML_HEADS
    o4 = 2 * h * ML_DQK + 2 * h * ML_DV
    wg = w_in[:, o4:].reshape(-1, 2, 2, h)
    bg = gate_b.reshape(2, 2, h)
    w_out = jnp.zeros((w_in.shape[0], 4 * LANE), F32)
    b_out = jnp.zeros((1, 4 * LANE), F32)
    for d in range(2):
        for gate in range(2):
            lo = (2 * d + gate) * LANE
            w_out = w_out.at[:, lo:lo + h].set(wg[:, gate, d])
            b_out = b_out.at[0, lo:lo + h].set(bg[gate, d])
    return w_out, b_out


def _mla_weights(w_in, w_uq):
    r2 = MLA_Q_RANK + MLA_KV_RANK
    half = MLA_ROPE // 2

    def spread(w):
        z = jnp.zeros(w.shape[:-1] + (half,), w.dtype)
        return jnp.concatenate([w[..., :half], z, w[..., half:], z], axis=-1)

    w_in_p = jnp.concatenate([w_in[:, :r2], spread(w_in[:, r2:])], axis=1)
    wq = w_uq.reshape(w_uq.shape[0], MLA_HEADS, MLA_NOPE + MLA_ROPE)
    wq_p = jnp.concatenate([wq[..., :MLA_NOPE], spread(wq[..., MLA_NOPE:])], axis=-1)
    wq_p = wq_p * ((MLA_NOPE + MLA_ROPE) ** -0.5 * math.log2(math.e))
    return w_in_p, wq_p.reshape(w_uq.shape[0], MLA_HEADS * 2 * LANE)


def kernel(x, c, ctx, c_ctx, ada_w, ada_b, norm_g, mlp_w1, mlp_w2, swa_w_qkv, swa_sink, swa_w_o, ml_w_in, ml_gate_b, ml_head_g, ml_w_o, mla_w_in, mla_q_g, mla_kv_g, mla_w_uq, mla_w_ukv, mla_w_o, hy_w_in, hy_conv_w, hy_conv_b, hy_f_w1, hy_f_b1, hy_f_w2, hy_f_b2, hy_f_w3, hy_f_b3, hy_f_w4, hy_f_b4, hy_f_freq, hy_decay, hy_skip, hy_w_o):
    b, seq, d = x.shape
    ctx_len = ctx.shape[1]
    depth = ada_w.shape[0]
    assert depth == 4 and b < 8
    rows = Rows(b, seq, ctx_len)
    tm = 512
    tm_lin = 1024 if rows.n_lat % 1024 == 0 and rows.n_ctx % 1024 == 0 else 512
    bf = lambda w: w.astype(BF16)
    mlp_w1b, mlp_w2b = bf(mlp_w1), bf(mlp_w2)

    cond = jnp.zeros((8, d), F32).at[:b].set(c).at[b].set(c_ctx)
    mods = ada_mods(cond, ada_w, ada_b).reshape(depth * 8, 1, N_MOD * d)

    common = dict(rows=rows, tm=tm_lin, mods=mods)

    def finish(h, o_args, w_o, layer, n_rows, mode="plain"):
        h = outproj_residual(o_args, bf(w_o), h, mods, norm_g[layer, 1], rows=rows, n_rows=n_rows, tm=tm,
                             layer=layer, mode=mode, name=f"outproj{layer}")
        return mlp_residual(h, mlp_w1b, mlp_w2b, mods, norm_g[layer, 2], norm_g[layer, 3],
                            rows=rows, n_rows=n_rows, tm=tm, tf=1024, layer=layer)

    n_qk_groups = SWA_HEADS + SWA_KV_HEADS
    q_cols = SWA_HEADS * SWA_HEAD_DIM
    q_scale = SWA_HEAD_DIM ** -0.5 * math.log2(math.e)
    w_qkv = jnp.concatenate([swa_w_qkv[0][:, :q_cols] * q_scale, swa_w_qkv[0][:, q_cols:]], axis=1)
    qkv, h = linear(x.reshape(rows.n_lat, d), bf(w_qkv), x2=ctx.reshape(rows.n_ctx, d), n_rows=rows.n_all, tn=512,
                    out_dtype=BF16, prologue="norm_mod",
                 gain=norm_g[0, 0], layer=0, rope_tabs=_rope_tables(seq, SWA_HEAD_DIM), rope_pattern=(True,) * 4,
                 rope_jmax=n_qk_groups // 4, name="swa_qkv", rows=rows, tm=tm, mods=mods)
    o = swa_attention(qkv, swa_sink[0], rows=rows)
    h = finish(h, o, swa_w_o[0], 0, rows.n_all)

    w_gate, b_gate = _mlstm_gate_weights(ml_w_in[0], ml_gate_b[0])
    z = linear(h, bf(ml_w_in[0]), n_rows=rows.n_all, tn=1024, out_dtype=BF16, prologue="norm_mod",
               gain=norm_g[1, 0], layer=1, n_out=2 * ML_HEADS * (ML_DQK + ML_DV), name="mlstm_in", **common)
    gates = linear(h, bf(w_gate), n_rows=rows.n_all, tn=4 * LANE, out_dtype=F32, prologue="norm_mod",
                   gain=norm_g[1, 0], layer=1, name="mlstm_gates", **common)
    h_fwd, h_bwd = mlstm_scan(z, gates, b_gate, rows=rows)
    h = finish(h, (h_fwd, h_bwd, z, ml_head_g[0]), ml_w_o[0], 1, rows.n_all, mode="mlstm")

    w_in_p, w_uq_p = _mla_weights(mla_w_in[0], mla_w_uq[0])
    rope_mla = _rope_tables(seq, MLA_ROPE)
    zc = linear(h, bf(w_in_p), n_rows=rows.n_all, tn=w_in_p.shape[1], out_dtype=BF16, prologue="norm_mod",
                gain=norm_g[2, 0], layer=2, rope_tabs=rope_mla, rope_pattern=(False,) * 8 + (True,),
                name="mla_in", **common)
    q = linear(zc, bf(w_uq_p), n_rows=rows.n_lat, tn=2048, out_dtype=BF16, x_cols=0, prologue="norm",
               gain=mla_q_g[0], rope_tabs=rope_mla, rope_pattern=(False, True) * 8, name="mla_q",
               rows=rows, tm=tm_lin)
    kv = linear(zc, bf(mla_w_ukv[0]), n_rows=rows.n_all, tn=2048, out_dtype=BF16, x_cols=1, prologue="norm",
                gain=mla_kv_g[0], name="mla_kv", rows=rows, tm=tm_lin)
    o = mla_attention(q, kv, zc, rows=rows)
    h = finish(h, o, mla_w_o[0], 2, rows.n_lat)

    x3 = linear(h, bf(hy_w_in[0]), n_rows=rows.n_lat, tn=1024, out_dtype=BF16, prologue="norm_mod",
                gain=norm_g[3, 0], layer=3, name="hyena_in", **common)
    x1, x2, v = hyena_short_conv(x3, hy_conv_w[0], hy_conv_b[0], rows=rows)
    filt = hyena_filters(seq, hy_f_w1[0], hy_f_b1[0], hy_f_w2[0], hy_f_b2[0], hy_f_w3[0], hy_f_b3[0],
                         hy_f_w4[0], hy_f_b4[0], hy_f_freq[0], hy_decay[0])
    y = hyena_long_convs(x1, x2, v, filt, hy_skip[0], rows=rows)
    h = finish(h, y, hy_w_o[0], 3, rows.n_lat)
    return h.reshape(b, seq, d)
```

```python
import functools
import math

import numpy as np
import jax
import jax.numpy as jnp
from jax import lax
from jax.experimental import pallas as pl
from jax.experimental.pallas import tpu as pltpu

F32, BF16 = jnp.float32, jnp.bfloat16
HIGHEST = lax.Precision.HIGHEST

RMS_EPS = 1e-6
ROPE_THETA = 10000.0
GRID_W = 64
N_MOD = 6
LANE = 128

SWA_HEADS, SWA_KV_HEADS, SWA_HEAD_DIM, SWA_WINDOW, SWA_BLOCK = 16, 4, 128, 128, 128
ML_HEADS, ML_DQK, ML_DV = 8, 128, 256
ML_CHUNK = 256
MLA_HEADS, MLA_Q_RANK, MLA_KV_RANK, MLA_NOPE, MLA_ROPE, MLA_V = 16, 512, 512, 128, 64, 128
HY_ORDER, HY_BANDS, HY_FILTER_W, HY_SHORT = 2, 16, 64, 3
HY_P = 256
HY_G = 16

VMEM_LIMIT = 48 * 1024 * 1024


def _params(sem):
    return pltpu.CompilerParams(dimension_semantics=sem, vmem_limit_bytes=VMEM_LIMIT)


def _dot(a, b):
    return jnp.dot(a, b, preferred_element_type=F32)


def _dot_nt(a, b):
    return lax.dot_general(a, b, (((1,), (1,)), ((), ())), preferred_element_type=F32)


def _dot_tn(a, b):
    return lax.dot_general(a, b, (((0,), (0,)), ((), ())), preferred_element_type=F32)


def _rms(x, g):
    return x * lax.rsqrt(jnp.mean(x * x, axis=-1, keepdims=True) + RMS_EPS) * g


def _rms_ref(x_ref, inv_ref, g):
    x = x_ref[...].astype(F32)
    inv_ref[...] = lax.rsqrt(jnp.mean(x * x, axis=-1, keepdims=True) + RMS_EPS)
    return x_ref[...].astype(F32) * inv_ref[...] * g


def _ada_kernel(s_ref, w_ref, b_ref, o_ref):
    s = s_ref[...]
    s = s * jax.nn.sigmoid(s)
    hi = s.astype(BF16)
    lo = (s - hi.astype(F32)).astype(BF16)
    w = w_ref[0]
    w_hi = w.astype(BF16)
    w_lo = (w - w_hi.astype(F32)).astype(BF16)
    r = _dot(jnp.concatenate([hi, lo], axis=0), w_hi)
    n = s.shape[0]
    o_ref[0] = r[:n] + r[n:] + _dot(hi, w_lo) + b_ref[0]


def ada_mods(cond, ada_w, ada_b, tn=1536):
    depth, d, n = ada_w.shape
    rows = cond.shape[0]
    return pl.pallas_call(
        _ada_kernel,
        grid=(depth, n // tn),
        in_specs=[
            pl.BlockSpec((rows, d), lambda l, j: (0, 0)),
            pl.BlockSpec((1, d, tn), lambda l, j: (l, 0, j)),
            pl.BlockSpec((1, 1, tn), lambda l, j: (l, 0, j)),
        ],
        out_specs=pl.BlockSpec((1, rows, tn), lambda l, j: (l, 0, j)),
        out_shape=jax.ShapeDtypeStruct((depth, rows, n), F32),
        compiler_params=_params(("parallel", "parallel")),
        name="ada_mods",
    )(cond, ada_w, ada_b.reshape(depth, 1, n))


class Rows:
    def __init__(self, batch, seq, ctx_len):
        self.batch, self.seq, self.ctx_len = batch, seq, ctx_len
        self.n_lat = batch * seq
        self.n_ctx = batch * ctx_len
        self.n_all = self.n_lat + self.n_ctx

    def mod_index(self, layer, k, tm):
        lat_blocks, per_batch = self.n_lat // tm, self.seq // tm

        def index(i, *_):
            b = jnp.where(i < lat_blocks, i // per_batch, self.batch)
            return (layer * 8 + b, 0, k)

        return index


def _linear_kernel(*refs, prologue, rope, rope_pattern, n_lat_blocks, dual):
    it = iter(refs)
    x_ref = next(it)
    x2_ref = next(it) if dual else None
    g_ref = next(it) if prologue in ("norm", "norm_mod") else None
    sh_ref = next(it) if prologue == "norm_mod" else None
    sc_ref = next(it) if prologue == "norm_mod" else None
    w_ref = next(it)
    cc_ref = next(it) if rope else None
    ss_ref = next(it) if rope else None
    o_ref = next(it)
    rows_ref = next(it) if dual else None
    u_ref = next(it)
    inv_ref = next(it)
    first = pl.program_id(1) == 0

    def prologue_from(src_ref):
        if prologue in ("norm", "norm_mod"):
            x = _rms_ref(src_ref, inv_ref, g_ref[...])
        else:
            x = src_ref[...].astype(F32)
        if prologue == "norm_mod":
            x = x * (1.0 + sc_ref[0]) + sh_ref[0]
        u_ref[...] = x.astype(BF16)
        if dual:
            rows_ref[...] = src_ref[...]

    if dual:
        is_lat = pl.program_id(0) < n_lat_blocks
        pl.when(jnp.logical_and(first, is_lat))(functools.partial(prologue_from, x_ref))
        pl.when(jnp.logical_and(first, jnp.logical_not(is_lat)))(functools.partial(prologue_from, x2_ref))
    else:
        pl.when(first)(functools.partial(prologue_from, x_ref))

    acc = _dot(u_ref[...], w_ref[...])

    def plain():
        o_ref[...] = acc.astype(o_ref.dtype)

    def roped():
        cc, ss = cc_ref[...], ss_ref[...]
        segs = []
        for gi, on in enumerate(rope_pattern):
            seg = acc[:, gi * LANE:(gi + 1) * LANE]
            if on:
                seg = seg * cc + pltpu.roll(seg, LANE // 2, 1) * ss
            segs.append(seg)
        o_ref[...] = jnp.concatenate(segs, axis=1).astype(o_ref.dtype)

    if rope:
        roped()
    else:
        plain()


def linear(x, w, *, rows, n_rows, tm, tn, out_dtype, x_cols=None, prologue="none", gain=None,
           mods=None, layer=0, mod_k=(0, 1), rope_tabs=None, rope_pattern=(), rope_jmax=1 << 30,
           n_out=None, x2=None, name="linear"):
    k, n = w.shape
    n = n if n_out is None else n_out
    xc = 0 if x_cols is None else x_cols
    rope = rope_tabs is not None
    dual = x2 is not None
    n_lat_blocks = rows.n_lat // tm
    if dual:
        in_specs = [pl.BlockSpec((tm, k), lambda i, j: (jnp.minimum(i, n_lat_blocks - 1), 0)),
                    pl.BlockSpec((tm, k), lambda i, j: (jnp.maximum(i - n_lat_blocks, 0), 0))]
        args = [x, x2]
    else:
        in_specs = [pl.BlockSpec((tm, k), lambda i, j: (i, xc))]
        args = [x]
    if prologue in ("norm", "norm_mod"):
        in_specs.append(pl.BlockSpec((1, k), lambda i, j: (0, 0)))
        args.append(gain.reshape(1, k))
    if prologue == "norm_mod":
        for mk in mod_k:
            in_specs.append(pl.BlockSpec((1, 1, k), rows.mod_index(layer, mk, tm)))
            args.append(mods)
    in_specs.append(pl.BlockSpec((k, tn), lambda i, j: (0, j)))
    args.append(w)
    if rope:
        per_seq = rows.seq // tm

        def tab_index(i, j):
            rotate = jnp.logical_and(i < n_lat_blocks, j < rope_jmax)
            return (jnp.where(rotate, 0, 1), i % per_seq, 0)

        for t, fill in zip(rope_tabs, (1.0, 0.0)):
            in_specs.append(pl.BlockSpec((None, tm, LANE), tab_index))
            args.append(jnp.stack([t, jnp.full_like(t, fill)]))
    kern = functools.partial(_linear_kernel, prologue=prologue, rope=rope, rope_pattern=tuple(rope_pattern),
                             n_lat_blocks=n_lat_blocks, dual=dual)
    out_specs = pl.BlockSpec((tm, tn), lambda i, j: (i, j))
    out_shape = jax.ShapeDtypeStruct((n_rows, n), out_dtype)
    if dual:
        out_specs = [out_specs, pl.BlockSpec((tm, k), lambda i, j: (i, 0))]
        out_shape = [out_shape, jax.ShapeDtypeStruct((n_rows, k), x.dtype)]
    return pl.pallas_call(
        kern,
        grid=(n_rows // tm, n // tn),
        in_specs=in_specs,
        out_specs=out_specs,
        out_shape=out_shape,
        scratch_shapes=[pltpu.VMEM((tm, k), BF16), pltpu.VMEM((tm, 1), F32)],
        compiler_params=_params(("parallel", "arbitrary")),
        name=name,
    )(*args)


def _outproj_kernel(*refs, mode):
    it = iter(refs)
    if mode == "mlstm":
        hf_ref, hb_ref, og_ref, hg_ref = next(it), next(it), next(it), next(it)
    else:
        o_ref_in = next(it)
    w_ref, h_ref, gate_ref, g_ref, out_ref = next(it), next(it), next(it), next(it), next(it)
    y_ref, inv_ref = next(it), next(it)
    if mode == "mlstm":
        hs = hf_ref[...].astype(F32) + hb_ref[...].astype(F32)
        og = jax.nn.sigmoid(og_ref[...].astype(F32))
        hg = hg_ref[...]
        parts = []
        for h in range(ML_HEADS):
            sl = slice(h * ML_DV, (h + 1) * ML_DV)
            parts.append((_rms(hs[:, sl], hg[:, sl]) * og[:, sl]).astype(BF16))
        o = jnp.concatenate(parts, axis=1)
    else:
        o = o_ref_in[...]
    y_ref[...] = _dot(o, w_ref[...])
    out_ref[...] = h_ref[...] + gate_ref[0] * _rms_ref(y_ref, inv_ref, g_ref[...])


def outproj_residual(o_args, w, h, mods, gain, *, rows, n_rows, tm, layer, mode="plain", name="outproj"):
    k, d = w.shape
    if mode == "mlstm":
        h_fwd, h_bwd, z, head_g = o_args
        in_specs = [
            pl.BlockSpec((tm, k), lambda i: (i, 0)),
            pl.BlockSpec((tm, k), lambda i: (i, 0)),
            pl.BlockSpec((tm, k), lambda i: (i, 2)),
            pl.BlockSpec((1, k), lambda i: (0, 0)),
        ]
        args = [h_fwd, h_bwd, z, head_g.reshape(1, k)]
    else:
        in_specs = [pl.BlockSpec((tm, k), lambda i: (i, 0))]
        args = [o_args]
    in_specs += [
        pl.BlockSpec((k, d), lambda i: (0, 0)),
        pl.BlockSpec((tm, d), lambda i: (i, 0)),
        pl.BlockSpec((1, 1, d), rows.mod_index(layer, 2, tm)),
        pl.BlockSpec((1, d), lambda i: (0, 0)),
    ]
    args += [w, h, mods, gain.reshape(1, d)]
    return pl.pallas_call(
        functools.partial(_outproj_kernel, mode=mode),
        grid=(n_rows // tm,),
        in_specs=in_specs,
        out_specs=pl.BlockSpec((tm, d), lambda i: (i, 0)),
        out_shape=jax.ShapeDtypeStruct((n_rows, d), F32),
        scratch_shapes=[pltpu.VMEM((tm, d), F32), pltpu.VMEM((tm, 1), F32)],
        compiler_params=_params(("parallel",)),
        name=name,
    )(*args)


def _mlp_kernel(h_ref, g2_ref, sh_ref, sc_ref, w1_ref, w2_ref, gate_ref, g3_ref, out_ref, v_ref, acc_ref,
                inv_ref):
    f = pl.program_id(1)

    @pl.when(f == 0)
    def _():
        v = _rms_ref(h_ref, inv_ref, g2_ref[...]) * (1.0 + sc_ref[0]) + sh_ref[0]
        v_ref[...] = v.astype(BF16)
        acc_ref[...] = jnp.zeros_like(acc_ref)

    a = jnp.maximum(_dot(v_ref[...], w1_ref[...]), 0.0)
    acc_ref[...] += _dot((a * a).astype(BF16), w2_ref[...])

    @pl.when(f == pl.num_programs(1) - 1)
    def _():
        out_ref[...] = h_ref[...] + gate_ref[0] * _rms_ref(acc_ref, inv_ref, g3_ref[...])


def mlp_residual(h, w1, w2, mods, g2, g3, *, rows, n_rows, tm, tf, layer):
    _, d, ff = w1.shape
    return pl.pallas_call(
        _mlp_kernel,
        grid=(n_rows // tm, ff // tf),
        in_specs=[
            pl.BlockSpec((tm, d), lambda i, f: (i, 0)),
            pl.BlockSpec((1, d), lambda i, f: (0, 0)),
            pl.BlockSpec((1, 1, d), rows.mod_index(layer, 3, tm)),
            pl.BlockSpec((1, 1, d), rows.mod_index(layer, 4, tm)),
            pl.BlockSpec((None, d, tf), lambda i, f: (layer, 0, f)),
            pl.BlockSpec((None, tf, d), lambda i, f: (layer, f, 0)),
            pl.BlockSpec((1, 1, d), rows.mod_index(layer, 5, tm)),
            pl.BlockSpec((1, d), lambda i, f: (0, 0)),
        ],
        out_specs=pl.BlockSpec((tm, d), lambda i, f: (i, 0)),
        out_shape=jax.ShapeDtypeStruct((n_rows, d), F32),
        scratch_shapes=[pltpu.VMEM((tm, d), BF16), pltpu.VMEM((tm, d), F32), pltpu.VMEM((tm, 1), F32)],
        compiler_params=_params(("parallel", "arbitrary")),
        name="mlp",
    )(h, g2.reshape(1, d), mods, mods, w1, w2, mods, g3.reshape(1, d))


def _rope_tables(seq, d_rot):
    pos = np.arange(seq)
    row, col = pos // GRID_W, pos % GRID_W
    n = d_rot // 4
    inv = ROPE_THETA ** (-np.arange(n, dtype=np.float64) / n)
    ang = np.concatenate([row[:, None] * inv, col[:, None] * inv], axis=-1)
    cos, sin = np.cos(ang), np.sin(ang)
    half = d_rot // 2
    cc = np.zeros((seq, LANE))
    ss = np.zeros((seq, LANE))
    cc[:, :half] = cos
    cc[:, LANE // 2:LANE // 2 + half] = cos
    ss[:, :half] = -sin
    ss[:, LANE // 2:LANE // 2 + half] = sin
    return jnp.asarray(cc, F32), jnp.asarray(ss, F32)


def _swa_kernel(sink_ref, q_ref, kp_ref, kc_ref, kn_ref, kx_ref, vp_ref, vc_ref, vn_ref, vx_ref, o_ref,
                *, nb, seq, ctx_len):
    n = pl.program_id(1)
    hd, grp, lb = SWA_HEAD_DIM, SWA_HEADS // SWA_KV_HEADS, SWA_BLOCK
    m_rows, n_loc = grp * lb, 3 * lb
    r = lax.broadcasted_iota(jnp.int32, (m_rows, n_loc + ctx_len), 0)
    c = lax.broadcasted_iota(jnp.int32, (m_rows, n_loc + ctx_len), 1)
    qpos = n * lb + (r & (lb - 1))
    kpos = (n - 1) * lb + c
    local_ok = (jnp.abs(kpos - qpos) <= SWA_WINDOW) & (kpos >= 0) & (kpos < seq) & (n < nb)
    valid = local_ok | (c >= n_loc)
    ones = jnp.ones((n_loc + ctx_len, hd), BF16)
    outs = []
    for kv in range(SWA_KV_HEADS):
        ks = slice(kv * hd, (kv + 1) * hd)
        qg = jnp.concatenate([q_ref[:, (kv * grp + g) * hd:(kv * grp + g + 1) * hd] for g in range(grp)], axis=0)
        keys = jnp.concatenate([kp_ref[:, ks], kc_ref[:, ks], kn_ref[:, ks], kx_ref[:, ks]], axis=0)
        vals = jnp.concatenate([vp_ref[:, ks], vc_ref[:, ks], vn_ref[:, ks], vx_ref[:, ks]], axis=0)
        v_aug = jnp.concatenate([vals, ones], axis=1)
        s = jnp.where(valid, _dot_nt(qg, keys), -1e30)
        for g in range(grp):
            sg = s[g * lb:(g + 1) * lb]
            snk = sink_ref[kv * grp + g] * math.log2(math.e)
            m = jnp.maximum(jnp.max(sg, axis=1, keepdims=True), snk)
            ov = _dot(jnp.exp2(sg - m).astype(BF16), v_aug)
            outs.append((ov[:, :hd] / (ov[:, hd:hd + 1] + jnp.exp2(snk - m))).astype(BF16))
    o_ref[...] = jnp.concatenate(outs, axis=1)


def swa_attention(qkv, sink, *, rows):
    b, seq, ctx_len = rows.batch, rows.seq, rows.ctx_len
    lb = SWA_BLOCK
    nb, ncb = seq // lb, ctx_len // lb
    lat_blocks = rows.n_lat // lb
    qw = SWA_HEADS * SWA_HEAD_DIM
    kw = SWA_KV_HEADS * SWA_HEAD_DIM
    kcol, vcol = qw // kw, qw // kw + 1

    def qidx(bi, n):
        return (jnp.where(n < nb, bi * nb + n, lat_blocks + bi * ncb + (n - nb)), 0)

    def kidx(off, col):
        return lambda bi, n: (bi * nb + jnp.clip(n + off, 0, nb - 1), col)

    def xidx(col):
        return lambda bi, n: (rows.n_lat // ctx_len + bi, col)

    kern = functools.partial(_swa_kernel, nb=nb, seq=seq, ctx_len=ctx_len)
    return pl.pallas_call(
        kern,
        grid=(b, nb + ncb),
        in_specs=[
            pl.BlockSpec(memory_space=pltpu.SMEM),
            pl.BlockSpec((lb, qw), qidx),
            pl.BlockSpec((lb, kw), kidx(-1, kcol)),
            pl.BlockSpec((lb, kw), kidx(0, kcol)),
            pl.BlockSpec((lb, kw), kidx(1, kcol)),
            pl.BlockSpec((ctx_len, kw), xidx(kcol)),
            pl.BlockSpec((lb, kw), kidx(-1, vcol)),
            pl.BlockSpec((lb, kw), kidx(0, vcol)),
            pl.BlockSpec((lb, kw), kidx(1, vcol)),
            pl.BlockSpec((ctx_len, kw), xidx(vcol)),
        ],
        out_specs=pl.BlockSpec((lb, qw), qidx),
        out_shape=jax.ShapeDtypeStruct((rows.n_all, qw), BF16),
        compiler_params=_params(("parallel", "parallel")),
        name="swa_attention",
    )(sink, qkv, qkv, qkv, qkv, qkv, qkv, qkv, qkv, qkv)


def _mlstm_kernel(*refs):
    ins, (of_ref, ob_ref, c_ref, m_ref) = refs[:14], refs[14:]
    c = pl.program_id(1)
    L, H, dqk, dv = ML_CHUNK, ML_HEADS, ML_DQK, ML_DV
    scale = dqk ** -0.5

    @pl.when(c == 0)
    def _():
        c_ref[...] = jnp.zeros_like(c_ref)
        m_ref[...] = jnp.zeros_like(m_ref)

    row = lax.broadcasted_iota(jnp.int32, (L, L), 0)
    col = lax.broadcasted_iota(jnp.int32, (L, L), 1)
    ones = jnp.ones((L, LANE), BF16)
    for d, o_ref in enumerate((of_ref, ob_ref)):
        q_ref, k_ref, v_ref, gi_ref, gf_ref, bi_ref, bf_ref = ins[7 * d:7 * d + 7]
        mask = (col <= row) if d == 0 else (col >= row)
        ones_mask = jnp.where(mask, 1.0, 0.0).astype(BF16)
        i_blk = gi_ref[...] + bi_ref[...]
        f_blk = jax.nn.log_sigmoid(gf_ref[...] + bf_ref[...])
        f_hi = f_blk.astype(BF16)
        r1 = f_blk - f_hi.astype(F32)
        f_mid = r1.astype(BF16)
        f_lo = (r1 - f_mid.astype(F32)).astype(BF16)
        b_blk = _dot(jnp.concatenate([ones_mask] * 3, axis=1), jnp.concatenate([f_hi, f_mid, f_lo], axis=0))
        b_end = jnp.sum(f_blk, axis=0, keepdims=True)
        e_rows = (i_blk - b_blk).T
        m_prev_blk = m_ref[d]
        dec = b_end - b_blk + i_blk
        m_new_blk = jnp.maximum(b_end + m_prev_blk, jnp.max(dec, axis=0, keepdims=True))
        ws_blk = jnp.exp(dec - m_new_blk)
        gs_blk = jnp.exp(b_end + m_prev_blk - m_new_blk)
        for h in range(H):
            qh = q_ref[:, h * dqk:(h + 1) * dqk]
            kh = k_ref[:, h * dqk:(h + 1) * dqk]
            v_aug = jnp.concatenate([v_ref[:, h * dv:(h + 1) * dv], ones], axis=1)
            e = jnp.where(mask, e_rows[h:h + 1, :], -jnp.inf)
            m_prev = m_prev_blk[:, h:h + 1]
            mm = jnp.maximum(m_prev, jnp.max(e, axis=1, keepdims=True))
            s = (_dot_nt(qh, kh) * jnp.exp(e - (mm - math.log(scale)))).astype(BF16)
            g = jnp.exp(m_prev - mm) * scale
            ct = c_ref[d, h]
            lhs = jnp.concatenate([s, (qh.astype(F32) * g).astype(BF16)], axis=1)
            res = _dot(lhs, jnp.concatenate([v_aug, ct.astype(BF16)], axis=0))
            floor = jnp.exp(-(b_blk[:, h:h + 1] + mm))
            hout = res[:, :dv] / jnp.maximum(jnp.abs(res[:, dv:dv + 1]), floor)
            o_ref[:, h * dv:(h + 1) * dv] = hout.astype(o_ref.dtype)
            kw = (kh.astype(F32) * ws_blk[:, h:h + 1]).astype(BF16)
            c_ref[d, h] = gs_blk[:, h:h + 1] * ct + _dot_tn(kw, v_aug)
        m_ref[d] = m_new_blk


def mlstm_scan(z, gates, gate_b, *, rows):
    b, seq, ctx_len = rows.batch, rows.seq, rows.ctx_len
    L = ML_CHUNK
    assert ctx_len == L and seq % L == 0
    nlc = seq // L
    hq, hv = ML_HEADS * ML_DQK, ML_HEADS * ML_DV

    def rb(d):
        def index(bi, c):
            lat = bi * nlc + (c - 1 if d == 0 else nlc - c)
            return jnp.where(c == 0, rows.n_lat // L + bi, lat)
        return index

    in_specs, args = [], []
    for d in range(2):
        r = rb(d)
        in_specs += [
            pl.BlockSpec((L, hq), lambda bi, c, r=r: (r(bi, c), 0)),
            pl.BlockSpec((L, hq), lambda bi, c, r=r: (r(bi, c), 1)),
            pl.BlockSpec((L, hv), lambda bi, c, r=r: (r(bi, c), 1)),
            pl.BlockSpec((L, LANE), lambda bi, c, r=r, d=d: (r(bi, c), 2 * d)),
            pl.BlockSpec((L, LANE), lambda bi, c, r=r, d=d: (r(bi, c), 2 * d + 1)),
            pl.BlockSpec((1, LANE), lambda bi, c, d=d: (0, 2 * d)),
            pl.BlockSpec((1, LANE), lambda bi, c, d=d: (0, 2 * d + 1)),
        ]
        args += [z, z, z, gates, gates, gate_b, gate_b]
    return pl.pallas_call(
        _mlstm_kernel,
        grid=(b, nlc + 1),
        in_specs=in_specs,
        out_specs=[pl.BlockSpec((L, hv), lambda bi, c, r=rb(d): (r(bi, c), 0)) for d in range(2)],
        out_shape=[jax.ShapeDtypeStruct((rows.n_all, hv), BF16)] * 2,
        scratch_shapes=[
            pltpu.VMEM((2, ML_HEADS, ML_DQK, ML_DV + LANE), F32),
            pltpu.VMEM((2, 1, LANE), F32),
        ],
        compiler_params=_params(("parallel", "arbitrary")),
        name="mlstm_scan",
    )(*args)


def _mla_kernel(q_ref, kvl_ref, kvx_ref, krl_ref, krx_ref, o_ref, kk_ref, vt_ref, *, seq, sub, ck):
    @pl.when(pl.program_id(2) == 0)
    def _():
        kk_ref[:seq, :LANE] = kvl_ref[:, :LANE]
        kk_ref[:seq, LANE:] = krl_ref[...]
        kk_ref[seq:, :LANE] = kvx_ref[:, :LANE]
        kk_ref[seq:, LANE:] = krx_ref[...]
        vt_ref[:LANE, :seq] = kvl_ref[:, LANE:].astype(F32).T.astype(BF16)
        vt_ref[:LANE, seq:] = kvx_ref[:, LANE:].astype(F32).T.astype(BF16)
        vt_ref[LANE:, :] = jnp.ones((vt_ref.shape[0] - LANE, vt_ref.shape[1]), BF16)

    nsub = q_ref.shape[0] // sub
    n_keys = kk_ref.shape[0]
    bounds = [(k0, min(k0 + ck, n_keys)) for k0 in range(0, n_keys, ck)]
    nck = len(bounds)
    qs = [q_ref[s * sub:(s + 1) * sub, :] for s in range(nsub)]

    def scores(s, c):
        return _dot_nt(kk_ref[bounds[c][0]:bounds[c][1], :], qs[s])

    m = [None] * nsub
    acc = [None] * nsub
    st = [None] * nsub

    def absorb(s, c):
        cm = jnp.max(st[s], axis=0, keepdims=True)
        m_new = cm if c == 0 else jnp.maximum(m[s], cm)
        pv = _dot(vt_ref[:, bounds[c][0]:bounds[c][1]], jnp.exp2(st[s] - m_new).astype(BF16))
        acc[s] = pv if c == 0 else acc[s] * jnp.exp2(m[s] - m_new) + pv
        m[s] = m_new

    for k in range(-1, nck + nsub - 1):
        nxt = [scores(s, k - s + 1) if 0 <= k - s + 1 < nck else None for s in range(nsub)]
        for s in range(nsub):
            if 0 <= k - s < nck:
                absorb(s, k - s)
        for s in range(nsub):
            if nxt[s] is not None:
                st[s] = nxt[s]
    for s in range(nsub):
        o = acc[s][:LANE] / acc[s][LANE:LANE + 1]
        o_ref[s * sub:(s + 1) * sub, :] = o.T.astype(o_ref.dtype)


def mla_attention(q, kv, z, *, rows, tq=2048, sub=256, ck=1024):
    b, seq, ctx_len = rows.batch, rows.seq, rows.ctx_len
    tq = min(tq, seq)
    nq = seq // tq
    kr_col = z.shape[1] // LANE - 1
    ctx0 = rows.n_lat // ctx_len
    n_keys = seq + ctx_len
    ones_rows = 16
    return pl.pallas_call(
        functools.partial(_mla_kernel, seq=seq, sub=sub, ck=ck),
        grid=(b, MLA_HEADS, nq),
        in_specs=[
            pl.BlockSpec((tq, 2 * LANE), lambda bi, h, i: (bi * nq + i, h)),
            pl.BlockSpec((seq, 2 * LANE), lambda bi, h, i: (bi, h)),
            pl.BlockSpec((ctx_len, 2 * LANE), lambda bi, h, i: (ctx0 + bi, h)),
            pl.BlockSpec((seq, LANE), lambda bi, h, i: (bi, kr_col)),
            pl.BlockSpec((ctx_len, LANE), lambda bi, h, i: (ctx0 + bi, kr_col)),
        ],
        out_specs=pl.BlockSpec((tq, LANE), lambda bi, h, i: (bi * nq + i, h)),
        out_shape=jax.ShapeDtypeStruct((rows.n_lat, MLA_HEADS * MLA_V), BF16),
        scratch_shapes=[pltpu.VMEM((n_keys, 2 * LANE), BF16), pltpu.VMEM((LANE + ones_rows, n_keys), BF16)],
        compiler_params=_params(("parallel", "parallel", "arbitrary")),
        name="mla_attention",
    )(q, kv, kv, z, z)


def _hy_filter_kernel(ft_ref, w1_ref, b1_ref, w2_ref, b2_ref, w3_ref, b3_ref, fr_ref, fr3_ref, w4_ref, b4_ref,
                      dec_ref, o_ref, a_ref, *, back_from):
    j = pl.program_id(0)
    fw = HY_FILTER_W
    hdot = functools.partial(jnp.dot, preferred_element_type=F32, precision=HIGHEST)

    @pl.when(j == 0)
    def _():
        fr = fr_ref[...]
        a = jnp.sin(fr * (hdot(ft_ref[...], w1_ref[...]) + b1_ref[...]))
        a = jnp.sin(fr * (hdot(a, w2_ref[...]) + b2_ref[...]))
        a3 = jnp.sin(fr3_ref[...] * (hdot(a, w3_ref[...]) + b3_ref[...]))
        hi = a3.astype(BF16)
        lo = (a3 - hi.astype(F32)).astype(BF16)
        lane = lax.broadcasted_iota(jnp.int32, a3.shape, 1)
        a_ref[...] = jnp.where((lane >= fw) & (lane < 2 * fw), lo, hi)

    w4 = w4_ref[...]
    w_hi = w4.astype(BF16)
    w_lo = (w4 - w_hi.astype(F32)).astype(BF16)
    rhs = jnp.concatenate([w_hi, w_hi, w_lo, jnp.zeros_like(w_hi)], axis=0)
    t = ft_ref[:, 0:1]
    filt = (_dot(a_ref[...], rhs) + b4_ref[...]) * jnp.exp(-t * jnp.abs(dec_ref[...]))
    row = lax.broadcasted_iota(jnp.int32, filt.shape, 0)
    o_ref[...] = jnp.where(jnp.logical_and(row == 0, j >= back_from), 0.0, filt)


def hyena_filters(seq, w1, b1, w2, b2, w3, b3, w4, b4, freq, decay, tn=512):
    t = np.linspace(0.0, 1.0, seq)[:, None]
    w = (2.0 * math.pi / seq) * np.arange(seq)[:, None]
    bands = np.linspace(1e-4, HY_BANDS - 1, HY_BANDS)[None, :]
    feats = np.zeros((seq, LANE))
    feats[:, :1 + 2 * HY_BANDS] = np.concatenate([t, np.cos(bands * w), -np.sin(bands * w)], axis=-1)
    n = w4.shape[1]
    fw = HY_FILTER_W
    w1p = jnp.zeros((LANE, fw), F32).at[:w1.shape[0]].set(w1)
    row = lambda a: a.reshape(1, -1)
    tile3 = lambda a: jnp.concatenate([a, a, a, jnp.zeros_like(a)], axis=-1)
    full = lambda shape: pl.BlockSpec(shape, lambda j: (0, 0))
    return pl.pallas_call(
        functools.partial(_hy_filter_kernel, back_from=(n // 2) // tn),
        grid=(n // tn,),
        in_specs=[full((seq, LANE)), full((LANE, fw)), full((1, fw)), full((fw, fw)), full((1, fw)),
                  full((fw, 4 * fw)), full((1, 4 * fw)), full((1, fw)), full((1, 4 * fw)),
                  pl.BlockSpec((fw, tn), lambda j: (0, j)),
                  pl.BlockSpec((1, tn), lambda j: (0, j)),
                  pl.BlockSpec((1, tn), lambda j: (0, j))],
        out_specs=pl.BlockSpec((seq, tn), lambda j: (0, j)),
        out_shape=jax.ShapeDtypeStruct((seq, n), F32),
        scratch_shapes=[pltpu.VMEM((seq, 4 * fw), BF16)],
        compiler_params=_params(("arbitrary",)),
        name="hyena_filters",
    )(jnp.asarray(feats, F32), w1p, row(b1), w2, row(b2), tile3(w3), tile3(row(b3)), row(freq), tile3(row(freq)),
      w4, row(b4), row(decay))


def _dft_mats(seq, p):
    n = 2 * seq
    q = n // p
    k1n = q // 2 + 1
    hi, k1 = np.arange(q // 2), np.arange(k1n)
    th = 2 * np.pi * np.outer(k1, hi) / q
    ma = np.zeros((2 * k1n, q // 2))
    ma[0::2], ma[1::2] = np.cos(th), -np.sin(th)
    lo = np.arange(p)
    wf = np.zeros((k1n, 2 * p, 2 * p))
    wi = np.zeros((k1n, 2 * p, 2 * p))
    for k in k1:
        ph = -2 * np.pi * (np.outer(lo, lo) / p + k * lo[None, :] / n)
        er, ei = np.cos(ph), np.sin(ph)
        wf[k] = np.block([[er, -ei], [ei, er]])
        wi[k] = np.block([[er.T, ei.T], [-ei.T, er.T]])
    c = np.full(k1n, 2.0)
    c[0] = c[-1] = 1.0
    th2 = 2 * np.pi * np.outer(hi, k1) / q
    md = np.zeros((q // 2, 2 * k1n))
    md[:, 0::2], md[:, 1::2] = c * np.cos(th2) / n, -c * np.sin(th2) / n
    as_bf16 = lambda a: jnp.asarray(a, F32).astype(BF16)
    return ma, as_bf16(wf), as_bf16(wi), md


def _hy_short_kernel(x1_ref, x2_ref, v_ref, w_ref, b_ref, o1_ref, o2_ref, o3_ref):
    for k, (x_ref, o_ref) in enumerate(((x1_ref, o1_ref), (x2_ref, o2_ref), (v_ref, o3_ref))):
        x = x_ref[...].astype(F32)
        n = x.shape[0]
        row = lax.broadcasted_iota(jnp.int32, x.shape, 0)
        prev = jnp.where(row == 0, 0.0, pltpu.roll(x, 1, 0))
        nxt = jnp.where(row == n - 1, 0.0, pltpu.roll(x, n - 1, 0))
        w = w_ref[k]
        o_ref[...] = (prev * w[0:1] + x * w[1:2] + nxt * w[2:3] + b_ref[k]).astype(o_ref.dtype)


def hyena_short_conv(x3, conv_w, conv_b, *, rows, cb=256):
    b, seq = rows.batch, rows.seq
    d = x3.shape[1] // 3
    nblk = d // cb
    w = jnp.transpose(conv_w.reshape(HY_SHORT, 3, d), (1, 0, 2))
    bias = conv_b.reshape(3, 1, d)
    in_x = [pl.BlockSpec((seq, cb), lambda bi, j, k=k: (bi, k * nblk + j)) for k in range(3)]
    out = pl.BlockSpec((seq, cb), lambda bi, j: (bi, j))
    shape = jax.ShapeDtypeStruct((rows.n_lat, d), BF16)
    return pl.pallas_call(
        _hy_short_kernel,
        grid=(b, nblk),
        in_specs=in_x + [pl.BlockSpec((3, HY_SHORT, cb), lambda bi, j: (0, 0, j)),
                         pl.BlockSpec((3, 1, cb), lambda bi, j: (0, 0, j))],
        out_specs=[out, out, out],
        out_shape=[shape, shape, shape],
        compiler_params=_params(("parallel", "parallel")),
        name="hyena_short_conv",
    )(x3, x3, x3, w, bias)


def _hy_stage_a_kernel(x_ref, m_ref, o_ref):
    qh, g, wc = x_ref.shape
    x = x_ref[...].reshape(qh * g, wc).astype(BF16)
    o_ref[...] = _dot(m_ref[...], x).reshape(o_ref.shape).astype(o_ref.dtype)


def hyena_stage_a(x, mak, *, p, width=2048):
    bx, seq, c = x.shape
    g = HY_G
    qh = seq // p
    mo = mak.shape[0] // g
    x5 = x.reshape(bx, qh, p // g, g, c)
    return pl.pallas_call(
        _hy_stage_a_kernel,
        grid=(bx, p // g, c // width),
        in_specs=[pl.BlockSpec((None, qh, None, g, width), lambda bi, l, j: (bi, 0, l, 0, j)),
                  pl.BlockSpec(mak.shape, lambda bi, l, j: (0, 0))],
        out_specs=pl.BlockSpec((None, mo, g, width), lambda bi, l, j: (bi, 0, l, j)),
        out_shape=jax.ShapeDtypeStruct((bx, mo, p, c), BF16),
        compiler_params=_params(("parallel", "parallel", "parallel")),
        name="hyena_stage_a",
    )(x5, mak)


def _hy_spec_kernel(a0_ref, a1_ref, wf_ref, o_ref):
    p = o_ref.shape[2]
    cb = o_ref.shape[3]
    wf = wf_ref[0]
    x0 = _dot(wf, a0_ref[0].reshape(2 * p, cb))
    x1 = _dot(wf, a1_ref[0].reshape(2 * p, cb))
    o_ref[0, 0] = x0[:p] + x1[:p]
    o_ref[0, 1] = x0[p:] - x1[p:]


def hyena_spectrum(af, wf, *, d, cb=2048):
    k1n, _, p, _ = af.shape
    nblk = (HY_ORDER * d) // cb
    return pl.pallas_call(
        _hy_spec_kernel,
        grid=(k1n, nblk),
        in_specs=[pl.BlockSpec((1, 2, p, cb), lambda k, j: (k, 0, 0, j)),
                  pl.BlockSpec((1, 2, p, cb), lambda k, j: (k, 0, 0, nblk + j)),
                  pl.BlockSpec((1, 2 * p, 2 * p), lambda k, j: (k, 0, 0))],
        out_specs=pl.BlockSpec((1, 2, p, cb), lambda k, j: (k, 0, 0, j)),
        out_shape=jax.ShapeDtypeStruct((k1n, 2, p, HY_ORDER * d), F32),
        compiler_params=_params(("parallel", "parallel")),
        name="hyena_spectrum",
    )(af, af, wf)


def _hy_mid_kernel(a_ref, wf_ref, wi_ref, h_ref, o_ref):
    b, _, p, cb = a_ref.shape
    hr, hi = h_ref[0], h_ref[1]
    for bi in range(b):
        x = _dot(wf_ref[...], a_ref[bi].reshape(2 * p, cb))
        xr, xi = x[:p], x[p:]
        y = jnp.concatenate([xr * hr - xi * hi, xr * hi + xi * hr], axis=0).astype(BF16)
        o_ref[bi] = _dot(wi_ref[...], y).reshape(2, p, cb).astype(o_ref.dtype)


def hyena_mid(a, wf, wi, spec, *, order, d, cb=1024):
    b, k1n, _, p, _ = a.shape
    nblk = d // cb
    data = pl.BlockSpec((b, None, 2, p, cb), lambda k, j: (0, k, 0, 0, j))
    mat = pl.BlockSpec((None, 2 * p, 2 * p), lambda k, j: (k, 0, 0))
    return pl.pallas_call(
        _hy_mid_kernel,
        grid=(k1n, nblk),
        in_specs=[data, mat, mat,
                  pl.BlockSpec((None, 2, p, cb), lambda k, j: (k, 0, 0, order * nblk + j))],
        out_specs=data,
        out_shape=jax.ShapeDtypeStruct(a.shape, BF16),
        compiler_params=_params(("parallel", "parallel")),
        name="hyena_mid",
    )(a, wf, wi, spec)


def _hy_stage_d_kernel(c_ref, m_ref, x_ref, z_ref, skip_ref, *rest, fused):
    ma_ref, o_ref, a_ref = rest if fused else (None, rest[0], None)
    mo, g, wc = c_ref.shape
    qh = x_ref.shape[0]
    conv = _dot(m_ref[...], c_ref[...].reshape(mo * g, wc))
    z = z_ref[...].reshape(qh * g, wc).astype(F32)
    x = x_ref[...].reshape(qh * g, wc).astype(F32)
    y = (x * (conv + z * skip_ref[...])).astype(o_ref.dtype)
    o_ref[...] = y.reshape(o_ref.shape)
    if fused:
        a_ref[...] = _dot(ma_ref[...], y).reshape(a_ref.shape).astype(a_ref.dtype)


def hyena_stage_d(c, mdk, xg, z, skip, *, p, mak=None, width=2048):
    b, mo, _, d = c.shape
    g = HY_G
    seq = xg.shape[1]
    qh = seq // p
    fused = mak is not None
    v5 = lambda a: a.reshape(b, qh, p // g, g, d)
    tspec = pl.BlockSpec((None, qh, None, g, width), lambda bi, l, j: (bi, 0, l, 0, j))
    fspec = pl.BlockSpec((None, mo, g, width), lambda bi, l, j: (bi, 0, l, j))
    in_specs = [fspec, pl.BlockSpec(mdk.shape, lambda bi, l, j: (0, 0)), tspec, tspec,
                pl.BlockSpec((1, width), lambda bi, l, j: (0, j))]
    args = [c, mdk, v5(xg), v5(z), skip.reshape(1, d)]
    out_specs, out_shape = [tspec], [jax.ShapeDtypeStruct((b, qh, p // g, g, d), BF16)]
    if fused:
        in_specs.append(pl.BlockSpec(mak.shape, lambda bi, l, j: (0, 0)))
        args.append(mak)
        out_specs.append(fspec)
        out_shape.append(jax.ShapeDtypeStruct(c.shape, BF16))
    outs = pl.pallas_call(
        functools.partial(_hy_stage_d_kernel, fused=fused),
        grid=(b, p // g, d // width),
        in_specs=in_specs,
        out_specs=out_specs,
        out_shape=out_shape,
        compiler_params=_params(("parallel", "parallel", "parallel")),
        name="hyena_stage_d",
    )(*args)
    y = outs[0].reshape(b, seq, d)
    return (y, outs[1]) if fused else y


def hyena_long_convs(x1, x2, v, filt, skip, *, rows):
    b, seq = rows.batch, rows.seq
    d = x1.shape[1]
    p = HY_P
    k1n = seq // p + 1
    ma, wf, wi, md = _dft_mats(seq, p)
    eye = np.eye(HY_G)
    as_bf16 = lambda a: jnp.asarray(a, F32).astype(BF16)
    mak, mdk = as_bf16(np.kron(ma, eye)), as_bf16(np.kron(md, eye))
    af = hyena_stage_a(filt.reshape(1, seq, filt.shape[1]), mak, p=p)
    spec = hyena_spectrum(af.reshape(k1n, 2, p, filt.shape[1]), wf, d=d)
    t3 = lambda a: a.reshape(b, seq, d)
    y = t3(v)
    a = hyena_stage_a(y, mak, p=p)
    c = hyena_mid(a.reshape(b, k1n, 2, p, d), wf, wi, spec, order=0, d=d)
    y, a = hyena_stage_d(c.reshape(a.shape), mdk, t3(x1), y, skip[0], p=p, mak=mak)
    c = hyena_mid(a.reshape(b, k1n, 2, p, d), wf, wi, spec, order=1, d=d)
    y = hyena_stage_d(c.reshape(a.shape), mdk, t3(x2), y, skip[1], p=p)
    return y.reshape(rows.n_lat, d)


def _mlstm_gate_weights(w_in, gate_b):
    h = ML_HEADS
    o4 = 2 * h * ML_DQK + 2 * h * ML_DV
    wg = w_in[:, o4:].reshape(-1, 2, 2, h)
    bg = gate_b.reshape(2, 2, h)
    w_out = jnp.zeros((w_in.shape[0], 4 * LANE), F32)
    b_out = jnp.zeros((1, 4 * LANE), F32)
    for d in range(2):
        for gate in range(2):
            lo = (2 * d + gate) * LANE
            w_out = w_out.at[:, lo:lo + h].set(wg[:, gate, d])
            b_out = b_out.at[0, lo:lo + h].set(bg[gate, d])
    return w_out, b_out


def _mla_weights(w_in, w_uq):
    r2 = MLA_Q_RANK + MLA_KV_RANK
    half = MLA_ROPE // 2

    def spread(w):
        z = jnp.zeros(w.shape[:-1] + (half,), w.dtype)
        return jnp.concatenate([w[..., :half], z, w[..., half:], z], axis=-1)

    w_in_p = jnp.concatenate([w_in[:, :r2], spread(w_in[:, r2:])], axis=1)
    wq = w_uq.reshape(w_uq.shape[0], MLA_HEADS, MLA_NOPE + MLA_ROPE)
    wq_p = jnp.concatenate([wq[..., :MLA_NOPE], spread(wq[..., MLA_NOPE:])], axis=-1)
    wq_p = wq_p * ((MLA_NOPE + MLA_ROPE) ** -0.5 * math.log2(math.e))
    return w_in_p, wq_p.reshape(w_uq.shape[0], MLA_HEADS * 2 * LANE)


def kernel(x, c, ctx, c_ctx, ada_w, ada_b, norm_g, mlp_w1, mlp_w2, swa_w_qkv, swa_sink, swa_w_o, ml_w_in, ml_gate_b, ml_head_g, ml_w_o, mla_w_in, mla_q_g, mla_kv_g, mla_w_uq, mla_w_ukv, mla_w_o, hy_w_in, hy_conv_w, hy_conv_b, hy_f_w1, hy_f_b1, hy_f_w2, hy_f_b2, hy_f_w3, hy_f_b3, hy_f_w4, hy_f_b4, hy_f_freq, hy_decay, hy_skip, hy_w_o):
    b, seq, d = x.shape
    ctx_len = ctx.shape[1]
    depth = ada_w.shape[0]
    assert depth == 4 and b < 8
    rows = Rows(b, seq, ctx_len)
    tm = 512
    tm_lin = 1024 if rows.n_lat % 1024 == 0 and rows.n_ctx % 1024 == 0 else 512
    bf = lambda w: w.astype(BF16)
    mlp_w1b, mlp_w2b = bf(mlp_w1), bf(mlp_w2)

    cond = jnp.zeros((8, d), F32).at[:b].set(c).at[b].set(c_ctx)
    mods = ada_mods(cond, ada_w, ada_b).reshape(depth * 8, 1, N_MOD * d)

    common = dict(rows=rows, tm=tm_lin, mods=mods)

    def finish(h, o_args, w_o, layer, n_rows, mode="plain"):
        h = outproj_residual(o_args, bf(w_o), h, mods, norm_g[layer, 1], rows=rows, n_rows=n_rows, tm=tm,
                             layer=layer, mode=mode, name=f"outproj{layer}")
        return mlp_residual(h, mlp_w1b, mlp_w2b, mods, norm_g[layer, 2], norm_g[layer, 3],
                            rows=rows, n_rows=n_rows, tm=tm, tf=1024, layer=layer)

    n_qk_groups = SWA_HEADS + SWA_KV_HEADS
    q_cols = SWA_HEADS * SWA_HEAD_DIM
    q_scale = SWA_HEAD_DIM ** -0.5 * math.log2(math.e)
    w_qkv = jnp.concatenate([swa_w_qkv[0][:, :q_cols] * q_scale, swa_w_qkv[0][:, q_cols:]], axis=1)
    qkv, h = linear(x.reshape(rows.n_lat, d), bf(w_qkv), x2=ctx.reshape(rows.n_ctx, d), n_rows=rows.n_all, tn=512,
                    out_dtype=BF16, prologue="norm_mod",
                 gain=norm_g[0, 0], layer=0, rope_tabs=_rope_tables(seq, SWA_HEAD_DIM), rope_pattern=(True,) * 4,
                 rope_jmax=n_qk_groups // 4, name="swa_qkv", rows=rows, tm=tm, mods=mods)
    o = swa_attention(qkv, swa_sink[0], rows=rows)
    h = finish(h, o, swa_w_o[0], 0, rows.n_all)

    w_gate, b_gate = _mlstm_gate_weights(ml_w_in[0], ml_gate_b[0])
    z = linear(h, bf(ml_w_in[0]), n_rows=rows.n_all, tn=1024, out_dtype=BF16, prologue="norm_mod",
               gain=norm_g[1, 0], layer=1, n_out=2 * ML_HEADS * (ML_DQK + ML_DV), name="mlstm_in", **common)
    gates = linear(h, bf(w_gate), n_rows=rows.n_all, tn=4 * LANE, out_dtype=F32, prologue="norm_mod",
                   gain=norm_g[1, 0], layer=1, name="mlstm_gates", **common)
    h_fwd, h_bwd = mlstm_scan(z, gates, b_gate, rows=rows)
    h = finish(h, (h_fwd, h_bwd, z, ml_head_g[0]), ml_w_o[0], 1, rows.n_all, mode="mlstm")

    w_in_p, w_uq_p = _mla_weights(mla_w_in[0], mla_w_uq[0])
    rope_mla = _rope_tables(seq, MLA_ROPE)
    zc = linear(h, bf(w_in_p), n_rows=rows.n_all, tn=w_in_p.shape[1], out_dtype=BF16, prologue="norm_mod",
                gain=norm_g[2, 0], layer=2, rope_tabs=rope_mla, rope_pattern=(False,) * 8 + (True,),
                name="mla_in", **common)
    q = linear(zc, bf(w_uq_p), n_rows=rows.n_lat, tn=2048, out_dtype=BF16, x_cols=0, prologue="norm",
               gain=mla_q_g[0], rope_tabs=rope_mla, rope_pattern=(False, True) * 8, name="mla_q",
               rows=rows, tm=tm_lin)
    kv = linear(zc, bf(mla_w_ukv[0]), n_rows=rows.n_all, tn=2048, out_dtype=BF16, x_cols=1, prologue="norm",
                gain=mla_kv_g[0], name="mla_kv", rows=rows, tm=tm_lin)
    o = mla_attention(q, kv, zc, rows=rows)
    h = finish(h, o, mla_w_o[0], 2, rows.n_lat)

    x3 = linear(h, bf(hy_w_in[0]), n_rows=rows.n_lat, tn=1024, out_dtype=BF16, prologue="norm_mod",
                gain=norm_g[3, 0], layer=3, name="hyena_in", **common)
    x1, x2, v = hyena_short_conv(x3, hy_conv_w[0], hy_conv_b[0], rows=rows)
    filt = hyena_filters(seq, hy_f_w1[0], hy_f_b1[0], hy_f_w2[0], hy_f_b2[0], hy_f_w3[0], hy_f_b3[0],
                         hy_f_w4[0], hy_f_b4[0], hy_f_freq[0], hy_decay[0])
    y = hyena_long_convs(x1, x2, v, filt, hy_skip[0], rows=rows)
    h = finish(h, y, hy_w_o[0], 3, rows.n_lat)
    return h.reshape(b, seq, d)
```

```python
import functools
import math

import numpy as np
import jax
import jax.numpy as jnp
from jax import lax
from jax.experimental import pallas as pl
from jax.experimental.pallas import tpu as pltpu

F32, BF16 = jnp.float32, jnp.bfloat16
HIGHEST = lax.Precision.HIGHEST

RMS_EPS = 1e-6
ROPE_THETA = 10000.0
GRID_W = 64
N_MOD = 6
LANE = 128

SWA_HEADS, SWA_KV_HEADS, SWA_HEAD_DIM, SWA_WINDOW, SWA_BLOCK = 16, 4, 128, 128, 128
ML_HEADS, ML_DQK, ML_DV = 8, 128, 256
ML_CHUNK = 256
MLA_HEADS, MLA_Q_RANK, MLA_KV_RANK, MLA_NOPE, MLA_ROPE, MLA_V = 16, 512, 512, 128, 64, 128
HY_ORDER, HY_BANDS, HY_FILTER_W, HY_SHORT = 2, 16, 64, 3
HY_P = 256
HY_G = 16

VMEM_LIMIT = 48 * 1024 * 1024


def _params(sem):
    return pltpu.CompilerParams(dimension_semantics=sem, vmem_limit_bytes=VMEM_LIMIT)


def _dot(a, b):
    return jnp.dot(a, b, preferred_element_type=F32)


def _dot_nt(a, b):
    return lax.dot_general(a, b, (((1,), (1,)), ((), ())), preferred_element_type=F32)


def _dot_tn(a, b):
    return lax.dot_general(a, b, (((0,), (0,)), ((), ())), preferred_element_type=F32)


def _rms(x, g):
    return x * lax.rsqrt(jnp.mean(x * x, axis=-1, keepdims=True) + RMS_EPS) * g


def _rms_ref(x_ref, inv_ref, g):
    x = x_ref[...].astype(F32)
    inv_ref[...] = lax.rsqrt(jnp.mean(x * x, axis=-1, keepdims=True) + RMS_EPS)
    return x_ref[...].astype(F32) * inv_ref[...] * g


def _ada_kernel(s_ref, w_ref, b_ref, o_ref):
    s = s_ref[...]
    s = s * jax.nn.sigmoid(s)
    hi = s.astype(BF16)
    lo = (s - hi.astype(F32)).astype(BF16)
    w = w_ref[0]
    w_hi = w.astype(BF16)
    w_lo = (w - w_hi.astype(F32)).astype(BF16)
    r = _dot(jnp.concatenate([hi, lo], axis=0), w_hi)
    n = s.shape[0]
    o_ref[0] = r[:n] + r[n:] + _dot(hi, w_lo) + b_ref[0]


def ada_mods(cond, ada_w, ada_b, tn=1536):
    depth, d, n = ada_w.shape
    rows = cond.shape[0]
    return pl.pallas_call(
        _ada_kernel,
        grid=(depth, n // tn),
        in_specs=[
            pl.BlockSpec((rows, d), lambda l, j: (0, 0)),
            pl.BlockSpec((1, d, tn), lambda l, j: (l, 0, j)),
            pl.BlockSpec((1, 1, tn), lambda l, j: (l, 0, j)),
        ],
        out_specs=pl.BlockSpec((1, rows, tn), lambda l, j: (l, 0, j)),
        out_shape=jax.ShapeDtypeStruct((depth, rows, n), F32),
        compiler_params=_params(("parallel", "parallel")),
        name="ada_mods",
    )(cond, ada_w, ada_b.reshape(depth, 1, n))


class Rows:
    def __init__(self, batch, seq, ctx_len):
        self.batch, self.seq, self.ctx_len = batch, seq, ctx_len
        self.n_lat = batch * seq
        self.n_ctx = batch * ctx_len
        self.n_all = self.n_lat + self.n_ctx

    def mod_index(self, layer, k, tm):
        lat_blocks, per_batch = self.n_lat // tm, self.seq // tm

        def index(i, *_):
            b = jnp.where(i < lat_blocks, i // per_batch, self.batch)
            return (layer * 8 + b, 0, k)

        return index


def _linear_kernel(*refs, prologue, rope, rope_pattern, n_lat_blocks, dual, single):
    it = iter(refs)
    x_ref = next(it)
    x2_ref = next(it) if dual else None
    g_ref = next(it) if prologue in ("norm", "norm_mod") else None
    sh_ref = next(it) if prologue == "norm_mod" else None
    sc_ref = next(it) if prologue == "norm_mod" else None
    w_ref = next(it)
    cc_ref = next(it) if rope else None
    ss_ref = next(it) if rope else None
    o_ref = next(it)
    rows_ref = next(it) if dual else None
    u_ref = next(it)
    inv_ref = next(it)
    first = pl.program_id(1) == 0

    def prologue_from(src_ref):
        if prologue in ("norm", "norm_mod"):
            x = _rms_ref(src_ref, inv_ref, g_ref[...])
        else:
            x = src_ref[...].astype(F32)
        if prologue == "norm_mod":
            x = x * (1.0 + sc_ref[0]) + sh_ref[0]
        u_ref[...] = x.astype(BF16)
        if dual:
            rows_ref[...] = src_ref[...]

    if dual:
        is_lat = pl.program_id(0) < n_lat_blocks
        pl.when(jnp.logical_and(first, is_lat))(functools.partial(prologue_from, x_ref))
        pl.when(jnp.logical_and(first, jnp.logical_not(is_lat)))(functools.partial(prologue_from, x2_ref))
    elif single:
        prologue_from(x_ref)
    else:
        pl.when(first)(functools.partial(prologue_from, x_ref))

    acc = _dot(u_ref[...], w_ref[...])

    def plain():
        o_ref[...] = acc.astype(o_ref.dtype)

    def roped():
        cc, ss = cc_ref[...], ss_ref[...]
        segs = []
        for gi, on in enumerate(rope_pattern):
            seg = acc[:, gi * LANE:(gi + 1) * LANE]
            if on:
                seg = seg * cc + pltpu.roll(seg, LANE // 2, 1) * ss
            segs.append(seg)
        o_ref[...] = jnp.concatenate(segs, axis=1).astype(o_ref.dtype)

    if rope:
        roped()
    else:
        plain()


def linear(x, w, *, rows, n_rows, tm, tn, out_dtype, x_cols=None, prologue="none", gain=None,
           mods=None, layer=0, mod_k=(0, 1), rope_tabs=None, rope_pattern=(), rope_jmax=1 << 30,
           n_out=None, x2=None, name="linear"):
    k, n = w.shape
    n = n if n_out is None else n_out
    xc = 0 if x_cols is None else x_cols
    rope = rope_tabs is not None
    dual = x2 is not None
    n_lat_blocks = rows.n_lat // tm
    if dual:
        in_specs = [pl.BlockSpec((tm, k), lambda i, j: (jnp.minimum(i, n_lat_blocks - 1), 0)),
                    pl.BlockSpec((tm, k), lambda i, j: (jnp.maximum(i - n_lat_blocks, 0), 0))]
        args = [x, x2]
    else:
        in_specs = [pl.BlockSpec((tm, k), lambda i, j: (i, xc))]
        args = [x]
    if prologue in ("norm", "norm_mod"):
        in_specs.append(pl.BlockSpec((1, k), lambda i, j: (0, 0)))
        args.append(gain.reshape(1, k))
    if prologue == "norm_mod":
        for mk in mod_k:
            in_specs.append(pl.BlockSpec((1, 1, k), rows.mod_index(layer, mk, tm)))
            args.append(mods)
    in_specs.append(pl.BlockSpec((k, tn), lambda i, j: (0, j)))
    args.append(w)
    if rope:
        per_seq = rows.seq // tm

        def tab_index(i, j):
            rotate = jnp.logical_and(i < n_lat_blocks, j < rope_jmax)
            return (jnp.where(rotate, 0, 1), i % per_seq, 0)

        for t, fill in zip(rope_tabs, (1.0, 0.0)):
            in_specs.append(pl.BlockSpec((None, tm, LANE), tab_index))
            args.append(jnp.stack([t, jnp.full_like(t, fill)]))
    kern = functools.partial(_linear_kernel, prologue=prologue, rope=rope, rope_pattern=tuple(rope_pattern),
                             n_lat_blocks=n_lat_blocks, dual=dual, single=(n == tn))
    out_specs = pl.BlockSpec((tm, tn), lambda i, j: (i, j))
    out_shape = jax.ShapeDtypeStruct((n_rows, n), out_dtype)
    if dual:
        out_specs = [out_specs, pl.BlockSpec((tm, k), lambda i, j: (i, 0))]
        out_shape = [out_shape, jax.ShapeDtypeStruct((n_rows, k), x.dtype)]
    return pl.pallas_call(
        kern,
        grid=(n_rows // tm, n // tn),
        in_specs=in_specs,
        out_specs=out_specs,
        out_shape=out_shape,
        scratch_shapes=[pltpu.VMEM((tm, k), BF16), pltpu.VMEM((tm, 1), F32)],
        compiler_params=_params(("parallel", "arbitrary")),
        name=name,
    )(*args)


def _outproj_kernel(*refs, mode):
    it = iter(refs)
    if mode == "mlstm":
        hf_ref, hb_ref, og_ref, hg_ref = next(it), next(it), next(it), next(it)
    else:
        o_ref_in = next(it)
    w_ref, h_ref, gate_ref, g_ref, out_ref = next(it), next(it), next(it), next(it), next(it)
    y_ref, inv_ref = next(it), next(it)
    if mode == "mlstm":
        hs = hf_ref[...].astype(F32) + hb_ref[...].astype(F32)
        og = jax.nn.sigmoid(og_ref[...].astype(F32))
        hg = hg_ref[...]
        parts = []
        for h in range(ML_HEADS):
            sl = slice(h * ML_DV, (h + 1) * ML_DV)
            parts.append((_rms(hs[:, sl], hg[:, sl]) * og[:, sl]).astype(BF16))
        o = jnp.concatenate(parts, axis=1)
    else:
        o = o_ref_in[...]
    y_ref[...] = _dot(o, w_ref[...])
    out_ref[...] = h_ref[...] + gate_ref[0] * _rms_ref(y_ref, inv_ref, g_ref[...])


def outproj_residual(o_args, w, h, mods, gain, *, rows, n_rows, tm, layer, mode="plain", name="outproj"):
    k, d = w.shape
    if mode == "mlstm":
        h_fwd, h_bwd, z, head_g = o_args
        in_specs = [
            pl.BlockSpec((tm, k), lambda i: (i, 0)),
            pl.BlockSpec((tm, k), lambda i: (i, 0)),
            pl.BlockSpec((tm, k), lambda i: (i, 2)),
            pl.BlockSpec((1, k), lambda i: (0, 0)),
        ]
        args = [h_fwd, h_bwd, z, head_g.reshape(1, k)]
    else:
        in_specs = [pl.BlockSpec((tm, k), lambda i: (i, 0))]
        args = [o_args]
    in_specs += [
        pl.BlockSpec((k, d), lambda i: (0, 0)),
        pl.BlockSpec((tm, d), lambda i: (i, 0)),
        pl.BlockSpec((1, 1, d), rows.mod_index(layer, 2, tm)),
        pl.BlockSpec((1, d), lambda i: (0, 0)),
    ]
    args += [w, h, mods, gain.reshape(1, d)]
    return pl.pallas_call(
        functools.partial(_outproj_kernel, mode=mode),
        grid=(n_rows // tm,),
        in_specs=in_specs,
        out_specs=pl.BlockSpec((tm, d), lambda i: (i, 0)),
        out_shape=jax.ShapeDtypeStruct((n_rows, d), F32),
        scratch_shapes=[pltpu.VMEM((tm, d), F32), pltpu.VMEM((tm, 1), F32)],
        compiler_params=_params(("parallel",)),
        name=name,
    )(*args)


def _mlp_kernel(h_ref, g2_ref, sh_ref, sc_ref, w1_ref, w2_ref, gate_ref, g3_ref, out_ref, v_ref, acc_ref,
                inv_ref):
    f = pl.program_id(1)

    @pl.when(f == 0)
    def _():
        v = _rms_ref(h_ref, inv_ref, g2_ref[...]) * (1.0 + sc_ref[0]) + sh_ref[0]
        v_ref[...] = v.astype(BF16)
        acc_ref[...] = jnp.zeros_like(acc_ref)

    a = jnp.maximum(_dot(v_ref[...], w1_ref[...]), 0.0)
    acc_ref[...] += _dot((a * a).astype(BF16), w2_ref[...])

    @pl.when(f == pl.num_programs(1) - 1)
    def _():
        out_ref[...] = h_ref[...] + gate_ref[0] * _rms_ref(acc_ref, inv_ref, g3_ref[...])


def mlp_residual(h, w1, w2, mods, g2, g3, *, rows, n_rows, tm, tf, layer):
    _, d, ff = w1.shape
    return pl.pallas_call(
        _mlp_kernel,
        grid=(n_rows // tm, ff // tf),
        in_specs=[
            pl.BlockSpec((tm, d), lambda i, f: (i, 0)),
            pl.BlockSpec((1, d), lambda i, f: (0, 0)),
            pl.BlockSpec((1, 1, d), rows.mod_index(layer, 3, tm)),
            pl.BlockSpec((1, 1, d), rows.mod_index(layer, 4, tm)),
            pl.BlockSpec((None, d, tf), lambda i, f: (layer, 0, f)),
            pl.BlockSpec((None, tf, d), lambda i, f: (layer, f, 0)),
            pl.BlockSpec((1, 1, d), rows.mod_index(layer, 5, tm)),
            pl.BlockSpec((1, d), lambda i, f: (0, 0)),
        ],
        out_specs=pl.BlockSpec((tm, d), lambda i, f: (i, 0)),
        out_shape=jax.ShapeDtypeStruct((n_rows, d), F32),
        scratch_shapes=[pltpu.VMEM((tm, d), BF16), pltpu.VMEM((tm, d), F32), pltpu.VMEM((tm, 1), F32)],
        compiler_params=_params(("parallel", "arbitrary")),
        name="mlp",
    )(h, g2.reshape(1, d), mods, mods, w1, w2, mods, g3.reshape(1, d))


def _rope_tables(seq, d_rot):
    pos = np.arange(seq)
    row, col = pos // GRID_W, pos % GRID_W
    n = d_rot // 4
    inv = ROPE_THETA ** (-np.arange(n, dtype=np.float64) / n)
    ang = np.concatenate([row[:, None] * inv, col[:, None] * inv], axis=-1)
    cos, sin = np.cos(ang), np.sin(ang)
    half = d_rot // 2
    cc = np.zeros((seq, LANE))
    ss = np.zeros((seq, LANE))
    cc[:, :half] = cos
    cc[:, LANE // 2:LANE // 2 + half] = cos
    ss[:, :half] = -sin
    ss[:, LANE // 2:LANE // 2 + half] = sin
    return jnp.asarray(cc, F32), jnp.asarray(ss, F32)


def _swa_kernel(sink_ref, q_ref, kp_ref, kc_ref, kn_ref, kx_ref, vp_ref, vc_ref, vn_ref, vx_ref, o_ref,
                *, nb, seq, ctx_len):
    n = pl.program_id(1)
    hd, grp, lb = SWA_HEAD_DIM, SWA_HEADS // SWA_KV_HEADS, SWA_BLOCK
    m_rows, n_loc = grp * lb, 3 * lb
    r = lax.broadcasted_iota(jnp.int32, (m_rows, n_loc + ctx_len), 0)
    c = lax.broadcasted_iota(jnp.int32, (m_rows, n_loc + ctx_len), 1)
    qpos = n * lb + (r & (lb - 1))
    kpos = (n - 1) * lb + c
    local_ok = (jnp.abs(kpos - qpos) <= SWA_WINDOW) & (kpos >= 0) & (kpos < seq) & (n < nb)
    valid = local_ok | (c >= n_loc)
    ones = jnp.ones((n_loc + ctx_len, hd), BF16)
    def scores(kv):
        ks = slice(kv * hd, (kv + 1) * hd)
        qg = jnp.concatenate([q_ref[:, (kv * grp + g) * hd:(kv * grp + g + 1) * hd] for g in range(grp)], axis=0)
        keys = jnp.concatenate([kp_ref[:, ks], kc_ref[:, ks], kn_ref[:, ks], kx_ref[:, ks]], axis=0)
        return jnp.where(valid, _dot_nt(qg, keys), -1e30)

    outs = []
    s_next = scores(0)
    for kv in range(SWA_KV_HEADS):
        s = s_next
        if kv + 1 < SWA_KV_HEADS:
            s_next = scores(kv + 1)
        ks = slice(kv * hd, (kv + 1) * hd)
        vals = jnp.concatenate([vp_ref[:, ks], vc_ref[:, ks], vn_ref[:, ks], vx_ref[:, ks]], axis=0)
        v_aug = jnp.concatenate([vals, ones], axis=1)
        for g in range(grp):
            sg = s[g * lb:(g + 1) * lb]
            snk = sink_ref[kv * grp + g] * math.log2(math.e)
            m = jnp.maximum(jnp.max(sg, axis=1, keepdims=True), snk)
            ov = _dot(jnp.exp2(sg - m).astype(BF16), v_aug)
            outs.append((ov[:, :hd] / (ov[:, hd:hd + 1] + jnp.exp2(snk - m))).astype(BF16))
    o_ref[...] = jnp.concatenate(outs, axis=1)


def swa_attention(qkv, sink, *, rows):
    b, seq, ctx_len = rows.batch, rows.seq, rows.ctx_len
    lb = SWA_BLOCK
    nb, ncb = seq // lb, ctx_len // lb
    lat_blocks = rows.n_lat // lb
    qw = SWA_HEADS * SWA_HEAD_DIM
    kw = SWA_KV_HEADS * SWA_HEAD_DIM
    kcol, vcol = qw // kw, qw // kw + 1

    def qidx(bi, n):
        return (jnp.where(n < nb, bi * nb + n, lat_blocks + bi * ncb + (n - nb)), 0)

    def kidx(off, col):
        return lambda bi, n: (bi * nb + jnp.clip(n + off, 0, nb - 1), col)

    def xidx(col):
        return lambda bi, n: (rows.n_lat // ctx_len + bi, col)

    kern = functools.partial(_swa_kernel, nb=nb, seq=seq, ctx_len=ctx_len)
    return pl.pallas_call(
        kern,
        grid=(b, nb + ncb),
        in_specs=[
            pl.BlockSpec(memory_space=pltpu.SMEM),
            pl.BlockSpec((lb, qw), qidx),
            pl.BlockSpec((lb, kw), kidx(-1, kcol)),
            pl.BlockSpec((lb, kw), kidx(0, kcol)),
            pl.BlockSpec((lb, kw), kidx(1, kcol)),
            pl.BlockSpec((ctx_len, kw), xidx(kcol)),
            pl.BlockSpec((lb, kw), kidx(-1, vcol)),
            pl.BlockSpec((lb, kw), kidx(0, vcol)),
            pl.BlockSpec((lb, kw), kidx(1, vcol)),
            pl.BlockSpec((ctx_len, kw), xidx(vcol)),
        ],
        out_specs=pl.BlockSpec((lb, qw), qidx),
        out_shape=jax.ShapeDtypeStruct((rows.n_all, qw), BF16),
        compiler_params=_params(("parallel", "parallel")),
        name="swa_attention",
    )(sink, qkv, qkv, qkv, qkv, qkv, qkv, qkv, qkv, qkv)


def _mlstm_kernel(*refs):
    ins, (of_ref, ob_ref, c_ref, m_ref) = refs[:14], refs[14:]
    c = pl.program_id(1)
    L, H, dqk, dv = ML_CHUNK, ML_HEADS, ML_DQK, ML_DV
    scale = dqk ** -0.5

    @pl.when(c == 0)
    def _():
        c_ref[...] = jnp.zeros_like(c_ref)
        m_ref[...] = jnp.zeros_like(m_ref)

    row = lax.broadcasted_iota(jnp.int32, (L, L), 0)
    col = lax.broadcasted_iota(jnp.int32, (L, L), 1)
    ones = jnp.ones((L, LANE), BF16)
    for d, o_ref in enumerate((of_ref, ob_ref)):
        q_ref, k_ref, v_ref, gi_ref, gf_ref, bi_ref, bf_ref = ins[7 * d:7 * d + 7]
        mask = (col <= row) if d == 0 else (col >= row)
        ones_mask = jnp.where(mask, 1.0, 0.0).astype(BF16)
        i_blk = gi_ref[...] + bi_ref[...]
        f_blk = jax.nn.log_sigmoid(gf_ref[...] + bf_ref[...])
        f_hi = f_blk.astype(BF16)
        r1 = f_blk - f_hi.astype(F32)
        f_mid = r1.astype(BF16)
        f_lo = (r1 - f_mid.astype(F32)).astype(BF16)
        b_blk = _dot(jnp.concatenate([ones_mask] * 3, axis=1), jnp.concatenate([f_hi, f_mid, f_lo], axis=0))
        b_end = jnp.sum(f_blk, axis=0, keepdims=True)
        e_rows = (i_blk - b_blk).T
        m_prev_blk = m_ref[d]
        dec = b_end - b_blk + i_blk
        m_new_blk = jnp.maximum(b_end + m_prev_blk, jnp.max(dec, axis=0, keepdims=True))
        ws_blk = jnp.exp(dec - m_new_blk)
        gs_blk = jnp.exp(b_end + m_prev_blk - m_new_blk)
        for h in range(H):
            qh = q_ref[:, h * dqk:(h + 1) * dqk]
            kh = k_ref[:, h * dqk:(h + 1) * dqk]
            v_aug = jnp.concatenate([v_ref[:, h * dv:(h + 1) * dv], ones], axis=1)
            e = jnp.where(mask, e_rows[h:h + 1, :], -jnp.inf)
            m_prev = m_prev_blk[:, h:h + 1]
            mm = jnp.maximum(m_prev, jnp.max(e, axis=1, keepdims=True))
            s = (_dot_nt(qh, kh) * jnp.exp(e - (mm - math.log(scale)))).astype(BF16)
            g = jnp.exp(m_prev - mm) * scale
            ct = c_ref[d, h]
            lhs = jnp.concatenate([s, (qh.astype(F32) * g).astype(BF16)], axis=1)
            res = _dot(lhs, jnp.concatenate([v_aug, ct.astype(BF16)], axis=0))
            floor = jnp.exp(-(b_blk[:, h:h + 1] + mm))
            hout = res[:, :dv] / jnp.maximum(jnp.abs(res[:, dv:dv + 1]), floor)
            o_ref[:, h * dv:(h + 1) * dv] = hout.astype(o_ref.dtype)
            kw = (kh.astype(F32) * ws_blk[:, h:h + 1]).astype(BF16)
            c_ref[d, h] = gs_blk[:, h:h + 1] * ct + _dot_tn(kw, v_aug)
        m_ref[d] = m_new_blk


def mlstm_scan(z, gates, gate_b, *, rows):
    b, seq, ctx_len = rows.batch, rows.seq, rows.ctx_len
    L = ML_CHUNK
    assert ctx_len == L and seq % L == 0
    nlc = seq // L
    hq, hv = ML_HEADS * ML_DQK, ML_HEADS * ML_DV

    def rb(d):
        def index(bi, c):
            lat = bi * nlc + (c - 1 if d == 0 else nlc - c)
            return jnp.where(c == 0, rows.n_lat // L + bi, lat)
        return index

    in_specs, args = [], []
    for d in range(2):
        r = rb(d)
        in_specs += [
            pl.BlockSpec((L, hq), lambda bi, c, r=r: (r(bi, c), 0)),
            pl.BlockSpec((L, hq), lambda bi, c, r=r: (r(bi, c), 1)),
            pl.BlockSpec((L, hv), lambda bi, c, r=r: (r(bi, c), 1)),
            pl.BlockSpec((L, LANE), lambda bi, c, r=r, d=d: (r(bi, c), 2 * d)),
            pl.BlockSpec((L, LANE), lambda bi, c, r=r, d=d: (r(bi, c), 2 * d + 1)),
            pl.BlockSpec((1, LANE), lambda bi, c, d=d: (0, 2 * d)),
            pl.BlockSpec((1, LANE), lambda bi, c, d=d: (0, 2 * d + 1)),
        ]
        args += [z, z, z, gates, gates, gate_b, gate_b]
    return pl.pallas_call(
        _mlstm_kernel,
        grid=(b, nlc + 1),
        in_specs=in_specs,
        out_specs=[pl.BlockSpec((L, hv), lambda bi, c, r=rb(d): (r(bi, c), 0)) for d in range(2)],
        out_shape=[jax.ShapeDtypeStruct((rows.n_all, hv), BF16)] * 2,
        scratch_shapes=[
            pltpu.VMEM((2, ML_HEADS, ML_DQK, ML_DV + LANE), F32),
            pltpu.VMEM((2, 1, LANE), F32),
        ],
        compiler_params=_params(("parallel", "arbitrary")),
        name="mlstm_scan",
    )(*args)


def _mla_kernel(q_ref, kvl_ref, kvx_ref, krl_ref, krx_ref, o_ref, kk_ref, vt_ref, *, seq, sub, ck):
    @pl.when(pl.program_id(2) == 0)
    def _():
        kk_ref[:seq, :LANE] = kvl_ref[:, :LANE]
        kk_ref[:seq, LANE:] = krl_ref[...]
        kk_ref[seq:, :LANE] = kvx_ref[:, :LANE]
        kk_ref[seq:, LANE:] = krx_ref[...]
        vt_ref[:LANE, :seq] = kvl_ref[:, LANE:].astype(F32).T.astype(BF16)
        vt_ref[:LANE, seq:] = kvx_ref[:, LANE:].astype(F32).T.astype(BF16)
        vt_ref[LANE:, :] = jnp.ones((vt_ref.shape[0] - LANE, vt_ref.shape[1]), BF16)

    nsub = q_ref.shape[0] // sub
    n_keys = kk_ref.shape[0]
    bounds = [(k0, min(k0 + ck, n_keys)) for k0 in range(0, n_keys, ck)]
    nck = len(bounds)
    qs = [q_ref[s * sub:(s + 1) * sub, :] for s in range(nsub)]

    def scores(s, c):
        return _dot_nt(kk_ref[bounds[c][0]:bounds[c][1], :], qs[s])

    m = [None] * nsub
    acc = [None] * nsub
    st = [None] * nsub

    def absorb(s, c):
        cm = jnp.max(st[s], axis=0, keepdims=True)
        m_new = cm if c == 0 else jnp.maximum(m[s], cm)
        pv = _dot(vt_ref[:, bounds[c][0]:bounds[c][1]], jnp.exp2(st[s] - m_new).astype(BF16))
        acc[s] = pv if c == 0 else acc[s] * jnp.exp2(m[s] - m_new) + pv
        m[s] = m_new

    for k in range(-1, nck + nsub - 1):
        nxt = [scores(s, k - s + 1) if 0 <= k - s + 1 < nck else None for s in range(nsub)]
        for s in range(nsub):
            if 0 <= k - s < nck:
                absorb(s, k - s)
        for s in range(nsub):
            if nxt[s] is not None:
                st[s] = nxt[s]
    for s in range(nsub):
        o = acc[s][:LANE] / acc[s][LANE:LANE + 1]
        o_ref[s * sub:(s + 1) * sub, :] = o.T.astype(o_ref.dtype)


def mla_attention(q, kv, z, *, rows, tq=2048, sub=256, ck=1024):
    b, seq, ctx_len = rows.batch, rows.seq, rows.ctx_len
    tq = min(tq, seq)
    nq = seq // tq
    kr_col = z.shape[1] // LANE - 1
    ctx0 = rows.n_lat // ctx_len
    n_keys = seq + ctx_len
    ones_rows = 16
    return pl.pallas_call(
        functools.partial(_mla_kernel, seq=seq, sub=sub, ck=ck),
        grid=(b, MLA_HEADS, nq),
        in_specs=[
            pl.BlockSpec((tq, 2 * LANE), lambda bi, h, i: (bi * nq + i, h)),
            pl.BlockSpec((seq, 2 * LANE), lambda bi, h, i: (bi, h)),
            pl.BlockSpec((ctx_len, 2 * LANE), lambda bi, h, i: (ctx0 + bi, h)),
            pl.BlockSpec((seq, LANE), lambda bi, h, i: (bi, kr_col)),
            pl.BlockSpec((ctx_len, LANE), lambda bi, h, i: (ctx0 + bi, kr_col)),
        ],
        out_specs=pl.BlockSpec((tq, LANE), lambda bi, h, i: (bi * nq + i, h)),
        out_shape=jax.ShapeDtypeStruct((rows.n_lat, MLA_HEADS * MLA_V), BF16),
        scratch_shapes=[pltpu.VMEM((n_keys, 2 * LANE), BF16), pltpu.VMEM((LANE + ones_rows, n_keys), BF16)],
        compiler_params=_params(("parallel", "parallel", "arbitrary")),
        name="mla_attention",
    )(q, kv, kv, z, z)


def _hy_filter_kernel(ft_ref, w1_ref, b1_ref, w2_ref, b2_ref, w3_ref, b3_ref, fr_ref, fr3_ref, w4_ref, b4_ref,
                      dec_ref, o_ref, a_ref, *, back_from):
    j = pl.program_id(0)
    fw = HY_FILTER_W
    hdot = functools.partial(jnp.dot, preferred_element_type=F32, precision=HIGHEST)

    @pl.when(j == 0)
    def _():
        fr = fr_ref[...]
        a = jnp.sin(fr * (hdot(ft_ref[...], w1_ref[...]) + b1_ref[...]))
        a = jnp.sin(fr * (hdot(a, w2_ref[...]) + b2_ref[...]))
        a3 = jnp.sin(fr3_ref[...] * (hdot(a, w3_ref[...]) + b3_ref[...]))
        hi = a3.astype(BF16)
        lo = (a3 - hi.astype(F32)).astype(BF16)
        lane = lax.broadcasted_iota(jnp.int32, a3.shape, 1)
        a_ref[...] = jnp.where((lane >= fw) & (lane < 2 * fw), lo, hi)

    w4 = w4_ref[...]
    w_hi = w4.astype(BF16)
    w_lo = (w4 - w_hi.astype(F32)).astype(BF16)
    rhs = jnp.concatenate([w_hi, w_hi, w_lo, jnp.zeros_like(w_hi)], axis=0)
    t = ft_ref[:, 0:1]
    filt = (_dot(a_ref[...], rhs) + b4_ref[...]) * jnp.exp(-t * jnp.abs(dec_ref[...]))
    row = lax.broadcasted_iota(jnp.int32, filt.shape, 0)
    o_ref[...] = jnp.where(jnp.logical_and(row == 0, j >= back_from), 0.0, filt)


def hyena_filters(seq, w1, b1, w2, b2, w3, b3, w4, b4, freq, decay, tn=512):
    t = np.linspace(0.0, 1.0, seq)[:, None]
    w = (2.0 * math.pi / seq) * np.arange(seq)[:, None]
    bands = np.linspace(1e-4, HY_BANDS - 1, HY_BANDS)[None, :]
    feats = np.zeros((seq, LANE))
    feats[:, :1 + 2 * HY_BANDS] = np.concatenate([t, np.cos(bands * w), -np.sin(bands * w)], axis=-1)
    n = w4.shape[1]
    fw = HY_FILTER_W
    w1p = jnp.zeros((LANE, fw), F32).at[:w1.shape[0]].set(w1)
    row = lambda a: a.reshape(1, -1)
    tile3 = lambda a: jnp.concatenate([a, a, a, jnp.zeros_like(a)], axis=-1)
    full = lambda shape: pl.BlockSpec(shape, lambda j: (0, 0))
    return pl.pallas_call(
        functools.partial(_hy_filter_kernel, back_from=(n // 2) // tn),
        grid=(n // tn,),
        in_specs=[full((seq, LANE)), full((LANE, fw)), full((1, fw)), full((fw, fw)), full((1, fw)),
                  full((fw, 4 * fw)), full((1, 4 * fw)), full((1, fw)), full((1, 4 * fw)),
                  pl.BlockSpec((fw, tn), lambda j: (0, j)),
                  pl.BlockSpec((1, tn), lambda j: (0, j)),
                  pl.BlockSpec((1, tn), lambda j: (0, j))],
        out_specs=pl.BlockSpec((seq, tn), lambda j: (0, j)),
        out_shape=jax.ShapeDtypeStruct((seq, n), F32),
        scratch_shapes=[pltpu.VMEM((seq, 4 * fw), BF16)],
        compiler_params=_params(("arbitrary",)),
        name="hyena_filters",
    )(jnp.asarray(feats, F32), w1p, row(b1), w2, row(b2), tile3(w3), tile3(row(b3)), row(freq), tile3(row(freq)),
      w4, row(b4), row(decay))


def _dft_mats(seq, p):
    n = 2 * seq
    q = n // p
    k1n = q // 2 + 1
    hi, k1 = np.arange(q // 2), np.arange(k1n)
    th = 2 * np.pi * np.outer(k1, hi) / q
    ma = np.zeros((2 * k1n, q // 2))
    ma[0::2], ma[1::2] = np.cos(th), -np.sin(th)
    lo = np.arange(p)
    wf = np.zeros((k1n, 2 * p, 2 * p))
    wi = np.zeros((k1n, 2 * p, 2 * p))
    for k in k1:
        ph = -2 * np.pi * (np.outer(lo, lo) / p + k * lo[None, :] / n)
        er, ei = np.cos(ph), np.sin(ph)
        wf[k] = np.block([[er, -ei], [ei, er]])
        wi[k] = np.block([[er.T, ei.T], [-ei.T, er.T]])
    c = np.full(k1n, 2.0)
    c[0] = c[-1] = 1.0
    th2 = 2 * np.pi * np.outer(hi, k1) / q
    md = np.zeros((q // 2, 2 * k1n))
    md[:, 0::2], md[:, 1::2] = c * np.cos(th2) / n, -c * np.sin(th2) / n
    as_bf16 = lambda a: jnp.asarray(a, F32).astype(BF16)
    return ma, as_bf16(wf), as_bf16(wi), md


def _hy_short_kernel(x1_ref, x2_ref, v_ref, w_ref, b_ref, o1_ref, o2_ref, o3_ref):
    for k, (x_ref, o_ref) in enumerate(((x1_ref, o1_ref), (x2_ref, o2_ref), (v_ref, o3_ref))):
        x = x_ref[...].astype(F32)
        n = x.shape[0]
        row = lax.broadcasted_iota(jnp.int32, x.shape, 0)
        prev = jnp.where(row == 0, 0.0, pltpu.roll(x, 1, 0))
        nxt = jnp.where(row == n - 1, 0.0, pltpu.roll(x, n - 1, 0))
        w = w_ref[k]
        o_ref[...] = (prev * w[0:1] + x * w[1:2] + nxt * w[2:3] + b_ref[k]).astype(o_ref.dtype)


def hyena_short_conv(x3, conv_w, conv_b, *, rows, cb=256):
    b, seq = rows.batch, rows.seq
    d = x3.shape[1] // 3
    nblk = d // cb
    w = jnp.transpose(conv_w.reshape(HY_SHORT, 3, d), (1, 0, 2))
    bias = conv_b.reshape(3, 1, d)
    in_x = [pl.BlockSpec((seq, cb), lambda bi, j, k=k: (bi, k * nblk + j)) for k in range(3)]
    out = pl.BlockSpec((seq, cb), lambda bi, j: (bi, j))
    shape = jax.ShapeDtypeStruct((rows.n_lat, d), BF16)
    return pl.pallas_call(
        _hy_short_kernel,
        grid=(b, nblk),
        in_specs=in_x + [pl.BlockSpec((3, HY_SHORT, cb), lambda bi, j: (0, 0, j)),
                         pl.BlockSpec((3, 1, cb), lambda bi, j: (0, 0, j))],
        out_specs=[out, out, out],
        out_shape=[shape, shape, shape],
        compiler_params=_params(("parallel", "parallel")),
        name="hyena_short_conv",
    )(x3, x3, x3, w, bias)


def _hy_stage_a_kernel(x_ref, m_ref, o_ref):
    qh, g, wc = x_ref.shape
    x = x_ref[...].reshape(qh * g, wc).astype(BF16)
    o_ref[...] = _dot(m_ref[...], x).reshape(o_ref.shape).astype(o_ref.dtype)


def hyena_stage_a(x, mak, *, p, width=2048):
    bx, seq, c = x.shape
    g = HY_G
    qh = seq // p
    mo = mak.shape[0] // g
    x5 = x.reshape(bx, qh, p // g, g, c)
    return pl.pallas_call(
        _hy_stage_a_kernel,
        grid=(bx, p // g, c // width),
        in_specs=[pl.BlockSpec((None, qh, None, g, width), lambda bi, l, j: (bi, 0, l, 0, j)),
                  pl.BlockSpec(mak.shape, lambda bi, l, j: (0, 0))],
        out_specs=pl.BlockSpec((None, mo, g, width), lambda bi, l, j: (bi, 0, l, j)),
        out_shape=jax.ShapeDtypeStruct((bx, mo, p, c), BF16),
        compiler_params=_params(("parallel", "parallel", "parallel")),
        name="hyena_stage_a",
    )(x5, mak)


def _hy_spec_kernel(a0_ref, a1_ref, wf_ref, o_ref):
    p = o_ref.shape[2]
    cb = o_ref.shape[3]
    wf = wf_ref[0]
    x0 = _dot(wf, a0_ref[0].reshape(2 * p, cb))
    x1 = _dot(wf, a1_ref[0].reshape(2 * p, cb))
    o_ref[0, 0] = x0[:p] + x1[:p]
    o_ref[0, 1] = x0[p:] - x1[p:]


def hyena_spectrum(af, wf, *, d, cb=2048):
    k1n, _, p, _ = af.shape
    nblk = (HY_ORDER * d) // cb
    return pl.pallas_call(
        _hy_spec_kernel,
        grid=(k1n, nblk),
        in_specs=[pl.BlockSpec((1, 2, p, cb), lambda k, j: (k, 0, 0, j)),
                  pl.BlockSpec((1, 2, p, cb), lambda k, j: (k, 0, 0, nblk + j)),
                  pl.BlockSpec((1, 2 * p, 2 * p), lambda k, j: (k, 0, 0))],
        out_specs=pl.BlockSpec((1, 2, p, cb), lambda k, j: (k, 0, 0, j)),
        out_shape=jax.ShapeDtypeStruct((k1n, 2, p, HY_ORDER * d), F32),
        compiler_params=_params(("parallel", "parallel")),
        name="hyena_spectrum",
    )(af, af, wf)


def _hy_mid_kernel(a_ref, wf_ref, wi_ref, h_ref, o_ref):
    b, _, p, cb = a_ref.shape
    hr, hi = h_ref[0], h_ref[1]
    for bi in range(b):
        x = _dot(wf_ref[...], a_ref[bi].reshape(2 * p, cb))
        xr, xi = x[:p], x[p:]
        y = jnp.concatenate([xr * hr - xi * hi, xr * hi + xi * hr], axis=0).astype(BF16)
        o_ref[bi] = _dot(wi_ref[...], y).reshape(2, p, cb).astype(o_ref.dtype)


def hyena_mid(a, wf, wi, spec, *, order, d, cb=1024):
    b, k1n, _, p, _ = a.shape
    nblk = d // cb
    data = pl.BlockSpec((b, None, 2, p, cb), lambda k, j: (0, k, 0, 0, j))
    mat = pl.BlockSpec((None, 2 * p, 2 * p), lambda k, j: (k, 0, 0))
    return pl.pallas_call(
        _hy_mid_kernel,
        grid=(k1n, nblk),
        in_specs=[data, mat, mat,
                  pl.BlockSpec((None, 2, p, cb), lambda k, j: (k, 0, 0, order * nblk + j))],
        out_specs=data,
        out_shape=jax.ShapeDtypeStruct(a.shape, BF16),
        compiler_params=_params(("parallel", "parallel")),
        name="hyena_mid",
    )(a, wf, wi, spec)


def _hy_stage_d_kernel(c_ref, m_ref, x_ref, z_ref, skip_ref, *rest, fused):
    ma_ref, o_ref, a_ref = rest if fused else (None, rest[0], None)
    mo, g, wc = c_ref.shape
    qh = x_ref.shape[0]
    conv = _dot(m_ref[...], c_ref[...].reshape(mo * g, wc))
    z = z_ref[...].reshape(qh * g, wc).astype(F32)
    x = x_ref[...].reshape(qh * g, wc).astype(F32)
    y = (x * (conv + z * skip_ref[...])).astype(o_ref.dtype)
    o_ref[...] = y.reshape(o_ref.shape)
    if fused:
        a_ref[...] = _dot(ma_ref[...], y).reshape(a_ref.shape).astype(a_ref.dtype)


def hyena_stage_d(c, mdk, xg, z, skip, *, p, mak=None, width=2048):
    b, mo, _, d = c.shape
    g = HY_G
    seq = xg.shape[1]
    qh = seq // p
    fused = mak is not None
    v5 = lambda a: a.reshape(b, qh, p // g, g, d)
    tspec = pl.BlockSpec((None, qh, None, g, width), lambda bi, l, j: (bi, 0, l, 0, j))
    fspec = pl.BlockSpec((None, mo, g, width), lambda bi, l, j: (bi, 0, l, j))
    in_specs = [fspec, pl.BlockSpec(mdk.shape, lambda bi, l, j: (0, 0)), tspec, tspec,
                pl.BlockSpec((1, width), lambda bi, l, j: (0, j))]
    args = [c, mdk, v5(xg), v5(z), skip.reshape(1, d)]
    out_specs, out_shape = [tspec], [jax.ShapeDtypeStruct((b, qh, p // g, g, d), BF16)]
    if fused:
        in_specs.append(pl.BlockSpec(mak.shape, lambda bi, l, j: (0, 0)))
        args.append(mak)
        out_specs.append(fspec)
        out_shape.append(jax.ShapeDtypeStruct(c.shape, BF16))
    outs = pl.pallas_call(
        functools.partial(_hy_stage_d_kernel, fused=fused),
        grid=(b, p // g, d // width),
        in_specs=in_specs,
        out_specs=out_specs,
        out_shape=out_shape,
        compiler_params=_params(("parallel", "parallel", "parallel")),
        name="hyena_stage_d",
    )(*args)
    y = outs[0].reshape(b, seq, d)
    return (y, outs[1]) if fused else y


def hyena_long_convs(x1, x2, v, filt, skip, *, rows):
    b, seq = rows.batch, rows.seq
    d = x1.shape[1]
    p = HY_P
    k1n = seq // p + 1
    ma, wf, wi, md = _dft_mats(seq, p)
    eye = np.eye(HY_G)
    as_bf16 = lambda a: jnp.asarray(a, F32).astype(BF16)
    mak, mdk = as_bf16(np.kron(ma, eye)), as_bf16(np.kron(md, eye))
    af = hyena_stage_a(filt.reshape(1, seq, filt.shape[1]), mak, p=p)
    spec = hyena_spectrum(af.reshape(k1n, 2, p, filt.shape[1]), wf, d=d)
    t3 = lambda a: a.reshape(b, seq, d)
    y = t3(v)
    a = hyena_stage_a(y, mak, p=p)
    c = hyena_mid(a.reshape(b, k1n, 2, p, d), wf, wi, spec, order=0, d=d)
    y, a = hyena_stage_d(c.reshape(a.shape), mdk, t3(x1), y, skip[0], p=p, mak=mak)
    c = hyena_mid(a.reshape(b, k1n, 2, p, d), wf, wi, spec, order=1, d=d)
    y = hyena_stage_d(c.reshape(a.shape), mdk, t3(x2), y, skip[1], p=p)
    return y.reshape(rows.n_lat, d)


def _mlstm_gate_weights(w_in, gate_b):
    h = ML_HEADS
    o4 = 2 * h * ML_DQK + 2 * h * ML_DV
    wg = w_in[:, o4:].reshape(-1, 2, 2, h)
    bg = gate_b.reshape(2, 2, h)
    w_out = jnp.zeros((w_in.shape[0], 4 * LANE), F32)
    b_out = jnp.zeros((1, 4 * LANE), F32)
    for d in range(2):
        for gate in range(2):
            lo = (2 * d + gate) * LANE
            w_out = w_out.at[:, lo:lo + h].set(wg[:, gate, d])
            b_out = b_out.at[0, lo:lo + h].set(bg[gate, d])
    return w_out, b_out


def _mla_weights(w_in, w_uq):
    r2 = MLA_Q_RANK + MLA_KV_RANK
    half = MLA_ROPE // 2

    def spread(w):
        z = jnp.zeros(w.shape[:-1] + (half,), w.dtype)
        return jnp.concatenate([w[..., :half], z, w[..., half:], z], axis=-1)

    w_in_p = jnp.concatenate([w_in[:, :r2], spread(w_in[:, r2:])], axis=1)
    wq = w_uq.reshape(w_uq.shape[0], MLA_HEADS, MLA_NOPE + MLA_ROPE)
    wq_p = jnp.concatenate([wq[..., :MLA_NOPE], spread(wq[..., MLA_NOPE:])], axis=-1)
    wq_p = wq_p * ((MLA_NOPE + MLA_ROPE) ** -0.5 * math.log2(math.e))
    return w_in_p, wq_p.reshape(w_uq.shape[0], MLA_HEADS * 2 * LANE)


def kernel(x, c, ctx, c_ctx, ada_w, ada_b, norm_g, mlp_w1, mlp_w2, swa_w_qkv, swa_sink, swa_w_o, ml_w_in, ml_gate_b, ml_head_g, ml_w_o, mla_w_in, mla_q_g, mla_kv_g, mla_w_uq, mla_w_ukv, mla_w_o, hy_w_in, hy_conv_w, hy_conv_b, hy_f_w1, hy_f_b1, hy_f_w2, hy_f_b2, hy_f_w3, hy_f_b3, hy_f_w4, hy_f_b4, hy_f_freq, hy_decay, hy_skip, hy_w_o):
    b, seq, d = x.shape
    ctx_len = ctx.shape[1]
    depth = ada_w.shape[0]
    assert depth == 4 and b < 8
    rows = Rows(b, seq, ctx_len)
    tm = 512
    tm_lin = 1024 if rows.n_lat % 1024 == 0 and rows.n_ctx % 1024 == 0 else 512
    bf = lambda w: w.astype(BF16)
    mlp_w1b, mlp_w2b = bf(mlp_w1), bf(mlp_w2)

    cond = jnp.zeros((8, d), F32).at[:b].set(c).at[b].set(c_ctx)
    mods = ada_mods(cond, ada_w, ada_b).reshape(depth * 8, 1, N_MOD * d)

    common = dict(rows=rows, tm=tm_lin, mods=mods)

    def finish(h, o_args, w_o, layer, n_rows, mode="plain"):
        h = outproj_residual(o_args, bf(w_o), h, mods, norm_g[layer, 1], rows=rows, n_rows=n_rows, tm=tm,
                             layer=layer, mode=mode, name=f"outproj{layer}")
        return mlp_residual(h, mlp_w1b, mlp_w2b, mods, norm_g[layer, 2], norm_g[layer, 3],
                            rows=rows, n_rows=n_rows, tm=tm, tf=1024, layer=layer)

    n_qk_groups = SWA_HEADS + SWA_KV_HEADS
    q_cols = SWA_HEADS * SWA_HEAD_DIM
    q_scale = SWA_HEAD_DIM ** -0.5 * math.log2(math.e)
    w_qkv = jnp.concatenate([swa_w_qkv[0][:, :q_cols] * q_scale, swa_w_qkv[0][:, q_cols:]], axis=1)
    qkv, h = linear(x.reshape(rows.n_lat, d), bf(w_qkv), x2=ctx.reshape(rows.n_ctx, d), n_rows=rows.n_all, tn=512,
                    out_dtype=BF16, prologue="norm_mod",
                 gain=norm_g[0, 0], layer=0, rope_tabs=_rope_tables(seq, SWA_HEAD_DIM), rope_pattern=(True,) * 4,
                 rope_jmax=n_qk_groups // 4, name="swa_qkv", rows=rows, tm=tm, mods=mods)
    o = swa_attention(qkv, swa_sink[0], rows=rows)
    h = finish(h, o, swa_w_o[0], 0, rows.n_all)

    w_gate, b_gate = _mlstm_gate_weights(ml_w_in[0], ml_gate_b[0])
    z = linear(h, bf(ml_w_in[0]), n_rows=rows.n_all, tn=1536, out_dtype=BF16, prologue="norm_mod",
               gain=norm_g[1, 0], layer=1, n_out=2 * ML_HEADS * (ML_DQK + ML_DV), name="mlstm_in", **common)
    gates = linear(h, bf(w_gate), n_rows=rows.n_all, tn=4 * LANE, out_dtype=F32, prologue="norm_mod",
                   gain=norm_g[1, 0], layer=1, name="mlstm_gates", **common)
    h_fwd, h_bwd = mlstm_scan(z, gates, b_gate, rows=rows)
    h = finish(h, (h_fwd, h_bwd, z, ml_head_g[0]), ml_w_o[0], 1, rows.n_all, mode="mlstm")

    w_in_p, w_uq_p = _mla_weights(mla_w_in[0], mla_w_uq[0])
    rope_mla = _rope_tables(seq, MLA_ROPE)
    zc = linear(h, bf(w_in_p), n_rows=rows.n_all, tn=w_in_p.shape[1], out_dtype=BF16, prologue="norm_mod",
                gain=norm_g[2, 0], layer=2, rope_tabs=rope_mla, rope_pattern=(False,) * 8 + (True,),
                name="mla_in", **common)
    q = linear(zc, bf(w_uq_p), n_rows=rows.n_lat, tn=2048, out_dtype=BF16, x_cols=0, prologue="norm",
               gain=mla_q_g[0], rope_tabs=rope_mla, rope_pattern=(False, True) * 8, name="mla_q",
               rows=rows, tm=tm_lin)
    kv = linear(zc, bf(mla_w_ukv[0]), n_rows=rows.n_all, tn=2048, out_dtype=BF16, x_cols=1, prologue="norm",
                gain=mla_kv_g[0], name="mla_kv", rows=rows, tm=tm_lin)
    o = mla_attention(q, kv, zc, rows=rows)
    h = finish(h, o, mla_w_o[0], 2, rows.n_lat)

    x3 = linear(h, bf(hy_w_in[0]), n_rows=rows.n_lat, tn=1536, out_dtype=BF16, prologue="norm_mod",
                gain=norm_g[3, 0], layer=3, name="hyena_in", **common)
    x1, x2, v = hyena_short_conv(x3, hy_conv_w[0], hy_conv_b[0], rows=rows)
    filt = hyena_filters(seq, hy_f_w1[0], hy_f_b1[0], hy_f_w2[0], hy_f_b2[0], hy_f_w3[0], hy_f_b3[0],
                         hy_f_w4[0], hy_f_b4[0], hy_f_freq[0], hy_decay[0])
    y = hyena_long_convs(x1, x2, v, filt, hy_skip[0], rows=rows)
    h = finish(h, y, hy_w_o[0], 3, rows.n_lat)
    return h.reshape(b, seq, d)
```

```python
import functools
import math

import numpy as np
import jax
import jax.numpy as jnp
from jax import lax
from jax.experimental import pallas as pl
from jax.experimental.pallas import tpu as pltpu

F32, BF16 = jnp.float32, jnp.bfloat16
HIGHEST = lax.Precision.HIGHEST

RMS_EPS = 1e-6
ROPE_THETA = 10000.0
GRID_W = 64
N_MOD = 6
LANE = 128

SWA_HEADS, SWA_KV_HEADS, SWA_HEAD_DIM, SWA_WINDOW, SWA_BLOCK = 16, 4, 128, 128, 128
ML_HEADS, ML_DQK, ML_DV = 8, 128, 256
ML_CHUNK = 256
MLA_HEADS, MLA_Q_RANK, MLA_KV_RANK, MLA_NOPE, MLA_ROPE, MLA_V = 16, 512, 512, 128, 64, 128
HY_ORDER, HY_BANDS, HY_FILTER_W, HY_SHORT = 2, 16, 64, 3
HY_P = 256
HY_G = 16

VMEM_LIMIT = 48 * 1024 * 1024


def _params(sem):
    return pltpu.CompilerParams(dimension_semantics=sem, vmem_limit_bytes=VMEM_LIMIT)


def _dot(a, b):
    return jnp.dot(a, b, preferred_element_type=F32)


def _dot_nt(a, b):
    return lax.dot_general(a, b, (((1,), (1,)), ((), ())), preferred_element_type=F32)


def _dot_tn(a, b):
    return lax.dot_general(a, b, (((0,), (0,)), ((), ())), preferred_element_type=F32)


def _rms(x, g):
    return x * lax.rsqrt(jnp.mean(x * x, axis=-1, keepdims=True) + RMS_EPS) * g


def _rms_ref(x_ref, inv_ref, g):
    x = x_ref[...].astype(F32)
    inv_ref[...] = lax.rsqrt(jnp.mean(x * x, axis=-1, keepdims=True) + RMS_EPS)
    return x_ref[...].astype(F32) * inv_ref[...] * g


def _ada_kernel(s_ref, w_ref, b_ref, o_ref):
    s = s_ref[...]
    s = s * jax.nn.sigmoid(s)
    hi = s.astype(BF16)
    lo = (s - hi.astype(F32)).astype(BF16)
    w = w_ref[0]
    w_hi = w.astype(BF16)
    w_lo = (w - w_hi.astype(F32)).astype(BF16)
    r = _dot(jnp.concatenate([hi, lo], axis=0), w_hi)
    n = s.shape[0]
    o_ref[0] = r[:n] + r[n:] + _dot(hi, w_lo) + b_ref[0]


def ada_mods(cond, ada_w, ada_b, tn=1536):
    depth, d, n = ada_w.shape
    rows = cond.shape[0]
    return pl.pallas_call(
        _ada_kernel,
        grid=(depth, n // tn),
        in_specs=[
            pl.BlockSpec((rows, d), lambda l, j: (0, 0)),
            pl.BlockSpec((1, d, tn), lambda l, j: (l, 0, j)),
            pl.BlockSpec((1, 1, tn), lambda l, j: (l, 0, j)),
        ],
        out_specs=pl.BlockSpec((1, rows, tn), lambda l, j: (l, 0, j)),
        out_shape=jax.ShapeDtypeStruct((depth, rows, n), F32),
        compiler_params=_params(("parallel", "parallel")),
        name="ada_mods",
    )(cond, ada_w, ada_b.reshape(depth, 1, n))


class Rows:
    def __init__(self, batch, seq, ctx_len):
        self.batch, self.seq, self.ctx_len = batch, seq, ctx_len
        self.n_lat = batch * seq
        self.n_ctx = batch * ctx_len
        self.n_all = self.n_lat + self.n_ctx

    def mod_index(self, layer, k, tm):
        lat_blocks, per_batch = self.n_lat // tm, self.seq // tm

        def index(i, *_):
            b = jnp.where(i < lat_blocks, i // per_batch, self.batch)
            return (layer * 8 + b, 0, k)

        return index


def _linear_kernel(*refs, prologue, rope, rope_pattern, n_lat_blocks, dual, single):
    it = iter(refs)
    x_ref = next(it)
    x2_ref = next(it) if dual else None
    g_ref = next(it) if prologue in ("norm", "norm_mod") else None
    sh_ref = next(it) if prologue == "norm_mod" else None
    sc_ref = next(it) if prologue == "norm_mod" else None
    w_ref = next(it)
    cc_ref = next(it) if rope else None
    ss_ref = next(it) if rope else None
    o_ref = next(it)
    u_ref = next(it)
    inv_ref = next(it)
    first = pl.program_id(1) == 0

    def prologue_from(src_ref):
        if prologue in ("norm", "norm_mod"):
            x = _rms_ref(src_ref, inv_ref, g_ref[...])
        else:
            x = src_ref[...].astype(F32)
        if prologue == "norm_mod":
            x = x * (1.0 + sc_ref[0]) + sh_ref[0]
        u_ref[...] = x.astype(BF16)

    if dual:
        is_lat = pl.program_id(0) < n_lat_blocks
        pl.when(jnp.logical_and(first, is_lat))(functools.partial(prologue_from, x_ref))
        pl.when(jnp.logical_and(first, jnp.logical_not(is_lat)))(functools.partial(prologue_from, x2_ref))
    elif single:
        prologue_from(x_ref)
    else:
        pl.when(first)(functools.partial(prologue_from, x_ref))

    acc = _dot(u_ref[...], w_ref[...])

    def plain():
        o_ref[...] = acc.astype(o_ref.dtype)

    def roped():
        cc, ss = cc_ref[...], ss_ref[...]
        segs = []
        for gi, on in enumerate(rope_pattern):
            seg = acc[:, gi * LANE:(gi + 1) * LANE]
            if on:
                seg = seg * cc + pltpu.roll(seg, LANE // 2, 1) * ss
            segs.append(seg)
        o_ref[...] = jnp.concatenate(segs, axis=1).astype(o_ref.dtype)

    if rope:
        roped()
    else:
        plain()


def linear(x, w, *, rows, n_rows, tm, tn, out_dtype, x_cols=None, prologue="none", gain=None,
           mods=None, layer=0, mod_k=(0, 1), rope_tabs=None, rope_pattern=(), rope_jmax=1 << 30,
           n_out=None, x2=None, name="linear"):
    k, n = w.shape
    n = n if n_out is None else n_out
    xc = 0 if x_cols is None else x_cols
    rope = rope_tabs is not None
    dual = x2 is not None
    n_lat_blocks = rows.n_lat // tm
    if dual:
        in_specs = [pl.BlockSpec((tm, k), lambda i, j: (jnp.minimum(i, n_lat_blocks - 1), 0)),
                    pl.BlockSpec((tm, k), lambda i, j: (jnp.maximum(i - n_lat_blocks, 0), 0),
                                 pipeline_mode=pl.Buffered(1))]
        args = [x, x2]
    else:
        in_specs = [pl.BlockSpec((tm, k), lambda i, j: (i, xc))]
        args = [x]
    if prologue in ("norm", "norm_mod"):
        in_specs.append(pl.BlockSpec((1, k), lambda i, j: (0, 0)))
        args.append(gain.reshape(1, k))
    if prologue == "norm_mod":
        for mk in mod_k:
            in_specs.append(pl.BlockSpec((1, 1, k), rows.mod_index(layer, mk, tm)))
            args.append(mods)
    in_specs.append(pl.BlockSpec((k, tn), lambda i, j: (0, j)))
    args.append(w)
    if rope:
        per_seq = rows.seq // tm

        def tab_index(i, j):
            rotate = jnp.logical_and(i < n_lat_blocks, j < rope_jmax)
            return (jnp.where(rotate, 0, 1), i % per_seq, 0)

        for t, fill in zip(rope_tabs, (1.0, 0.0)):
            in_specs.append(pl.BlockSpec((None, tm, LANE), tab_index))
            args.append(jnp.stack([t, jnp.full_like(t, fill)]))
    kern = functools.partial(_linear_kernel, prologue=prologue, rope=rope, rope_pattern=tuple(rope_pattern),
                             n_lat_blocks=n_lat_blocks, dual=dual, single=(n == tn))
    return pl.pallas_call(
        kern,
        grid=(n_rows // tm, n // tn),
        in_specs=in_specs,
        out_specs=pl.BlockSpec((tm, tn), lambda i, j: (i, j)),
        out_shape=jax.ShapeDtypeStruct((n_rows, n), out_dtype),
        scratch_shapes=[pltpu.VMEM((tm, k), BF16), pltpu.VMEM((tm, 1), F32)],
        compiler_params=_params(("parallel", "arbitrary")),
        name=name,
    )(*args)


def _outproj_kernel(*refs, mode, n_lat_blocks):
    it = iter(refs)
    if mode == "mlstm":
        hf_ref, hb_ref, og_ref, hg_ref = next(it), next(it), next(it), next(it)
    else:
        o_ref_in = next(it)
    w_ref, h_ref = next(it), next(it)
    h2_ref = next(it) if n_lat_blocks is not None else None
    gate_ref, g_ref, out_ref = next(it), next(it), next(it)
    y_ref, inv_ref = next(it), next(it)
    if mode == "mlstm":
        hs = hf_ref[...].astype(F32) + hb_ref[...].astype(F32)
        og = jax.nn.sigmoid(og_ref[...].astype(F32))
        hg = hg_ref[...]
        parts = []
        for h in range(ML_HEADS):
            sl = slice(h * ML_DV, (h + 1) * ML_DV)
            parts.append((_rms(hs[:, sl], hg[:, sl]) * og[:, sl]).astype(BF16))
        o = jnp.concatenate(parts, axis=1)
    else:
        o = o_ref_in[...]
    y_ref[...] = _dot(o, w_ref[...])
    upd = gate_ref[0] * _rms_ref(y_ref, inv_ref, g_ref[...])
    if h2_ref is None:
        out_ref[...] = h_ref[...] + upd
    else:
        out_ref[...] = jnp.where(pl.program_id(0) < n_lat_blocks, h_ref[...], h2_ref[...]) + upd


def outproj_residual(o_args, w, h, mods, gain, *, rows, n_rows, tm, layer, mode="plain", h2=None, name="outproj"):
    k, d = w.shape
    n_lat_blocks = rows.n_lat // tm if h2 is not None else None
    if mode == "mlstm":
        h_fwd, h_bwd, z, head_g = o_args
        in_specs = [
            pl.BlockSpec((tm, k), lambda i: (i, 0)),
            pl.BlockSpec((tm, k), lambda i: (i, 0)),
            pl.BlockSpec((tm, k), lambda i: (i, 2)),
            pl.BlockSpec((1, k), lambda i: (0, 0)),
        ]
        args = [h_fwd, h_bwd, z, head_g.reshape(1, k)]
    else:
        in_specs = [pl.BlockSpec((tm, k), lambda i: (i, 0))]
        args = [o_args]
    in_specs.append(pl.BlockSpec((k, d), lambda i: (0, 0)))
    args.append(w)
    if h2 is None:
        in_specs.append(pl.BlockSpec((tm, d), lambda i: (i, 0)))
        args.append(h)
    else:
        in_specs += [pl.BlockSpec((tm, d), lambda i: (jnp.minimum(i, n_lat_blocks - 1), 0)),
                     pl.BlockSpec((tm, d), lambda i: (jnp.maximum(i - n_lat_blocks, 0), 0))]
        args += [h, h2]
    in_specs += [
        pl.BlockSpec((1, 1, d), rows.mod_index(layer, 2, tm)),
        pl.BlockSpec((1, d), lambda i: (0, 0)),
    ]
    args += [mods, gain.reshape(1, d)]
    return pl.pallas_call(
        functools.partial(_outproj_kernel, mode=mode, n_lat_blocks=n_lat_blocks),
        grid=(n_rows // tm,),
        in_specs=in_specs,
        out_specs=pl.BlockSpec((tm, d), lambda i: (i, 0)),
        out_shape=jax.ShapeDtypeStruct((n_rows, d), F32),
        scratch_shapes=[pltpu.VMEM((tm, d), F32), pltpu.VMEM((tm, 1), F32)],
        compiler_params=_params(("parallel",)),
        name=name,
    )(*args)


def _mlp_kernel(h_ref, g2_ref, sh_ref, sc_ref, w1_ref, w2_ref, gate_ref, g3_ref, out_ref, v_ref, acc_ref,
                inv_ref):
    f = pl.program_id(1)

    @pl.when(f == 0)
    def _():
        v = _rms_ref(h_ref, inv_ref, g2_ref[...]) * (1.0 + sc_ref[0]) + sh_ref[0]
        v_ref[...] = v.astype(BF16)
        acc_ref[...] = jnp.zeros_like(acc_ref)

    a = jnp.maximum(_dot(v_ref[...], w1_ref[...]), 0.0)
    acc_ref[...] += _dot((a * a).astype(BF16), w2_ref[...])

    @pl.when(f == pl.num_programs(1) - 1)
    def _():
        out_ref[...] = h_ref[...] + gate_ref[0] * _rms_ref(acc_ref, inv_ref, g3_ref[...])


def mlp_residual(h, w1, w2, mods, g2, g3, *, rows, n_rows, tm, tf, layer):
    _, d, ff = w1.shape
    return pl.pallas_call(
        _mlp_kernel,
        grid=(n_rows // tm, ff // tf),
        in_specs=[
            pl.BlockSpec((tm, d), lambda i, f: (i, 0)),
            pl.BlockSpec((1, d), lambda i, f: (0, 0)),
            pl.BlockSpec((1, 1, d), rows.mod_index(layer, 3, tm)),
            pl.BlockSpec((1, 1, d), rows.mod_index(layer, 4, tm)),
            pl.BlockSpec((None, d, tf), lambda i, f: (layer, 0, f)),
            pl.BlockSpec((None, tf, d), lambda i, f: (layer, f, 0)),
            pl.BlockSpec((1, 1, d), rows.mod_index(layer, 5, tm)),
            pl.BlockSpec((1, d), lambda i, f: (0, 0)),
        ],
        out_specs=pl.BlockSpec((tm, d), lambda i, f: (i, 0)),
        out_shape=jax.ShapeDtypeStruct((n_rows, d), F32),
        scratch_shapes=[pltpu.VMEM((tm, d), BF16), pltpu.VMEM((tm, d), F32), pltpu.VMEM((tm, 1), F32)],
        compiler_params=_params(("parallel", "arbitrary")),
        name="mlp",
    )(h, g2.reshape(1, d), mods, mods, w1, w2, mods, g3.reshape(1, d))


def _rope_tables(seq, d_rot):
    pos = np.arange(seq)
    row, col = pos // GRID_W, pos % GRID_W
    n = d_rot // 4
    inv = ROPE_THETA ** (-np.arange(n, dtype=np.float64) / n)
    ang = np.concatenate([row[:, None] * inv, col[:, None] * inv], axis=-1)
    cos, sin = np.cos(ang), np.sin(ang)
    half = d_rot // 2
    cc = np.zeros((seq, LANE))
    ss = np.zeros((seq, LANE))
    cc[:, :half] = cos
    cc[:, LANE // 2:LANE // 2 + half] = cos
    ss[:, :half] = -sin
    ss[:, LANE // 2:LANE // 2 + half] = sin
    return jnp.asarray(cc, F32), jnp.asarray(ss, F32)


def _swa_kernel(sink_ref, q_ref, kp_ref, kc_ref, kn_ref, kx_ref, vp_ref, vc_ref, vn_ref, vx_ref, o_ref,
                *, nb, seq, ctx_len):
    n = pl.program_id(1)
    hd, grp, lb = SWA_HEAD_DIM, SWA_HEADS // SWA_KV_HEADS, SWA_BLOCK
    m_rows, n_loc = grp * lb, 3 * lb
    r = lax.broadcasted_iota(jnp.int32, (m_rows, n_loc + ctx_len), 0)
    c = lax.broadcasted_iota(jnp.int32, (m_rows, n_loc + ctx_len), 1)
    qpos = n * lb + (r & (lb - 1))
    kpos = (n - 1) * lb + c
    local_ok = (jnp.abs(kpos - qpos) <= SWA_WINDOW) & (kpos >= 0) & (kpos < seq) & (n < nb)
    valid = local_ok | (c >= n_loc)
    ones = jnp.ones((n_loc + ctx_len, hd), BF16)
    def scores(kv):
        ks = slice(kv * hd, (kv + 1) * hd)
        qg = jnp.concatenate([q_ref[:, (kv * grp + g) * hd:(kv * grp + g + 1) * hd] for g in range(grp)], axis=0)
        keys = jnp.concatenate([kp_ref[:, ks], kc_ref[:, ks], kn_ref[:, ks], kx_ref[:, ks]], axis=0)
        return jnp.where(valid, _dot_nt(qg, keys), -1e30)

    outs = []
    s_next = scores(0)
    for kv in range(SWA_KV_HEADS):
        s = s_next
        if kv + 1 < SWA_KV_HEADS:
            s_next = scores(kv + 1)
        ks = slice(kv * hd, (kv + 1) * hd)
        vals = jnp.concatenate([vp_ref[:, ks], vc_ref[:, ks], vn_ref[:, ks], vx_ref[:, ks]], axis=0)
        v_aug = jnp.concatenate([vals, ones], axis=1)
        for g in range(grp):
            sg = s[g * lb:(g + 1) * lb]
            snk = sink_ref[kv * grp + g] * math.log2(math.e)
            m = jnp.maximum(jnp.max(sg, axis=1, keepdims=True), snk)
            ov = _dot(jnp.exp2(sg - m).astype(BF16), v_aug)
            outs.append((ov[:, :hd] / (ov[:, hd:hd + 1] + jnp.exp2(snk - m))).astype(BF16))
    o_ref[...] = jnp.concatenate(outs, axis=1)


def swa_attention(qkv, sink, *, rows):
    b, seq, ctx_len = rows.batch, rows.seq, rows.ctx_len
    lb = SWA_BLOCK
    nb, ncb = seq // lb, ctx_len // lb
    lat_blocks = rows.n_lat // lb
    qw = SWA_HEADS * SWA_HEAD_DIM
    kw = SWA_KV_HEADS * SWA_HEAD_DIM
    kcol, vcol = qw // kw, qw // kw + 1

    def qidx(bi, n):
        return (jnp.where(n < nb, bi * nb + n, lat_blocks + bi * ncb + (n - nb)), 0)

    def kidx(off, col):
        return lambda bi, n: (bi * nb + jnp.clip(n + off, 0, nb - 1), col)

    def xidx(col):
        return lambda bi, n: (rows.n_lat // ctx_len + bi, col)

    kern = functools.partial(_swa_kernel, nb=nb, seq=seq, ctx_len=ctx_len)
    return pl.pallas_call(
        kern,
        grid=(b, nb + ncb),
        in_specs=[
            pl.BlockSpec(memory_space=pltpu.SMEM),
            pl.BlockSpec((lb, qw), qidx),
            pl.BlockSpec((lb, kw), kidx(-1, kcol)),
            pl.BlockSpec((lb, kw), kidx(0, kcol)),
            pl.BlockSpec((lb, kw), kidx(1, kcol)),
            pl.BlockSpec((ctx_len, kw), xidx(kcol)),
            pl.BlockSpec((lb, kw), kidx(-1, vcol)),
            pl.BlockSpec((lb, kw), kidx(0, vcol)),
            pl.BlockSpec((lb, kw), kidx(1, vcol)),
            pl.BlockSpec((ctx_len, kw), xidx(vcol)),
        ],
        out_specs=pl.BlockSpec((lb, qw), qidx),
        out_shape=jax.ShapeDtypeStruct((rows.n_all, qw), BF16),
        compiler_params=_params(("parallel", "parallel")),
        name="swa_attention",
    )(sink, qkv, qkv, qkv, qkv, qkv, qkv, qkv, qkv, qkv)


def _mlstm_kernel(*refs):
    ins, (of_ref, ob_ref, c_ref, m_ref) = refs[:14], refs[14:]
    c = pl.program_id(1)
    L, H, dqk, dv = ML_CHUNK, ML_HEADS, ML_DQK, ML_DV
    scale = dqk ** -0.5

    @pl.when(c == 0)
    def _():
        c_ref[...] = jnp.zeros_like(c_ref)
        m_ref[...] = jnp.zeros_like(m_ref)

    row = lax.broadcasted_iota(jnp.int32, (L, L), 0)
    col = lax.broadcasted_iota(jnp.int32, (L, L), 1)
    ones = jnp.ones((L, LANE), BF16)
    for d, o_ref in enumerate((of_ref, ob_ref)):
        q_ref, k_ref, v_ref, gi_ref, gf_ref, bi_ref, bf_ref = ins[7 * d:7 * d + 7]
        mask = (col <= row) if d == 0 else (col >= row)
        ones_mask = jnp.where(mask, 1.0, 0.0).astype(BF16)
        i_blk = gi_ref[...] + bi_ref[...]
        f_blk = jax.nn.log_sigmoid(gf_ref[...] + bf_ref[...])
        f_hi = f_blk.astype(BF16)
        r1 = f_blk - f_hi.astype(F32)
        f_mid = r1.astype(BF16)
        f_lo = (r1 - f_mid.astype(F32)).astype(BF16)
        b_blk = _dot(jnp.concatenate([ones_mask] * 3, axis=1), jnp.concatenate([f_hi, f_mid, f_lo], axis=0))
        b_end = jnp.sum(f_blk, axis=0, keepdims=True)
        e_rows = (i_blk - b_blk).T
        m_prev_blk = m_ref[d]
        dec = b_end - b_blk + i_blk
        m_new_blk = jnp.maximum(b_end + m_prev_blk, jnp.max(dec, axis=0, keepdims=True))
        ws_blk = jnp.exp(dec - m_new_blk)
        gs_blk = jnp.exp(b_end + m_prev_blk - m_new_blk)
        for h in range(H):
            qh = q_ref[:, h * dqk:(h + 1) * dqk]
            kh = k_ref[:, h * dqk:(h + 1) * dqk]
            v_aug = jnp.concatenate([v_ref[:, h * dv:(h + 1) * dv], ones], axis=1)
            e = jnp.where(mask, e_rows[h:h + 1, :], -jnp.inf)
            m_prev = m_prev_blk[:, h:h + 1]
            mm = jnp.maximum(m_prev, jnp.max(e, axis=1, keepdims=True))
            s = (_dot_nt(qh, kh) * jnp.exp(e - (mm - math.log(scale)))).astype(BF16)
            g = jnp.exp(m_prev - mm) * scale
            ct = c_ref[d, h]
            lhs = jnp.concatenate([s, (qh.astype(F32) * g).astype(BF16)], axis=1)
            res = _dot(lhs, jnp.concatenate([v_aug, ct.astype(BF16)], axis=0))
            floor = jnp.exp(-(b_blk[:, h:h + 1] + mm))
            hout = res[:, :dv] / jnp.maximum(jnp.abs(res[:, dv:dv + 1]), floor)
            o_ref[:, h * dv:(h + 1) * dv] = hout.astype(o_ref.dtype)
            kw = (kh.astype(F32) * ws_blk[:, h:h + 1]).astype(BF16)
            c_ref[d, h] = gs_blk[:, h:h + 1] * ct + _dot_tn(kw, v_aug)
        m_ref[d] = m_new_blk


def mlstm_scan(z, gates, gate_b, *, rows):
    b, seq, ctx_len = rows.batch, rows.seq, rows.ctx_len
    L = ML_CHUNK
    assert ctx_len == L and seq % L == 0
    nlc = seq // L
    hq, hv = ML_HEADS * ML_DQK, ML_HEADS * ML_DV

    def rb(d):
        def index(bi, c):
            lat = bi * nlc + (c - 1 if d == 0 else nlc - c)
            return jnp.where(c == 0, rows.n_lat // L + bi, lat)
        return index

    in_specs, args = [], []
    for d in range(2):
        r = rb(d)
        in_specs += [
            pl.BlockSpec((L, hq), lambda bi, c, r=r: (r(bi, c), 0)),
            pl.BlockSpec((L, hq), lambda bi, c, r=r: (r(bi, c), 1)),
            pl.BlockSpec((L, hv), lambda bi, c, r=r: (r(bi, c), 1)),
            pl.BlockSpec((L, LANE), lambda bi, c, r=r, d=d: (r(bi, c), 2 * d)),
            pl.BlockSpec((L, LANE), lambda bi, c, r=r, d=d: (r(bi, c), 2 * d + 1)),
            pl.BlockSpec((1, LANE), lambda bi, c, d=d: (0, 2 * d)),
            pl.BlockSpec((1, LANE), lambda bi, c, d=d: (0, 2 * d + 1)),
        ]
        args += [z, z, z, gates, gates, gate_b, gate_b]
    return pl.pallas_call(
        _mlstm_kernel,
        grid=(b, nlc + 1),
        in_specs=in_specs,
        out_specs=[pl.BlockSpec((L, hv), lambda bi, c, r=rb(d): (r(bi, c), 0)) for d in range(2)],
        out_shape=[jax.ShapeDtypeStruct((rows.n_all, hv), BF16)] * 2,
        scratch_shapes=[
            pltpu.VMEM((2, ML_HEADS, ML_DQK, ML_DV + LANE), F32),
            pltpu.VMEM((2, 1, LANE), F32),
        ],
        compiler_params=_params(("parallel", "arbitrary")),
        name="mlstm_scan",
    )(*args)


def _mla_kernel(q_ref, kvl_ref, kvx_ref, krl_ref, krx_ref, o_ref, kk_ref, vt_ref, *, seq, sub, ck):
    @pl.when(pl.program_id(2) == 0)
    def _():
        kk_ref[:seq, :LANE] = kvl_ref[:, :LANE]
        kk_ref[:seq, LANE:] = krl_ref[...]
        kk_ref[seq:, :LANE] = kvx_ref[:, :LANE]
        kk_ref[seq:, LANE:] = krx_ref[...]
        vt_ref[:LANE, :seq] = kvl_ref[:, LANE:].astype(F32).T.astype(BF16)
        vt_ref[:LANE, seq:] = kvx_ref[:, LANE:].astype(F32).T.astype(BF16)
        vt_ref[LANE:, :] = jnp.ones((vt_ref.shape[0] - LANE, vt_ref.shape[1]), BF16)

    nsub = q_ref.shape[0] // sub
    n_keys = kk_ref.shape[0]
    bounds = [(k0, min(k0 + ck, n_keys)) for k0 in range(0, n_keys, ck)]
    nck = len(bounds)
    qs = [q_ref[s * sub:(s + 1) * sub, :] for s in range(nsub)]

    def scores(s, c):
        return _dot_nt(kk_ref[bounds[c][0]:bounds[c][1], :], qs[s])

    m = [None] * nsub
    acc = [None] * nsub
    st = [None] * nsub

    def absorb(s, c):
        cm = jnp.max(st[s], axis=0, keepdims=True)
        m_new = cm if c == 0 else jnp.maximum(m[s], cm)
        pv = _dot(vt_ref[:, bounds[c][0]:bounds[c][1]], jnp.exp2(st[s] - m_new).astype(BF16))
        acc[s] = pv if c == 0 else acc[s] * jnp.exp2(m[s] - m_new) + pv
        m[s] = m_new

    for k in range(-1, nck + nsub - 1):
        nxt = [scores(s, k - s + 1) if 0 <= k - s + 1 < nck else None for s in range(nsub)]
        for s in range(nsub):
            if 0 <= k - s < nck:
                absorb(s, k - s)
        for s in range(nsub):
            if nxt[s] is not None:
                st[s] = nxt[s]
    for s in range(nsub):
        o = acc[s][:LANE] / acc[s][LANE:LANE + 1]
        o_ref[s * sub:(s + 1) * sub, :] = o.T.astype(o_ref.dtype)


def mla_attention(q, kv, z, *, rows, tq=2048, sub=256, ck=1024):
    b, seq, ctx_len = rows.batch, rows.seq, rows.ctx_len
    tq = min(tq, seq)
    nq = seq // tq
    kr_col = z.shape[1] // LANE - 1
    ctx0 = rows.n_lat // ctx_len
    n_keys = seq + ctx_len
    ones_rows = 16
    return pl.pallas_call(
        functools.partial(_mla_kernel, seq=seq, sub=sub, ck=ck),
        grid=(b, MLA_HEADS, nq),
        in_specs=[
            pl.BlockSpec((tq, 2 * LANE), lambda bi, h, i: (bi * nq + i, h)),
            pl.BlockSpec((seq, 2 * LANE), lambda bi, h, i: (bi, h)),
            pl.BlockSpec((ctx_len, 2 * LANE), lambda bi, h, i: (ctx0 + bi, h)),
            pl.BlockSpec((seq, LANE), lambda bi, h, i: (bi, kr_col)),
            pl.BlockSpec((ctx_len, LANE), lambda bi, h, i: (ctx0 + bi, kr_col)),
        ],
        out_specs=pl.BlockSpec((tq, LANE), lambda bi, h, i: (bi * nq + i, h)),
        out_shape=jax.ShapeDtypeStruct((rows.n_lat, MLA_HEADS * MLA_V), BF16),
        scratch_shapes=[pltpu.VMEM((n_keys, 2 * LANE), BF16), pltpu.VMEM((LANE + ones_rows, n_keys), BF16)],
        compiler_params=_params(("parallel", "parallel", "arbitrary")),
        name="mla_attention",
    )(q, kv, kv, z, z)


def _hy_filter_kernel(ft_ref, w1_ref, b1_ref, w2_ref, b2_ref, w3_ref, b3_ref, fr_ref, fr3_ref, w4_ref, b4_ref,
                      dec_ref, o_ref, a_ref, *, back_from):
    j = pl.program_id(0)
    fw = HY_FILTER_W
    hdot = functools.partial(jnp.dot, preferred_element_type=F32, precision=HIGHEST)

    @pl.when(j == 0)
    def _():
        fr = fr_ref[...]
        a = jnp.sin(fr * (hdot(ft_ref[...], w1_ref[...]) + b1_ref[...]))
        a = jnp.sin(fr * (hdot(a, w2_ref[...]) + b2_ref[...]))
        a3 = jnp.sin(fr3_ref[...] * (hdot(a, w3_ref[...]) + b3_ref[...]))
        hi = a3.astype(BF16)
        lo = (a3 - hi.astype(F32)).astype(BF16)
        lane = lax.broadcasted_iota(jnp.int32, a3.shape, 1)
        a_ref[...] = jnp.where((lane >= fw) & (lane < 2 * fw), lo, hi)

    w4 = w4_ref[...]
    w_hi = w4.astype(BF16)
    w_lo = (w4 - w_hi.astype(F32)).astype(BF16)
    rhs = jnp.concatenate([w_hi, w_hi, w_lo, jnp.zeros_like(w_hi)], axis=0)
    t = ft_ref[:, 0:1]
    filt = (_dot(a_ref[...], rhs) + b4_ref[...]) * jnp.exp(-t * jnp.abs(dec_ref[...]))
    row = lax.broadcasted_iota(jnp.int32, filt.shape, 0)
    o_ref[...] = jnp.where(jnp.logical_and(row == 0, j >= back_from), 0.0, filt).astype(o_ref.dtype)


def hyena_filters(seq, w1, b1, w2, b2, w3, b3, w4, b4, freq, decay, tn=512):
    t = np.linspace(0.0, 1.0, seq)[:, None]
    w = (2.0 * math.pi / seq) * np.arange(seq)[:, None]
    bands = np.linspace(1e-4, HY_BANDS - 1, HY_BANDS)[None, :]
    feats = np.zeros((seq, LANE))
    feats[:, :1 + 2 * HY_BANDS] = np.concatenate([t, np.cos(bands * w), -np.sin(bands * w)], axis=-1)
    n = w4.shape[1]
    fw = HY_FILTER_W
    w1p = jnp.zeros((LANE, fw), F32).at[:w1.shape[0]].set(w1)
    row = lambda a: a.reshape(1, -1)
    tile3 = lambda a: jnp.concatenate([a, a, a, jnp.zeros_like(a)], axis=-1)
    full = lambda shape: pl.BlockSpec(shape, lambda j: (0, 0))
    return pl.pallas_call(
        functools.partial(_hy_filter_kernel, back_from=(n // 2) // tn),
        grid=(n // tn,),
        in_specs=[full((seq, LANE)), full((LANE, fw)), full((1, fw)), full((fw, fw)), full((1, fw)),
                  full((fw, 4 * fw)), full((1, 4 * fw)), full((1, fw)), full((1, 4 * fw)),
                  pl.BlockSpec((fw, tn), lambda j: (0, j)),
                  pl.BlockSpec((1, tn), lambda j: (0, j)),
                  pl.BlockSpec((1, tn), lambda j: (0, j))],
        out_specs=pl.BlockSpec((seq, tn), lambda j: (0, j)),
        out_shape=jax.ShapeDtypeStruct((seq, n), BF16),
        scratch_shapes=[pltpu.VMEM((seq, 4 * fw), BF16)],
        compiler_params=_params(("arbitrary",)),
        name="hyena_filters",
    )(jnp.asarray(feats, F32), w1p, row(b1), w2, row(b2), tile3(w3), tile3(row(b3)), row(freq), tile3(row(freq)),
      w4, row(b4), row(decay))


def _dft_mats(seq, p):
    n = 2 * seq
    q = n // p
    k1n = q // 2 + 1
    hi, k1 = np.arange(q // 2), np.arange(k1n)
    th = 2 * np.pi * np.outer(k1, hi) / q
    ma = np.zeros((2 * k1n, q // 2))
    ma[0::2], ma[1::2] = np.cos(th), -np.sin(th)
    lo = np.arange(p)
    wf = np.zeros((k1n, 2 * p, 2 * p))
    wi = np.zeros((k1n, 2 * p, 2 * p))
    for k in k1:
        ph = -2 * np.pi * (np.outer(lo, lo) / p + k * lo[None, :] / n)
        er, ei = np.cos(ph), np.sin(ph)
        wf[k] = np.block([[er, -ei], [ei, er]])
        wi[k] = np.block([[er.T, ei.T], [-ei.T, er.T]])
    c = np.full(k1n, 2.0)
    c[0] = c[-1] = 1.0
    th2 = 2 * np.pi * np.outer(hi, k1) / q
    md = np.zeros((q // 2, 2 * k1n))
    md[:, 0::2], md[:, 1::2] = c * np.cos(th2) / n, -c * np.sin(th2) / n
    as_bf16 = lambda a: jnp.asarray(a, F32).astype(BF16)
    return ma, as_bf16(wf), as_bf16(wi), md


def _hy_short_kernel(x1_ref, x2_ref, v_ref, w_ref, b_ref, o1_ref, o2_ref, o3_ref):
    for k, (x_ref, o_ref) in enumerate(((x1_ref, o1_ref), (x2_ref, o2_ref), (v_ref, o3_ref))):
        x = x_ref[...].astype(F32)
        n = x.shape[0]
        row = lax.broadcasted_iota(jnp.int32, x.shape, 0)
        prev = jnp.where(row == 0, 0.0, pltpu.roll(x, 1, 0))
        nxt = jnp.where(row == n - 1, 0.0, pltpu.roll(x, n - 1, 0))
        w = w_ref[k]
        o_ref[...] = (prev * w[0:1] + x * w[1:2] + nxt * w[2:3] + b_ref[k]).astype(o_ref.dtype)


def hyena_short_conv(x3, conv_w, conv_b, *, rows, cb=256):
    b, seq = rows.batch, rows.seq
    d = x3.shape[1] // 3
    nblk = d // cb
    w = jnp.transpose(conv_w.reshape(HY_SHORT, 3, d), (1, 0, 2))
    bias = conv_b.reshape(3, 1, d)
    in_x = [pl.BlockSpec((seq, cb), lambda bi, j, k=k: (bi, k * nblk + j)) for k in range(3)]
    out = pl.BlockSpec((seq, cb), lambda bi, j: (bi, j))
    shape = jax.ShapeDtypeStruct((rows.n_lat, d), BF16)
    return pl.pallas_call(
        _hy_short_kernel,
        grid=(b, nblk),
        in_specs=in_x + [pl.BlockSpec((3, HY_SHORT, cb), lambda bi, j: (0, 0, j)),
                         pl.BlockSpec((3, 1, cb), lambda bi, j: (0, 0, j))],
        out_specs=[out, out, out],
        out_shape=[shape, shape, shape],
        compiler_params=_params(("parallel", "parallel")),
        name="hyena_short_conv",
    )(x3, x3, x3, w, bias)


def _hy_stage_a_kernel(x_ref, m_ref, o_ref):
    qh, g, wc = x_ref.shape
    x = x_ref[...].reshape(qh * g, wc).astype(BF16)
    o_ref[...] = _dot(m_ref[...], x).reshape(o_ref.shape).astype(o_ref.dtype)


def hyena_stage_a(x, mak, *, p, width=2048):
    bx, seq, c = x.shape
    g = HY_G
    qh = seq // p
    mo = mak.shape[0] // g
    x5 = x.reshape(bx, qh, p // g, g, c)
    return pl.pallas_call(
        _hy_stage_a_kernel,
        grid=(bx, p // g, c // width),
        in_specs=[pl.BlockSpec((None, qh, None, g, width), lambda bi, l, j: (bi, 0, l, 0, j)),
                  pl.BlockSpec(mak.shape, lambda bi, l, j: (0, 0))],
        out_specs=pl.BlockSpec((None, mo, g, width), lambda bi, l, j: (bi, 0, l, j)),
        out_shape=jax.ShapeDtypeStruct((bx, mo, p, c), BF16),
        compiler_params=_params(("parallel", "parallel", "parallel")),
        name="hyena_stage_a",
    )(x5, mak)


def _hy_spec_kernel(a0_ref, a1_ref, wf_ref, o_ref):
    p = o_ref.shape[2]
    cb = o_ref.shape[3]
    wf = wf_ref[0]
    x0 = _dot(wf, a0_ref[0].reshape(2 * p, cb))
    x1 = _dot(wf, a1_ref[0].reshape(2 * p, cb))
    o_ref[0, 0] = x0[:p] + x1[:p]
    o_ref[0, 1] = x0[p:] - x1[p:]


def hyena_spectrum(af, wf, *, d, cb=2048):
    k1n, _, p, _ = af.shape
    nblk = (HY_ORDER * d) // cb
    return pl.pallas_call(
        _hy_spec_kernel,
        grid=(k1n, nblk),
        in_specs=[pl.BlockSpec((1, 2, p, cb), lambda k, j: (k, 0, 0, j)),
                  pl.BlockSpec((1, 2, p, cb), lambda k, j: (k, 0, 0, nblk + j)),
                  pl.BlockSpec((1, 2 * p, 2 * p), lambda k, j: (k, 0, 0))],
        out_specs=pl.BlockSpec((1, 2, p, cb), lambda k, j: (k, 0, 0, j)),
        out_shape=jax.ShapeDtypeStruct((k1n, 2, p, HY_ORDER * d), F32),
        compiler_params=_params(("parallel", "parallel")),
        name="hyena_spectrum",
    )(af, af, wf)


def _hy_mid_kernel(a_ref, wf_ref, wi_ref, h_ref, o_ref):
    b, _, p, cb = a_ref.shape
    hr, hi = h_ref[0], h_ref[1]
    for bi in range(b):
        x = _dot(wf_ref[...], a_ref[bi].reshape(2 * p, cb))
        xr, xi = x[:p], x[p:]
        y = jnp.concatenate([xr * hr - xi * hi, xr * hi + xi * hr], axis=0).astype(BF16)
        o_ref[bi] = _dot(wi_ref[...], y).reshape(2, p, cb).astype(o_ref.dtype)


def hyena_mid(a, wf, wi, spec, *, order, d, cb=1024):
    b, k1n, _, p, _ = a.shape
    nblk = d // cb
    data = pl.BlockSpec((b, None, 2, p, cb), lambda k, j: (0, k, 0, 0, j))
    mat = pl.BlockSpec((None, 2 * p, 2 * p), lambda k, j: (k, 0, 0))
    return pl.pallas_call(
        _hy_mid_kernel,
        grid=(k1n, nblk),
        in_specs=[data, mat, mat,
                  pl.BlockSpec((None, 2, p, cb), lambda k, j: (k, 0, 0, order * nblk + j))],
        out_specs=data,
        out_shape=jax.ShapeDtypeStruct(a.shape, BF16),
        compiler_params=_params(("parallel", "parallel")),
        name="hyena_mid",
    )(a, wf, wi, spec)


def _hy_stage_d_kernel(c_ref, m_ref, x_ref, z_ref, skip_ref, *rest, fused):
    ma_ref, o_ref, a_ref = rest if fused else (None, rest[0], None)
    mo, g, wc = c_ref.shape
    qh = x_ref.shape[0]
    conv = _dot(m_ref[...], c_ref[...].reshape(mo * g, wc))
    z = z_ref[...].reshape(qh * g, wc).astype(F32)
    x = x_ref[...].reshape(qh * g, wc).astype(F32)
    y = (x * (conv + z * skip_ref[...])).astype(o_ref.dtype)
    o_ref[...] = y.reshape(o_ref.shape)
    if fused:
        a_ref[...] = _dot(ma_ref[...], y).reshape(a_ref.shape).astype(a_ref.dtype)


def hyena_stage_d(c, mdk, xg, z, skip, *, p, mak=None, width=2048):
    b, mo, _, d = c.shape
    g = HY_G
    seq = xg.shape[1]
    qh = seq // p
    fused = mak is not None
    v5 = lambda a: a.reshape(b, qh, p // g, g, d)
    tspec = pl.BlockSpec((None, qh, None, g, width), lambda bi, l, j: (bi, 0, l, 0, j))
    fspec = pl.BlockSpec((None, mo, g, width), lambda bi, l, j: (bi, 0, l, j))
    in_specs = [fspec, pl.BlockSpec(mdk.shape, lambda bi, l, j: (0, 0)), tspec, tspec,
                pl.BlockSpec((1, width), lambda bi, l, j: (0, j))]
    args = [c, mdk, v5(xg), v5(z), skip.reshape(1, d)]
    out_specs, out_shape = [tspec], [jax.ShapeDtypeStruct((b, qh, p // g, g, d), BF16)]
    if fused:
        in_specs.append(pl.BlockSpec(mak.shape, lambda bi, l, j: (0, 0)))
        args.append(mak)
        out_specs.append(fspec)
        out_shape.append(jax.ShapeDtypeStruct(c.shape, BF16))
    outs = pl.pallas_call(
        functools.partial(_hy_stage_d_kernel, fused=fused),
        grid=(b, p // g, d // width),
        in_specs=in_specs,
        out_specs=out_specs,
        out_shape=out_shape,
        compiler_params=_params(("parallel", "parallel", "parallel")),
        name="hyena_stage_d",
    )(*args)
    y = outs[0].reshape(b, seq, d)
    return (y, outs[1]) if fused else y


def hyena_long_convs(x1, x2, v, filt, skip, *, rows):
    b, seq = rows.batch, rows.seq
    d = x1.shape[1]
    p = HY_P
    k1n = seq // p + 1
    ma, wf, wi, md = _dft_mats(seq, p)
    eye = np.eye(HY_G)
    as_bf16 = lambda a: jnp.asarray(a, F32).astype(BF16)
    mak, mdk = as_bf16(np.kron(ma, eye)), as_bf16(np.kron(md, eye))
    af = hyena_stage_a(filt.reshape(1, seq, filt.shape[1]), mak, p=p)
    spec = hyena_spectrum(af.reshape(k1n, 2, p, filt.shape[1]), wf, d=d)
    t3 = lambda a: a.reshape(b, seq, d)
    y = t3(v)
    a = hyena_stage_a(y, mak, p=p)
    c = hyena_mid(a.reshape(b, k1n, 2, p, d), wf, wi, spec, order=0, d=d)
    y, a = hyena_stage_d(c.reshape(a.shape), mdk, t3(x1), y, skip[0], p=p, mak=mak)
    c = hyena_mid(a.reshape(b, k1n, 2, p, d), wf, wi, spec, order=1, d=d)
    y = hyena_stage_d(c.reshape(a.shape), mdk, t3(x2), y, skip[1], p=p)
    return y.reshape(rows.n_lat, d)


def _mlstm_gate_weights(w_in, gate_b):
    h = ML_HEADS
    o4 = 2 * h * ML_DQK + 2 * h * ML_DV
    wg = w_in[:, o4:].reshape(-1, 2, 2, h)
    bg = gate_b.reshape(2, 2, h)
    w_out = jnp.zeros((w_in.shape[0], 4 * LANE), F32)
    b_out = jnp.zeros((1, 4 * LANE), F32)
    for d in range(2):
        for gate in range(2):
            lo = (2 * d + gate) * LANE
            w_out = w_out.at[:, lo:lo + h].set(wg[:, gate, d])
            b_out = b_out.at[0, lo:lo + h].set(bg[gate, d])
    return w_out, b_out


def _mla_weights(w_in, w_uq):
    r2 = MLA_Q_RANK + MLA_KV_RANK
    half = MLA_ROPE // 2

    def spread(w):
        z = jnp.zeros(w.shape[:-1] + (half,), w.dtype)
        return jnp.concatenate([w[..., :half], z, w[..., half:], z], axis=-1)

    w_in_p = jnp.concatenate([w_in[:, :r2], spread(w_in[:, r2:])], axis=1)
    wq = w_uq.reshape(w_uq.shape[0], MLA_HEADS, MLA_NOPE + MLA_ROPE)
    wq_p = jnp.concatenate([wq[..., :MLA_NOPE], spread(wq[..., MLA_NOPE:])], axis=-1)
    wq_p = wq_p * ((MLA_NOPE + MLA_ROPE) ** -0.5 * math.log2(math.e))
    return w_in_p, wq_p.reshape(w_uq.shape[0], MLA_HEADS * 2 * LANE)


def kernel(x, c, ctx, c_ctx, ada_w, ada_b, norm_g, mlp_w1, mlp_w2, swa_w_qkv, swa_sink, swa_w_o, ml_w_in, ml_gate_b, ml_head_g, ml_w_o, mla_w_in, mla_q_g, mla_kv_g, mla_w_uq, mla_w_ukv, mla_w_o, hy_w_in, hy_conv_w, hy_conv_b, hy_f_w1, hy_f_b1, hy_f_w2, hy_f_b2, hy_f_w3, hy_f_b3, hy_f_w4, hy_f_b4, hy_f_freq, hy_decay, hy_skip, hy_w_o):
    b, seq, d = x.shape
    ctx_len = ctx.shape[1]
    depth = ada_w.shape[0]
    assert depth == 4 and b < 8
    rows = Rows(b, seq, ctx_len)
    tm = 512
    tm_lin = 1024 if rows.n_lat % 1024 == 0 and rows.n_ctx % 1024 == 0 else 512
    bf = lambda w: w.astype(BF16)
    mlp_w1b, mlp_w2b = bf(mlp_w1), bf(mlp_w2)

    cond = jnp.zeros((8, d), F32).at[:b].set(c).at[b].set(c_ctx)
    mods = ada_mods(cond, ada_w, ada_b).reshape(depth * 8, 1, N_MOD * d)

    common = dict(rows=rows, tm=tm_lin, mods=mods)

    def finish(h, o_args, w_o, layer, n_rows, mode="plain", h2=None):
        h = outproj_residual(o_args, bf(w_o), h, mods, norm_g[layer, 1], rows=rows, n_rows=n_rows, tm=tm,
                             layer=layer, mode=mode, h2=h2, name=f"outproj{layer}")
        return mlp_residual(h, mlp_w1b, mlp_w2b, mods, norm_g[layer, 2], norm_g[layer, 3],
                            rows=rows, n_rows=n_rows, tm=tm, tf=1024, layer=layer)

    n_qk_groups = SWA_HEADS + SWA_KV_HEADS
    q_cols = SWA_HEADS * SWA_HEAD_DIM
    q_scale = SWA_HEAD_DIM ** -0.5 * math.log2(math.e)
    w_qkv = jnp.concatenate([swa_w_qkv[0][:, :q_cols] * q_scale, swa_w_qkv[0][:, q_cols:]], axis=1)
    x_rows, ctx_rows = x.reshape(rows.n_lat, d), ctx.reshape(rows.n_ctx, d)
    qkv = linear(x_rows, bf(w_qkv), x2=ctx_rows, n_rows=rows.n_all, tn=512, out_dtype=BF16, prologue="norm_mod",
                 gain=norm_g[0, 0], layer=0, rope_tabs=_rope_tables(seq, SWA_HEAD_DIM), rope_pattern=(True,) * 4,
                 rope_jmax=n_qk_groups // 4, name="swa_qkv", **common)
    o = swa_attention(qkv, swa_sink[0], rows=rows)
    h = finish(x_rows, o, swa_w_o[0], 0, rows.n_all, h2=ctx_rows)

    w_gate, b_gate = _mlstm_gate_weights(ml_w_in[0], ml_gate_b[0])
    z = linear(h, bf(ml_w_in[0]), n_rows=rows.n_all, tn=1536, out_dtype=BF16, prologue="norm_mod",
               gain=norm_g[1, 0], layer=1, n_out=2 * ML_HEADS * (ML_DQK + ML_DV), name="mlstm_in", **common)
    gates = linear(h, bf(w_gate), n_rows=rows.n_all, tn=4 * LANE, out_dtype=F32, prologue="norm_mod",
                   gain=norm_g[1, 0], layer=1, name="mlstm_gates", **common)
    h_fwd, h_bwd = mlstm_scan(z, gates, b_gate, rows=rows)
    h = finish(h, (h_fwd, h_bwd, z, ml_head_g[0]), ml_w_o[0], 1, rows.n_all, mode="mlstm")

    w_in_p, w_uq_p = _mla_weights(mla_w_in[0], mla_w_uq[0])
    rope_mla = _rope_tables(seq, MLA_ROPE)
    zc = linear(h, bf(w_in_p), n_rows=rows.n_all, tn=w_in_p.shape[1], out_dtype=BF16, prologue="norm_mod",
                gain=norm_g[2, 0], layer=2, rope_tabs=rope_mla, rope_pattern=(False,) * 8 + (True,),
                name="mla_in", **common)
    q = linear(zc, bf(w_uq_p), n_rows=rows.n_lat, tn=2048, out_dtype=BF16, x_cols=0, prologue="norm",
               gain=mla_q_g[0], rope_tabs=rope_mla, rope_pattern=(False, True) * 8, name="mla_q",
               rows=rows, tm=tm_lin)
    kv = linear(zc, bf(mla_w_ukv[0]), n_rows=rows.n_all, tn=2048, out_dtype=BF16, x_cols=1, prologue="norm",
                gain=mla_kv_g[0], name="mla_kv", rows=rows, tm=tm_lin)
    o = mla_attention(q, kv, zc, rows=rows)
    h = finish(h, o, mla_w_o[0], 2, rows.n_lat)

    x3 = linear(h, bf(hy_w_in[0]), n_rows=rows.n_lat, tn=1536, out_dtype=BF16, prologue="norm_mod",
                gain=norm_g[3, 0], layer=3, name="hyena_in", **common)
    x1, x2, v = hyena_short_conv(x3, hy_conv_w[0], hy_conv_b[0], rows=rows)
    filt = hyena_filters(seq, hy_f_w1[0], hy_f_b1[0], hy_f_w2[0], hy_f_b2[0], hy_f_w3[0], hy_f_b3[0],
                         hy_f_w4[0], hy_f_b4[0], hy_f_freq[0], hy_decay[0])
    y = hyena_long_convs(x1, x2, v, filt, hy_skip[0], rows=rows)
    h = finish(h, y, hy_w_o[0], 3, rows.n_lat)
    return h.reshape(b, seq, d)
```

```python
import functools
import math

import numpy as np
import jax
import jax.numpy as jnp
from jax import lax
from jax.experimental import pallas as pl
from jax.experimental.pallas import tpu as pltpu

F32, BF16 = jnp.float32, jnp.bfloat16
HIGHEST = lax.Precision.HIGHEST

RMS_EPS = 1e-6
ROPE_THETA = 10000.0
GRID_W = 64
N_MOD = 6
LANE = 128

SWA_HEADS, SWA_KV_HEADS, SWA_HEAD_DIM, SWA_WINDOW, SWA_BLOCK = 16, 4, 128, 128, 128
ML_HEADS, ML_DQK, ML_DV = 8, 128, 256
ML_CHUNK = 256
MLA_HEADS, MLA_Q_RANK, MLA_KV_RANK, MLA_NOPE, MLA_ROPE, MLA_V = 16, 512, 512, 128, 64, 128
HY_ORDER, HY_BANDS, HY_FILTER_W, HY_SHORT = 2, 16, 64, 3
HY_P = 256
HY_G = 16

VMEM_LIMIT = 48 * 1024 * 1024


def _params(sem):
    return pltpu.CompilerParams(dimension_semantics=sem, vmem_limit_bytes=VMEM_LIMIT)


def _dot(a, b):
    return jnp.dot(a, b, preferred_element_type=F32)


def _dot_nt(a, b):
    return lax.dot_general(a, b, (((1,), (1,)), ((), ())), preferred_element_type=F32)


def _dot_tn(a, b):
    return lax.dot_general(a, b, (((0,), (0,)), ((), ())), preferred_element_type=F32)


def _rms(x, g):
    return x * lax.rsqrt(jnp.mean(x * x, axis=-1, keepdims=True) + RMS_EPS) * g


def _rms_ref(x_ref, inv_ref, g):
    x = x_ref[...].astype(F32)
    inv_ref[...] = lax.rsqrt(jnp.mean(x * x, axis=-1, keepdims=True) + RMS_EPS)
    return x_ref[...].astype(F32) * inv_ref[...] * g


def _ada_kernel(s_ref, w_ref, b_ref, o_ref):
    s = s_ref[...]
    s = s * jax.nn.sigmoid(s)
    hi = s.astype(BF16)
    lo = (s - hi.astype(F32)).astype(BF16)
    w = w_ref[0]
    w_hi = w.astype(BF16)
    w_lo = (w - w_hi.astype(F32)).astype(BF16)
    r = _dot(jnp.concatenate([hi, lo], axis=0), w_hi)
    n = s.shape[0]
    o_ref[0] = r[:n] + r[n:] + _dot(hi, w_lo) + b_ref[0]


def ada_mods(cond, ada_w, ada_b, tn=1536):
    depth, d, n = ada_w.shape
    rows = cond.shape[0]
    return pl.pallas_call(
        _ada_kernel,
        grid=(depth, n // tn),
        in_specs=[
            pl.BlockSpec((rows, d), lambda l, j: (0, 0)),
            pl.BlockSpec((1, d, tn), lambda l, j: (l, 0, j)),
            pl.BlockSpec((1, 1, tn), lambda l, j: (l, 0, j)),
        ],
        out_specs=pl.BlockSpec((1, rows, tn), lambda l, j: (l, 0, j)),
        out_shape=jax.ShapeDtypeStruct((depth, rows, n), F32),
        compiler_params=_params(("parallel", "parallel")),
        name="ada_mods",
    )(cond, ada_w, ada_b.reshape(depth, 1, n))


class Rows:
    def __init__(self, batch, seq, ctx_len):
        self.batch, self.seq, self.ctx_len = batch, seq, ctx_len
        self.n_lat = batch * seq
        self.n_ctx = batch * ctx_len
        self.n_all = self.n_lat + self.n_ctx

    def mod_index(self, layer, k, tm):
        lat_blocks, per_batch = self.n_lat // tm, self.seq // tm

        def index(i, *_):
            b = jnp.where(i < lat_blocks, i // per_batch, self.batch)
            return (layer * 8 + b, 0, k)

        return index


def _linear_kernel(*refs, prologue, rope, rope_pattern, n_lat_blocks, dual, single):
    it = iter(refs)
    x_ref = next(it)
    x2_ref = next(it) if dual else None
    g_ref = next(it) if prologue in ("norm", "norm_mod") else None
    sh_ref = next(it) if prologue == "norm_mod" else None
    sc_ref = next(it) if prologue == "norm_mod" else None
    w_ref = next(it)
    cc_ref = next(it) if rope else None
    ss_ref = next(it) if rope else None
    o_ref = next(it)
    u_ref = next(it)
    inv_ref = next(it)
    first = pl.program_id(1) == 0

    def prologue_from(src_ref):
        if prologue in ("norm", "norm_mod"):
            x = _rms_ref(src_ref, inv_ref, g_ref[...])
        else:
            x = src_ref[...].astype(F32)
        if prologue == "norm_mod":
            x = x * (1.0 + sc_ref[0]) + sh_ref[0]
        u_ref[...] = x.astype(BF16)

    if dual:
        is_lat = pl.program_id(0) < n_lat_blocks
        pl.when(jnp.logical_and(first, is_lat))(functools.partial(prologue_from, x_ref))
        pl.when(jnp.logical_and(first, jnp.logical_not(is_lat)))(functools.partial(prologue_from, x2_ref))
    elif single:
        prologue_from(x_ref)
    else:
        pl.when(first)(functools.partial(prologue_from, x_ref))

    acc = _dot(u_ref[...], w_ref[...])

    def plain():
        o_ref[...] = acc.astype(o_ref.dtype)

    def roped():
        cc, ss = cc_ref[...], ss_ref[...]
        segs = []
        for gi, on in enumerate(rope_pattern):
            seg = acc[:, gi * LANE:(gi + 1) * LANE]
            if on:
                seg = seg * cc + pltpu.roll(seg, LANE // 2, 1) * ss
            segs.append(seg)
        o_ref[...] = jnp.concatenate(segs, axis=1).astype(o_ref.dtype)

    if rope:
        roped()
    else:
        plain()


def linear(x, w, *, rows, n_rows, tm, tn, out_dtype, x_cols=None, prologue="none", gain=None,
           mods=None, layer=0, mod_k=(0, 1), rope_tabs=None, rope_pattern=(), rope_jmax=1 << 30,
           n_out=None, x2=None, name="linear"):
    k, n = w.shape
    n = n if n_out is None else n_out
    xc = 0 if x_cols is None else x_cols
    rope = rope_tabs is not None
    dual = x2 is not None
    n_lat_blocks = rows.n_lat // tm
    if dual:
        in_specs = [pl.BlockSpec((tm, k), lambda i, j: (jnp.minimum(i, n_lat_blocks - 1), 0)),
                    pl.BlockSpec((tm, k), lambda i, j: (jnp.maximum(i - n_lat_blocks, 0), 0),
                                 pipeline_mode=pl.Buffered(1))]
        args = [x, x2]
    else:
        in_specs = [pl.BlockSpec((tm, k), lambda i, j: (i, xc))]
        args = [x]
    if prologue in ("norm", "norm_mod"):
        in_specs.append(pl.BlockSpec((1, k), lambda i, j: (0, 0)))
        args.append(gain.reshape(1, k))
    if prologue == "norm_mod":
        for mk in mod_k:
            in_specs.append(pl.BlockSpec((1, 1, k), rows.mod_index(layer, mk, tm)))
            args.append(mods)
    in_specs.append(pl.BlockSpec((k, tn), lambda i, j: (0, j)))
    args.append(w)
    if rope:
        per_seq = rows.seq // tm

        def tab_index(i, j):
            rotate = jnp.logical_and(i < n_lat_blocks, j < rope_jmax)
            return (jnp.where(rotate, 0, 1), i % per_seq, 0)

        for t, fill in zip(rope_tabs, (1.0, 0.0)):
            in_specs.append(pl.BlockSpec((None, tm, LANE), tab_index))
            args.append(jnp.stack([t, jnp.full_like(t, fill)]))
    kern = functools.partial(_linear_kernel, prologue=prologue, rope=rope, rope_pattern=tuple(rope_pattern),
                             n_lat_blocks=n_lat_blocks, dual=dual, single=(n == tn))
    return pl.pallas_call(
        kern,
        grid=(n_rows // tm, n // tn),
        in_specs=in_specs,
        out_specs=pl.BlockSpec((tm, tn), lambda i, j: (i, j)),
        out_shape=jax.ShapeDtypeStruct((n_rows, n), out_dtype),
        scratch_shapes=[pltpu.VMEM((tm, k), BF16), pltpu.VMEM((tm, 1), F32)],
        compiler_params=_params(("parallel", "arbitrary")),
        name=name,
    )(*args)


def _outproj_kernel(*refs, mode, n_lat_blocks):
    it = iter(refs)
    if mode == "mlstm":
        hf_ref, hb_ref, og_ref, hg_ref = next(it), next(it), next(it), next(it)
    else:
        o_ref_in = next(it)
    w_ref, h_ref = next(it), next(it)
    h2_ref = next(it) if n_lat_blocks is not None else None
    gate_ref, g_ref, out_ref = next(it), next(it), next(it)
    y_ref, inv_ref = next(it), next(it)
    if mode == "mlstm":
        hs = hf_ref[...].astype(F32) + hb_ref[...].astype(F32)
        og = jax.nn.sigmoid(og_ref[...].astype(F32))
        hg = hg_ref[...]
        parts = []
        for h in range(ML_HEADS):
            sl = slice(h * ML_DV, (h + 1) * ML_DV)
            parts.append((_rms(hs[:, sl], hg[:, sl]) * og[:, sl]).astype(BF16))
        o = jnp.concatenate(parts, axis=1)
    else:
        o = o_ref_in[...]
    y_ref[...] = _dot(o, w_ref[...])
    upd = gate_ref[0] * _rms_ref(y_ref, inv_ref, g_ref[...])
    if h2_ref is None:
        out_ref[...] = h_ref[...] + upd
    else:
        out_ref[...] = jnp.where(pl.program_id(0) < n_lat_blocks, h_ref[...], h2_ref[...]) + upd


def outproj_residual(o_args, w, h, mods, gain, *, rows, n_rows, tm, layer, mode="plain", h2=None, name="outproj"):
    k, d = w.shape
    n_lat_blocks = rows.n_lat // tm if h2 is not None else None
    if mode == "mlstm":
        h_fwd, h_bwd, z, head_g = o_args
        in_specs = [
            pl.BlockSpec((tm, k), lambda i: (i, 0)),
            pl.BlockSpec((tm, k), lambda i: (i, 0)),
            pl.BlockSpec((tm, k), lambda i: (i, 2)),
            pl.BlockSpec((1, k), lambda i: (0, 0)),
        ]
        args = [h_fwd, h_bwd, z, head_g.reshape(1, k)]
    else:
        in_specs = [pl.BlockSpec((tm, k), lambda i: (i, 0))]
        args = [o_args]
    in_specs.append(pl.BlockSpec((k, d), lambda i: (0, 0)))
    args.append(w)
    if h2 is None:
        in_specs.append(pl.BlockSpec((tm, d), lambda i: (i, 0)))
        args.append(h)
    else:
        in_specs += [pl.BlockSpec((tm, d), lambda i: (jnp.minimum(i, n_lat_blocks - 1), 0)),
                     pl.BlockSpec((tm, d), lambda i: (jnp.maximum(i - n_lat_blocks, 0), 0))]
        args += [h, h2]
    in_specs += [
        pl.BlockSpec((1, 1, d), rows.mod_index(layer, 2, tm)),
        pl.BlockSpec((1, d), lambda i: (0, 0)),
    ]
    args += [mods, gain.reshape(1, d)]
    return pl.pallas_call(
        functools.partial(_outproj_kernel, mode=mode, n_lat_blocks=n_lat_blocks),
        grid=(n_rows // tm,),
        in_specs=in_specs,
        out_specs=pl.BlockSpec((tm, d), lambda i: (i, 0)),
        out_shape=jax.ShapeDtypeStruct((n_rows, d), F32),
        scratch_shapes=[pltpu.VMEM((tm, d), F32), pltpu.VMEM((tm, 1), F32)],
        compiler_params=_params(("parallel",)),
        name=name,
    )(*args)


def _mlp_kernel(h_ref, g2_ref, sh_ref, sc_ref, w1_ref, w2_ref, gate_ref, g3_ref, out_ref, v_ref, acc_ref,
                inv_ref):
    f = pl.program_id(1)

    @pl.when(f == 0)
    def _():
        v = _rms_ref(h_ref, inv_ref, g2_ref[...]) * (1.0 + sc_ref[0]) + sh_ref[0]
        v_ref[...] = v.astype(BF16)
        acc_ref[...] = jnp.zeros_like(acc_ref)

    a = jnp.maximum(_dot(v_ref[...], w1_ref[...]), 0.0)
    acc_ref[...] += _dot((a * a).astype(BF16), w2_ref[...])

    @pl.when(f == pl.num_programs(1) - 1)
    def _():
        out_ref[...] = h_ref[...] + gate_ref[0] * _rms_ref(acc_ref, inv_ref, g3_ref[...])


def mlp_residual(h, w1, w2, mods, g2, g3, *, rows, n_rows, tm, tf, layer):
    _, d, ff = w1.shape
    return pl.pallas_call(
        _mlp_kernel,
        grid=(n_rows // tm, ff // tf),
        in_specs=[
            pl.BlockSpec((tm, d), lambda i, f: (i, 0)),
            pl.BlockSpec((1, d), lambda i, f: (0, 0)),
            pl.BlockSpec((1, 1, d), rows.mod_index(layer, 3, tm)),
            pl.BlockSpec((1, 1, d), rows.mod_index(layer, 4, tm)),
            pl.BlockSpec((None, d, tf), lambda i, f: (layer, 0, f)),
            pl.BlockSpec((None, tf, d), lambda i, f: (layer, f, 0)),
            pl.BlockSpec((1, 1, d), rows.mod_index(layer, 5, tm)),
            pl.BlockSpec((1, d), lambda i, f: (0, 0)),
        ],
        out_specs=pl.BlockSpec((tm, d), lambda i, f: (i, 0)),
        out_shape=jax.ShapeDtypeStruct((n_rows, d), F32),
        scratch_shapes=[pltpu.VMEM((tm, d), BF16), pltpu.VMEM((tm, d), F32), pltpu.VMEM((tm, 1), F32)],
        compiler_params=_params(("parallel", "arbitrary")),
        name="mlp",
    )(h, g2.reshape(1, d), mods, mods, w1, w2, mods, g3.reshape(1, d))


def _rope_tables(seq, d_rot):
    pos = np.arange(seq)
    row, col = pos // GRID_W, pos % GRID_W
    n = d_rot // 4
    inv = ROPE_THETA ** (-np.arange(n, dtype=np.float64) / n)
    ang = np.concatenate([row[:, None] * inv, col[:, None] * inv], axis=-1)
    cos, sin = np.cos(ang), np.sin(ang)
    half = d_rot // 2
    cc = np.zeros((seq, LANE))
    ss = np.zeros((seq, LANE))
    cc[:, :half] = cos
    cc[:, LANE // 2:LANE // 2 + half] = cos
    ss[:, :half] = -sin
    ss[:, LANE // 2:LANE // 2 + half] = sin
    return jnp.asarray(cc, F32), jnp.asarray(ss, F32)


def _swa_kernel(sink_ref, q_ref, kp_ref, kc_ref, kn_ref, kx_ref, vp_ref, vc_ref, vn_ref, vx_ref, o_ref,
                *, nb, seq, ctx_len):
    n = pl.program_id(1)
    hd, grp, lb = SWA_HEAD_DIM, SWA_HEADS // SWA_KV_HEADS, SWA_BLOCK
    m_rows, n_loc = grp * lb, 3 * lb
    r = lax.broadcasted_iota(jnp.int32, (m_rows, n_loc + ctx_len), 0)
    c = lax.broadcasted_iota(jnp.int32, (m_rows, n_loc + ctx_len), 1)
    qpos = n * lb + (r & (lb - 1))
    kpos = (n - 1) * lb + c
    local_ok = (jnp.abs(kpos - qpos) <= SWA_WINDOW) & (kpos >= 0) & (kpos < seq) & (n < nb)
    valid = local_ok | (c >= n_loc)
    ones = jnp.ones((n_loc + ctx_len, hd), BF16)
    def scores(kv):
        ks = slice(kv * hd, (kv + 1) * hd)
        qg = jnp.concatenate([q_ref[:, (kv * grp + g) * hd:(kv * grp + g + 1) * hd] for g in range(grp)], axis=0)
        keys = jnp.concatenate([kp_ref[:, ks], kc_ref[:, ks], kn_ref[:, ks], kx_ref[:, ks]], axis=0)
        return jnp.where(valid, _dot_nt(qg, keys), -1e30)

    outs = []
    s_next = scores(0)
    for kv in range(SWA_KV_HEADS):
        s = s_next
        if kv + 1 < SWA_KV_HEADS:
            s_next = scores(kv + 1)
        ks = slice(kv * hd, (kv + 1) * hd)
        vals = jnp.concatenate([vp_ref[:, ks], vc_ref[:, ks], vn_ref[:, ks], vx_ref[:, ks]], axis=0)
        v_aug = jnp.concatenate([vals, ones], axis=1)
        for g in range(grp):
            sg = s[g * lb:(g + 1) * lb]
            snk = sink_ref[kv * grp + g] * math.log2(math.e)
            m = jnp.maximum(jnp.max(sg, axis=1, keepdims=True), snk)
            ov = _dot(jnp.exp2(sg - m).astype(BF16), v_aug)
            outs.append((ov[:, :hd] / (ov[:, hd:hd + 1] + jnp.exp2(snk - m))).astype(BF16))
    o_ref[...] = jnp.concatenate(outs, axis=1)


def swa_attention(qkv, sink, *, rows):
    b, seq, ctx_len = rows.batch, rows.seq, rows.ctx_len
    lb = SWA_BLOCK
    nb, ncb = seq // lb, ctx_len // lb
    lat_blocks = rows.n_lat // lb
    qw = SWA_HEADS * SWA_HEAD_DIM
    kw = SWA_KV_HEADS * SWA_HEAD_DIM
    kcol, vcol = qw // kw, qw // kw + 1

    def qidx(bi, n):
        return (jnp.where(n < nb, bi * nb + n, lat_blocks + bi * ncb + (n - nb)), 0)

    def kidx(off, col):
        return lambda bi, n: (bi * nb + jnp.clip(n + off, 0, nb - 1), col)

    def xidx(col):
        return lambda bi, n: (rows.n_lat // ctx_len + bi, col)

    kern = functools.partial(_swa_kernel, nb=nb, seq=seq, ctx_len=ctx_len)
    return pl.pallas_call(
        kern,
        grid=(b, nb + ncb),
        in_specs=[
            pl.BlockSpec(memory_space=pltpu.SMEM),
            pl.BlockSpec((lb, qw), qidx),
            pl.BlockSpec((lb, kw), kidx(-1, kcol)),
            pl.BlockSpec((lb, kw), kidx(0, kcol)),
            pl.BlockSpec((lb, kw), kidx(1, kcol)),
            pl.BlockSpec((ctx_len, kw), xidx(kcol)),
            pl.BlockSpec((lb, kw), kidx(-1, vcol)),
            pl.BlockSpec((lb, kw), kidx(0, vcol)),
            pl.BlockSpec((lb, kw), kidx(1, vcol)),
            pl.BlockSpec((ctx_len, kw), xidx(vcol)),
        ],
        out_specs=pl.BlockSpec((lb, qw), qidx),
        out_shape=jax.ShapeDtypeStruct((rows.n_all, qw), BF16),
        compiler_params=_params(("parallel", "parallel")),
        name="swa_attention",
    )(sink, qkv, qkv, qkv, qkv, qkv, qkv, qkv, qkv, qkv)


def _mlstm_kernel(*refs):
    ins, (of_ref, ob_ref, c_ref, m_ref) = refs[:14], refs[14:]
    c = pl.program_id(1)
    L, H, dqk, dv = ML_CHUNK, ML_HEADS, ML_DQK, ML_DV
    scale = dqk ** -0.5

    @pl.when(c == 0)
    def _():
        c_ref[...] = jnp.zeros_like(c_ref)
        m_ref[...] = jnp.zeros_like(m_ref)

    row = lax.broadcasted_iota(jnp.int32, (L, L), 0)
    col = lax.broadcasted_iota(jnp.int32, (L, L), 1)
    ones = jnp.ones((L, LANE), BF16)
    for d, o_ref in enumerate((of_ref, ob_ref)):
        q_ref, k_ref, v_ref, gi_ref, gf_ref, bi_ref, bf_ref = ins[7 * d:7 * d + 7]
        mask = (col <= row) if d == 0 else (col >= row)
        ones_mask = jnp.where(mask, 1.0, 0.0).astype(BF16)
        i_blk = gi_ref[...] + bi_ref[...]
        f_blk = jax.nn.log_sigmoid(gf_ref[...] + bf_ref[...])
        f_hi = f_blk.astype(BF16)
        r1 = f_blk - f_hi.astype(F32)
        f_mid = r1.astype(BF16)
        f_lo = (r1 - f_mid.astype(F32)).astype(BF16)
        b_blk = _dot(jnp.concatenate([ones_mask] * 3, axis=1), jnp.concatenate([f_hi, f_mid, f_lo], axis=0))
        b_end = jnp.sum(f_blk, axis=0, keepdims=True)
        e_rows = (i_blk - b_blk).T
        m_prev_blk = m_ref[d]
        dec = b_end - b_blk + i_blk
        m_new_blk = jnp.maximum(b_end + m_prev_blk, jnp.max(dec, axis=0, keepdims=True))
        ws_blk = jnp.exp(dec - m_new_blk)
        gs_blk = jnp.exp(b_end + m_prev_blk - m_new_blk)
        for h in range(H):
            qh = q_ref[:, h * dqk:(h + 1) * dqk]
            kh = k_ref[:, h * dqk:(h + 1) * dqk]
            v_aug = jnp.concatenate([v_ref[:, h * dv:(h + 1) * dv], ones], axis=1)
            e = jnp.where(mask, e_rows[h:h + 1, :], -jnp.inf)
            m_prev = m_prev_blk[:, h:h + 1]
            mm = jnp.maximum(m_prev, jnp.max(e, axis=1, keepdims=True))
            s = (_dot_nt(qh, kh) * jnp.exp(e - (mm - math.log(scale)))).astype(BF16)
            g = jnp.exp(m_prev - mm) * scale
            ct = c_ref[d, h]
            lhs = jnp.concatenate([s, (qh.astype(F32) * g).astype(BF16)], axis=1)
            res = _dot(lhs, jnp.concatenate([v_aug, ct.astype(BF16)], axis=0))
            floor = jnp.exp(-(b_blk[:, h:h + 1] + mm))
            hout = res[:, :dv] / jnp.maximum(jnp.abs(res[:, dv:dv + 1]), floor)
            o_ref[:, h * dv:(h + 1) * dv] = hout.astype(o_ref.dtype)
            kw = (kh.astype(F32) * ws_blk[:, h:h + 1]).astype(BF16)
            c_ref[d, h] = gs_blk[:, h:h + 1] * ct + _dot_tn(kw, v_aug)
        m_ref[d] = m_new_blk


def mlstm_scan(z, gates, gate_b, *, rows):
    b, seq, ctx_len = rows.batch, rows.seq, rows.ctx_len
    L = ML_CHUNK
    assert ctx_len == L and seq % L == 0
    nlc = seq // L
    hq, hv = ML_HEADS * ML_DQK, ML_HEADS * ML_DV

    def rb(d):
        def index(bi, c):
            lat = bi * nlc + (c - 1 if d == 0 else nlc - c)
            return jnp.where(c == 0, rows.n_lat // L + bi, lat)
        return index

    in_specs, args = [], []
    for d in range(2):
        r = rb(d)
        in_specs += [
            pl.BlockSpec((L, hq), lambda bi, c, r=r: (r(bi, c), 0)),
            pl.BlockSpec((L, hq), lambda bi, c, r=r: (r(bi, c), 1)),
            pl.BlockSpec((L, hv), lambda bi, c, r=r: (r(bi, c), 1)),
            pl.BlockSpec((L, LANE), lambda bi, c, r=r, d=d: (r(bi, c), 2 * d)),
            pl.BlockSpec((L, LANE), lambda bi, c, r=r, d=d: (r(bi, c), 2 * d + 1)),
            pl.BlockSpec((1, LANE), lambda bi, c, d=d: (0, 2 * d)),
            pl.BlockSpec((1, LANE), lambda bi, c, d=d: (0, 2 * d + 1)),
        ]
        args += [z, z, z, gates, gates, gate_b, gate_b]
    return pl.pallas_call(
        _mlstm_kernel,
        grid=(b, nlc + 1),
        in_specs=in_specs,
        out_specs=[pl.BlockSpec((L, hv), lambda bi, c, r=rb(d): (r(bi, c), 0)) for d in range(2)],
        out_shape=[jax.ShapeDtypeStruct((rows.n_all, hv), BF16)] * 2,
        scratch_shapes=[
            pltpu.VMEM((2, ML_HEADS, ML_DQK, ML_DV + LANE), F32),
            pltpu.VMEM((2, 1, LANE), F32),
        ],
        compiler_params=_params(("parallel", "arbitrary")),
        name="mlstm_scan",
    )(*args)


def _mla_kernel(q_ref, kvl_ref, kvx_ref, krl_ref, krx_ref, o_ref, kk_ref, vt_ref, *, seq, sub, ck):
    @pl.when(pl.program_id(2) == 0)
    def _():
        kk_ref[:seq, :LANE] = kvl_ref[:, :LANE]
        kk_ref[:seq, LANE:] = krl_ref[...]
        kk_ref[seq:, :LANE] = kvx_ref[:, :LANE]
        kk_ref[seq:, LANE:] = krx_ref[...]
        vt_ref[:LANE, :seq] = kvl_ref[:, LANE:].astype(F32).T.astype(BF16)
        vt_ref[:LANE, seq:] = kvx_ref[:, LANE:].astype(F32).T.astype(BF16)
        vt_ref[LANE:, :] = jnp.ones((vt_ref.shape[0] - LANE, vt_ref.shape[1]), BF16)

    nsub = q_ref.shape[0] // sub
    n_keys = kk_ref.shape[0]
    bounds = [(k0, min(k0 + ck, n_keys)) for k0 in range(0, n_keys, ck)]
    nck = len(bounds)
    qs = [q_ref[s * sub:(s + 1) * sub, :] for s in range(nsub)]

    def scores(s, c):
        return _dot_nt(kk_ref[bounds[c][0]:bounds[c][1], :], qs[s])

    m = [None] * nsub
    acc = [None] * nsub
    st = [None] * nsub

    def absorb(s, c):
        cm = jnp.max(st[s], axis=0, keepdims=True)
        m_new = cm if c == 0 else jnp.maximum(m[s], cm)
        pv = _dot(vt_ref[:, bounds[c][0]:bounds[c][1]], jnp.exp2(st[s] - m_new).astype(BF16))
        acc[s] = pv if c == 0 else acc[s] * jnp.exp2(m[s] - m_new) + pv
        m[s] = m_new

    for k in range(-1, nck + nsub - 1):
        nxt = [scores(s, k - s + 1) if 0 <= k - s + 1 < nck else None for s in range(nsub)]
        for s in range(nsub):
            if 0 <= k - s < nck:
                absorb(s, k - s)
        for s in range(nsub):
            if nxt[s] is not None:
                st[s] = nxt[s]
    for s in range(nsub):
        o = acc[s][:LANE] / acc[s][LANE:LANE + 1]
        o_ref[s * sub:(s + 1) * sub, :] = o.T.astype(o_ref.dtype)


def mla_attention(q, kv, z, *, rows, tq=4096, sub=256, ck=1024):
    b, seq, ctx_len = rows.batch, rows.seq, rows.ctx_len
    tq = min(tq, seq)
    nq = seq // tq
    kr_col = z.shape[1] // LANE - 1
    ctx0 = rows.n_lat // ctx_len
    n_keys = seq + ctx_len
    ones_rows = 16
    return pl.pallas_call(
        functools.partial(_mla_kernel, seq=seq, sub=sub, ck=ck),
        grid=(b, MLA_HEADS, nq),
        in_specs=[
            pl.BlockSpec((tq, 2 * LANE), lambda bi, h, i: (bi * nq + i, h)),
            pl.BlockSpec((seq, 2 * LANE), lambda bi, h, i: (bi, h)),
            pl.BlockSpec((ctx_len, 2 * LANE), lambda bi, h, i: (ctx0 + bi, h)),
            pl.BlockSpec((seq, LANE), lambda bi, h, i: (bi, kr_col)),
            pl.BlockSpec((ctx_len, LANE), lambda bi, h, i: (ctx0 + bi, kr_col)),
        ],
        out_specs=pl.BlockSpec((tq, LANE), lambda bi, h, i: (bi * nq + i, h)),
        out_shape=jax.ShapeDtypeStruct((rows.n_lat, MLA_HEADS * MLA_V), BF16),
        scratch_shapes=[pltpu.VMEM((n_keys, 2 * LANE), BF16), pltpu.VMEM((LANE + ones_rows, n_keys), BF16)],
        compiler_params=_params(("parallel", "parallel", "arbitrary")),
        name="mla_attention",
    )(q, kv, kv, z, z)


def _hy_filter_kernel(ft_ref, w1_ref, b1_ref, w2_ref, b2_ref, w3_ref, b3_ref, fr_ref, fr3_ref, w4_ref, b4_ref,
                      dec_ref, o_ref, a_ref, *, back_from):
    j = pl.program_id(0)
    fw = HY_FILTER_W
    hdot = functools.partial(jnp.dot, preferred_element_type=F32, precision=HIGHEST)

    @pl.when(j == 0)
    def _():
        fr = fr_ref[...]
        a = jnp.sin(fr * (hdot(ft_ref[...], w1_ref[...]) + b1_ref[...]))
        a = jnp.sin(fr * (hdot(a, w2_ref[...]) + b2_ref[...]))
        a3 = jnp.sin(fr3_ref[...] * (hdot(a, w3_ref[...]) + b3_ref[...]))
        hi = a3.astype(BF16)
        lo = (a3 - hi.astype(F32)).astype(BF16)
        lane = lax.broadcasted_iota(jnp.int32, a3.shape, 1)
        a_ref[...] = jnp.where((lane >= fw) & (lane < 2 * fw), lo, hi)

    w4 = w4_ref[...]
    w_hi = w4.astype(BF16)
    w_lo = (w4 - w_hi.astype(F32)).astype(BF16)
    rhs = jnp.concatenate([w_hi, w_hi, w_lo, jnp.zeros_like(w_hi)], axis=0)
    t = ft_ref[:, 0:1]
    filt = (_dot(a_ref[...], rhs) + b4_ref[...]) * jnp.exp(-t * jnp.abs(dec_ref[...]))
    row = lax.broadcasted_iota(jnp.int32, filt.shape, 0)
    o_ref[...] = jnp.where(jnp.logical_and(row == 0, j >= back_from), 0.0, filt).astype(o_ref.dtype)


def hyena_filters(seq, w1, b1, w2, b2, w3, b3, w4, b4, freq, decay, tn=512):
    t = np.linspace(0.0, 1.0, seq)[:, None]
    w = (2.0 * math.pi / seq) * np.arange(seq)[:, None]
    bands = np.linspace(1e-4, HY_BANDS - 1, HY_BANDS)[None, :]
    feats = np.zeros((seq, LANE))
    feats[:, :1 + 2 * HY_BANDS] = np.concatenate([t, np.cos(bands * w), -np.sin(bands * w)], axis=-1)
    n = w4.shape[1]
    fw = HY_FILTER_W
    w1p = jnp.zeros((LANE, fw), F32).at[:w1.shape[0]].set(w1)
    row = lambda a: a.reshape(1, -1)
    tile3 = lambda a: jnp.concatenate([a, a, a, jnp.zeros_like(a)], axis=-1)
    full = lambda shape: pl.BlockSpec(shape, lambda j: (0, 0))
    return pl.pallas_call(
        functools.partial(_hy_filter_kernel, back_from=(n // 2) // tn),
        grid=(n // tn,),
        in_specs=[full((seq, LANE)), full((LANE, fw)), full((1, fw)), full((fw, fw)), full((1, fw)),
                  full((fw, 4 * fw)), full((1, 4 * fw)), full((1, fw)), full((1, 4 * fw)),
                  pl.BlockSpec((fw, tn), lambda j: (0, j)),
                  pl.BlockSpec((1, tn), lambda j: (0, j)),
                  pl.BlockSpec((1, tn), lambda j: (0, j))],
        out_specs=pl.BlockSpec((seq, tn), lambda j: (0, j)),
        out_shape=jax.ShapeDtypeStruct((seq, n), BF16),
        scratch_shapes=[pltpu.VMEM((seq, 4 * fw), BF16)],
        compiler_params=_params(("arbitrary",)),
        name="hyena_filters",
    )(jnp.asarray(feats, F32), w1p, row(b1), w2, row(b2), tile3(w3), tile3(row(b3)), row(freq), tile3(row(freq)),
      w4, row(b4), row(decay))


def _dft_mats(seq, p):
    n = 2 * seq
    q = n // p
    k1n = q // 2 + 1
    hi, k1 = np.arange(q // 2), np.arange(k1n)
    th = 2 * np.pi * np.outer(k1, hi) / q
    ma = np.zeros((2 * k1n, q // 2))
    ma[0::2], ma[1::2] = np.cos(th), -np.sin(th)
    lo = np.arange(p)
    wf = np.zeros((k1n, 2 * p, 2 * p))
    wi = np.zeros((k1n, 2 * p, 2 * p))
    for k in k1:
        ph = -2 * np.pi * (np.outer(lo, lo) / p + k * lo[None, :] / n)
        er, ei = np.cos(ph), np.sin(ph)
        wf[k] = np.block([[er, -ei], [ei, er]])
        wi[k] = np.block([[er.T, ei.T], [-ei.T, er.T]])
    c = np.full(k1n, 2.0)
    c[0] = c[-1] = 1.0
    th2 = 2 * np.pi * np.outer(hi, k1) / q
    md = np.zeros((q // 2, 2 * k1n))
    md[:, 0::2], md[:, 1::2] = c * np.cos(th2) / n, -c * np.sin(th2) / n
    as_bf16 = lambda a: jnp.asarray(a, F32).astype(BF16)
    return ma, as_bf16(wf), as_bf16(wi), md


def _hy_short_kernel(x1_ref, x2_ref, v_ref, w_ref, b_ref, o1_ref, o2_ref, o3_ref):
    for k, (x_ref, o_ref) in enumerate(((x1_ref, o1_ref), (x2_ref, o2_ref), (v_ref, o3_ref))):
        x = x_ref[...].astype(F32)
        n = x.shape[0]
        row = lax.broadcasted_iota(jnp.int32, x.shape, 0)
        prev = jnp.where(row == 0, 0.0, pltpu.roll(x, 1, 0))
        nxt = jnp.where(row == n - 1, 0.0, pltpu.roll(x, n - 1, 0))
        w = w_ref[k]
        o_ref[...] = (prev * w[0:1] + x * w[1:2] + nxt * w[2:3] + b_ref[k]).astype(o_ref.dtype)


def hyena_short_conv(x3, conv_w, conv_b, *, rows, cb=256):
    b, seq = rows.batch, rows.seq
    d = x3.shape[1] // 3
    nblk = d // cb
    w = jnp.transpose(conv_w.reshape(HY_SHORT, 3, d), (1, 0, 2))
    bias = conv_b.reshape(3, 1, d)
    in_x = [pl.BlockSpec((seq, cb), lambda bi, j, k=k: (bi, k * nblk + j)) for k in range(3)]
    out = pl.BlockSpec((seq, cb), lambda bi, j: (bi, j))
    shape = jax.ShapeDtypeStruct((rows.n_lat, d), BF16)
    return pl.pallas_call(
        _hy_short_kernel,
        grid=(b, nblk),
        in_specs=in_x + [pl.BlockSpec((3, HY_SHORT, cb), lambda bi, j: (0, 0, j)),
                         pl.BlockSpec((3, 1, cb), lambda bi, j: (0, 0, j))],
        out_specs=[out, out, out],
        out_shape=[shape, shape, shape],
        compiler_params=_params(("parallel", "parallel")),
        name="hyena_short_conv",
    )(x3, x3, x3, w, bias)


def _hy_stage_a_kernel(x_ref, m_ref, o_ref):
    qh, g, wc = x_ref.shape
    x = x_ref[...].reshape(qh * g, wc).astype(BF16)
    o_ref[...] = _dot(m_ref[...], x).reshape(o_ref.shape).astype(o_ref.dtype)


def hyena_stage_a(x, mak, *, p, width=2048):
    bx, seq, c = x.shape
    g = HY_G
    qh = seq // p
    mo = mak.shape[0] // g
    x5 = x.reshape(bx, qh, p // g, g, c)
    return pl.pallas_call(
        _hy_stage_a_kernel,
        grid=(bx, p // g, c // width),
        in_specs=[pl.BlockSpec((None, qh, None, g, width), lambda bi, l, j: (bi, 0, l, 0, j)),
                  pl.BlockSpec(mak.shape, lambda bi, l, j: (0, 0))],
        out_specs=pl.BlockSpec((None, mo, g, width), lambda bi, l, j: (bi, 0, l, j)),
        out_shape=jax.ShapeDtypeStruct((bx, mo, p, c), BF16),
        compiler_params=_params(("parallel", "parallel", "parallel")),
        name="hyena_stage_a",
    )(x5, mak)


def _hy_spec_kernel(a0_ref, a1_ref, wf_ref, o_ref):
    p = o_ref.shape[2]
    cb = o_ref.shape[3]
    wf = wf_ref[0]
    x0 = _dot(wf, a0_ref[0].reshape(2 * p, cb))
    x1 = _dot(wf, a1_ref[0].reshape(2 * p, cb))
    o_ref[0, 0] = x0[:p] + x1[:p]
    o_ref[0, 1] = x0[p:] - x1[p:]


def hyena_spectrum(af, wf, *, d, cb=2048):
    k1n, _, p, _ = af.shape
    nblk = (HY_ORDER * d) // cb
    return pl.pallas_call(
        _hy_spec_kernel,
        grid=(k1n, nblk),
        in_specs=[pl.BlockSpec((1, 2, p, cb), lambda k, j: (k, 0, 0, j)),
                  pl.BlockSpec((1, 2, p, cb), lambda k, j: (k, 0, 0, nblk + j)),
                  pl.BlockSpec((1, 2 * p, 2 * p), lambda k, j: (k, 0, 0))],
        out_specs=pl.BlockSpec((1, 2, p, cb), lambda k, j: (k, 0, 0, j)),
        out_shape=jax.ShapeDtypeStruct((k1n, 2, p, HY_ORDER * d), F32),
        compiler_params=_params(("parallel", "parallel")),
        name="hyena_spectrum",
    )(af, af, wf)


def _hy_mid_kernel(a_ref, wf_ref, wi_ref, h_ref, o_ref):
    b, _, p, cb = a_ref.shape
    hr, hi = h_ref[0], h_ref[1]
    for bi in range(b):
        x = _dot(wf_ref[...], a_ref[bi].reshape(2 * p, cb))
        xr, xi = x[:p], x[p:]
        y = jnp.concatenate([xr * hr - xi * hi, xr * hi + xi * hr], axis=0).astype(BF16)
        o_ref[bi] = _dot(wi_ref[...], y).reshape(2, p, cb).astype(o_ref.dtype)


def hyena_mid(a, wf, wi, spec, *, order, d, cb=1024):
    b, k1n, _, p, _ = a.shape
    nblk = d // cb
    data = pl.BlockSpec((b, None, 2, p, cb), lambda k, j: (0, k, 0, 0, j))
    mat = pl.BlockSpec((None, 2 * p, 2 * p), lambda k, j: (k, 0, 0))
    return pl.pallas_call(
        _hy_mid_kernel,
        grid=(k1n, nblk),
        in_specs=[data, mat, mat,
                  pl.BlockSpec((None, 2, p, cb), lambda k, j: (k, 0, 0, order * nblk + j))],
        out_specs=data,
        out_shape=jax.ShapeDtypeStruct(a.shape, BF16),
        compiler_params=_params(("parallel", "parallel")),
        name="hyena_mid",
    )(a, wf, wi, spec)


def _hy_stage_d_kernel(c_ref, m_ref, x_ref, z_ref, skip_ref, *rest, fused):
    ma_ref, o_ref, a_ref = rest if fused else (None, rest[0], None)
    mo, g, wc = c_ref.shape
    qh = x_ref.shape[0]
    conv = _dot(m_ref[...], c_ref[...].reshape(mo * g, wc))
    z = z_ref[...].reshape(qh * g, wc).astype(F32)
    x = x_ref[...].reshape(qh * g, wc).astype(F32)
    y = (x * (conv + z * skip_ref[...])).astype(o_ref.dtype)
    o_ref[...] = y.reshape(o_ref.shape)
    if fused:
        a_ref[...] = _dot(ma_ref[...], y).reshape(a_ref.shape).astype(a_ref.dtype)


def hyena_stage_d(c, mdk, xg, z, skip, *, p, mak=None, width=2048):
    b, mo, _, d = c.shape
    g = HY_G
    seq = xg.shape[1]
    qh = seq // p
    fused = mak is not None
    v5 = lambda a: a.reshape(b, qh, p // g, g, d)
    tspec = pl.BlockSpec((None, qh, None, g, width), lambda bi, l, j: (bi, 0, l, 0, j))
    fspec = pl.BlockSpec((None, mo, g, width), lambda bi, l, j: (bi, 0, l, j))
    in_specs = [fspec, pl.BlockSpec(mdk.shape, lambda bi, l, j: (0, 0)), tspec, tspec,
                pl.BlockSpec((1, width), lambda bi, l, j: (0, j))]
    args = [c, mdk, v5(xg), v5(z), skip.reshape(1, d)]
    out_specs, out_shape = [tspec], [jax.ShapeDtypeStruct((b, qh, p // g, g, d), BF16)]
    if fused:
        in_specs.append(pl.BlockSpec(mak.shape, lambda bi, l, j: (0, 0)))
        args.append(mak)
        out_specs.append(fspec)
        out_shape.append(jax.ShapeDtypeStruct(c.shape, BF16))
    outs = pl.pallas_call(
        functools.partial(_hy_stage_d_kernel, fused=fused),
        grid=(b, p // g, d // width),
        in_specs=in_specs,
        out_specs=out_specs,
        out_shape=out_shape,
        compiler_params=_params(("parallel", "parallel", "parallel")),
        name="hyena_stage_d",
    )(*args)
    y = outs[0].reshape(b, seq, d)
    return (y, outs[1]) if fused else y


def hyena_long_convs(x1, x2, v, filt, skip, *, rows):
    b, seq = rows.batch, rows.seq
    d = x1.shape[1]
    p = HY_P
    k1n = seq // p + 1
    ma, wf, wi, md = _dft_mats(seq, p)
    eye = np.eye(HY_G)
    as_bf16 = lambda a: jnp.asarray(a, F32).astype(BF16)
    mak, mdk = as_bf16(np.kron(ma, eye)), as_bf16(np.kron(md, eye))
    af = hyena_stage_a(filt.reshape(1, seq, filt.shape[1]), mak, p=p)
    spec = hyena_spectrum(af.reshape(k1n, 2, p, filt.shape[1]), wf, d=d)
    t3 = lambda a: a.reshape(b, seq, d)
    y = t3(v)
    a = hyena_stage_a(y, mak, p=p)
    c = hyena_mid(a.reshape(b, k1n, 2, p, d), wf, wi, spec, order=0, d=d)
    y, a = hyena_stage_d(c.reshape(a.shape), mdk, t3(x1), y, skip[0], p=p, mak=mak)
    c = hyena_mid(a.reshape(b, k1n, 2, p, d), wf, wi, spec, order=1, d=d)
    y = hyena_stage_d(c.reshape(a.shape), mdk, t3(x2), y, skip[1], p=p)
    return y.reshape(rows.n_lat, d)


def _mlstm_gate_weights(w_in, gate_b):
    h = ML_HEADS
    o4 = 2 * h * ML_DQK + 2 * h * ML_DV
    wg = w_in[:, o4:].reshape(-1, 2, 2, h)
    bg = gate_b.reshape(2, 2, h)
    w_out = jnp.zeros((w_in.shape[0], 4 * LANE), F32)
    b_out = jnp.zeros((1, 4 * LANE), F32)
    for d in range(2):
        for gate in range(2):
            lo = (2 * d + gate) * LANE
            w_out = w_out.at[:, lo:lo + h].set(wg[:, gate, d])
            b_out = b_out.at[0, lo:lo + h].set(bg[gate, d])
    return w_out, b_out


def _mla_weights(w_in, w_uq):
    r2 = MLA_Q_RANK + MLA_KV_RANK
    half = MLA_ROPE // 2

    def spread(w):
        z = jnp.zeros(w.shape[:-1] + (half,), w.dtype)
        return jnp.concatenate([w[..., :half], z, w[..., half:], z], axis=-1)

    w_in_p = jnp.concatenate([w_in[:, :r2], spread(w_in[:, r2:])], axis=1)
    wq = w_uq.reshape(w_uq.shape[0], MLA_HEADS, MLA_NOPE + MLA_ROPE)
    wq_p = jnp.concatenate([wq[..., :MLA_NOPE], spread(wq[..., MLA_NOPE:])], axis=-1)
    wq_p = wq_p * ((MLA_NOPE + MLA_ROPE) ** -0.5 * math.log2(math.e))
    return w_in_p, wq_p.reshape(w_uq.shape[0], MLA_HEADS * 2 * LANE)


def kernel(x, c, ctx, c_ctx, ada_w, ada_b, norm_g, mlp_w1, mlp_w2, swa_w_qkv, swa_sink, swa_w_o, ml_w_in, ml_gate_b, ml_head_g, ml_w_o, mla_w_in, mla_q_g, mla_kv_g, mla_w_uq, mla_w_ukv, mla_w_o, hy_w_in, hy_conv_w, hy_conv_b, hy_f_w1, hy_f_b1, hy_f_w2, hy_f_b2, hy_f_w3, hy_f_b3, hy_f_w4, hy_f_b4, hy_f_freq, hy_decay, hy_skip, hy_w_o):
    b, seq, d = x.shape
    ctx_len = ctx.shape[1]
    depth = ada_w.shape[0]
    assert depth == 4 and b < 8
    rows = Rows(b, seq, ctx_len)
    tm = 512
    tm_lin = 1024 if rows.n_lat % 1024 == 0 and rows.n_ctx % 1024 == 0 else 512
    bf = lambda w: w.astype(BF16)
    mlp_w1b, mlp_w2b = bf(mlp_w1), bf(mlp_w2)

    cond = jnp.zeros((8, d), F32).at[:b].set(c).at[b].set(c_ctx)
    mods = ada_mods(cond, ada_w, ada_b).reshape(depth * 8, 1, N_MOD * d)

    common = dict(rows=rows, tm=tm_lin, mods=mods)

    def finish(h, o_args, w_o, layer, n_rows, mode="plain", h2=None):
        h = outproj_residual(o_args, bf(w_o), h, mods, norm_g[layer, 1], rows=rows, n_rows=n_rows, tm=tm,
                             layer=layer, mode=mode, h2=h2, name=f"outproj{layer}")
        return mlp_residual(h, mlp_w1b, mlp_w2b, mods, norm_g[layer, 2], norm_g[layer, 3],
                            rows=rows, n_rows=n_rows, tm=tm, tf=1024, layer=layer)

    n_qk_groups = SWA_HEADS + SWA_KV_HEADS
    q_cols = SWA_HEADS * SWA_HEAD_DIM
    q_scale = SWA_HEAD_DIM ** -0.5 * math.log2(math.e)
    w_qkv = jnp.concatenate([swa_w_qkv[0][:, :q_cols] * q_scale, swa_w_qkv[0][:, q_cols:]], axis=1)
    x_rows, ctx_rows = x.reshape(rows.n_lat, d), ctx.reshape(rows.n_ctx, d)
    qkv = linear(x_rows, bf(w_qkv), x2=ctx_rows, n_rows=rows.n_all, tn=512, out_dtype=BF16, prologue="norm_mod",
                 gain=norm_g[0, 0], layer=0, rope_tabs=_rope_tables(seq, SWA_HEAD_DIM), rope_pattern=(True,) * 4,
                 rope_jmax=n_qk_groups // 4, name="swa_qkv", **common)
    o = swa_attention(qkv, swa_sink[0], rows=rows)
    h = finish(x_rows, o, swa_w_o[0], 0, rows.n_all, h2=ctx_rows)

    w_gate, b_gate = _mlstm_gate_weights(ml_w_in[0], ml_gate_b[0])
    z = linear(h, bf(ml_w_in[0]), n_rows=rows.n_all, tn=1536, out_dtype=BF16, prologue="norm_mod",
               gain=norm_g[1, 0], layer=1, n_out=2 * ML_HEADS * (ML_DQK + ML_DV), name="mlstm_in", **common)
    gates = linear(h, bf(w_gate), n_rows=rows.n_all, tn=4 * LANE, out_dtype=F32, prologue="norm_mod",
                   gain=norm_g[1, 0], layer=1, name="mlstm_gates", **common)
    h_fwd, h_bwd = mlstm_scan(z, gates, b_gate, rows=rows)
    h = finish(h, (h_fwd, h_bwd, z, ml_head_g[0]), ml_w_o[0], 1, rows.n_all, mode="mlstm")

    w_in_p, w_uq_p = _mla_weights(mla_w_in[0], mla_w_uq[0])
    rope_mla = _rope_tables(seq, MLA_ROPE)
    zc = linear(h, bf(w_in_p), n_rows=rows.n_all, tn=w_in_p.shape[1], out_dtype=BF16, prologue="norm_mod",
                gain=norm_g[2, 0], layer=2, rope_tabs=rope_mla, rope_pattern=(False,) * 8 + (True,),
                name="mla_in", **common)
    q = linear(zc, bf(w_uq_p), n_rows=rows.n_lat, tn=2048, out_dtype=BF16, x_cols=0, prologue="norm",
               gain=mla_q_g[0], rope_tabs=rope_mla, rope_pattern=(False, True) * 8, name="mla_q",
               rows=rows, tm=tm_lin)
    kv = linear(zc, bf(mla_w_ukv[0]), n_rows=rows.n_all, tn=2048, out_dtype=BF16, x_cols=1, prologue="norm",
                gain=mla_kv_g[0], name="mla_kv", rows=rows, tm=tm_lin)
    o = mla_attention(q, kv, zc, rows=rows)
    h = finish(h, o, mla_w_o[0], 2, rows.n_lat)

    x3 = linear(h, bf(hy_w_in[0]), n_rows=rows.n_lat, tn=1536, out_dtype=BF16, prologue="norm_mod",
                gain=norm_g[3, 0], layer=3, name="hyena_in", **common)
    x1, x2, v = hyena_short_conv(x3, hy_conv_w[0], hy_conv_b[0], rows=rows)
    filt = hyena_filters(seq, hy_f_w1[0], hy_f_b1[0], hy_f_w2[0], hy_f_b2[0], hy_f_w3[0], hy_f_b3[0],
                         hy_f_w4[0], hy_f_b4[0], hy_f_freq[0], hy_decay[0])
    y = hyena_long_convs(x1, x2, v, filt, hy_skip[0], rows=rows)
    h = finish(h, y, hy_w_o[0], 3, rows.n_lat)
    return h.reshape(b, seq, d)
```

```python
import functools
import math

import numpy as np
import jax
import jax.numpy as jnp
from jax import lax
from jax.experimental import pallas as pl
from jax.experimental.pallas import tpu as pltpu

F32, BF16 = jnp.float32, jnp.bfloat16
HIGHEST = lax.Precision.HIGHEST

RMS_EPS = 1e-6
ROPE_THETA = 10000.0
GRID_W = 64
N_MOD = 6
LANE = 128

SWA_HEADS, SWA_KV_HEADS, SWA_HEAD_DIM, SWA_WINDOW, SWA_BLOCK = 16, 4, 128, 128, 128
ML_HEADS, ML_DQK, ML_DV = 8, 128, 256
ML_CHUNK = 256
MLA_HEADS, MLA_Q_RANK, MLA_KV_RANK, MLA_NOPE, MLA_ROPE, MLA_V = 16, 512, 512, 128, 64, 128
HY_ORDER, HY_BANDS, HY_FILTER_W, HY_SHORT = 2, 16, 64, 3
HY_P = 256
HY_G = 16

VMEM_LIMIT = 48 * 1024 * 1024


def _params(sem):
    return pltpu.CompilerParams(dimension_semantics=sem, vmem_limit_bytes=VMEM_LIMIT)


def _dot(a, b):
    return jnp.dot(a, b, preferred_element_type=F32)


def _dot_nt(a, b):
    return lax.dot_general(a, b, (((1,), (1,)), ((), ())), preferred_element_type=F32)


def _dot_tn(a, b):
    return lax.dot_general(a, b, (((0,), (0,)), ((), ())), preferred_element_type=F32)


def _rms(x, g):
    return x * lax.rsqrt(jnp.mean(x * x, axis=-1, keepdims=True) + RMS_EPS) * g


def _rms_ref(x_ref, inv_ref, g):
    x = x_ref[...].astype(F32)
    inv_ref[...] = lax.rsqrt(jnp.mean(x * x, axis=-1, keepdims=True) + RMS_EPS)
    return x_ref[...].astype(F32) * inv_ref[...] * g


def _ada_kernel(s_ref, w_ref, b_ref, o_ref):
    s = s_ref[...]
    s = s * jax.nn.sigmoid(s)
    hi = s.astype(BF16)
    lo = (s - hi.astype(F32)).astype(BF16)
    w = w_ref[0]
    w_hi = w.astype(BF16)
    w_lo = (w - w_hi.astype(F32)).astype(BF16)
    r = _dot(jnp.concatenate([hi, lo], axis=0), w_hi)
    n = s.shape[0]
    o_ref[0] = r[:n] + r[n:] + _dot(hi, w_lo) + b_ref[0]


def ada_mods(cond, ada_w, ada_b, tn=1536):
    depth, d, n = ada_w.shape
    rows = cond.shape[0]
    return pl.pallas_call(
        _ada_kernel,
        grid=(depth, n // tn),
        in_specs=[
            pl.BlockSpec((rows, d), lambda l, j: (0, 0)),
            pl.BlockSpec((1, d, tn), lambda l, j: (l, 0, j)),
            pl.BlockSpec((1, 1, tn), lambda l, j: (l, 0, j)),
        ],
        out_specs=pl.BlockSpec((1, rows, tn), lambda l, j: (l, 0, j)),
        out_shape=jax.ShapeDtypeStruct((depth, rows, n), F32),
        compiler_params=_params(("parallel", "parallel")),
        name="ada_mods",
    )(cond, ada_w, ada_b.reshape(depth, 1, n))


class Rows:
    def __init__(self, batch, seq, ctx_len):
        self.batch, self.seq, self.ctx_len = batch, seq, ctx_len
        self.n_lat = batch * seq
        self.n_ctx = batch * ctx_len
        self.n_all = self.n_lat + self.n_ctx

    def mod_index(self, layer, k, tm):
        lat_blocks, per_batch = self.n_lat // tm, self.seq // tm

        def index(i, *_):
            b = jnp.where(i < lat_blocks, i // per_batch, self.batch)
            return (layer * 8 + b, 0, k)

        return index


def _linear_kernel(*refs, prologue, rope, rope_pattern, n_lat_blocks, dual, single):
    it = iter(refs)
    x_ref = next(it)
    x2_ref = next(it) if dual else None
    g_ref = next(it) if prologue in ("norm", "norm_mod") else None
    sh_ref = next(it) if prologue == "norm_mod" else None
    sc_ref = next(it) if prologue == "norm_mod" else None
    w_ref = next(it)
    cc_ref = next(it) if rope else None
    ss_ref = next(it) if rope else None
    o_ref = next(it)
    u_ref = next(it)
    inv_ref = next(it)
    first = pl.program_id(1) == 0

    def prologue_from(src_ref):
        if prologue in ("norm", "norm_mod"):
            x = _rms_ref(src_ref, inv_ref, g_ref[...])
        else:
            x = src_ref[...].astype(F32)
        if prologue == "norm_mod":
            x = x * (1.0 + sc_ref[0]) + sh_ref[0]
        u_ref[...] = x.astype(BF16)

    if dual:
        is_lat = pl.program_id(0) < n_lat_blocks
        pl.when(jnp.logical_and(first, is_lat))(functools.partial(prologue_from, x_ref))
        pl.when(jnp.logical_and(first, jnp.logical_not(is_lat)))(functools.partial(prologue_from, x2_ref))
    elif single:
        prologue_from(x_ref)
    else:
        pl.when(first)(functools.partial(prologue_from, x_ref))

    acc = _dot(u_ref[...], w_ref[...])

    def plain():
        o_ref[...] = acc.astype(o_ref.dtype)

    def roped():
        cc, ss = cc_ref[...], ss_ref[...]
        segs = []
        for gi, on in enumerate(rope_pattern):
            seg = acc[:, gi * LANE:(gi + 1) * LANE]
            if on:
                seg = seg * cc + pltpu.roll(seg, LANE // 2, 1) * ss
            segs.append(seg)
        o_ref[...] = jnp.concatenate(segs, axis=1).astype(o_ref.dtype)

    if rope:
        roped()
    else:
        plain()


def linear(x, w, *, rows, n_rows, tm, tn, out_dtype, x_cols=None, prologue="none", gain=None,
           mods=None, layer=0, mod_k=(0, 1), rope_tabs=None, rope_pattern=(), rope_jmax=1 << 30,
           n_out=None, x2=None, name="linear"):
    k, n = w.shape
    n = n if n_out is None else n_out
    xc = 0 if x_cols is None else x_cols
    rope = rope_tabs is not None
    dual = x2 is not None
    n_lat_blocks = rows.n_lat // tm
    if dual:
        in_specs = [pl.BlockSpec((tm, k), lambda i, j: (jnp.minimum(i, n_lat_blocks - 1), 0)),
                    pl.BlockSpec((tm, k), lambda i, j: (jnp.maximum(i - n_lat_blocks, 0), 0),
                                 pipeline_mode=pl.Buffered(1))]
        args = [x, x2]
    else:
        in_specs = [pl.BlockSpec((tm, k), lambda i, j: (i, xc))]
        args = [x]
    if prologue in ("norm", "norm_mod"):
        in_specs.append(pl.BlockSpec((1, k), lambda i, j: (0, 0)))
        args.append(gain.reshape(1, k))
    if prologue == "norm_mod":
        for mk in mod_k:
            in_specs.append(pl.BlockSpec((1, 1, k), rows.mod_index(layer, mk, tm)))
            args.append(mods)
    in_specs.append(pl.BlockSpec((k, tn), lambda i, j: (0, j)))
    args.append(w)
    if rope:
        per_seq = rows.seq // tm

        def tab_index(i, j):
            rotate = jnp.logical_and(i < n_lat_blocks, j < rope_jmax)
            return (jnp.where(rotate, 0, 1), i % per_seq, 0)

        for t, fill in zip(rope_tabs, (1.0, 0.0)):
            in_specs.append(pl.BlockSpec((None, tm, LANE), tab_index))
            args.append(jnp.stack([t, jnp.full_like(t, fill)]))
    kern = functools.partial(_linear_kernel, prologue=prologue, rope=rope, rope_pattern=tuple(rope_pattern),
                             n_lat_blocks=n_lat_blocks, dual=dual, single=(n == tn))
    return pl.pallas_call(
        kern,
        grid=(n_rows // tm, n // tn),
        in_specs=in_specs,
        out_specs=pl.BlockSpec((tm, tn), lambda i, j: (i, j)),
        out_shape=jax.ShapeDtypeStruct((n_rows, n), out_dtype),
        scratch_shapes=[pltpu.VMEM((tm, k), BF16), pltpu.VMEM((tm, 1), F32)],
        compiler_params=_params(("parallel", "arbitrary")),
        name=name,
    )(*args)


def _outproj_kernel(*refs, mode, n_lat_blocks):
    it = iter(refs)
    if mode == "mlstm":
        hf_ref, hb_ref, og_ref, hg_ref = next(it), next(it), next(it), next(it)
    else:
        o_ref_in = next(it)
    w_ref, h_ref = next(it), next(it)
    h2_ref = next(it) if n_lat_blocks is not None else None
    gate_ref, g_ref, out_ref = next(it), next(it), next(it)
    y_ref, inv_ref = next(it), next(it)
    if mode == "mlstm":
        hs = hf_ref[...].astype(F32) + hb_ref[...].astype(F32)
        og = jax.nn.sigmoid(og_ref[...].astype(F32))
        hg = hg_ref[...]
        parts = []
        for h in range(ML_HEADS):
            sl = slice(h * ML_DV, (h + 1) * ML_DV)
            parts.append((_rms(hs[:, sl], hg[:, sl]) * og[:, sl]).astype(BF16))
        o = jnp.concatenate(parts, axis=1)
    else:
        o = o_ref_in[...]
    y_ref[...] = _dot(o, w_ref[...])
    upd = gate_ref[0] * _rms_ref(y_ref, inv_ref, g_ref[...])
    if h2_ref is None:
        out_ref[...] = h_ref[...] + upd
    else:
        out_ref[...] = jnp.where(pl.program_id(0) < n_lat_blocks, h_ref[...], h2_ref[...]) + upd


def outproj_residual(o_args, w, h, mods, gain, *, rows, n_rows, tm, layer, mode="plain", h2=None, name="outproj"):
    k, d = w.shape
    n_lat_blocks = rows.n_lat // tm if h2 is not None else None
    if mode == "mlstm":
        h_fwd, h_bwd, z, head_g = o_args
        in_specs = [
            pl.BlockSpec((tm, k), lambda i: (i, 0)),
            pl.BlockSpec((tm, k), lambda i: (i, 0)),
            pl.BlockSpec((tm, k), lambda i: (i, 2)),
            pl.BlockSpec((1, k), lambda i: (0, 0)),
        ]
        args = [h_fwd, h_bwd, z, head_g.reshape(1, k)]
    else:
        in_specs = [pl.BlockSpec((tm, k), lambda i: (i, 0))]
        args = [o_args]
    in_specs.append(pl.BlockSpec((k, d), lambda i: (0, 0)))
    args.append(w)
    if h2 is None:
        in_specs.append(pl.BlockSpec((tm, d), lambda i: (i, 0)))
        args.append(h)
    else:
        in_specs += [pl.BlockSpec((tm, d), lambda i: (jnp.minimum(i, n_lat_blocks - 1), 0)),
                     pl.BlockSpec((tm, d), lambda i: (jnp.maximum(i - n_lat_blocks, 0), 0))]
        args += [h, h2]
    in_specs += [
        pl.BlockSpec((1, 1, d), rows.mod_index(layer, 2, tm)),
        pl.BlockSpec((1, d), lambda i: (0, 0)),
    ]
    args += [mods, gain.reshape(1, d)]
    return pl.pallas_call(
        functools.partial(_outproj_kernel, mode=mode, n_lat_blocks=n_lat_blocks),
        grid=(n_rows // tm,),
        in_specs=in_specs,
        out_specs=pl.BlockSpec((tm, d), lambda i: (i, 0)),
        out_shape=jax.ShapeDtypeStruct((n_rows, d), F32),
        scratch_shapes=[pltpu.VMEM((tm, d), F32), pltpu.VMEM((tm, 1), F32)],
        compiler_params=_params(("parallel",)),
        name=name,
    )(*args)


def _mlp_kernel(h_ref, g2_ref, sh_ref, sc_ref, w1_ref, w2_ref, gate_ref, g3_ref, out_ref, v_ref, acc_ref,
                inv_ref):
    f = pl.program_id(1)

    @pl.when(f == 0)
    def _():
        v = _rms_ref(h_ref, inv_ref, g2_ref[...]) * (1.0 + sc_ref[0]) + sh_ref[0]
        v_ref[...] = v.astype(BF16)
        acc_ref[...] = jnp.zeros_like(acc_ref)

    a = jnp.maximum(_dot(v_ref[...], w1_ref[...]), 0.0)
    acc_ref[...] += _dot((a * a).astype(BF16), w2_ref[...])

    @pl.when(f == pl.num_programs(1) - 1)
    def _():
        out_ref[...] = h_ref[...] + gate_ref[0] * _rms_ref(acc_ref, inv_ref, g3_ref[...])


def mlp_residual(h, w1, w2, mods, g2, g3, *, rows, n_rows, tm, tf, layer):
    _, d, ff = w1.shape
    return pl.pallas_call(
        _mlp_kernel,
        grid=(n_rows // tm, ff // tf),
        in_specs=[
            pl.BlockSpec((tm, d), lambda i, f: (i, 0)),
            pl.BlockSpec((1, d), lambda i, f: (0, 0)),
            pl.BlockSpec((1, 1, d), rows.mod_index(layer, 3, tm)),
            pl.BlockSpec((1, 1, d), rows.mod_index(layer, 4, tm)),
            pl.BlockSpec((None, d, tf), lambda i, f: (layer, 0, f)),
            pl.BlockSpec((None, tf, d), lambda i, f: (layer, f, 0)),
            pl.BlockSpec((1, 1, d), rows.mod_index(layer, 5, tm)),
            pl.BlockSpec((1, d), lambda i, f: (0, 0)),
        ],
        out_specs=pl.BlockSpec((tm, d), lambda i, f: (i, 0)),
        out_shape=jax.ShapeDtypeStruct((n_rows, d), F32),
        scratch_shapes=[pltpu.VMEM((tm, d), BF16), pltpu.VMEM((tm, d), F32), pltpu.VMEM((tm, 1), F32)],
        compiler_params=_params(("parallel", "arbitrary")),
        name="mlp",
    )(h, g2.reshape(1, d), mods, mods, w1, w2, mods, g3.reshape(1, d))


def _rope_tables(seq, d_rot):
    pos = np.arange(seq)
    row, col = pos // GRID_W, pos % GRID_W
    n = d_rot // 4
    inv = ROPE_THETA ** (-np.arange(n, dtype=np.float64) / n)
    ang = np.concatenate([row[:, None] * inv, col[:, None] * inv], axis=-1)
    cos, sin = np.cos(ang), np.sin(ang)
    half = d_rot // 2
    cc = np.zeros((seq, LANE))
    ss = np.zeros((seq, LANE))
    cc[:, :half] = cos
    cc[:, LANE // 2:LANE // 2 + half] = cos
    ss[:, :half] = -sin
    ss[:, LANE // 2:LANE // 2 + half] = sin
    return jnp.asarray(cc, F32), jnp.asarray(ss, F32)


def _swa_kernel(sink_ref, q_ref, kp_ref, kc_ref, kn_ref, kx_ref, vp_ref, vc_ref, vn_ref, vx_ref, o_ref,
                *, nb, seq, ctx_len):
    n = pl.program_id(1)
    hd, grp, lb = SWA_HEAD_DIM, SWA_HEADS // SWA_KV_HEADS, SWA_BLOCK
    m_rows, n_loc = grp * lb, 3 * lb
    r = lax.broadcasted_iota(jnp.int32, (m_rows, n_loc + ctx_len), 0)
    c = lax.broadcasted_iota(jnp.int32, (m_rows, n_loc + ctx_len), 1)
    qpos = n * lb + (r & (lb - 1))
    kpos = (n - 1) * lb + c
    local_ok = (jnp.abs(kpos - qpos) <= SWA_WINDOW) & (kpos >= 0) & (kpos < seq) & (n < nb)
    valid = local_ok | (c >= n_loc)
    ones = jnp.ones((n_loc + ctx_len, hd), BF16)
    def scores(kv):
        ks = slice(kv * hd, (kv + 1) * hd)
        qg = jnp.concatenate([q_ref[:, (kv * grp + g) * hd:(kv * grp + g + 1) * hd] for g in range(grp)], axis=0)
        keys = jnp.concatenate([kp_ref[:, ks], kc_ref[:, ks], kn_ref[:, ks], kx_ref[:, ks]], axis=0)
        return jnp.where(valid, _dot_nt(qg, keys), -1e30)

    outs = []
    s_next = scores(0)
    for kv in range(SWA_KV_HEADS):
        s = s_next
        if kv + 1 < SWA_KV_HEADS:
            s_next = scores(kv + 1)
        ks = slice(kv * hd, (kv + 1) * hd)
        vals = jnp.concatenate([vp_ref[:, ks], vc_ref[:, ks], vn_ref[:, ks], vx_ref[:, ks]], axis=0)
        v_aug = jnp.concatenate([vals, ones], axis=1)
        for g in range(grp):
            sg = s[g * lb:(g + 1) * lb]
            snk = sink_ref[kv * grp + g] * math.log2(math.e)
            m = jnp.maximum(jnp.max(sg, axis=1, keepdims=True), snk)
            ov = _dot(jnp.exp2(sg - m).astype(BF16), v_aug)
            outs.append((ov[:, :hd] / (ov[:, hd:hd + 1] + jnp.exp2(snk - m))).astype(BF16))
    o_ref[...] = jnp.concatenate(outs, axis=1)


def swa_attention(qkv, sink, *, rows):
    b, seq, ctx_len = rows.batch, rows.seq, rows.ctx_len
    lb = SWA_BLOCK
    nb, ncb = seq // lb, ctx_len // lb
    lat_blocks = rows.n_lat // lb
    qw = SWA_HEADS * SWA_HEAD_DIM
    kw = SWA_KV_HEADS * SWA_HEAD_DIM
    kcol, vcol = qw // kw, qw // kw + 1

    def qidx(bi, n):
        return (jnp.where(n < nb, bi * nb + n, lat_blocks + bi * ncb + (n - nb)), 0)

    def kidx(off, col):
        return lambda bi, n: (bi * nb + jnp.clip(n + off, 0, nb - 1), col)

    def xidx(col):
        return lambda bi, n: (rows.n_lat // ctx_len + bi, col)

    kern = functools.partial(_swa_kernel, nb=nb, seq=seq, ctx_len=ctx_len)
    return pl.pallas_call(
        kern,
        grid=(b, nb + ncb),
        in_specs=[
            pl.BlockSpec(memory_space=pltpu.SMEM),
            pl.BlockSpec((lb, qw), qidx),
            pl.BlockSpec((lb, kw), kidx(-1, kcol)),
            pl.BlockSpec((lb, kw), kidx(0, kcol)),
            pl.BlockSpec((lb, kw), kidx(1, kcol)),
            pl.BlockSpec((ctx_len, kw), xidx(kcol)),
            pl.BlockSpec((lb, kw), kidx(-1, vcol)),
            pl.BlockSpec((lb, kw), kidx(0, vcol)),
            pl.BlockSpec((lb, kw), kidx(1, vcol)),
            pl.BlockSpec((ctx_len, kw), xidx(vcol)),
        ],
        out_specs=pl.BlockSpec((lb, qw), qidx),
        out_shape=jax.ShapeDtypeStruct((rows.n_all, qw), BF16),
        compiler_params=_params(("parallel", "parallel")),
        name="swa_attention",
    )(sink, qkv, qkv, qkv, qkv, qkv, qkv, qkv, qkv, qkv)


def _mlstm_kernel(*refs):
    ins, (of_ref, ob_ref, c_ref, m_ref) = refs[:14], refs[14:]
    c = pl.program_id(1)
    L, H, dqk, dv = ML_CHUNK, ML_HEADS, ML_DQK, ML_DV
    scale = dqk ** -0.5

    @pl.when(c == 0)
    def _():
        c_ref[...] = jnp.zeros_like(c_ref)
        m_ref[...] = jnp.zeros_like(m_ref)

    row = lax.broadcasted_iota(jnp.int32, (L, L), 0)
    col = lax.broadcasted_iota(jnp.int32, (L, L), 1)
    ones = jnp.ones((L, LANE), BF16)
    for d, o_ref in enumerate((of_ref, ob_ref)):
        q_ref, k_ref, v_ref, gi_ref, gf_ref, bi_ref, bf_ref = ins[7 * d:7 * d + 7]
        mask = (col <= row) if d == 0 else (col >= row)
        ones_mask = jnp.where(mask, 1.0, 0.0).astype(BF16)
        i_blk = gi_ref[...] + bi_ref[...]
        f_blk = jax.nn.log_sigmoid(gf_ref[...] + bf_ref[...])
        f_hi = f_blk.astype(BF16)
        r1 = f_blk - f_hi.astype(F32)
        f_mid = r1.astype(BF16)
        f_lo = (r1 - f_mid.astype(F32)).astype(BF16)
        b_blk = _dot(jnp.concatenate([ones_mask] * 3, axis=1), jnp.concatenate([f_hi, f_mid, f_lo], axis=0))
        b_end = jnp.sum(f_blk, axis=0, keepdims=True)
        e_rows = (i_blk - b_blk).T
        m_prev_blk = m_ref[d]
        dec = b_end - b_blk + i_blk
        m_new_blk = jnp.maximum(b_end + m_prev_blk, jnp.max(dec, axis=0, keepdims=True))
        ws_blk = jnp.exp(dec - m_new_blk)
        gs_blk = jnp.exp(b_end + m_prev_blk - m_new_blk)
        for h in range(H):
            qh = q_ref[:, h * dqk:(h + 1) * dqk]
            kh = k_ref[:, h * dqk:(h + 1) * dqk]
            v_aug = jnp.concatenate([v_ref[:, h * dv:(h + 1) * dv], ones], axis=1)
            e = jnp.where(mask, e_rows[h:h + 1, :], -jnp.inf)
            m_prev = m_prev_blk[:, h:h + 1]
            mm = jnp.maximum(m_prev, jnp.max(e, axis=1, keepdims=True))
            s = (_dot_nt(qh, kh) * jnp.exp(e - (mm - math.log(scale)))).astype(BF16)
            g = jnp.exp(m_prev - mm) * scale
            ct = c_ref[d, h]
            lhs = jnp.concatenate([s, (qh.astype(F32) * g).astype(BF16)], axis=1)
            res = _dot(lhs, jnp.concatenate([v_aug, ct.astype(BF16)], axis=0))
            floor = jnp.exp(-(b_blk[:, h:h + 1] + mm))
            hout = res[:, :dv] / jnp.maximum(jnp.abs(res[:, dv:dv + 1]), floor)
            o_ref[:, h * dv:(h + 1) * dv] = hout.astype(o_ref.dtype)
            kw = (kh.astype(F32) * ws_blk[:, h:h + 1]).astype(BF16)
            c_ref[d, h] = gs_blk[:, h:h + 1] * ct + _dot_tn(kw, v_aug)
        m_ref[d] = m_new_blk


def mlstm_scan(z, gates, gate_b, *, rows):
    b, seq, ctx_len = rows.batch, rows.seq, rows.ctx_len
    L = ML_CHUNK
    assert ctx_len == L and seq % L == 0
    nlc = seq // L
    hq, hv = ML_HEADS * ML_DQK, ML_HEADS * ML_DV

    def rb(d):
        def index(bi, c):
            lat = bi * nlc + (c - 1 if d == 0 else nlc - c)
            return jnp.where(c == 0, rows.n_lat // L + bi, lat)
        return index

    in_specs, args = [], []
    for d in range(2):
        r = rb(d)
        in_specs += [
            pl.BlockSpec((L, hq), lambda bi, c, r=r: (r(bi, c), 0)),
            pl.BlockSpec((L, hq), lambda bi, c, r=r: (r(bi, c), 1)),
            pl.BlockSpec((L, hv), lambda bi, c, r=r: (r(bi, c), 1)),
            pl.BlockSpec((L, LANE), lambda bi, c, r=r, d=d: (r(bi, c), 2 * d)),
            pl.BlockSpec((L, LANE), lambda bi, c, r=r, d=d: (r(bi, c), 2 * d + 1)),
            pl.BlockSpec((1, LANE), lambda bi, c, d=d: (0, 2 * d)),
            pl.BlockSpec((1, LANE), lambda bi, c, d=d: (0, 2 * d + 1)),
        ]
        args += [z, z, z, gates, gates, gate_b, gate_b]
    return pl.pallas_call(
        _mlstm_kernel,
        grid=(b, nlc + 1),
        in_specs=in_specs,
        out_specs=[pl.BlockSpec((L, hv), lambda bi, c, r=rb(d): (r(bi, c), 0)) for d in range(2)],
        out_shape=[jax.ShapeDtypeStruct((rows.n_all, hv), BF16)] * 2,
        scratch_shapes=[
            pltpu.VMEM((2, ML_HEADS, ML_DQK, ML_DV + LANE), F32),
            pltpu.VMEM((2, 1, LANE), F32),
        ],
        compiler_params=_params(("parallel", "arbitrary")),
        name="mlstm_scan",
    )(*args)


def _mla_kernel(q_ref, kvl_ref, kvx_ref, krl_ref, krx_ref, o_ref, kk_ref, vt_ref, *, seq, sub, ck):
    @pl.when(pl.program_id(2) == 0)
    def _():
        kk_ref[:seq, :LANE] = kvl_ref[:, :LANE]
        kk_ref[:seq, LANE:] = krl_ref[...]
        kk_ref[seq:, :LANE] = kvx_ref[:, :LANE]
        kk_ref[seq:, LANE:] = krx_ref[...]
        vt_ref[:LANE, :seq] = kvl_ref[:, LANE:].astype(F32).T.astype(BF16)
        vt_ref[:LANE, seq:] = kvx_ref[:, LANE:].astype(F32).T.astype(BF16)
        vt_ref[LANE:, :] = jnp.ones((vt_ref.shape[0] - LANE, vt_ref.shape[1]), BF16)

    nsub = q_ref.shape[0] // sub
    n_keys = kk_ref.shape[0]
    bounds = [(k0, min(k0 + ck, n_keys)) for k0 in range(0, n_keys, ck)]
    nck = len(bounds)
    qs = [q_ref[s * sub:(s + 1) * sub, :] for s in range(nsub)]

    def scores(s, c):
        return _dot_nt(kk_ref[bounds[c][0]:bounds[c][1], :], qs[s])

    m = [None] * nsub
    acc = [None] * nsub
    st = [None] * nsub

    def absorb(s, c):
        cm = jnp.max(st[s], axis=0, keepdims=True)
        m_new = cm if c == 0 else jnp.maximum(m[s], cm)
        pv = _dot(vt_ref[:, bounds[c][0]:bounds[c][1]], jnp.exp2(st[s] - m_new).astype(BF16))
        acc[s] = pv if c == 0 else acc[s] * jnp.exp2(m[s] - m_new) + pv
        m[s] = m_new

    for k in range(-1, nck + nsub - 1):
        nxt = [scores(s, k - s + 1) if 0 <= k - s + 1 < nck else None for s in range(nsub)]
        for s in range(nsub):
            if 0 <= k - s < nck:
                absorb(s, k - s)
        for s in range(nsub):
            if nxt[s] is not None:
                st[s] = nxt[s]
    for s in range(nsub):
        o = acc[s][:LANE] / acc[s][LANE:LANE + 1]
        o_ref[s * sub:(s + 1) * sub, :] = o.T.astype(o_ref.dtype)


def mla_attention(q, kv, z, *, rows, tq=4096, sub=256, ck=1024):
    b, seq, ctx_len = rows.batch, rows.seq, rows.ctx_len
    tq = min(tq, seq)
    nq = seq // tq
    kr_col = z.shape[1] // LANE - 1
    ctx0 = rows.n_lat // ctx_len
    n_keys = seq + ctx_len
    ones_rows = 16
    return pl.pallas_call(
        functools.partial(_mla_kernel, seq=seq, sub=sub, ck=ck),
        grid=(b, MLA_HEADS, nq),
        in_specs=[
            pl.BlockSpec((tq, 2 * LANE), lambda bi, h, i: (bi * nq + i, h)),
            pl.BlockSpec((seq, 2 * LANE), lambda bi, h, i: (bi, h)),
            pl.BlockSpec((ctx_len, 2 * LANE), lambda bi, h, i: (ctx0 + bi, h)),
            pl.BlockSpec((seq, LANE), lambda bi, h, i: (bi, kr_col)),
            pl.BlockSpec((ctx_len, LANE), lambda bi, h, i: (ctx0 + bi, kr_col)),
        ],
        out_specs=pl.BlockSpec((tq, LANE), lambda bi, h, i: (bi * nq + i, h)),
        out_shape=jax.ShapeDtypeStruct((rows.n_lat, MLA_HEADS * MLA_V), BF16),
        scratch_shapes=[pltpu.VMEM((n_keys, 2 * LANE), BF16), pltpu.VMEM((LANE + ones_rows, n_keys), BF16)],
        compiler_params=_params(("parallel", "parallel", "arbitrary")),
        name="mla_attention",
    )(q, kv, kv, z, z)


def _hy_filter_kernel(ft_ref, w1_ref, b1_ref, w2_ref, b2_ref, w3_ref, b3_ref, fr_ref, fr3_ref, w4_ref, b4_ref,
                      dec_ref, o_ref, a_ref, *, back_from):
    j = pl.program_id(0)
    fw = HY_FILTER_W
    hdot = functools.partial(jnp.dot, preferred_element_type=F32, precision=HIGHEST)

    @pl.when(j == 0)
    def _():
        fr = fr_ref[...]
        a = jnp.sin(fr * (hdot(ft_ref[...], w1_ref[...]) + b1_ref[...]))
        a = jnp.sin(fr * (hdot(a, w2_ref[...]) + b2_ref[...]))
        a3 = jnp.sin(fr3_ref[...] * (hdot(a, w3_ref[...]) + b3_ref[...]))
        hi = a3.astype(BF16)
        lo = (a3 - hi.astype(F32)).astype(BF16)
        lane = lax.broadcasted_iota(jnp.int32, a3.shape, 1)
        a_ref[...] = jnp.where((lane >= fw) & (lane < 2 * fw), lo, hi)

    w4 = w4_ref[...]
    w_hi = w4.astype(BF16)
    w_lo = (w4 - w_hi.astype(F32)).astype(BF16)
    rhs = jnp.concatenate([w_hi, w_hi, w_lo, jnp.zeros_like(w_hi)], axis=0)
    t = ft_ref[:, 0:1]
    filt = (_dot(a_ref[...], rhs) + b4_ref[...]) * jnp.exp(-t * jnp.abs(dec_ref[...]))
    row = lax.broadcasted_iota(jnp.int32, filt.shape, 0)
    o_ref[...] = jnp.where(jnp.logical_and(row == 0, j >= back_from), 0.0, filt).astype(o_ref.dtype)


def hyena_filters(seq, w1, b1, w2, b2, w3, b3, w4, b4, freq, decay, tn=512):
    t = np.linspace(0.0, 1.0, seq)[:, None]
    w = (2.0 * math.pi / seq) * np.arange(seq)[:, None]
    bands = np.linspace(1e-4, HY_BANDS - 1, HY_BANDS)[None, :]
    feats = np.zeros((seq, LANE))
    feats[:, :1 + 2 * HY_BANDS] = np.concatenate([t, np.cos(bands * w), -np.sin(bands * w)], axis=-1)
    n = w4.shape[1]
    fw = HY_FILTER_W
    w1p = jnp.zeros((LANE, fw), F32).at[:w1.shape[0]].set(w1)
    row = lambda a: a.reshape(1, -1)
    tile3 = lambda a: jnp.concatenate([a, a, a, jnp.zeros_like(a)], axis=-1)
    full = lambda shape: pl.BlockSpec(shape, lambda j: (0, 0))
    return pl.pallas_call(
        functools.partial(_hy_filter_kernel, back_from=(n // 2) // tn),
        grid=(n // tn,),
        in_specs=[full((seq, LANE)), full((LANE, fw)), full((1, fw)), full((fw, fw)), full((1, fw)),
                  full((fw, 4 * fw)), full((1, 4 * fw)), full((1, fw)), full((1, 4 * fw)),
                  pl.BlockSpec((fw, tn), lambda j: (0, j)),
                  pl.BlockSpec((1, tn), lambda j: (0, j)),
                  pl.BlockSpec((1, tn), lambda j: (0, j))],
        out_specs=pl.BlockSpec((seq, tn), lambda j: (0, j)),
        out_shape=jax.ShapeDtypeStruct((seq, n), BF16),
        scratch_shapes=[pltpu.VMEM((seq, 4 * fw), BF16)],
        compiler_params=_params(("arbitrary",)),
        name="hyena_filters",
    )(jnp.asarray(feats, F32), w1p, row(b1), w2, row(b2), tile3(w3), tile3(row(b3)), row(freq), tile3(row(freq)),
      w4, row(b4), row(decay))


def _dft_mats(seq, p):
    n = 2 * seq
    q = n // p
    k1n = q // 2 + 1
    hi, k1 = np.arange(q // 2), np.arange(k1n)
    th = 2 * np.pi * np.outer(k1, hi) / q
    ma = np.zeros((2 * k1n, q // 2))
    ma[0::2], ma[1::2] = np.cos(th), -np.sin(th)
    lo = np.arange(p)
    wf = np.zeros((k1n, 2 * p, 2 * p))
    wi = np.zeros((k1n, 2 * p, 2 * p))
    for k in k1:
        ph = -2 * np.pi * (np.outer(lo, lo) / p + k * lo[None, :] / n)
        er, ei = np.cos(ph), np.sin(ph)
        wf[k] = np.block([[er, -ei], [ei, er]])
        wi[k] = np.block([[er.T, ei.T], [-ei.T, er.T]])
    c = np.full(k1n, 2.0)
    c[0] = c[-1] = 1.0
    th2 = 2 * np.pi * np.outer(hi, k1) / q
    md = np.zeros((q // 2, 2 * k1n))
    md[:, 0::2], md[:, 1::2] = c * np.cos(th2) / n, -c * np.sin(th2) / n
    as_bf16 = lambda a: jnp.asarray(a, F32).astype(BF16)
    return ma, as_bf16(wf), as_bf16(wi), md


HY_HALO = 16


def _hy_in_kernel(h_ref, hp_ref, hn_ref, g_ref, sh_ref, sc_ref, w_ref, cw_ref, cb_ref, o_ref, u_ref, inv_ref,
                  invh_ref, *, per_seq):
    i = pl.program_id(0)
    tm = h_ref.shape[0]

    @pl.when(pl.program_id(1) == 0)
    def _():
        mod = lambda x: x * (1.0 + sc_ref[0]) + sh_ref[0]
        u_ref[HY_HALO:HY_HALO + tm, :] = mod(_rms_ref(h_ref, inv_ref, g_ref[...])).astype(BF16)
        keep_prev = (i % per_seq != 0).astype(F32)
        keep_next = (i % per_seq != per_seq - 1).astype(F32)
        u_ref[:HY_HALO, :] = (mod(_rms_ref(hp_ref, invh_ref, g_ref[...])) * keep_prev).astype(BF16)
        u_ref[HY_HALO + tm:, :] = (mod(_rms_ref(hn_ref, invh_ref, g_ref[...])) * keep_next).astype(BF16)

    acc = _dot(u_ref[...], w_ref[...])
    n = acc.shape[0]
    mid = slice(HY_HALO, HY_HALO + tm)
    prev = pltpu.roll(acc, 1, 0)[mid]
    nxt = pltpu.roll(acc, n - 1, 0)[mid]
    cw = cw_ref[...]
    o_ref[...] = (prev * cw[0:1] + acc[mid] * cw[1:2] + nxt * cw[2:3] + cb_ref[...]).astype(o_ref.dtype)


def hyena_in_conv(h, w, conv_w, conv_b, mods, gain, *, rows, tm, tn, layer):
    k, n = w.shape
    n_rows = rows.n_lat
    per_seq = rows.seq // tm
    halo_per_blk, n_halo = tm // HY_HALO, n_rows // HY_HALO
    return pl.pallas_call(
        functools.partial(_hy_in_kernel, per_seq=per_seq),
        grid=(n_rows // tm, n // tn),
        in_specs=[
            pl.BlockSpec((tm, k), lambda i, j: (i, 0)),
            pl.BlockSpec((HY_HALO, k), lambda i, j: (jnp.maximum(i * halo_per_blk - 1, 0), 0)),
            pl.BlockSpec((HY_HALO, k), lambda i, j: (jnp.minimum((i + 1) * halo_per_blk, n_halo - 1), 0)),
            pl.BlockSpec((1, k), lambda i, j: (0, 0)),
            pl.BlockSpec((1, 1, k), rows.mod_index(layer, 0, tm)),
            pl.BlockSpec((1, 1, k), rows.mod_index(layer, 1, tm)),
            pl.BlockSpec((k, tn), lambda i, j: (0, j)),
            pl.BlockSpec((HY_SHORT, tn), lambda i, j: (0, j)),
            pl.BlockSpec((1, tn), lambda i, j: (0, j)),
        ],
        out_specs=pl.BlockSpec((tm, tn), lambda i, j: (i, j)),
        out_shape=jax.ShapeDtypeStruct((n_rows, n), BF16),
        scratch_shapes=[pltpu.VMEM((tm + 2 * HY_HALO, k), BF16), pltpu.VMEM((tm, 1), F32),
                        pltpu.VMEM((HY_HALO, 1), F32)],
        compiler_params=_params(("parallel", "arbitrary")),
        name="hyena_in",
    )(h, h, h, gain.reshape(1, k), mods, mods, w, conv_w, conv_b.reshape(1, n))


def _hy_stage_a_kernel(x_ref, m_ref, o_ref):
    qh, g, wc = x_ref.shape
    x = x_ref[...].reshape(qh * g, wc).astype(BF16)
    o_ref[...] = _dot(m_ref[...], x).reshape(o_ref.shape).astype(o_ref.dtype)


def hyena_stage_a(x, mak, *, p, width=2048, col_blk=0, channels=None):
    bx, seq, c_all = x.shape
    c = c_all if channels is None else channels
    g = HY_G
    qh = seq // p
    mo = mak.shape[0] // g
    x5 = x.reshape(bx, qh, p // g, g, c_all)
    return pl.pallas_call(
        _hy_stage_a_kernel,
        grid=(bx, p // g, c // width),
        in_specs=[pl.BlockSpec((None, qh, None, g, width), lambda bi, l, j: (bi, 0, l, 0, col_blk + j)),
                  pl.BlockSpec(mak.shape, lambda bi, l, j: (0, 0))],
        out_specs=pl.BlockSpec((None, mo, g, width), lambda bi, l, j: (bi, 0, l, j)),
        out_shape=jax.ShapeDtypeStruct((bx, mo, p, c), BF16),
        compiler_params=_params(("parallel", "parallel", "parallel")),
        name="hyena_stage_a",
    )(x5, mak)


def _hy_spec_kernel(a0_ref, a1_ref, wf_ref, o_ref):
    p = o_ref.shape[2]
    cb = o_ref.shape[3]
    wf = wf_ref[0]
    x0 = _dot(wf, a0_ref[0].reshape(2 * p, cb))
    x1 = _dot(wf, a1_ref[0].reshape(2 * p, cb))
    o_ref[0, 0] = x0[:p] + x1[:p]
    o_ref[0, 1] = x0[p:] - x1[p:]


def hyena_spectrum(af, wf, *, d, cb=2048):
    k1n, _, p, _ = af.shape
    nblk = (HY_ORDER * d) // cb
    return pl.pallas_call(
        _hy_spec_kernel,
        grid=(k1n, nblk),
        in_specs=[pl.BlockSpec((1, 2, p, cb), lambda k, j: (k, 0, 0, j)),
                  pl.BlockSpec((1, 2, p, cb), lambda k, j: (k, 0, 0, nblk + j)),
                  pl.BlockSpec((1, 2 * p, 2 * p), lambda k, j: (k, 0, 0))],
        out_specs=pl.BlockSpec((1, 2, p, cb), lambda k, j: (k, 0, 0, j)),
        out_shape=jax.ShapeDtypeStruct((k1n, 2, p, HY_ORDER * d), F32),
        compiler_params=_params(("parallel", "parallel")),
        name="hyena_spectrum",
    )(af, af, wf)


def _hy_mid_kernel(a_ref, wf_ref, wi_ref, h_ref, o_ref):
    b, _, p, cb = a_ref.shape
    hr, hi = h_ref[0], h_ref[1]
    for bi in range(b):
        x = _dot(wf_ref[...], a_ref[bi].reshape(2 * p, cb))
        xr, xi = x[:p], x[p:]
        y = jnp.concatenate([xr * hr - xi * hi, xr * hi + xi * hr], axis=0).astype(BF16)
        o_ref[bi] = _dot(wi_ref[...], y).reshape(2, p, cb).astype(o_ref.dtype)


def hyena_mid(a, wf, wi, spec, *, order, d, cb=1024):
    b, k1n, _, p, _ = a.shape
    nblk = d // cb
    data = pl.BlockSpec((b, None, 2, p, cb), lambda k, j: (0, k, 0, 0, j))
    mat = pl.BlockSpec((None, 2 * p, 2 * p), lambda k, j: (k, 0, 0))
    return pl.pallas_call(
        _hy_mid_kernel,
        grid=(k1n, nblk),
        in_specs=[data, mat, mat,
                  pl.BlockSpec((None, 2, p, cb), lambda k, j: (k, 0, 0, order * nblk + j))],
        out_specs=data,
        out_shape=jax.ShapeDtypeStruct(a.shape, BF16),
        compiler_params=_params(("parallel", "parallel")),
        name="hyena_mid",
    )(a, wf, wi, spec)


def _hy_stage_d_kernel(c_ref, m_ref, x_ref, z_ref, skip_ref, *rest, fused):
    ma_ref, o_ref, a_ref = rest if fused else (None, rest[0], None)
    mo, g, wc = c_ref.shape
    qh = x_ref.shape[0]
    conv = _dot(m_ref[...], c_ref[...].reshape(mo * g, wc))
    z = z_ref[...].reshape(qh * g, wc).astype(F32)
    x = x_ref[...].reshape(qh * g, wc).astype(F32)
    y = (x * (conv + z * skip_ref[...])).astype(o_ref.dtype)
    o_ref[...] = y.reshape(o_ref.shape)
    if fused:
        a_ref[...] = _dot(ma_ref[...], y).reshape(a_ref.shape).astype(a_ref.dtype)


def hyena_stage_d(c, mdk, xg, z, skip, *, p, mak=None, width=2048, xg_blk=0, z_blk=0):
    b, mo, _, d = c.shape
    g = HY_G
    seq = xg.shape[1]
    qh = seq // p
    fused = mak is not None
    v5 = lambda a: a.reshape(b, qh, p // g, g, a.shape[2])
    tblk = lambda off: pl.BlockSpec((None, qh, None, g, width), lambda bi, l, j: (bi, 0, l, 0, off + j))
    tspec = tblk(0)
    fspec = pl.BlockSpec((None, mo, g, width), lambda bi, l, j: (bi, 0, l, j))
    in_specs = [fspec, pl.BlockSpec(mdk.shape, lambda bi, l, j: (0, 0)), tblk(xg_blk), tblk(z_blk),
                pl.BlockSpec((1, width), lambda bi, l, j: (0, j))]
    args = [c, mdk, v5(xg), v5(z), skip.reshape(1, d)]
    out_specs, out_shape = [tspec], [jax.ShapeDtypeStruct((b, qh, p // g, g, d), BF16)]
    if fused:
        in_specs.append(pl.BlockSpec(mak.shape, lambda bi, l, j: (0, 0)))
        args.append(mak)
        out_specs.append(fspec)
        out_shape.append(jax.ShapeDtypeStruct(c.shape, BF16))
    outs = pl.pallas_call(
        functools.partial(_hy_stage_d_kernel, fused=fused),
        grid=(b, p // g, d // width),
        in_specs=in_specs,
        out_specs=out_specs,
        out_shape=out_shape,
        compiler_params=_params(("parallel", "parallel", "parallel")),
        name="hyena_stage_d",
    )(*args)
    y = outs[0].reshape(b, seq, d)
    return (y, outs[1]) if fused else y


def hyena_long_convs(x3, filt, skip, *, rows, width=2048):
    b, seq = rows.batch, rows.seq
    d = x3.shape[1] // 3
    p = HY_P
    k1n = seq // p + 1
    ma, wf, wi, md = _dft_mats(seq, p)
    eye = np.eye(HY_G)
    as_bf16 = lambda a: jnp.asarray(a, F32).astype(BF16)
    mak, mdk = as_bf16(np.kron(ma, eye)), as_bf16(np.kron(md, eye))
    af = hyena_stage_a(filt.reshape(1, seq, filt.shape[1]), mak, p=p)
    spec = hyena_spectrum(af.reshape(k1n, 2, p, filt.shape[1]), wf, d=d)
    x3 = x3.reshape(b, seq, 3 * d)
    blk = d // width
    a = hyena_stage_a(x3, mak, p=p, width=width, col_blk=2 * blk, channels=d)
    c = hyena_mid(a.reshape(b, k1n, 2, p, d), wf, wi, spec, order=0, d=d)
    y, a = hyena_stage_d(c.reshape(a.shape), mdk, x3, x3, skip[0], p=p, mak=mak, width=width, xg_blk=0, z_blk=2 * blk)
    c = hyena_mid(a.reshape(b, k1n, 2, p, d), wf, wi, spec, order=1, d=d)
    y = hyena_stage_d(c.reshape(a.shape), mdk, x3, y, skip[1], p=p, width=width, xg_blk=blk, z_blk=0)
    return y.reshape(rows.n_lat, d)


def _mlstm_gate_weights(w_in, gate_b):
    h = ML_HEADS
    o4 = 2 * h * ML_DQK + 2 * h * ML_DV
    wg = w_in[:, o4:].reshape(-1, 2, 2, h)
    bg = gate_b.reshape(2, 2, h)
    w_out = jnp.zeros((w_in.shape[0], 4 * LANE), F32)
    b_out = jnp.zeros((1, 4 * LANE), F32)
    for d in range(2):
        for gate in range(2):
            lo = (2 * d + gate) * LANE
            w_out = w_out.at[:, lo:lo + h].set(wg[:, gate, d])
            b_out = b_out.at[0, lo:lo + h].set(bg[gate, d])
    return w_out, b_out


def _mla_weights(w_in, w_uq):
    r2 = MLA_Q_RANK + MLA_KV_RANK
    half = MLA_ROPE // 2

    def spread(w):
        z = jnp.zeros(w.shape[:-1] + (half,), w.dtype)
        return jnp.concatenate([w[..., :half], z, w[..., half:], z], axis=-1)

    w_in_p = jnp.concatenate([w_in[:, :r2], spread(w_in[:, r2:])], axis=1)
    wq = w_uq.reshape(w_uq.shape[0], MLA_HEADS, MLA_NOPE + MLA_ROPE)
    wq_p = jnp.concatenate([wq[..., :MLA_NOPE], spread(wq[..., MLA_NOPE:])], axis=-1)
    wq_p = wq_p * ((MLA_NOPE + MLA_ROPE) ** -0.5 * math.log2(math.e))
    return w_in_p, wq_p.reshape(w_uq.shape[0], MLA_HEADS * 2 * LANE)


def kernel(x, c, ctx, c_ctx, ada_w, ada_b, norm_g, mlp_w1, mlp_w2, swa_w_qkv, swa_sink, swa_w_o, ml_w_in, ml_gate_b, ml_head_g, ml_w_o, mla_w_in, mla_q_g, mla_kv_g, mla_w_uq, mla_w_ukv, mla_w_o, hy_w_in, hy_conv_w, hy_conv_b, hy_f_w1, hy_f_b1, hy_f_w2, hy_f_b2, hy_f_w3, hy_f_b3, hy_f_w4, hy_f_b4, hy_f_freq, hy_decay, hy_skip, hy_w_o):
    b, seq, d = x.shape
    ctx_len = ctx.shape[1]
    depth = ada_w.shape[0]
    assert depth == 4 and b < 8
    rows = Rows(b, seq, ctx_len)
    tm = 512
    tm_lin = 1024 if rows.n_lat % 1024 == 0 and rows.n_ctx % 1024 == 0 else 512
    bf = lambda w: w.astype(BF16)
    mlp_w1b, mlp_w2b = bf(mlp_w1), bf(mlp_w2)

    cond = jnp.zeros((8, d), F32).at[:b].set(c).at[b].set(c_ctx)
    mods = ada_mods(cond, ada_w, ada_b).reshape(depth * 8, 1, N_MOD * d)

    common = dict(rows=rows, tm=tm_lin, mods=mods)

    def finish(h, o_args, w_o, layer, n_rows, mode="plain", h2=None):
        h = outproj_residual(o_args, bf(w_o), h, mods, norm_g[layer, 1], rows=rows, n_rows=n_rows, tm=tm,
                             layer=layer, mode=mode, h2=h2, name=f"outproj{layer}")
        return mlp_residual(h, mlp_w1b, mlp_w2b, mods, norm_g[layer, 2], norm_g[layer, 3],
                            rows=rows, n_rows=n_rows, tm=tm, tf=1024, layer=layer)

    n_qk_groups = SWA_HEADS + SWA_KV_HEADS
    q_cols = SWA_HEADS * SWA_HEAD_DIM
    q_scale = SWA_HEAD_DIM ** -0.5 * math.log2(math.e)
    w_qkv = jnp.concatenate([swa_w_qkv[0][:, :q_cols] * q_scale, swa_w_qkv[0][:, q_cols:]], axis=1)
    x_rows, ctx_rows = x.reshape(rows.n_lat, d), ctx.reshape(rows.n_ctx, d)
    qkv = linear(x_rows, bf(w_qkv), x2=ctx_rows, n_rows=rows.n_all, tn=512, out_dtype=BF16, prologue="norm_mod",
                 gain=norm_g[0, 0], layer=0, rope_tabs=_rope_tables(seq, SWA_HEAD_DIM), rope_pattern=(True,) * 4,
                 rope_jmax=n_qk_groups // 4, name="swa_qkv", **common)
    o = swa_attention(qkv, swa_sink[0], rows=rows)
    h = finish(x_rows, o, swa_w_o[0], 0, rows.n_all, h2=ctx_rows)

    w_gate, b_gate = _mlstm_gate_weights(ml_w_in[0], ml_gate_b[0])
    z = linear(h, bf(ml_w_in[0]), n_rows=rows.n_all, tn=1536, out_dtype=BF16, prologue="norm_mod",
               gain=norm_g[1, 0], layer=1, n_out=2 * ML_HEADS * (ML_DQK + ML_DV), name="mlstm_in", **common)
    gates = linear(h, bf(w_gate), n_rows=rows.n_all, tn=4 * LANE, out_dtype=F32, prologue="norm_mod",
                   gain=norm_g[1, 0], layer=1, name="mlstm_gates", **common)
    h_fwd, h_bwd = mlstm_scan(z, gates, b_gate, rows=rows)
    h = finish(h, (h_fwd, h_bwd, z, ml_head_g[0]), ml_w_o[0], 1, rows.n_all, mode="mlstm")

    w_in_p, w_uq_p = _mla_weights(mla_w_in[0], mla_w_uq[0])
    rope_mla = _rope_tables(seq, MLA_ROPE)
    zc = linear(h, bf(w_in_p), n_rows=rows.n_all, tn=w_in_p.shape[1], out_dtype=BF16, prologue="norm_mod",
                gain=norm_g[2, 0], layer=2, rope_tabs=rope_mla, rope_pattern=(False,) * 8 + (True,),
                name="mla_in", **common)
    q = linear(zc, bf(w_uq_p), n_rows=rows.n_lat, tn=2048, out_dtype=BF16, x_cols=0, prologue="norm",
               gain=mla_q_g[0], rope_tabs=rope_mla, rope_pattern=(False, True) * 8, name="mla_q",
               rows=rows, tm=tm_lin)
    kv = linear(zc, bf(mla_w_ukv[0]), n_rows=rows.n_all, tn=2048, out_dtype=BF16, x_cols=1, prologue="norm",
                gain=mla_kv_g[0], name="mla_kv", rows=rows, tm=tm_lin)
    o = mla_attention(q, kv, zc, rows=rows)
    h = finish(h, o, mla_w_o[0], 2, rows.n_lat)

    x3 = hyena_in_conv(h, bf(hy_w_in[0]), hy_conv_w[0], hy_conv_b[0], mods, norm_g[3, 0], rows=rows, tm=tm_lin,
                       tn=1536, layer=3)
    filt = hyena_filters(seq, hy_f_w1[0], hy_f_b1[0], hy_f_w2[0], hy_f_b2[0], hy_f_w3[0], hy_f_b3[0],
                         hy_f_w4[0], hy_f_b4[0], hy_f_freq[0], hy_decay[0])
    y = hyena_long_convs(x3, filt, hy_skip[0], rows=rows)
    h = finish(h, y, hy_w_o[0], 3, rows.n_lat)
    return h.reshape(b, seq, d)
```

```python
import functools
import math

import numpy as np
import jax
import jax.numpy as jnp
from jax import lax
from jax.experimental import pallas as pl
from jax.experimental.pallas import tpu as pltpu

F32, BF16 = jnp.float32, jnp.bfloat16
HIGHEST = lax.Precision.HIGHEST

RMS_EPS = 1e-6
ROPE_THETA = 10000.0
GRID_W = 64
N_MOD = 6
LANE = 128

SWA_HEADS, SWA_KV_HEADS, SWA_HEAD_DIM, SWA_WINDOW, SWA_BLOCK = 16, 4, 128, 128, 128
ML_HEADS, ML_DQK, ML_DV = 8, 128, 256
ML_CHUNK = 256
MLA_HEADS, MLA_Q_RANK, MLA_KV_RANK, MLA_NOPE, MLA_ROPE, MLA_V = 16, 512, 512, 128, 64, 128
HY_ORDER, HY_BANDS, HY_FILTER_W, HY_SHORT = 2, 16, 64, 3
HY_P = 256
HY_G = 16
HY_LG = 2

VMEM_LIMIT = 48 * 1024 * 1024


def _params(sem):
    return pltpu.CompilerParams(dimension_semantics=sem, vmem_limit_bytes=VMEM_LIMIT)


def _dot(a, b):
    return jnp.dot(a, b, preferred_element_type=F32)


def _dot_nt(a, b):
    return lax.dot_general(a, b, (((1,), (1,)), ((), ())), preferred_element_type=F32)


def _dot_tn(a, b):
    return lax.dot_general(a, b, (((0,), (0,)), ((), ())), preferred_element_type=F32)


def _rms(x, g):
    return x * lax.rsqrt(jnp.mean(x * x, axis=-1, keepdims=True) + RMS_EPS) * g


def _rms_ref(x_ref, inv_ref, g):
    x = x_ref[...].astype(F32)
    inv_ref[...] = lax.rsqrt(jnp.mean(x * x, axis=-1, keepdims=True) + RMS_EPS)
    return x_ref[...].astype(F32) * inv_ref[...] * g


def _ada_kernel(s_ref, w_ref, b_ref, o_ref):
    s = s_ref[...]
    s = s * jax.nn.sigmoid(s)
    hi = s.astype(BF16)
    lo = (s - hi.astype(F32)).astype(BF16)
    w = w_ref[0]
    w_hi = w.astype(BF16)
    w_lo = (w - w_hi.astype(F32)).astype(BF16)
    r = _dot(jnp.concatenate([hi, lo], axis=0), w_hi)
    n = s.shape[0]
    o_ref[0] = r[:n] + r[n:] + _dot(hi, w_lo) + b_ref[0]


def ada_mods(cond, ada_w, ada_b, tn=1536):
    depth, d, n = ada_w.shape
    rows = cond.shape[0]
    return pl.pallas_call(
        _ada_kernel,
        grid=(depth, n // tn),
        in_specs=[
            pl.BlockSpec((rows, d), lambda l, j: (0, 0)),
            pl.BlockSpec((1, d, tn), lambda l, j: (l, 0, j)),
            pl.BlockSpec((1, 1, tn), lambda l, j: (l, 0, j)),
        ],
        out_specs=pl.BlockSpec((1, rows, tn), lambda l, j: (l, 0, j)),
        out_shape=jax.ShapeDtypeStruct((depth, rows, n), F32),
        compiler_params=_params(("parallel", "parallel")),
        name="ada_mods",
    )(cond, ada_w, ada_b.reshape(depth, 1, n))


class Rows:
    def __init__(self, batch, seq, ctx_len):
        self.batch, self.seq, self.ctx_len = batch, seq, ctx_len
        self.n_lat = batch * seq
        self.n_ctx = batch * ctx_len
        self.n_all = self.n_lat + self.n_ctx

    def mod_index(self, layer, k, tm):
        lat_blocks, per_batch = self.n_lat // tm, self.seq // tm

        def index(i, *_):
            b = jnp.where(i < lat_blocks, i // per_batch, self.batch)
            return (layer * 8 + b, 0, k)

        return index


def _linear_kernel(*refs, prologue, rope, rope_pattern, n_lat_blocks, dual, single):
    it = iter(refs)
    x_ref = next(it)
    x2_ref = next(it) if dual else None
    g_ref = next(it) if prologue in ("norm", "norm_mod") else None
    sh_ref = next(it) if prologue == "norm_mod" else None
    sc_ref = next(it) if prologue == "norm_mod" else None
    w_ref = next(it)
    cc_ref = next(it) if rope else None
    ss_ref = next(it) if rope else None
    o_ref = next(it)
    u_ref = next(it)
    inv_ref = next(it)
    first = pl.program_id(1) == 0

    def prologue_from(src_ref):
        if prologue in ("norm", "norm_mod"):
            x = _rms_ref(src_ref, inv_ref, g_ref[...])
        else:
            x = src_ref[...].astype(F32)
        if prologue == "norm_mod":
            x = x * (1.0 + sc_ref[0]) + sh_ref[0]
        u_ref[...] = x.astype(BF16)

    if dual:
        is_lat = pl.program_id(0) < n_lat_blocks
        pl.when(jnp.logical_and(first, is_lat))(functools.partial(prologue_from, x_ref))
        pl.when(jnp.logical_and(first, jnp.logical_not(is_lat)))(functools.partial(prologue_from, x2_ref))
    elif single:
        prologue_from(x_ref)
    else:
        pl.when(first)(functools.partial(prologue_from, x_ref))

    acc = _dot(u_ref[...], w_ref[...])

    def plain():
        o_ref[...] = acc.astype(o_ref.dtype)

    def roped():
        cc, ss = cc_ref[...], ss_ref[...]
        segs = []
        for gi, on in enumerate(rope_pattern):
            seg = acc[:, gi * LANE:(gi + 1) * LANE]
            if on:
                seg = seg * cc + pltpu.roll(seg, LANE // 2, 1) * ss
            segs.append(seg)
        o_ref[...] = jnp.concatenate(segs, axis=1).astype(o_ref.dtype)

    if rope:
        roped()
    else:
        plain()


def linear(x, w, *, rows, n_rows, tm, tn, out_dtype, x_cols=None, prologue="none", gain=None,
           mods=None, layer=0, mod_k=(0, 1), rope_tabs=None, rope_pattern=(), rope_jmax=1 << 30,
           n_out=None, x2=None, name="linear"):
    k, n = w.shape
    n = n if n_out is None else n_out
    xc = 0 if x_cols is None else x_cols
    rope = rope_tabs is not None
    dual = x2 is not None
    n_lat_blocks = rows.n_lat // tm
    if dual:
        in_specs = [pl.BlockSpec((tm, k), lambda i, j: (jnp.minimum(i, n_lat_blocks - 1), 0)),
                    pl.BlockSpec((tm, k), lambda i, j: (jnp.maximum(i - n_lat_blocks, 0), 0),
                                 pipeline_mode=pl.Buffered(1))]
        args = [x, x2]
    else:
        in_specs = [pl.BlockSpec((tm, k), lambda i, j: (i, xc))]
        args = [x]
    if prologue in ("norm", "norm_mod"):
        in_specs.append(pl.BlockSpec((1, k), lambda i, j: (0, 0)))
        args.append(gain.reshape(1, k))
    if prologue == "norm_mod":
        for mk in mod_k:
            in_specs.append(pl.BlockSpec((1, 1, k), rows.mod_index(layer, mk, tm)))
            args.append(mods)
    in_specs.append(pl.BlockSpec((k, tn), lambda i, j: (0, j)))
    args.append(w)
    if rope:
        per_seq = rows.seq // tm

        def tab_index(i, j):
            rotate = jnp.logical_and(i < n_lat_blocks, j < rope_jmax)
            return (jnp.where(rotate, 0, 1), i % per_seq, 0)

        for t, fill in zip(rope_tabs, (1.0, 0.0)):
            in_specs.append(pl.BlockSpec((None, tm, LANE), tab_index))
            args.append(jnp.stack([t, jnp.full_like(t, fill)]))
    kern = functools.partial(_linear_kernel, prologue=prologue, rope=rope, rope_pattern=tuple(rope_pattern),
                             n_lat_blocks=n_lat_blocks, dual=dual, single=(n == tn))
    return pl.pallas_call(
        kern,
        grid=(n_rows // tm, n // tn),
        in_specs=in_specs,
        out_specs=pl.BlockSpec((tm, tn), lambda i, j: (i, j)),
        out_shape=jax.ShapeDtypeStruct((n_rows, n), out_dtype),
        scratch_shapes=[pltpu.VMEM((tm, k), BF16), pltpu.VMEM((tm, 1), F32)],
        compiler_params=_params(("parallel", "arbitrary")),
        name=name,
    )(*args)


def _outproj_kernel(*refs, mode, n_lat_blocks):
    it = iter(refs)
    if mode == "mlstm":
        hf_ref, hb_ref, og_ref, hg_ref = next(it), next(it), next(it), next(it)
    else:
        o_ref_in = next(it)
    w_ref, h_ref = next(it), next(it)
    h2_ref = next(it) if n_lat_blocks is not None else None
    gate_ref, g_ref, out_ref = next(it), next(it), next(it)
    y_ref, inv_ref = next(it), next(it)
    if mode == "mlstm":
        hs = hf_ref[...].astype(F32) + hb_ref[...].astype(F32)
        og = jax.nn.sigmoid(og_ref[...].astype(F32))
        hg = hg_ref[...]
        parts = []
        for h in range(ML_HEADS):
            sl = slice(h * ML_DV, (h + 1) * ML_DV)
            parts.append((_rms(hs[:, sl], hg[:, sl]) * og[:, sl]).astype(BF16))
        o = jnp.concatenate(parts, axis=1)
    else:
        o = o_ref_in[...]
    y_ref[...] = _dot(o, w_ref[...])
    upd = gate_ref[0] * _rms_ref(y_ref, inv_ref, g_ref[...])
    if h2_ref is None:
        out_ref[...] = h_ref[...] + upd
    else:
        out_ref[...] = jnp.where(pl.program_id(0) < n_lat_blocks, h_ref[...], h2_ref[...]) + upd


def outproj_residual(o_args, w, h, mods, gain, *, rows, n_rows, tm, layer, mode="plain", h2=None, name="outproj"):
    k, d = w.shape
    n_lat_blocks = rows.n_lat // tm if h2 is not None else None
    if mode == "mlstm":
        h_fwd, h_bwd, z, head_g = o_args
        in_specs = [
            pl.BlockSpec((tm, k), lambda i: (i, 0)),
            pl.BlockSpec((tm, k), lambda i: (i, 0)),
            pl.BlockSpec((tm, k), lambda i: (i, 2)),
            pl.BlockSpec((1, k), lambda i: (0, 0)),
        ]
        args = [h_fwd, h_bwd, z, head_g.reshape(1, k)]
    else:
        in_specs = [pl.BlockSpec((tm, k), lambda i: (i, 0))]
        args = [o_args]
    in_specs.append(pl.BlockSpec((k, d), lambda i: (0, 0)))
    args.append(w)
    if h2 is None:
        in_specs.append(pl.BlockSpec((tm, d), lambda i: (i, 0)))
        args.append(h)
    else:
        in_specs += [pl.BlockSpec((tm, d), lambda i: (jnp.minimum(i, n_lat_blocks - 1), 0)),
                     pl.BlockSpec((tm, d), lambda i: (jnp.maximum(i - n_lat_blocks, 0), 0))]
        args += [h, h2]
    in_specs += [
        pl.BlockSpec((1, 1, d), rows.mod_index(layer, 2, tm)),
        pl.BlockSpec((1, d), lambda i: (0, 0)),
    ]
    args += [mods, gain.reshape(1, d)]
    return pl.pallas_call(
        functools.partial(_outproj_kernel, mode=mode, n_lat_blocks=n_lat_blocks),
        grid=(n_rows // tm,),
        in_specs=in_specs,
        out_specs=pl.BlockSpec((tm, d), lambda i: (i, 0)),
        out_shape=jax.ShapeDtypeStruct((n_rows, d), F32),
        scratch_shapes=[pltpu.VMEM((tm, d), F32), pltpu.VMEM((tm, 1), F32)],
        compiler_params=_params(("parallel",)),
        name=name,
    )(*args)


def _mlp_kernel(h_ref, g2_ref, sh_ref, sc_ref, w1_ref, w2_ref, gate_ref, g3_ref, out_ref, v_ref, acc_ref,
                inv_ref):
    f = pl.program_id(1)

    @pl.when(f == 0)
    def _():
        v = _rms_ref(h_ref, inv_ref, g2_ref[...]) * (1.0 + sc_ref[0]) + sh_ref[0]
        v_ref[...] = v.astype(BF16)
        acc_ref[...] = jnp.zeros_like(acc_ref)

    a = jnp.maximum(_dot(v_ref[...], w1_ref[...]), 0.0)
    acc_ref[...] += _dot((a * a).astype(BF16), w2_ref[...])

    @pl.when(f == pl.num_programs(1) - 1)
    def _():
        out_ref[...] = h_ref[...] + gate_ref[0] * _rms_ref(acc_ref, inv_ref, g3_ref[...])


def mlp_residual(h, w1, w2, mods, g2, g3, *, rows, n_rows, tm, tf, layer):
    _, d, ff = w1.shape
    return pl.pallas_call(
        _mlp_kernel,
        grid=(n_rows // tm, ff // tf),
        in_specs=[
            pl.BlockSpec((tm, d), lambda i, f: (i, 0)),
            pl.BlockSpec((1, d), lambda i, f: (0, 0)),
            pl.BlockSpec((1, 1, d), rows.mod_index(layer, 3, tm)),
            pl.BlockSpec((1, 1, d), rows.mod_index(layer, 4, tm)),
            pl.BlockSpec((None, d, tf), lambda i, f: (layer, 0, f)),
            pl.BlockSpec((None, tf, d), lambda i, f: (layer, f, 0)),
            pl.BlockSpec((1, 1, d), rows.mod_index(layer, 5, tm)),
            pl.BlockSpec((1, d), lambda i, f: (0, 0)),
        ],
        out_specs=pl.BlockSpec((tm, d), lambda i, f: (i, 0)),
        out_shape=jax.ShapeDtypeStruct((n_rows, d), F32),
        scratch_shapes=[pltpu.VMEM((tm, d), BF16), pltpu.VMEM((tm, d), F32), pltpu.VMEM((tm, 1), F32)],
        compiler_params=_params(("parallel", "arbitrary")),
        name="mlp",
    )(h, g2.reshape(1, d), mods, mods, w1, w2, mods, g3.reshape(1, d))


def _rope_tables(seq, d_rot):
    pos = np.arange(seq)
    row, col = pos // GRID_W, pos % GRID_W
    n = d_rot // 4
    inv = ROPE_THETA ** (-np.arange(n, dtype=np.float64) / n)
    ang = np.concatenate([row[:, None] * inv, col[:, None] * inv], axis=-1)
    cos, sin = np.cos(ang), np.sin(ang)
    half = d_rot // 2
    cc = np.zeros((seq, LANE))
    ss = np.zeros((seq, LANE))
    cc[:, :half] = cos
    cc[:, LANE // 2:LANE // 2 + half] = cos
    ss[:, :half] = -sin
    ss[:, LANE // 2:LANE // 2 + half] = sin
    return jnp.asarray(cc, F32), jnp.asarray(ss, F32)


def _swa_kernel(sink_ref, q_ref, kp_ref, kc_ref, kn_ref, kx_ref, vp_ref, vc_ref, vn_ref, vx_ref, o_ref,
                *, nb, seq, ctx_len):
    n = pl.program_id(1)
    hd, grp, lb = SWA_HEAD_DIM, SWA_HEADS // SWA_KV_HEADS, SWA_BLOCK
    m_rows, n_loc = grp * lb, 3 * lb
    r = lax.broadcasted_iota(jnp.int32, (m_rows, n_loc + ctx_len), 0)
    c = lax.broadcasted_iota(jnp.int32, (m_rows, n_loc + ctx_len), 1)
    qpos = n * lb + (r & (lb - 1))
    kpos = (n - 1) * lb + c
    local_ok = (jnp.abs(kpos - qpos) <= SWA_WINDOW) & (kpos >= 0) & (kpos < seq) & (n < nb)
    valid = local_ok | (c >= n_loc)
    ones = jnp.ones((n_loc + ctx_len, hd), BF16)
    def scores(kv):
        ks = slice(kv * hd, (kv + 1) * hd)
        qg = jnp.concatenate([q_ref[:, (kv * grp + g) * hd:(kv * grp + g + 1) * hd] for g in range(grp)], axis=0)
        keys = jnp.concatenate([kp_ref[:, ks], kc_ref[:, ks], kn_ref[:, ks], kx_ref[:, ks]], axis=0)
        return jnp.where(valid, _dot_nt(qg, keys), -1e30)

    outs = []
    s_next = scores(0)
    for kv in range(SWA_KV_HEADS):
        s = s_next
        if kv + 1 < SWA_KV_HEADS:
            s_next = scores(kv + 1)
        ks = slice(kv * hd, (kv + 1) * hd)
        vals = jnp.concatenate([vp_ref[:, ks], vc_ref[:, ks], vn_ref[:, ks], vx_ref[:, ks]], axis=0)
        v_aug = jnp.concatenate([vals, ones], axis=1)
        for g in range(grp):
            sg = s[g * lb:(g + 1) * lb]
            snk = sink_ref[kv * grp + g] * math.log2(math.e)
            m = jnp.maximum(jnp.max(sg, axis=1, keepdims=True), snk)
            ov = _dot(jnp.exp2(sg - m).astype(BF16), v_aug)
            outs.append((ov[:, :hd] / (ov[:, hd:hd + 1] + jnp.exp2(snk - m))).astype(BF16))
    o_ref[...] = jnp.concatenate(outs, axis=1)


def swa_attention(qkv, sink, *, rows):
    b, seq, ctx_len = rows.batch, rows.seq, rows.ctx_len
    lb = SWA_BLOCK
    nb, ncb = seq // lb, ctx_len // lb
    lat_blocks = rows.n_lat // lb
    qw = SWA_HEADS * SWA_HEAD_DIM
    kw = SWA_KV_HEADS * SWA_HEAD_DIM
    kcol, vcol = qw // kw, qw // kw + 1

    def qidx(bi, n):
        return (jnp.where(n < nb, bi * nb + n, lat_blocks + bi * ncb + (n - nb)), 0)

    def kidx(off, col):
        return lambda bi, n: (bi * nb + jnp.clip(n + off, 0, nb - 1), col)

    def xidx(col):
        return lambda bi, n: (rows.n_lat // ctx_len + bi, col)

    kern = functools.partial(_swa_kernel, nb=nb, seq=seq, ctx_len=ctx_len)
    return pl.pallas_call(
        kern,
        grid=(b, nb + ncb),
        in_specs=[
            pl.BlockSpec(memory_space=pltpu.SMEM),
            pl.BlockSpec((lb, qw), qidx),
            pl.BlockSpec((lb, kw), kidx(-1, kcol)),
            pl.BlockSpec((lb, kw), kidx(0, kcol)),
            pl.BlockSpec((lb, kw), kidx(1, kcol)),
            pl.BlockSpec((ctx_len, kw), xidx(kcol)),
            pl.BlockSpec((lb, kw), kidx(-1, vcol)),
            pl.BlockSpec((lb, kw), kidx(0, vcol)),
            pl.BlockSpec((lb, kw), kidx(1, vcol)),
            pl.BlockSpec((ctx_len, kw), xidx(vcol)),
        ],
        out_specs=pl.BlockSpec((lb, qw), qidx),
        out_shape=jax.ShapeDtypeStruct((rows.n_all, qw), BF16),
        compiler_params=_params(("parallel", "parallel")),
        name="swa_attention",
    )(sink, qkv, qkv, qkv, qkv, qkv, qkv, qkv, qkv, qkv)


def _mlstm_kernel(*refs):
    ins, (of_ref, ob_ref, c_ref, m_ref) = refs[:14], refs[14:]
    c = pl.program_id(1)
    L, H, dqk, dv = ML_CHUNK, ML_HEADS, ML_DQK, ML_DV
    scale = dqk ** -0.5

    @pl.when(c == 0)
    def _():
        c_ref[...] = jnp.zeros_like(c_ref)
        m_ref[...] = jnp.zeros_like(m_ref)

    row = lax.broadcasted_iota(jnp.int32, (L, L), 0)
    col = lax.broadcasted_iota(jnp.int32, (L, L), 1)
    ones = jnp.ones((L, LANE), BF16)
    for d, o_ref in enumerate((of_ref, ob_ref)):
        q_ref, k_ref, v_ref, gi_ref, gf_ref, bi_ref, bf_ref = ins[7 * d:7 * d + 7]
        mask = (col <= row) if d == 0 else (col >= row)
        ones_mask = jnp.where(mask, 1.0, 0.0).astype(BF16)
        i_blk = gi_ref[...] + bi_ref[...]
        f_blk = jax.nn.log_sigmoid(gf_ref[...] + bf_ref[...])
        f_hi = f_blk.astype(BF16)
        r1 = f_blk - f_hi.astype(F32)
        f_mid = r1.astype(BF16)
        f_lo = (r1 - f_mid.astype(F32)).astype(BF16)
        b_blk = _dot(jnp.concatenate([ones_mask] * 3, axis=1), jnp.concatenate([f_hi, f_mid, f_lo], axis=0))
        b_end = jnp.sum(f_blk, axis=0, keepdims=True)
        e_rows = (i_blk - b_blk).T
        m_prev_blk = m_ref[d]
        dec = b_end - b_blk + i_blk
        m_new_blk = jnp.maximum(b_end + m_prev_blk, jnp.max(dec, axis=0, keepdims=True))
        ws_blk = jnp.exp(dec - m_new_blk)
        gs_blk = jnp.exp(b_end + m_prev_blk - m_new_blk)
        for h in range(H):
            qh = q_ref[:, h * dqk:(h + 1) * dqk]
            kh = k_ref[:, h * dqk:(h + 1) * dqk]
            v_aug = jnp.concatenate([v_ref[:, h * dv:(h + 1) * dv], ones], axis=1)
            e = jnp.where(mask, e_rows[h:h + 1, :], -jnp.inf)
            m_prev = m_prev_blk[:, h:h + 1]
            mm = jnp.maximum(m_prev, jnp.max(e, axis=1, keepdims=True))
            s = (_dot_nt(qh, kh) * jnp.exp(e - (mm - math.log(scale)))).astype(BF16)
            g = jnp.exp(m_prev - mm) * scale
            ct = c_ref[d, h]
            lhs = jnp.concatenate([s, (qh.astype(F32) * g).astype(BF16)], axis=1)
            res = _dot(lhs, jnp.concatenate([v_aug, ct.astype(BF16)], axis=0))
            floor = jnp.exp(-(b_blk[:, h:h + 1] + mm))
            hout = res[:, :dv] / jnp.maximum(jnp.abs(res[:, dv:dv + 1]), floor)
            o_ref[:, h * dv:(h + 1) * dv] = hout.astype(o_ref.dtype)
            kw = (kh.astype(F32) * ws_blk[:, h:h + 1]).astype(BF16)
            c_ref[d, h] = gs_blk[:, h:h + 1] * ct + _dot_tn(kw, v_aug)
        m_ref[d] = m_new_blk


def mlstm_scan(z, gates, gate_b, *, rows):
    b, seq, ctx_len = rows.batch, rows.seq, rows.ctx_len
    L = ML_CHUNK
    assert ctx_len == L and seq % L == 0
    nlc = seq // L
    hq, hv = ML_HEADS * ML_DQK, ML_HEADS * ML_DV

    def rb(d):
        def index(bi, c):
            lat = bi * nlc + (c - 1 if d == 0 else nlc - c)
            return jnp.where(c == 0, rows.n_lat // L + bi, lat)
        return index

    in_specs, args = [], []
    for d in range(2):
        r = rb(d)
        in_specs += [
            pl.BlockSpec((L, hq), lambda bi, c, r=r: (r(bi, c), 0)),
            pl.BlockSpec((L, hq), lambda bi, c, r=r: (r(bi, c), 1)),
            pl.BlockSpec((L, hv), lambda bi, c, r=r: (r(bi, c), 1)),
            pl.BlockSpec((L, LANE), lambda bi, c, r=r, d=d: (r(bi, c), 2 * d)),
            pl.BlockSpec((L, LANE), lambda bi, c, r=r, d=d: (r(bi, c), 2 * d + 1)),
            pl.BlockSpec((1, LANE), lambda bi, c, d=d: (0, 2 * d)),
            pl.BlockSpec((1, LANE), lambda bi, c, d=d: (0, 2 * d + 1)),
        ]
        args += [z, z, z, gates, gates, gate_b, gate_b]
    return pl.pallas_call(
        _mlstm_kernel,
        grid=(b, nlc + 1),
        in_specs=in_specs,
        out_specs=[pl.BlockSpec((L, hv), lambda bi, c, r=rb(d): (r(bi, c), 0)) for d in range(2)],
        out_shape=[jax.ShapeDtypeStruct((rows.n_all, hv), BF16)] * 2,
        scratch_shapes=[
            pltpu.VMEM((2, ML_HEADS, ML_DQK, ML_DV + LANE), F32),
            pltpu.VMEM((2, 1, LANE), F32),
        ],
        compiler_params=_params(("parallel", "arbitrary")),
        name="mlstm_scan",
    )(*args)


def _mla_kernel(q_ref, kvl_ref, kvx_ref, krl_ref, krx_ref, o_ref, kk_ref, vt_ref, *, seq, sub, ck):
    @pl.when(pl.program_id(2) == 0)
    def _():
        kk_ref[:seq, :LANE] = kvl_ref[:, :LANE]
        kk_ref[:seq, LANE:] = krl_ref[...]
        kk_ref[seq:, :LANE] = kvx_ref[:, :LANE]
        kk_ref[seq:, LANE:] = krx_ref[...]
        vt_ref[:LANE, :seq] = kvl_ref[:, LANE:].astype(F32).T.astype(BF16)
        vt_ref[:LANE, seq:] = kvx_ref[:, LANE:].astype(F32).T.astype(BF16)
        vt_ref[LANE:, :] = jnp.ones((vt_ref.shape[0] - LANE, vt_ref.shape[1]), BF16)

    nsub = q_ref.shape[0] // sub
    n_keys = kk_ref.shape[0]
    bounds = [(k0, min(k0 + ck, n_keys)) for k0 in range(0, n_keys, ck)]
    nck = len(bounds)
    qs = [q_ref[s * sub:(s + 1) * sub, :] for s in range(nsub)]

    def scores(s, c):
        return _dot_nt(kk_ref[bounds[c][0]:bounds[c][1], :], qs[s])

    m = [None] * nsub
    acc = [None] * nsub
    st = [None] * nsub

    def absorb(s, c):
        cm = jnp.max(st[s], axis=0, keepdims=True)
        m_new = cm if c == 0 else jnp.maximum(m[s], cm)
        pv = _dot(vt_ref[:, bounds[c][0]:bounds[c][1]], jnp.exp2(st[s] - m_new).astype(BF16))
        acc[s] = pv if c == 0 else acc[s] * jnp.exp2(m[s] - m_new) + pv
        m[s] = m_new

    for k in range(-1, nck + nsub - 1):
        nxt = [scores(s, k - s + 1) if 0 <= k - s + 1 < nck else None for s in range(nsub)]
        for s in range(nsub):
            if 0 <= k - s < nck:
                absorb(s, k - s)
        for s in range(nsub):
            if nxt[s] is not None:
                st[s] = nxt[s]
    for s in range(nsub):
        o = acc[s][:LANE] / acc[s][LANE:LANE + 1]
        o_ref[s * sub:(s + 1) * sub, :] = o.T.astype(o_ref.dtype)


def mla_attention(q, kv, z, *, rows, tq=4096, sub=256, ck=1024):
    b, seq, ctx_len = rows.batch, rows.seq, rows.ctx_len
    tq = min(tq, seq)
    nq = seq // tq
    kr_col = z.shape[1] // LANE - 1
    ctx0 = rows.n_lat // ctx_len
    n_keys = seq + ctx_len
    ones_rows = 16
    return pl.pallas_call(
        functools.partial(_mla_kernel, seq=seq, sub=sub, ck=ck),
        grid=(b, MLA_HEADS, nq),
        in_specs=[
            pl.BlockSpec((tq, 2 * LANE), lambda bi, h, i: (bi * nq + i, h)),
            pl.BlockSpec((seq, 2 * LANE), lambda bi, h, i: (bi, h)),
            pl.BlockSpec((ctx_len, 2 * LANE), lambda bi, h, i: (ctx0 + bi, h)),
            pl.BlockSpec((seq, LANE), lambda bi, h, i: (bi, kr_col)),
            pl.BlockSpec((ctx_len, LANE), lambda bi, h, i: (ctx0 + bi, kr_col)),
        ],
        out_specs=pl.BlockSpec((tq, LANE), lambda bi, h, i: (bi * nq + i, h)),
        out_shape=jax.ShapeDtypeStruct((rows.n_lat, MLA_HEADS * MLA_V), BF16),
        scratch_shapes=[pltpu.VMEM((n_keys, 2 * LANE), BF16), pltpu.VMEM((LANE + ones_rows, n_keys), BF16)],
        compiler_params=_params(("parallel", "parallel", "arbitrary")),
        name="mla_attention",
    )(q, kv, kv, z, z)


def _hy_filter_kernel(ft_ref, w1_ref, b1_ref, w2_ref, b2_ref, w3_ref, b3_ref, fr_ref, fr3_ref, w4_ref, b4_ref,
                      dec_ref, o_ref, a_ref, *, back_from):
    j = pl.program_id(0)
    fw = HY_FILTER_W
    hdot = functools.partial(jnp.dot, preferred_element_type=F32, precision=HIGHEST)

    @pl.when(j == 0)
    def _():
        fr = fr_ref[...]
        a = jnp.sin(fr * (hdot(ft_ref[...], w1_ref[...]) + b1_ref[...]))
        a = jnp.sin(fr * (hdot(a, w2_ref[...]) + b2_ref[...]))
        a3 = jnp.sin(fr3_ref[...] * (hdot(a, w3_ref[...]) + b3_ref[...]))
        hi = a3.astype(BF16)
        lo = (a3 - hi.astype(F32)).astype(BF16)
        lane = lax.broadcasted_iota(jnp.int32, a3.shape, 1)
        a_ref[...] = jnp.where((lane >= fw) & (lane < 2 * fw), lo, hi)

    w4 = w4_ref[...]
    w_hi = w4.astype(BF16)
    w_lo = (w4 - w_hi.astype(F32)).astype(BF16)
    rhs = jnp.concatenate([w_hi, w_hi, w_lo, jnp.zeros_like(w_hi)], axis=0)
    t = ft_ref[:, 0:1]
    filt = (_dot(a_ref[...], rhs) + b4_ref[...]) * jnp.exp(-t * jnp.abs(dec_ref[...]))
    row = lax.broadcasted_iota(jnp.int32, filt.shape, 0)
    o_ref[...] = jnp.where(jnp.logical_and(row == 0, j >= back_from), 0.0, filt).astype(o_ref.dtype)


def hyena_filters(seq, w1, b1, w2, b2, w3, b3, w4, b4, freq, decay, tn=512):
    t = np.linspace(0.0, 1.0, seq)[:, None]
    w = (2.0 * math.pi / seq) * np.arange(seq)[:, None]
    bands = np.linspace(1e-4, HY_BANDS - 1, HY_BANDS)[None, :]
    feats = np.zeros((seq, LANE))
    feats[:, :1 + 2 * HY_BANDS] = np.concatenate([t, np.cos(bands * w), -np.sin(bands * w)], axis=-1)
    n = w4.shape[1]
    fw = HY_FILTER_W
    w1p = jnp.zeros((LANE, fw), F32).at[:w1.shape[0]].set(w1)
    row = lambda a: a.reshape(1, -1)
    tile3 = lambda a: jnp.concatenate([a, a, a, jnp.zeros_like(a)], axis=-1)
    full = lambda shape: pl.BlockSpec(shape, lambda j: (0, 0))
    return pl.pallas_call(
        functools.partial(_hy_filter_kernel, back_from=(n // 2) // tn),
        grid=(n // tn,),
        in_specs=[full((seq, LANE)), full((LANE, fw)), full((1, fw)), full((fw, fw)), full((1, fw)),
                  full((fw, 4 * fw)), full((1, 4 * fw)), full((1, fw)), full((1, 4 * fw)),
                  pl.BlockSpec((fw, tn), lambda j: (0, j)),
                  pl.BlockSpec((1, tn), lambda j: (0, j)),
                  pl.BlockSpec((1, tn), lambda j: (0, j))],
        out_specs=pl.BlockSpec((seq, tn), lambda j: (0, j)),
        out_shape=jax.ShapeDtypeStruct((seq, n), BF16),
        scratch_shapes=[pltpu.VMEM((seq, 4 * fw), BF16)],
        compiler_params=_params(("arbitrary",)),
        name="hyena_filters",
    )(jnp.asarray(feats, F32), w1p, row(b1), w2, row(b2), tile3(w3), tile3(row(b3)), row(freq), tile3(row(freq)),
      w4, row(b4), row(decay))


def _dft_mats(seq, p):
    n = 2 * seq
    q = n // p
    k1n = q // 2 + 1
    hi, k1 = np.arange(q // 2), np.arange(k1n)
    th = 2 * np.pi * np.outer(k1, hi) / q
    ma = np.zeros((2 * k1n, q // 2))
    ma[0::2], ma[1::2] = np.cos(th), -np.sin(th)
    lo = np.arange(p)
    wf = np.zeros((k1n, 2 * p, 2 * p))
    wi = np.zeros((k1n, 2 * p, 2 * p))
    for k in k1:
        ph = -2 * np.pi * (np.outer(lo, lo) / p + k * lo[None, :] / n)
        er, ei = np.cos(ph), np.sin(ph)
        wf[k] = np.block([[er, -ei], [ei, er]])
        wi[k] = np.block([[er.T, ei.T], [-ei.T, er.T]])
    c = np.full(k1n, 2.0)
    c[0] = c[-1] = 1.0
    th2 = 2 * np.pi * np.outer(hi, k1) / q
    md = np.zeros((q // 2, 2 * k1n))
    md[:, 0::2], md[:, 1::2] = c * np.cos(th2) / n, -c * np.sin(th2) / n
    as_bf16 = lambda a: jnp.asarray(a, F32).astype(BF16)
    return ma, as_bf16(wf), as_bf16(wi), md


HY_HALO = 16


def _hy_in_kernel(h_ref, hp_ref, hn_ref, g_ref, sh_ref, sc_ref, w_ref, cw_ref, cb_ref, o_ref, u_ref, inv_ref,
                  invh_ref, *, per_seq):
    i = pl.program_id(0)
    tm = h_ref.shape[0]

    @pl.when(pl.program_id(1) == 0)
    def _():
        mod = lambda x: x * (1.0 + sc_ref[0]) + sh_ref[0]
        u_ref[HY_HALO:HY_HALO + tm, :] = mod(_rms_ref(h_ref, inv_ref, g_ref[...])).astype(BF16)
        keep_prev = (i % per_seq != 0).astype(F32)
        keep_next = (i % per_seq != per_seq - 1).astype(F32)
        u_ref[:HY_HALO, :] = (mod(_rms_ref(hp_ref, invh_ref, g_ref[...])) * keep_prev).astype(BF16)
        u_ref[HY_HALO + tm:, :] = (mod(_rms_ref(hn_ref, invh_ref, g_ref[...])) * keep_next).astype(BF16)

    acc = _dot(u_ref[...], w_ref[...])
    n = acc.shape[0]
    mid = slice(HY_HALO, HY_HALO + tm)
    prev = pltpu.roll(acc, 1, 0)[mid]
    nxt = pltpu.roll(acc, n - 1, 0)[mid]
    cw = cw_ref[...]
    o_ref[...] = (prev * cw[0:1] + acc[mid] * cw[1:2] + nxt * cw[2:3] + cb_ref[...]).astype(o_ref.dtype)


def hyena_in_conv(h, w, conv_w, conv_b, mods, gain, *, rows, tm, tn, layer):
    k, n = w.shape
    n_rows = rows.n_lat
    per_seq = rows.seq // tm
    halo_per_blk, n_halo = tm // HY_HALO, n_rows // HY_HALO
    return pl.pallas_call(
        functools.partial(_hy_in_kernel, per_seq=per_seq),
        grid=(n_rows // tm, n // tn),
        in_specs=[
            pl.BlockSpec((tm, k), lambda i, j: (i, 0)),
            pl.BlockSpec((HY_HALO, k), lambda i, j: (jnp.maximum(i * halo_per_blk - 1, 0), 0)),
            pl.BlockSpec((HY_HALO, k), lambda i, j: (jnp.minimum((i + 1) * halo_per_blk, n_halo - 1), 0)),
            pl.BlockSpec((1, k), lambda i, j: (0, 0)),
            pl.BlockSpec((1, 1, k), rows.mod_index(layer, 0, tm)),
            pl.BlockSpec((1, 1, k), rows.mod_index(layer, 1, tm)),
            pl.BlockSpec((k, tn), lambda i, j: (0, j)),
            pl.BlockSpec((HY_SHORT, tn), lambda i, j: (0, j)),
            pl.BlockSpec((1, tn), lambda i, j: (0, j)),
        ],
        out_specs=pl.BlockSpec((tm, tn), lambda i, j: (i, j)),
        out_shape=jax.ShapeDtypeStruct((n_rows, n), BF16),
        scratch_shapes=[pltpu.VMEM((tm + 2 * HY_HALO, k), BF16), pltpu.VMEM((tm, 1), F32),
                        pltpu.VMEM((HY_HALO, 1), F32)],
        compiler_params=_params(("parallel", "arbitrary")),
        name="hyena_in",
    )(h, h, h, gain.reshape(1, k), mods, mods, w, conv_w, conv_b.reshape(1, n))


def _hy_stage_a_kernel(x_ref, m_ref, o_ref):
    qh, lg, g, wc = x_ref.shape
    mo = o_ref.shape[0]
    for l in range(lg):
        x = x_ref[:, l].reshape(qh * g, wc).astype(BF16)
        y = _dot(m_ref[...], x).reshape(mo, g, wc)
        o_ref[:, l * g:(l + 1) * g, :] = y.astype(o_ref.dtype)


def hyena_stage_a(x, mak, *, p, width=2048, col_blk=0, channels=None):
    bx, seq, c_all = x.shape
    c = c_all if channels is None else channels
    g = HY_G
    qh = seq // p
    mo = mak.shape[0] // g
    x5 = x.reshape(bx, qh, p // g, g, c_all)
    lg = HY_LG
    return pl.pallas_call(
        _hy_stage_a_kernel,
        grid=(bx, p // (g * lg), c // width),
        in_specs=[pl.BlockSpec((None, qh, lg, g, width), lambda bi, l, j: (bi, 0, l, 0, col_blk + j)),
                  pl.BlockSpec(mak.shape, lambda bi, l, j: (0, 0))],
        out_specs=pl.BlockSpec((None, mo, lg * g, width), lambda bi, l, j: (bi, 0, l, j)),
        out_shape=jax.ShapeDtypeStruct((bx, mo, p, c), BF16),
        compiler_params=_params(("parallel", "parallel", "parallel")),
        name="hyena_stage_a",
    )(x5, mak)


def _hy_spec_kernel(a0_ref, a1_ref, wf_ref, o_ref):
    p = o_ref.shape[2]
    cb = o_ref.shape[3]
    wf = wf_ref[0]
    x0 = _dot(wf, a0_ref[0].reshape(2 * p, cb))
    x1 = _dot(wf, a1_ref[0].reshape(2 * p, cb))
    o_ref[0, 0] = x0[:p] + x1[:p]
    o_ref[0, 1] = x0[p:] - x1[p:]


def hyena_spectrum(af, wf, *, d, cb=2048):
    k1n, _, p, _ = af.shape
    nblk = (HY_ORDER * d) // cb
    return pl.pallas_call(
        _hy_spec_kernel,
        grid=(k1n, nblk),
        in_specs=[pl.BlockSpec((1, 2, p, cb), lambda k, j: (k, 0, 0, j)),
                  pl.BlockSpec((1, 2, p, cb), lambda k, j: (k, 0, 0, nblk + j)),
                  pl.BlockSpec((1, 2 * p, 2 * p), lambda k, j: (k, 0, 0))],
        out_specs=pl.BlockSpec((1, 2, p, cb), lambda k, j: (k, 0, 0, j)),
        out_shape=jax.ShapeDtypeStruct((k1n, 2, p, HY_ORDER * d), F32),
        compiler_params=_params(("parallel", "parallel")),
        name="hyena_spectrum",
    )(af, af, wf)


def _hy_mid_kernel(a_ref, wf_ref, wi_ref, h_ref, o_ref):
    b, _, p, cb = a_ref.shape
    hr, hi = h_ref[0], h_ref[1]
    for bi in range(b):
        x = _dot(wf_ref[...], a_ref[bi].reshape(2 * p, cb))
        xr, xi = x[:p], x[p:]
        y = jnp.concatenate([xr * hr - xi * hi, xr * hi + xi * hr], axis=0).astype(BF16)
        o_ref[bi] = _dot(wi_ref[...], y).reshape(2, p, cb).astype(o_ref.dtype)


def hyena_mid(a, wf, wi, spec, *, order, d, cb=1024):
    b, k1n, _, p, _ = a.shape
    nblk = d // cb
    data = pl.BlockSpec((b, None, 2, p, cb), lambda k, j: (0, k, 0, 0, j))
    mat = pl.BlockSpec((None, 2 * p, 2 * p), lambda k, j: (k, 0, 0))
    return pl.pallas_call(
        _hy_mid_kernel,
        grid=(k1n, nblk),
        in_specs=[data, mat, mat,
                  pl.BlockSpec((None, 2, p, cb), lambda k, j: (k, 0, 0, order * nblk + j))],
        out_specs=data,
        out_shape=jax.ShapeDtypeStruct(a.shape, BF16),
        compiler_params=_params(("parallel", "parallel")),
        name="hyena_mid",
    )(a, wf, wi, spec)


def _hy_stage_d_kernel(c_ref, m_ref, x_ref, z_ref, skip_ref, *rest, fused):
    ma_ref, o_ref, a_ref = rest if fused else (None, rest[0], None)
    mo, rows_c, wc = c_ref.shape
    qh, lg, g, _ = x_ref.shape
    for l in range(lg):
        conv = _dot(m_ref[...], c_ref[:, l * g:(l + 1) * g, :].reshape(mo * g, wc))
        z = z_ref[:, l].reshape(qh * g, wc).astype(F32)
        x = x_ref[:, l].reshape(qh * g, wc).astype(F32)
        y = (x * (conv + z * skip_ref[...])).astype(o_ref.dtype)
        o_ref[:, l] = y.reshape(qh, g, wc)
        if fused:
            a_ref[:, l * g:(l + 1) * g, :] = _dot(ma_ref[...], y).reshape(mo, g, wc).astype(a_ref.dtype)


def hyena_stage_d(c, mdk, xg, z, skip, *, p, mak=None, width=2048, xg_blk=0, z_blk=0):
    b, mo, _, d = c.shape
    g = HY_G
    seq = xg.shape[1]
    qh = seq // p
    fused = mak is not None
    v5 = lambda a: a.reshape(b, qh, p // g, g, a.shape[2])
    lg = HY_LG
    tblk = lambda off: pl.BlockSpec((None, qh, lg, g, width), lambda bi, l, j: (bi, 0, l, 0, off + j))
    tspec = tblk(0)
    fspec = pl.BlockSpec((None, mo, lg * g, width), lambda bi, l, j: (bi, 0, l, j))
    in_specs = [fspec, pl.BlockSpec(mdk.shape, lambda bi, l, j: (0, 0)), tblk(xg_blk), tblk(z_blk),
                pl.BlockSpec((1, width), lambda bi, l, j: (0, j))]
    args = [c, mdk, v5(xg), v5(z), skip.reshape(1, d)]
    out_specs, out_shape = [tspec], [jax.ShapeDtypeStruct((b, qh, p // g, g, d), BF16)]
    if fused:
        in_specs.append(pl.BlockSpec(mak.shape, lambda bi, l, j: (0, 0)))
        args.append(mak)
        out_specs.append(fspec)
        out_shape.append(jax.ShapeDtypeStruct(c.shape, BF16))
    outs = pl.pallas_call(
        functools.partial(_hy_stage_d_kernel, fused=fused),
        grid=(b, p // (g * lg), d // width),
        in_specs=in_specs,
        out_specs=out_specs,
        out_shape=out_shape,
        compiler_params=_params(("parallel", "parallel", "parallel")),
        name="hyena_stage_d",
    )(*args)
    y = outs[0].reshape(b, seq, d)
    return (y, outs[1]) if fused else y


def hyena_long_convs(x3, filt, skip, *, rows, width=2048):
    b, seq = rows.batch, rows.seq
    d = x3.shape[1] // 3
    p = HY_P
    k1n = seq // p + 1
    ma, wf, wi, md = _dft_mats(seq, p)
    eye = np.eye(HY_G)
    as_bf16 = lambda a: jnp.asarray(a, F32).astype(BF16)
    mak, mdk = as_bf16(np.kron(ma, eye)), as_bf16(np.kron(md, eye))
    af = hyena_stage_a(filt.reshape(1, seq, filt.shape[1]), mak, p=p)
    spec = hyena_spectrum(af.reshape(k1n, 2, p, filt.shape[1]), wf, d=d)
    x3 = x3.reshape(b, seq, 3 * d)
    blk = d // width
    a = hyena_stage_a(x3, mak, p=p, width=width, col_blk=2 * blk, channels=d)
    c = hyena_mid(a.reshape(b, k1n, 2, p, d), wf, wi, spec, order=0, d=d)
    y, a = hyena_stage_d(c.reshape(a.shape), mdk, x3, x3, skip[0], p=p, mak=mak, width=width, xg_blk=0, z_blk=2 * blk)
    c = hyena_mid(a.reshape(b, k1n, 2, p, d), wf, wi, spec, order=1, d=d)
    y = hyena_stage_d(c.reshape(a.shape), mdk, x3, y, skip[1], p=p, width=width, xg_blk=blk, z_blk=0)
    return y.reshape(rows.n_lat, d)


def _mlstm_gate_weights(w_in, gate_b):
    h = ML_HEADS
    o4 = 2 * h * ML_DQK + 2 * h * ML_DV
    wg = w_in[:, o4:].reshape(-1, 2, 2, h)
    bg = gate_b.reshape(2, 2, h)
    w_out = jnp.zeros((w_in.shape[0], 4 * LANE), F32)
    b_out = jnp.zeros((1, 4 * LANE), F32)
    for d in range(2):
        for gate in range(2):
            lo = (2 * d + gate) * LANE
            w_out = w_out.at[:, lo:lo + h].set(wg[:, gate, d])
            b_out = b_out.at[0, lo:lo + h].set(bg[gate, d])
    return w_out, b_out


def _mla_weights(w_in, w_uq):
    r2 = MLA_Q_RANK + MLA_KV_RANK
    half = MLA_ROPE // 2

    def spread(w):
        z = jnp.zeros(w.shape[:-1] + (half,), w.dtype)
        return jnp.concatenate([w[..., :half], z, w[..., half:], z], axis=-1)

    w_in_p = jnp.concatenate([w_in[:, :r2], spread(w_in[:, r2:])], axis=1)
    wq = w_uq.reshape(w_uq.shape[0], MLA_HEADS, MLA_NOPE + MLA_ROPE)
    wq_p = jnp.concatenate([wq[..., :MLA_NOPE], spread(wq[..., MLA_NOPE:])], axis=-1)
    wq_p = wq_p * ((MLA_NOPE + MLA_ROPE) ** -0.5 * math.log2(math.e))
    return w_in_p, wq_p.reshape(w_uq.shape[0], MLA_HEADS * 2 * LANE)


def kernel(x, c, ctx, c_ctx, ada_w, ada_b, norm_g, mlp_w1, mlp_w2, swa_w_qkv, swa_sink, swa_w_o, ml_w_in, ml_gate_b, ml_head_g, ml_w_o, mla_w_in, mla_q_g, mla_kv_g, mla_w_uq, mla_w_ukv, mla_w_o, hy_w_in, hy_conv_w, hy_conv_b, hy_f_w1, hy_f_b1, hy_f_w2, hy_f_b2, hy_f_w3, hy_f_b3, hy_f_w4, hy_f_b4, hy_f_freq, hy_decay, hy_skip, hy_w_o):
    b, seq, d = x.shape
    ctx_len = ctx.shape[1]
    depth = ada_w.shape[0]
    assert depth == 4 and b < 8
    rows = Rows(b, seq, ctx_len)
    tm = 512
    tm_lin = 1024 if rows.n_lat % 1024 == 0 and rows.n_ctx % 1024 == 0 else 512
    bf = lambda w: w.astype(BF16)
    mlp_w1b, mlp_w2b = bf(mlp_w1), bf(mlp_w2)

    cond = jnp.zeros((8, d), F32).at[:b].set(c).at[b].set(c_ctx)
    mods = ada_mods(cond, ada_w, ada_b).reshape(depth * 8, 1, N_MOD * d)

    common = dict(rows=rows, tm=tm_lin, mods=mods)

    def finish(h, o_args, w_o, layer, n_rows, mode="plain", h2=None):
        h = outproj_residual(o_args, bf(w_o), h, mods, norm_g[layer, 1], rows=rows, n_rows=n_rows, tm=tm,
                             layer=layer, mode=mode, h2=h2, name=f"outproj{layer}")
        return mlp_residual(h, mlp_w1b, mlp_w2b, mods, norm_g[layer, 2], norm_g[layer, 3],
                            rows=rows, n_rows=n_rows, tm=tm, tf=1024, layer=layer)

    n_qk_groups = SWA_HEADS + SWA_KV_HEADS
    q_cols = SWA_HEADS * SWA_HEAD_DIM
    q_scale = SWA_HEAD_DIM ** -0.5 * math.log2(math.e)
    w_qkv = jnp.concatenate([swa_w_qkv[0][:, :q_cols] * q_scale, swa_w_qkv[0][:, q_cols:]], axis=1)
    x_rows, ctx_rows = x.reshape(rows.n_lat, d), ctx.reshape(rows.n_ctx, d)
    qkv = linear(x_rows, bf(w_qkv), x2=ctx_rows, n_rows=rows.n_all, tn=512, out_dtype=BF16, prologue="norm_mod",
                 gain=norm_g[0, 0], layer=0, rope_tabs=_rope_tables(seq, SWA_HEAD_DIM), rope_pattern=(True,) * 4,
                 rope_jmax=n_qk_groups // 4, name="swa_qkv", **common)
    o = swa_attention(qkv, swa_sink[0], rows=rows)
    h = finish(x_rows, o, swa_w_o[0], 0, rows.n_all, h2=ctx_rows)

    w_gate, b_gate = _mlstm_gate_weights(ml_w_in[0], ml_gate_b[0])
    z = linear(h, bf(ml_w_in[0][:, :2 * ML_HEADS * (ML_DQK + ML_DV)]), n_rows=rows.n_all, tn=1536, out_dtype=BF16,
               prologue="norm_mod", gain=norm_g[1, 0], layer=1, name="mlstm_in", **common)
    gates = linear(h, bf(w_gate), n_rows=rows.n_all, tn=4 * LANE, out_dtype=F32, prologue="norm_mod",
                   gain=norm_g[1, 0], layer=1, name="mlstm_gates", **common)
    h_fwd, h_bwd = mlstm_scan(z, gates, b_gate, rows=rows)
    h = finish(h, (h_fwd, h_bwd, z, ml_head_g[0]), ml_w_o[0], 1, rows.n_all, mode="mlstm")

    w_in_p, w_uq_p = _mla_weights(mla_w_in[0], mla_w_uq[0])
    rope_mla = _rope_tables(seq, MLA_ROPE)
    zc = linear(h, bf(w_in_p), n_rows=rows.n_all, tn=w_in_p.shape[1], out_dtype=BF16, prologue="norm_mod",
                gain=norm_g[2, 0], layer=2, rope_tabs=rope_mla, rope_pattern=(False,) * 8 + (True,),
                name="mla_in", **common)
    q = linear(zc, bf(w_uq_p), n_rows=rows.n_lat, tn=2048, out_dtype=BF16, x_cols=0, prologue="norm",
               gain=mla_q_g[0], rope_tabs=rope_mla, rope_pattern=(False, True) * 8, name="mla_q",
               rows=rows, tm=tm_lin)
    kv = linear(zc, bf(mla_w_ukv[0]), n_rows=rows.n_all, tn=2048, out_dtype=BF16, x_cols=1, prologue="norm",
                gain=mla_kv_g[0], name="mla_kv", rows=rows, tm=tm_lin)
    o = mla_attention(q, kv, zc, rows=rows)
    h = finish(h, o, mla_w_o[0], 2, rows.n_lat)

    x3 = hyena_in_conv(h, bf(hy_w_in[0]), hy_conv_w[0], hy_conv_b[0], mods, norm_g[3, 0], rows=rows, tm=tm_lin,
                       tn=1536, layer=3)
    filt = hyena_filters(seq, hy_f_w1[0], hy_f_b1[0], hy_f_w2[0], hy_f_b2[0], hy_f_w3[0], hy_f_b3[0],
                         hy_f_w4[0], hy_f_b4[0], hy_f_freq[0], hy_decay[0])
    y = hyena_long_convs(x3, filt, hy_skip[0], rows=rows)
    h = finish(h, y, hy_w_o[0], 3, rows.n_lat)
    return h.reshape(b, seq, d)
```

```python
import functools
import math

import numpy as np
import jax
import jax.numpy as jnp
from jax import lax
from jax.experimental import pallas as pl
from jax.experimental.pallas import tpu as pltpu

F32, BF16 = jnp.float32, jnp.bfloat16
HIGHEST = lax.Precision.HIGHEST

RMS_EPS = 1e-6
ROPE_THETA = 10000.0
GRID_W = 64
N_MOD = 6
LANE = 128

SWA_HEADS, SWA_KV_HEADS, SWA_HEAD_DIM, SWA_WINDOW, SWA_BLOCK = 16, 4, 128, 128, 128
ML_HEADS, ML_DQK, ML_DV = 8, 128, 256
ML_CHUNK = 256
MLA_HEADS, MLA_Q_RANK, MLA_KV_RANK, MLA_NOPE, MLA_ROPE, MLA_V = 16, 512, 512, 128, 64, 128
HY_ORDER, HY_BANDS, HY_FILTER_W, HY_SHORT = 2, 16, 64, 3
HY_P = 256
HY_G = 16
HY_LG = 2

VMEM_LIMIT = 48 * 1024 * 1024


def _params(sem):
    return pltpu.CompilerParams(dimension_semantics=sem, vmem_limit_bytes=VMEM_LIMIT)


def _dot(a, b):
    return jnp.dot(a, b, preferred_element_type=F32)


def _dot_nt(a, b):
    return lax.dot_general(a, b, (((1,), (1,)), ((), ())), preferred_element_type=F32)


def _dot_tn(a, b):
    return lax.dot_general(a, b, (((0,), (0,)), ((), ())), preferred_element_type=F32)


def _rms(x, g):
    return x * lax.rsqrt(jnp.mean(x * x, axis=-1, keepdims=True) + RMS_EPS) * g


def _rms_ref(x_ref, inv_ref, g):
    x = x_ref[...].astype(F32)
    inv_ref[...] = lax.rsqrt(jnp.mean(x * x, axis=-1, keepdims=True) + RMS_EPS)
    return x_ref[...].astype(F32) * inv_ref[...] * g


def _ada_kernel(s_ref, w_ref, b_ref, o_ref):
    s = s_ref[...]
    s = s * jax.nn.sigmoid(s)
    hi = s.astype(BF16)
    lo = (s - hi.astype(F32)).astype(BF16)
    w = w_ref[0]
    w_hi = w.astype(BF16)
    w_lo = (w - w_hi.astype(F32)).astype(BF16)
    r = _dot(jnp.concatenate([hi, lo], axis=0), w_hi)
    n = s.shape[0]
    o_ref[0] = r[:n] + r[n:] + _dot(hi, w_lo) + b_ref[0]


def ada_mods(cond, ada_w, ada_b, tn=1536):
    depth, d, n = ada_w.shape
    rows = cond.shape[0]
    return pl.pallas_call(
        _ada_kernel,
        grid=(depth, n // tn),
        in_specs=[
            pl.BlockSpec((rows, d), lambda l, j: (0, 0)),
            pl.BlockSpec((1, d, tn), lambda l, j: (l, 0, j)),
            pl.BlockSpec((1, 1, tn), lambda l, j: (l, 0, j)),
        ],
        out_specs=pl.BlockSpec((1, rows, tn), lambda l, j: (l, 0, j)),
        out_shape=jax.ShapeDtypeStruct((depth, rows, n), F32),
        compiler_params=_params(("parallel", "parallel")),
        name="ada_mods",
    )(cond, ada_w, ada_b.reshape(depth, 1, n))


class Rows:
    def __init__(self, batch, seq, ctx_len):
        self.batch, self.seq, self.ctx_len = batch, seq, ctx_len
        self.n_lat = batch * seq
        self.n_ctx = batch * ctx_len
        self.n_all = self.n_lat + self.n_ctx

    def mod_index(self, layer, k, tm):
        lat_blocks, per_batch = self.n_lat // tm, self.seq // tm

        def index(i, *_):
            b = jnp.where(i < lat_blocks, i // per_batch, self.batch)
            return (layer * 8 + b, 0, k)

        return index


def _linear_kernel(*refs, prologue, rope, rope_pattern, n_lat_blocks, dual, single):
    it = iter(refs)
    x_ref = next(it)
    x2_ref = next(it) if dual else None
    g_ref = next(it) if prologue in ("norm", "norm_mod") else None
    sh_ref = next(it) if prologue == "norm_mod" else None
    sc_ref = next(it) if prologue == "norm_mod" else None
    w_ref = next(it)
    cc_ref = next(it) if rope else None
    ss_ref = next(it) if rope else None
    o_ref = next(it)
    u_ref = next(it)
    inv_ref = next(it)
    first = pl.program_id(1) == 0

    def prologue_from(src_ref):
        if prologue in ("norm", "norm_mod"):
            x = _rms_ref(src_ref, inv_ref, g_ref[...])
        else:
            x = src_ref[...].astype(F32)
        if prologue == "norm_mod":
            x = x * (1.0 + sc_ref[0]) + sh_ref[0]
        u_ref[...] = x.astype(BF16)

    if dual:
        is_lat = pl.program_id(0) < n_lat_blocks
        pl.when(jnp.logical_and(first, is_lat))(functools.partial(prologue_from, x_ref))
        pl.when(jnp.logical_and(first, jnp.logical_not(is_lat)))(functools.partial(prologue_from, x2_ref))
    elif single:
        prologue_from(x_ref)
    else:
        pl.when(first)(functools.partial(prologue_from, x_ref))

    acc = _dot(u_ref[...], w_ref[...])

    def plain():
        o_ref[...] = acc.astype(o_ref.dtype)

    def roped():
        cc, ss = cc_ref[...], ss_ref[...]
        segs = []
        for gi, on in enumerate(rope_pattern):
            seg = acc[:, gi * LANE:(gi + 1) * LANE]
            if on:
                seg = seg * cc + pltpu.roll(seg, LANE // 2, 1) * ss
            segs.append(seg)
        o_ref[...] = jnp.concatenate(segs, axis=1).astype(o_ref.dtype)

    if rope:
        roped()
    else:
        plain()


def linear(x, w, *, rows, n_rows, tm, tn, out_dtype, x_cols=None, prologue="none", gain=None,
           mods=None, layer=0, mod_k=(0, 1), rope_tabs=None, rope_pattern=(), rope_jmax=1 << 30,
           n_out=None, x2=None, name="linear"):
    k, n = w.shape
    n = n if n_out is None else n_out
    xc = 0 if x_cols is None else x_cols
    rope = rope_tabs is not None
    dual = x2 is not None
    n_lat_blocks = rows.n_lat // tm
    if dual:
        in_specs = [pl.BlockSpec((tm, k), lambda i, j: (jnp.minimum(i, n_lat_blocks - 1), 0)),
                    pl.BlockSpec((tm, k), lambda i, j: (jnp.maximum(i - n_lat_blocks, 0), 0),
                                 pipeline_mode=pl.Buffered(1))]
        args = [x, x2]
    else:
        in_specs = [pl.BlockSpec((tm, k), lambda i, j: (i, xc))]
        args = [x]
    if prologue in ("norm", "norm_mod"):
        in_specs.append(pl.BlockSpec((1, k), lambda i, j: (0, 0)))
        args.append(gain.reshape(1, k))
    if prologue == "norm_mod":
        for mk in mod_k:
            in_specs.append(pl.BlockSpec((1, 1, k), rows.mod_index(layer, mk, tm)))
            args.append(mods)
    in_specs.append(pl.BlockSpec((k, tn), lambda i, j: (0, j)))
    args.append(w)
    if rope:
        per_seq = rows.seq // tm

        def tab_index(i, j):
            rotate = jnp.logical_and(i < n_lat_blocks, j < rope_jmax)
            return (jnp.where(rotate, 0, 1), i % per_seq, 0)

        for t, fill in zip(rope_tabs, (1.0, 0.0)):
            in_specs.append(pl.BlockSpec((None, tm, LANE), tab_index))
            args.append(jnp.stack([t, jnp.full_like(t, fill)]))
    kern = functools.partial(_linear_kernel, prologue=prologue, rope=rope, rope_pattern=tuple(rope_pattern),
                             n_lat_blocks=n_lat_blocks, dual=dual, single=(n == tn))
    return pl.pallas_call(
        kern,
        grid=(n_rows // tm, n // tn),
        in_specs=in_specs,
        out_specs=pl.BlockSpec((tm, tn), lambda i, j: (i, j)),
        out_shape=jax.ShapeDtypeStruct((n_rows, n), out_dtype),
        scratch_shapes=[pltpu.VMEM((tm, k), BF16), pltpu.VMEM((tm, 1), F32)],
        compiler_params=_params(("parallel", "arbitrary")),
        name=name,
    )(*args)


def _outproj_kernel(*refs, mode, n_lat_blocks):
    it = iter(refs)
    if mode == "mlstm":
        hf_ref, hb_ref, og_ref, hg_ref = next(it), next(it), next(it), next(it)
    else:
        o_ref_in = next(it)
    w_ref, h_ref = next(it), next(it)
    h2_ref = next(it) if n_lat_blocks is not None else None
    gate_ref, g_ref, out_ref = next(it), next(it), next(it)
    y_ref, inv_ref = next(it), next(it)
    if mode == "mlstm":
        hs = hf_ref[...].astype(F32) + hb_ref[...].astype(F32)
        og = jax.nn.sigmoid(og_ref[...].astype(F32))
        hg = hg_ref[...]
        parts = []
        for h in range(ML_HEADS):
            sl = slice(h * ML_DV, (h + 1) * ML_DV)
            parts.append((_rms(hs[:, sl], hg[:, sl]) * og[:, sl]).astype(BF16))
        o = jnp.concatenate(parts, axis=1)
    else:
        o = o_ref_in[...]
    y_ref[...] = _dot(o, w_ref[...])
    upd = gate_ref[0] * _rms_ref(y_ref, inv_ref, g_ref[...])
    if h2_ref is None:
        out_ref[...] = h_ref[...] + upd
    else:
        out_ref[...] = jnp.where(pl.program_id(0) < n_lat_blocks, h_ref[...], h2_ref[...]) + upd


def outproj_residual(o_args, w, h, mods, gain, *, rows, n_rows, tm, layer, mode="plain", h2=None, name="outproj"):
    k, d = w.shape
    n_lat_blocks = rows.n_lat // tm if h2 is not None else None
    if mode == "mlstm":
        h_fwd, h_bwd, z, head_g = o_args
        in_specs = [
            pl.BlockSpec((tm, k), lambda i: (i, 0)),
            pl.BlockSpec((tm, k), lambda i: (i, 0)),
            pl.BlockSpec((tm, k), lambda i: (i, 2)),
            pl.BlockSpec((1, k), lambda i: (0, 0)),
        ]
        args = [h_fwd, h_bwd, z, head_g.reshape(1, k)]
    else:
        in_specs = [pl.BlockSpec((tm, k), lambda i: (i, 0))]
        args = [o_args]
    in_specs.append(pl.BlockSpec((k, d), lambda i: (0, 0)))
    args.append(w)
    if h2 is None:
        in_specs.append(pl.BlockSpec((tm, d), lambda i: (i, 0)))
        args.append(h)
    else:
        in_specs += [pl.BlockSpec((tm, d), lambda i: (jnp.minimum(i, n_lat_blocks - 1), 0)),
                     pl.BlockSpec((tm, d), lambda i: (jnp.maximum(i - n_lat_blocks, 0), 0))]
        args += [h, h2]
    in_specs += [
        pl.BlockSpec((1, 1, d), rows.mod_index(layer, 2, tm)),
        pl.BlockSpec((1, d), lambda i: (0, 0)),
    ]
    args += [mods, gain.reshape(1, d)]
    return pl.pallas_call(
        functools.partial(_outproj_kernel, mode=mode, n_lat_blocks=n_lat_blocks),
        grid=(n_rows // tm,),
        in_specs=in_specs,
        out_specs=pl.BlockSpec((tm, d), lambda i: (i, 0)),
        out_shape=jax.ShapeDtypeStruct((n_rows, d), F32),
        scratch_shapes=[pltpu.VMEM((tm, d), F32), pltpu.VMEM((tm, 1), F32)],
        compiler_params=_params(("parallel",)),
        name=name,
    )(*args)


def _mlp_kernel(h_ref, g2_ref, sh_ref, sc_ref, w1_ref, w2_ref, gate_ref, g3_ref, out_ref, v_ref, acc_ref,
                inv_ref):
    f = pl.program_id(1)

    @pl.when(f == 0)
    def _():
        v = _rms_ref(h_ref, inv_ref, g2_ref[...]) * (1.0 + sc_ref[0]) + sh_ref[0]
        v_ref[...] = v.astype(BF16)
        acc_ref[...] = jnp.zeros_like(acc_ref)

    a = jnp.maximum(_dot(v_ref[...], w1_ref[...]), 0.0)
    acc_ref[...] += _dot((a * a).astype(BF16), w2_ref[...])

    @pl.when(f == pl.num_programs(1) - 1)
    def _():
        out_ref[...] = h_ref[...] + gate_ref[0] * _rms_ref(acc_ref, inv_ref, g3_ref[...])


def mlp_residual(h, w1, w2, mods, g2, g3, *, rows, n_rows, tm, tf, layer):
    _, d, ff = w1.shape
    return pl.pallas_call(
        _mlp_kernel,
        grid=(n_rows // tm, ff // tf),
        in_specs=[
            pl.BlockSpec((tm, d), lambda i, f: (i, 0)),
            pl.BlockSpec((1, d), lambda i, f: (0, 0)),
            pl.BlockSpec((1, 1, d), rows.mod_index(layer, 3, tm)),
            pl.BlockSpec((1, 1, d), rows.mod_index(layer, 4, tm)),
            pl.BlockSpec((None, d, tf), lambda i, f: (layer, 0, f)),
            pl.BlockSpec((None, tf, d), lambda i, f: (layer, f, 0)),
            pl.BlockSpec((1, 1, d), rows.mod_index(layer, 5, tm)),
            pl.BlockSpec((1, d), lambda i, f: (0, 0)),
        ],
        out_specs=pl.BlockSpec((tm, d), lambda i, f: (i, 0)),
        out_shape=jax.ShapeDtypeStruct((n_rows, d), F32),
        scratch_shapes=[pltpu.VMEM((tm, d), BF16), pltpu.VMEM((tm, d), F32), pltpu.VMEM((tm, 1), F32)],
        compiler_params=_params(("parallel", "arbitrary")),
        name="mlp",
    )(h, g2.reshape(1, d), mods, mods, w1, w2, mods, g3.reshape(1, d))


def _rope_tables(seq, d_rot):
    pos = np.arange(seq)
    row, col = pos // GRID_W, pos % GRID_W
    n = d_rot // 4
    inv = ROPE_THETA ** (-np.arange(n, dtype=np.float64) / n)
    ang = np.concatenate([row[:, None] * inv, col[:, None] * inv], axis=-1)
    cos, sin = np.cos(ang), np.sin(ang)
    half = d_rot // 2
    cc = np.zeros((seq, LANE))
    ss = np.zeros((seq, LANE))
    cc[:, :half] = cos
    cc[:, LANE // 2:LANE // 2 + half] = cos
    ss[:, :half] = -sin
    ss[:, LANE // 2:LANE // 2 + half] = sin
    return jnp.asarray(cc, F32), jnp.asarray(ss, F32)


def _swa_kernel(sink_ref, q_ref, kp_ref, kc_ref, kn_ref, kx_ref, vp_ref, vc_ref, vn_ref, vx_ref, o_ref,
                *, nb, seq, ctx_len):
    n = pl.program_id(1)
    hd, grp, lb = SWA_HEAD_DIM, SWA_HEADS // SWA_KV_HEADS, SWA_BLOCK
    m_rows, n_loc = grp * lb, 3 * lb
    r = lax.broadcasted_iota(jnp.int32, (m_rows, n_loc + ctx_len), 0)
    c = lax.broadcasted_iota(jnp.int32, (m_rows, n_loc + ctx_len), 1)
    qpos = n * lb + (r & (lb - 1))
    kpos = (n - 1) * lb + c
    local_ok = (jnp.abs(kpos - qpos) <= SWA_WINDOW) & (kpos >= 0) & (kpos < seq) & (n < nb)
    valid = local_ok | (c >= n_loc)
    ones = jnp.ones((n_loc + ctx_len, hd), BF16)
    def scores(kv):
        ks = slice(kv * hd, (kv + 1) * hd)
        qg = jnp.concatenate([q_ref[:, (kv * grp + g) * hd:(kv * grp + g + 1) * hd] for g in range(grp)], axis=0)
        keys = jnp.concatenate([kp_ref[:, ks], kc_ref[:, ks], kn_ref[:, ks], kx_ref[:, ks]], axis=0)
        return jnp.where(valid, _dot_nt(qg, keys), -1e30)

    outs = []
    s_next = scores(0)
    for kv in range(SWA_KV_HEADS):
        s = s_next
        if kv + 1 < SWA_KV_HEADS:
            s_next = scores(kv + 1)
        ks = slice(kv * hd, (kv + 1) * hd)
        vals = jnp.concatenate([vp_ref[:, ks], vc_ref[:, ks], vn_ref[:, ks], vx_ref[:, ks]], axis=0)
        v_aug = jnp.concatenate([vals, ones], axis=1)
        for g in range(grp):
            sg = s[g * lb:(g + 1) * lb]
            snk = sink_ref[kv * grp + g] * math.log2(math.e)
            m = jnp.maximum(jnp.max(sg, axis=1, keepdims=True), snk)
            ov = _dot(jnp.exp2(sg - m).astype(BF16), v_aug)
            outs.append((ov[:, :hd] / (ov[:, hd:hd + 1] + jnp.exp2(snk - m))).astype(BF16))
    o_ref[...] = jnp.concatenate(outs, axis=1)


def swa_attention(qkv, sink, *, rows):
    b, seq, ctx_len = rows.batch, rows.seq, rows.ctx_len
    lb = SWA_BLOCK
    nb, ncb = seq // lb, ctx_len // lb
    lat_blocks = rows.n_lat // lb
    qw = SWA_HEADS * SWA_HEAD_DIM
    kw = SWA_KV_HEADS * SWA_HEAD_DIM
    kcol, vcol = qw // kw, qw // kw + 1

    def qidx(bi, n):
        return (jnp.where(n < nb, bi * nb + n, lat_blocks + bi * ncb + (n - nb)), 0)

    def kidx(off, col):
        return lambda bi, n: (bi * nb + jnp.clip(n + off, 0, nb - 1), col)

    def xidx(col):
        return lambda bi, n: (rows.n_lat // ctx_len + bi, col)

    kern = functools.partial(_swa_kernel, nb=nb, seq=seq, ctx_len=ctx_len)
    return pl.pallas_call(
        kern,
        grid=(b, nb + ncb),
        in_specs=[
            pl.BlockSpec(memory_space=pltpu.SMEM),
            pl.BlockSpec((lb, qw), qidx),
            pl.BlockSpec((lb, kw), kidx(-1, kcol)),
            pl.BlockSpec((lb, kw), kidx(0, kcol)),
            pl.BlockSpec((lb, kw), kidx(1, kcol)),
            pl.BlockSpec((ctx_len, kw), xidx(kcol)),
            pl.BlockSpec((lb, kw), kidx(-1, vcol)),
            pl.BlockSpec((lb, kw), kidx(0, vcol)),
            pl.BlockSpec((lb, kw), kidx(1, vcol)),
            pl.BlockSpec((ctx_len, kw), xidx(vcol)),
        ],
        out_specs=pl.BlockSpec((lb, qw), qidx),
        out_shape=jax.ShapeDtypeStruct((rows.n_all, qw), BF16),
        compiler_params=_params(("parallel", "parallel")),
        name="swa_attention",
    )(sink, qkv, qkv, qkv, qkv, qkv, qkv, qkv, qkv, qkv)


def _mlstm_kernel(*refs):
    ins, (of_ref, ob_ref, c_ref, m_ref) = refs[:14], refs[14:]
    c = pl.program_id(1)
    L, H, dqk, dv = ML_CHUNK, ML_HEADS, ML_DQK, ML_DV
    scale = dqk ** -0.5

    @pl.when(c == 0)
    def _():
        c_ref[...] = jnp.zeros_like(c_ref)
        m_ref[...] = jnp.zeros_like(m_ref)

    row = lax.broadcasted_iota(jnp.int32, (L, L), 0)
    col = lax.broadcasted_iota(jnp.int32, (L, L), 1)
    ones = jnp.ones((L, LANE), BF16)
    for d, o_ref in enumerate((of_ref, ob_ref)):
        q_ref, k_ref, v_ref, gi_ref, gf_ref, bi_ref, bf_ref = ins[7 * d:7 * d + 7]
        mask = (col <= row) if d == 0 else (col >= row)
        ones_mask = jnp.where(mask, 1.0, 0.0).astype(BF16)
        i_blk = gi_ref[...] + bi_ref[...]
        f_blk = jax.nn.log_sigmoid(gf_ref[...] + bf_ref[...])
        f_hi = f_blk.astype(BF16)
        r1 = f_blk - f_hi.astype(F32)
        f_mid = r1.astype(BF16)
        f_lo = (r1 - f_mid.astype(F32)).astype(BF16)
        b_blk = _dot(jnp.concatenate([ones_mask] * 3, axis=1), jnp.concatenate([f_hi, f_mid, f_lo], axis=0))
        b_end = jnp.sum(f_blk, axis=0, keepdims=True)
        e_rows = (i_blk - b_blk).T
        m_prev_blk = m_ref[d]
        dec = b_end - b_blk + i_blk
        m_new_blk = jnp.maximum(b_end + m_prev_blk, jnp.max(dec, axis=0, keepdims=True))
        ws_blk = jnp.exp(dec - m_new_blk)
        gs_blk = jnp.exp(b_end + m_prev_blk - m_new_blk)
        for h in range(H):
            qh = q_ref[:, h * dqk:(h + 1) * dqk]
            kh = k_ref[:, h * dqk:(h + 1) * dqk]
            v_aug = jnp.concatenate([v_ref[:, h * dv:(h + 1) * dv], ones], axis=1)
            e = jnp.where(mask, e_rows[h:h + 1, :], -jnp.inf)
            m_prev = m_prev_blk[:, h:h + 1]
            mm = jnp.maximum(m_prev, jnp.max(e, axis=1, keepdims=True))
            s = (_dot_nt(qh, kh) * jnp.exp(e - (mm - math.log(scale)))).astype(BF16)
            g = jnp.exp(m_prev - mm) * scale
            ct = c_ref[d, h]
            lhs = jnp.concatenate([s, (qh.astype(F32) * g).astype(BF16)], axis=1)
            res = _dot(lhs, jnp.concatenate([v_aug, ct.astype(BF16)], axis=0))
            floor = jnp.exp(-(b_blk[:, h:h + 1] + mm))
            hout = res[:, :dv] / jnp.maximum(jnp.abs(res[:, dv:dv + 1]), floor)
            o_ref[:, h * dv:(h + 1) * dv] = hout.astype(o_ref.dtype)
            kw = (kh.astype(F32) * ws_blk[:, h:h + 1]).astype(BF16)
            c_ref[d, h] = gs_blk[:, h:h + 1] * ct + _dot_tn(kw, v_aug)
        m_ref[d] = m_new_blk


def mlstm_scan(z, gates, gate_b, *, rows):
    b, seq, ctx_len = rows.batch, rows.seq, rows.ctx_len
    L = ML_CHUNK
    assert ctx_len == L and seq % L == 0
    nlc = seq // L
    hq, hv = ML_HEADS * ML_DQK, ML_HEADS * ML_DV

    def rb(d):
        def index(bi, c):
            lat = bi * nlc + (c - 1 if d == 0 else nlc - c)
            return jnp.where(c == 0, rows.n_lat // L + bi, lat)
        return index

    in_specs, args = [], []
    for d in range(2):
        r = rb(d)
        in_specs += [
            pl.BlockSpec((L, hq), lambda bi, c, r=r: (r(bi, c), 0)),
            pl.BlockSpec((L, hq), lambda bi, c, r=r: (r(bi, c), 1)),
            pl.BlockSpec((L, hv), lambda bi, c, r=r: (r(bi, c), 1)),
            pl.BlockSpec((L, LANE), lambda bi, c, r=r, d=d: (r(bi, c), 2 * d)),
            pl.BlockSpec((L, LANE), lambda bi, c, r=r, d=d: (r(bi, c), 2 * d + 1)),
            pl.BlockSpec((1, LANE), lambda bi, c, d=d: (0, 2 * d)),
            pl.BlockSpec((1, LANE), lambda bi, c, d=d: (0, 2 * d + 1)),
        ]
        args += [z, z, z, gates, gates, gate_b, gate_b]
    return pl.pallas_call(
        _mlstm_kernel,
        grid=(b, nlc + 1),
        in_specs=in_specs,
        out_specs=[pl.BlockSpec((L, hv), lambda bi, c, r=rb(d): (r(bi, c), 0)) for d in range(2)],
        out_shape=[jax.ShapeDtypeStruct((rows.n_all, hv), BF16)] * 2,
        scratch_shapes=[
            pltpu.VMEM((2, ML_HEADS, ML_DQK, ML_DV + LANE), F32),
            pltpu.VMEM((2, 1, LANE), F32),
        ],
        compiler_params=_params(("parallel", "arbitrary")),
        name="mlstm_scan",
    )(*args)


def _mla_kernel(q_ref, kvl_ref, kvx_ref, krl_ref, krx_ref, o_ref, kk_ref, vt_ref, *, seq, sub, ck):
    @pl.when(pl.program_id(2) == 0)
    def _():
        kk_ref[:seq, :LANE] = kvl_ref[:, :LANE]
        kk_ref[:seq, LANE:] = krl_ref[...]
        kk_ref[seq:, :LANE] = kvx_ref[:, :LANE]
        kk_ref[seq:, LANE:] = krx_ref[...]
        vt_ref[:LANE, :seq] = kvl_ref[:, LANE:].astype(F32).T.astype(BF16)
        vt_ref[:LANE, seq:] = kvx_ref[:, LANE:].astype(F32).T.astype(BF16)
        vt_ref[LANE:, :] = jnp.ones((vt_ref.shape[0] - LANE, vt_ref.shape[1]), BF16)

    nsub = q_ref.shape[0] // sub
    n_keys = kk_ref.shape[0]
    bounds = [(k0, min(k0 + ck, n_keys)) for k0 in range(0, n_keys, ck)]
    nck = len(bounds)
    qs = [q_ref[s * sub:(s + 1) * sub, :] for s in range(nsub)]

    def scores(s, c):
        return _dot_nt(kk_ref[bounds[c][0]:bounds[c][1], :], qs[s])

    m = [None] * nsub
    acc = [None] * nsub
    st = [None] * nsub

    def absorb(s, c):
        cm = jnp.max(st[s], axis=0, keepdims=True)
        m_new = cm if c == 0 else jnp.maximum(m[s], cm)
        pv = _dot(vt_ref[:, bounds[c][0]:bounds[c][1]], jnp.exp2(st[s] - m_new).astype(BF16))
        acc[s] = pv if c == 0 else acc[s] * jnp.exp2(m[s] - m_new) + pv
        m[s] = m_new

    for k in range(-1, nck + nsub - 1):
        nxt = [scores(s, k - s + 1) if 0 <= k - s + 1 < nck else None for s in range(nsub)]
        for s in range(nsub):
            if 0 <= k - s < nck:
                absorb(s, k - s)
        for s in range(nsub):
            if nxt[s] is not None:
                st[s] = nxt[s]
    for s in range(nsub):
        o = acc[s][:LANE] / acc[s][LANE:LANE + 1]
        o_ref[s * sub:(s + 1) * sub, :] = o.T.astype(o_ref.dtype)


def mla_attention(q, kv, z, *, rows, tq=4096, sub=256, ck=1024):
    b, seq, ctx_len = rows.batch, rows.seq, rows.ctx_len
    tq = min(tq, seq)
    nq = seq // tq
    kr_col = z.shape[1] // LANE - 1
    ctx0 = rows.n_lat // ctx_len
    n_keys = seq + ctx_len
    ones_rows = 16
    return pl.pallas_call(
        functools.partial(_mla_kernel, seq=seq, sub=sub, ck=ck),
        grid=(b, MLA_HEADS, nq),
        in_specs=[
            pl.BlockSpec((tq, 2 * LANE), lambda bi, h, i: (bi * nq + i, h)),
            pl.BlockSpec((seq, 2 * LANE), lambda bi, h, i: (bi, h)),
            pl.BlockSpec((ctx_len, 2 * LANE), lambda bi, h, i: (ctx0 + bi, h)),
            pl.BlockSpec((seq, LANE), lambda bi, h, i: (bi, kr_col)),
            pl.BlockSpec((ctx_len, LANE), lambda bi, h, i: (ctx0 + bi, kr_col)),
        ],
        out_specs=pl.BlockSpec((tq, LANE), lambda bi, h, i: (bi * nq + i, h)),
        out_shape=jax.ShapeDtypeStruct((rows.n_lat, MLA_HEADS * MLA_V), BF16),
        scratch_shapes=[pltpu.VMEM((n_keys, 2 * LANE), BF16), pltpu.VMEM((LANE + ones_rows, n_keys), BF16)],
        compiler_params=_params(("parallel", "parallel", "arbitrary")),
        name="mla_attention",
    )(q, kv, kv, z, z)


def _hy_filter_kernel(ft_ref, w1_ref, b1_ref, w2_ref, b2_ref, w3_ref, b3_ref, fr_ref, fr3_ref, w4_ref, b4_ref,
                      dec_ref, o_ref, a_ref, *, back_from):
    j = pl.program_id(0)
    fw = HY_FILTER_W
    hdot = functools.partial(jnp.dot, preferred_element_type=F32, precision=HIGHEST)

    @pl.when(j == 0)
    def _():
        fr = fr_ref[...]
        a = jnp.sin(fr * (hdot(ft_ref[...], w1_ref[...]) + b1_ref[...]))
        a = jnp.sin(fr * (hdot(a, w2_ref[...]) + b2_ref[...]))
        a3 = jnp.sin(fr3_ref[...] * (hdot(a, w3_ref[...]) + b3_ref[...]))
        hi = a3.astype(BF16)
        lo = (a3 - hi.astype(F32)).astype(BF16)
        lane = lax.broadcasted_iota(jnp.int32, a3.shape, 1)
        a_ref[...] = jnp.where((lane >= fw) & (lane < 2 * fw), lo, hi)

    w4 = w4_ref[...]
    w_hi = w4.astype(BF16)
    w_lo = (w4 - w_hi.astype(F32)).astype(BF16)
    rhs = jnp.concatenate([w_hi, w_hi, w_lo, jnp.zeros_like(w_hi)], axis=0)
    t = ft_ref[:, 0:1]
    filt = (_dot(a_ref[...], rhs) + b4_ref[...]) * jnp.exp(-t * jnp.abs(dec_ref[...]))
    row = lax.broadcasted_iota(jnp.int32, filt.shape, 0)
    o_ref[...] = jnp.where(jnp.logical_and(row == 0, j >= back_from), 0.0, filt).astype(o_ref.dtype)


def hyena_filters(seq, w1, b1, w2, b2, w3, b3, w4, b4, freq, decay, tn=512):
    t = np.linspace(0.0, 1.0, seq)[:, None]
    w = (2.0 * math.pi / seq) * np.arange(seq)[:, None]
    bands = np.linspace(1e-4, HY_BANDS - 1, HY_BANDS)[None, :]
    feats = np.zeros((seq, LANE))
    feats[:, :1 + 2 * HY_BANDS] = np.concatenate([t, np.cos(bands * w), -np.sin(bands * w)], axis=-1)
    n = w4.shape[1]
    fw = HY_FILTER_W
    w1p = jnp.zeros((LANE, fw), F32).at[:w1.shape[0]].set(w1)
    row = lambda a: a.reshape(1, -1)
    tile3 = lambda a: jnp.concatenate([a, a, a, jnp.zeros_like(a)], axis=-1)
    full = lambda shape: pl.BlockSpec(shape, lambda j: (0, 0))
    return pl.pallas_call(
        functools.partial(_hy_filter_kernel, back_from=(n // 2) // tn),
        grid=(n // tn,),
        in_specs=[full((seq, LANE)), full((LANE, fw)), full((1, fw)), full((fw, fw)), full((1, fw)),
                  full((fw, 4 * fw)), full((1, 4 * fw)), full((1, fw)), full((1, 4 * fw)),
                  pl.BlockSpec((fw, tn), lambda j: (0, j)),
                  pl.BlockSpec((1, tn), lambda j: (0, j)),
                  pl.BlockSpec((1, tn), lambda j: (0, j))],
        out_specs=pl.BlockSpec((seq, tn), lambda j: (0, j)),
        out_shape=jax.ShapeDtypeStruct((seq, n), BF16),
        scratch_shapes=[pltpu.VMEM((seq, 4 * fw), BF16)],
        compiler_params=_params(("arbitrary",)),
        name="hyena_filters",
    )(jnp.asarray(feats, F32), w1p, row(b1), w2, row(b2), tile3(w3), tile3(row(b3)), row(freq), tile3(row(freq)),
      w4, row(b4), row(decay))


def _dft_mats(seq, p):
    n = 2 * seq
    q = n // p
    k1n = q // 2 + 1
    hi, k1 = np.arange(q // 2), np.arange(k1n)
    th = 2 * np.pi * np.outer(k1, hi) / q
    ma = np.zeros((2 * k1n, q // 2))
    ma[0::2], ma[1::2] = np.cos(th), -np.sin(th)
    lo = np.arange(p)
    wf = np.zeros((k1n, 2 * p, 2 * p))
    wi = np.zeros((k1n, 2 * p, 2 * p))
    for k in k1:
        ph = -2 * np.pi * (np.outer(lo, lo) / p + k * lo[None, :] / n)
        er, ei = np.cos(ph), np.sin(ph)
        wf[k] = np.block([[er, -ei], [ei, er]])
        wi[k] = np.block([[er.T, ei.T], [-ei.T, er.T]])
    c = np.full(k1n, 2.0)
    c[0] = c[-1] = 1.0
    th2 = 2 * np.pi * np.outer(hi, k1) / q
    md = np.zeros((q // 2, 2 * k1n))
    md[:, 0::2], md[:, 1::2] = c * np.cos(th2) / n, -c * np.sin(th2) / n
    as_bf16 = lambda a: jnp.asarray(a, F32).astype(BF16)
    return ma, as_bf16(wf), as_bf16(wi), md


HY_HALO = 16


def _hy_in_kernel(h_ref, hp_ref, hn_ref, g_ref, sh_ref, sc_ref, w_ref, cw_ref, cb_ref, o_ref, u_ref, inv_ref,
                  invh_ref, *, per_seq):
    i = pl.program_id(0)
    tm = h_ref.shape[0]

    @pl.when(pl.program_id(1) == 0)
    def _():
        mod = lambda x: x * (1.0 + sc_ref[0]) + sh_ref[0]
        u_ref[HY_HALO:HY_HALO + tm, :] = mod(_rms_ref(h_ref, inv_ref, g_ref[...])).astype(BF16)
        keep_prev = (i % per_seq != 0).astype(F32)
        keep_next = (i % per_seq != per_seq - 1).astype(F32)
        u_ref[:HY_HALO, :] = (mod(_rms_ref(hp_ref, invh_ref, g_ref[...])) * keep_prev).astype(BF16)
        u_ref[HY_HALO + tm:, :] = (mod(_rms_ref(hn_ref, invh_ref, g_ref[...])) * keep_next).astype(BF16)

    acc = _dot(u_ref[...], w_ref[...])
    n = acc.shape[0]
    mid = slice(HY_HALO, HY_HALO + tm)
    prev = pltpu.roll(acc, 1, 0)[mid]
    nxt = pltpu.roll(acc, n - 1, 0)[mid]
    cw = cw_ref[...]
    o_ref[...] = (prev * cw[0:1] + acc[mid] * cw[1:2] + nxt * cw[2:3] + cb_ref[...]).astype(o_ref.dtype)


def hyena_in_conv(h, w, conv_w, conv_b, mods, gain, *, rows, tm, tn, layer):
    k, n = w.shape
    n_rows = rows.n_lat
    per_seq = rows.seq // tm
    halo_per_blk, n_halo = tm // HY_HALO, n_rows // HY_HALO
    return pl.pallas_call(
        functools.partial(_hy_in_kernel, per_seq=per_seq),
        grid=(n_rows // tm, n // tn),
        in_specs=[
            pl.BlockSpec((tm, k), lambda i, j: (i, 0)),
            pl.BlockSpec((HY_HALO, k), lambda i, j: (jnp.maximum(i * halo_per_blk - 1, 0), 0)),
            pl.BlockSpec((HY_HALO, k), lambda i, j: (jnp.minimum((i + 1) * halo_per_blk, n_halo - 1), 0)),
            pl.BlockSpec((1, k), lambda i, j: (0, 0)),
            pl.BlockSpec((1, 1, k), rows.mod_index(layer, 0, tm)),
            pl.BlockSpec((1, 1, k), rows.mod_index(layer, 1, tm)),
            pl.BlockSpec((k, tn), lambda i, j: (0, j)),
            pl.BlockSpec((HY_SHORT, tn), lambda i, j: (0, j)),
            pl.BlockSpec((1, tn), lambda i, j: (0, j)),
        ],
        out_specs=pl.BlockSpec((tm, tn), lambda i, j: (i, j)),
        out_shape=jax.ShapeDtypeStruct((n_rows, n), BF16),
        scratch_shapes=[pltpu.VMEM((tm + 2 * HY_HALO, k), BF16), pltpu.VMEM((tm, 1), F32),
                        pltpu.VMEM((HY_HALO, 1), F32)],
        compiler_params=_params(("parallel", "arbitrary")),
        name="hyena_in",
    )(h, h, h, gain.reshape(1, k), mods, mods, w, conv_w, conv_b.reshape(1, n))


def _hy_stage_a_kernel(x_ref, m_ref, o_ref):
    qh, lg, g, wc = x_ref.shape
    mo = o_ref.shape[0]
    for l in range(lg):
        x = x_ref[:, l].reshape(qh * g, wc).astype(BF16)
        y = _dot(m_ref[...], x).reshape(mo, g, wc)
        o_ref[:, l * g:(l + 1) * g, :] = y.astype(o_ref.dtype)


def hyena_stage_a(x, mak, *, p, width=2048, col_blk=0, channels=None):
    bx, seq, c_all = x.shape
    c = c_all if channels is None else channels
    g = HY_G
    qh = seq // p
    mo = mak.shape[0] // g
    x5 = x.reshape(bx, qh, p // g, g, c_all)
    lg = HY_LG
    return pl.pallas_call(
        _hy_stage_a_kernel,
        grid=(bx, p // (g * lg), c // width),
        in_specs=[pl.BlockSpec((None, qh, lg, g, width), lambda bi, l, j: (bi, 0, l, 0, col_blk + j)),
                  pl.BlockSpec(mak.shape, lambda bi, l, j: (0, 0))],
        out_specs=pl.BlockSpec((None, mo, lg * g, width), lambda bi, l, j: (bi, 0, l, j)),
        out_shape=jax.ShapeDtypeStruct((bx, mo, p, c), BF16),
        compiler_params=_params(("parallel", "parallel", "parallel")),
        name="hyena_stage_a",
    )(x5, mak)


def _hy_spec_kernel(a0_ref, a1_ref, wf_ref, o_ref):
    p = o_ref.shape[2]
    cb = o_ref.shape[3]
    wf = wf_ref[0]
    x0 = _dot(wf, a0_ref[0].reshape(2 * p, cb))
    x1 = _dot(wf, a1_ref[0].reshape(2 * p, cb))
    o_ref[0, 0] = x0[:p] + x1[:p]
    o_ref[0, 1] = x0[p:] - x1[p:]


def hyena_spectrum(af, wf, *, d, cb=2048):
    k1n, _, p, _ = af.shape
    nblk = (HY_ORDER * d) // cb
    return pl.pallas_call(
        _hy_spec_kernel,
        grid=(k1n, nblk),
        in_specs=[pl.BlockSpec((1, 2, p, cb), lambda k, j: (k, 0, 0, j)),
                  pl.BlockSpec((1, 2, p, cb), lambda k, j: (k, 0, 0, nblk + j)),
                  pl.BlockSpec((1, 2 * p, 2 * p), lambda k, j: (k, 0, 0))],
        out_specs=pl.BlockSpec((1, 2, p, cb), lambda k, j: (k, 0, 0, j)),
        out_shape=jax.ShapeDtypeStruct((k1n, 2, p, HY_ORDER * d), F32),
        compiler_params=_params(("parallel", "parallel")),
        name="hyena_spectrum",
    )(af, af, wf)


def _hy_mid_kernel(a_ref, wf_ref, wi_ref, h_ref, o_ref):
    b, _, p, cb = a_ref.shape
    hr, hi = h_ref[0], h_ref[1]
    for bi in range(b):
        x = _dot(wf_ref[...], a_ref[bi].reshape(2 * p, cb))
        xr, xi = x[:p], x[p:]
        y = jnp.concatenate([xr * hr - xi * hi, xr * hi + xi * hr], axis=0).astype(BF16)
        o_ref[bi] = _dot(wi_ref[...], y).reshape(2, p, cb).astype(o_ref.dtype)


def hyena_mid(a, wf, wi, spec, *, order, d, cb=1024):
    b, k1n, _, p, _ = a.shape
    nblk = d // cb
    data = pl.BlockSpec((b, None, 2, p, cb), lambda k, j: (0, k, 0, 0, j))
    mat = pl.BlockSpec((None, 2 * p, 2 * p), lambda k, j: (k, 0, 0))
    return pl.pallas_call(
        _hy_mid_kernel,
        grid=(k1n, nblk),
        in_specs=[data, mat, mat,
                  pl.BlockSpec((None, 2, p, cb), lambda k, j: (k, 0, 0, order * nblk + j))],
        out_specs=data,
        out_shape=jax.ShapeDtypeStruct(a.shape, BF16),
        compiler_params=_params(("parallel", "parallel")),
        name="hyena_mid",
    )(a, wf, wi, spec)


def _hy_stage_d_kernel(c_ref, m_ref, x_ref, z_ref, skip_ref, *rest, fused):
    ma_ref, o_ref, a_ref = rest if fused else (None, rest[0], None)
    mo, rows_c, wc = c_ref.shape
    qh, lg, g, _ = x_ref.shape
    for l in range(lg):
        conv = _dot(m_ref[...], c_ref[:, l * g:(l + 1) * g, :].reshape(mo * g, wc))
        z = z_ref[:, l].reshape(qh * g, wc).astype(F32)
        x = x_ref[:, l].reshape(qh * g, wc).astype(F32)
        y = (x * (conv + z * skip_ref[...])).astype(o_ref.dtype)
        o_ref[:, l] = y.reshape(qh, g, wc)
        if fused:
            a_ref[:, l * g:(l + 1) * g, :] = _dot(ma_ref[...], y).reshape(mo, g, wc).astype(a_ref.dtype)


def hyena_stage_d(c, mdk, xg, z, skip, *, p, mak=None, width=2048, xg_blk=0, z_blk=0):
    b, mo, _, d = c.shape
    g = HY_G
    seq = xg.shape[1]
    qh = seq // p
    fused = mak is not None
    v5 = lambda a: a.reshape(b, qh, p // g, g, a.shape[2])
    lg = HY_LG
    tblk = lambda off: pl.BlockSpec((None, qh, lg, g, width), lambda bi, l, j: (bi, 0, l, 0, off + j))
    tspec = tblk(0)
    fspec = pl.BlockSpec((None, mo, lg * g, width), lambda bi, l, j: (bi, 0, l, j))
    in_specs = [fspec, pl.BlockSpec(mdk.shape, lambda bi, l, j: (0, 0)), tblk(xg_blk), tblk(z_blk),
                pl.BlockSpec((1, width), lambda bi, l, j: (0, j))]
    args = [c, mdk, v5(xg), v5(z), skip.reshape(1, d)]
    out_specs, out_shape = [tspec], [jax.ShapeDtypeStruct((b, qh, p // g, g, d), BF16)]
    if fused:
        in_specs.append(pl.BlockSpec(mak.shape, lambda bi, l, j: (0, 0)))
        args.append(mak)
        out_specs.append(fspec)
        out_shape.append(jax.ShapeDtypeStruct(c.shape, BF16))
    outs = pl.pallas_call(
        functools.partial(_hy_stage_d_kernel, fused=fused),
        grid=(b, p // (g * lg), d // width),
        in_specs=in_specs,
        out_specs=out_specs,
        out_shape=out_shape,
        compiler_params=_params(("parallel", "parallel", "parallel")),
        name="hyena_stage_d",
    )(*args)
    y = outs[0].reshape(b, seq, d)
    return (y, outs[1]) if fused else y


def hyena_long_convs(x3, filt, skip, *, rows, width=2048):
    b, seq = rows.batch, rows.seq
    d = x3.shape[1] // 3
    p = HY_P
    k1n = seq // p + 1
    ma, wf, wi, md = _dft_mats(seq, p)
    eye = np.eye(HY_G)
    as_bf16 = lambda a: jnp.asarray(a, F32).astype(BF16)
    mak, mdk = as_bf16(np.kron(ma, eye)), as_bf16(np.kron(md, eye))
    af = hyena_stage_a(filt.reshape(1, seq, filt.shape[1]), mak, p=p)
    spec = hyena_spectrum(af.reshape(k1n, 2, p, filt.shape[1]), wf, d=d)
    x3 = x3.reshape(b, seq, 3 * d)
    blk = d // width
    a = hyena_stage_a(x3, mak, p=p, width=width, col_blk=2 * blk, channels=d)
    c = hyena_mid(a.reshape(b, k1n, 2, p, d), wf, wi, spec, order=0, d=d)
    y, a = hyena_stage_d(c.reshape(a.shape), mdk, x3, x3, skip[0], p=p, mak=mak, width=width, xg_blk=0, z_blk=2 * blk)
    c = hyena_mid(a.reshape(b, k1n, 2, p, d), wf, wi, spec, order=1, d=d)
    y = hyena_stage_d(c.reshape(a.shape), mdk, x3, y, skip[1], p=p, width=width, xg_blk=blk, z_blk=0)
    return y.reshape(rows.n_lat, d)


def _mlstm_gate_weights(w_in, gate_b):
    h = ML_HEADS
    o4 = 2 * h * ML_DQK + 2 * h * ML_DV
    wg = w_in[:, o4:].reshape(-1, 2, 2, h)
    bg = gate_b.reshape(2, 2, h)
    w_out = jnp.zeros((w_in.shape[0], 4 * LANE), F32)
    b_out = jnp.zeros((1, 4 * LANE), F32)
    for d in range(2):
        for gate in range(2):
            lo = (2 * d + gate) * LANE
            w_out = w_out.at[:, lo:lo + h].set(wg[:, gate, d])
            b_out = b_out.at[0, lo:lo + h].set(bg[gate, d])
    return w_out, b_out


def _mla_weights(w_in, w_uq):
    r2 = MLA_Q_RANK + MLA_KV_RANK
    half = MLA_ROPE // 2

    def spread(w):
        z = jnp.zeros(w.shape[:-1] + (half,), w.dtype)
        return jnp.concatenate([w[..., :half], z, w[..., half:], z], axis=-1)

    w_in_p = jnp.concatenate([w_in[:, :r2], spread(w_in[:, r2:])], axis=1)
    wq = w_uq.reshape(w_uq.shape[0], MLA_HEADS, MLA_NOPE + MLA_ROPE)
    wq_p = jnp.concatenate([wq[..., :MLA_NOPE], spread(wq[..., MLA_NOPE:])], axis=-1)
    wq_p = wq_p * ((MLA_NOPE + MLA_ROPE) ** -0.5 * math.log2(math.e))
    return w_in_p, wq_p.reshape(w_uq.shape[0], MLA_HEADS * 2 * LANE)


def kernel(x, c, ctx, c_ctx, ada_w, ada_b, norm_g, mlp_w1, mlp_w2, swa_w_qkv, swa_sink, swa_w_o, ml_w_in, ml_gate_b, ml_head_g, ml_w_o, mla_w_in, mla_q_g, mla_kv_g, mla_w_uq, mla_w_ukv, mla_w_o, hy_w_in, hy_conv_w, hy_conv_b, hy_f_w1, hy_f_b1, hy_f_w2, hy_f_b2, hy_f_w3, hy_f_b3, hy_f_w4, hy_f_b4, hy_f_freq, hy_decay, hy_skip, hy_w_o):
    b, seq, d = x.shape
    ctx_len = ctx.shape[1]
    depth = ada_w.shape[0]
    assert depth == 4 and b < 8
    rows = Rows(b, seq, ctx_len)
    tm = 512
    tm_lin = 1024 if rows.n_lat % 1024 == 0 and rows.n_ctx % 1024 == 0 else 512
    bf = lambda w: w.astype(BF16)
    mlp_w1b, mlp_w2b = bf(mlp_w1), bf(mlp_w2)

    cond = jnp.zeros((8, d), F32).at[:b].set(c).at[b].set(c_ctx)
    mods = ada_mods(cond, ada_w, ada_b).reshape(depth * 8, 1, N_MOD * d)

    common = dict(rows=rows, tm=tm_lin, mods=mods)

    def finish(h, o_args, w_o, layer, n_rows, mode="plain", h2=None):
        h = outproj_residual(o_args, bf(w_o), h, mods, norm_g[layer, 1], rows=rows, n_rows=n_rows, tm=tm,
                             layer=layer, mode=mode, h2=h2, name=f"outproj{layer}")
        return mlp_residual(h, mlp_w1b, mlp_w2b, mods, norm_g[layer, 2], norm_g[layer, 3],
                            rows=rows, n_rows=n_rows, tm=tm, tf=1024, layer=layer)

    n_qk_groups = SWA_HEADS + SWA_KV_HEADS
    q_cols = SWA_HEADS * SWA_HEAD_DIM
    q_scale = SWA_HEAD_DIM ** -0.5 * math.log2(math.e)
    w_qkv = jnp.concatenate([swa_w_qkv[0][:, :q_cols] * q_scale, swa_w_qkv[0][:, q_cols:]], axis=1)
    x_rows, ctx_rows = x.reshape(rows.n_lat, d), ctx.reshape(rows.n_ctx, d)
    qkv = linear(x_rows, bf(w_qkv), x2=ctx_rows, n_rows=rows.n_all, tn=512, out_dtype=BF16, prologue="norm_mod",
                 gain=norm_g[0, 0], layer=0, rope_tabs=_rope_tables(seq, SWA_HEAD_DIM), rope_pattern=(True,) * 4,
                 rope_jmax=n_qk_groups // 4, name="swa_qkv", **common)
    o = swa_attention(qkv, swa_sink[0], rows=rows)
    h = finish(x_rows, o, swa_w_o[0], 0, rows.n_all, h2=ctx_rows)

    w_gate, b_gate = _mlstm_gate_weights(ml_w_in[0], ml_gate_b[0])
    z = linear(h, bf(ml_w_in[0]), n_rows=rows.n_all, tn=1536, out_dtype=BF16, prologue="norm_mod",
               gain=norm_g[1, 0], layer=1, n_out=2 * ML_HEADS * (ML_DQK + ML_DV), name="mlstm_in", **common)
    gates = linear(h, bf(w_gate), n_rows=rows.n_all, tn=4 * LANE, out_dtype=F32, prologue="norm_mod",
                   gain=norm_g[1, 0], layer=1, name="mlstm_gates", **common)
    h_fwd, h_bwd = mlstm_scan(z, gates, b_gate, rows=rows)
    h = finish(h, (h_fwd, h_bwd, z, ml_head_g[0]), ml_w_o[0], 1, rows.n_all, mode="mlstm")

    w_in_p, w_uq_p = _mla_weights(mla_w_in[0], mla_w_uq[0])
    rope_mla = _rope_tables(seq, MLA_ROPE)
    zc = linear(h, bf(w_in_p), n_rows=rows.n_all, tn=w_in_p.shape[1], out_dtype=BF16, prologue="norm_mod",
                gain=norm_g[2, 0], layer=2, rope_tabs=rope_mla, rope_pattern=(False,) * 8 + (True,),
                name="mla_in", **common)
    q = linear(zc, bf(w_uq_p), n_rows=rows.n_lat, tn=2048, out_dtype=BF16, x_cols=0, prologue="norm",
               gain=mla_q_g[0], rope_tabs=rope_mla, rope_pattern=(False, True) * 8, name="mla_q",
               rows=rows, tm=tm_lin)
    kv = linear(zc, bf(mla_w_ukv[0]), n_rows=rows.n_all, tn=2048, out_dtype=BF16, x_cols=1, prologue="norm",
                gain=mla_kv_g[0], name="mla_kv", rows=rows, tm=tm_lin)
    o = mla_attention(q, kv, zc, rows=rows)
    h = finish(h, o, mla_w_o[0], 2, rows.n_lat)

    x3 = hyena_in_conv(h, bf(hy_w_in[0]), hy_conv_w[0], hy_conv_b[0], mods, norm_g[3, 0], rows=rows, tm=tm_lin,
                       tn=1536, layer=3)
    filt = hyena_filters(seq, hy_f_w1[0], hy_f_b1[0], hy_f_w2[0], hy_f_b2[0], hy_f_w3[0], hy_f_b3[0],
                         hy_f_w4[0], hy_f_b4[0], hy_f_freq[0], hy_decay[0])
    y = hyena_long_convs(x3, filt, hy_skip[0], rows=rows)
    h = finish(h, y, hy_w_o[0], 3, rows.n_lat)
    return h.reshape(b, seq, d)
```

```python
import functools
import math

import numpy as np
import jax
import jax.numpy as jnp
from jax import lax
from jax.experimental import pallas as pl
from jax.experimental.pallas import tpu as pltpu

F32, BF16 = jnp.float32, jnp.bfloat16
HIGHEST = lax.Precision.HIGHEST

RMS_EPS = 1e-6
ROPE_THETA = 10000.0
GRID_W = 64
N_MOD = 6
LANE = 128

SWA_HEADS, SWA_KV_HEADS, SWA_HEAD_DIM, SWA_WINDOW, SWA_BLOCK = 16, 4, 128, 128, 128
ML_HEADS, ML_DQK, ML_DV = 8, 128, 256
ML_CHUNK = 256
MLA_HEADS, MLA_Q_RANK, MLA_KV_RANK, MLA_NOPE, MLA_ROPE, MLA_V = 16, 512, 512, 128, 64, 128
HY_ORDER, HY_BANDS, HY_FILTER_W, HY_SHORT = 2, 16, 64, 3
HY_P = 256
HY_G = 16
HY_LG = 2

VMEM_LIMIT = 48 * 1024 * 1024


def _params(sem):
    return pltpu.CompilerParams(dimension_semantics=sem, vmem_limit_bytes=VMEM_LIMIT)


def _dot(a, b):
    return jnp.dot(a, b, preferred_element_type=F32)


def _dot_nt(a, b):
    return lax.dot_general(a, b, (((1,), (1,)), ((), ())), preferred_element_type=F32)


def _dot_tn(a, b):
    return lax.dot_general(a, b, (((0,), (0,)), ((), ())), preferred_element_type=F32)


def _rms(x, g):
    return x * lax.rsqrt(jnp.mean(x * x, axis=-1, keepdims=True) + RMS_EPS) * g


def _rms_ref(x_ref, inv_ref, g):
    x = x_ref[...].astype(F32)
    inv_ref[...] = lax.rsqrt(jnp.mean(x * x, axis=-1, keepdims=True) + RMS_EPS)
    return x_ref[...].astype(F32) * inv_ref[...] * g


def _ada_kernel(s_ref, w_ref, b_ref, o_ref):
    s = s_ref[...]
    s = s * jax.nn.sigmoid(s)
    hi = s.astype(BF16)
    lo = (s - hi.astype(F32)).astype(BF16)
    w = w_ref[0]
    w_hi = w.astype(BF16)
    w_lo = (w - w_hi.astype(F32)).astype(BF16)
    r = _dot(jnp.concatenate([hi, lo], axis=0), w_hi)
    n = s.shape[0]
    o_ref[0] = r[:n] + r[n:] + _dot(hi, w_lo) + b_ref[0]


def ada_mods(cond, ada_w, ada_b, tn=1536):
    depth, d, n = ada_w.shape
    rows = cond.shape[0]
    return pl.pallas_call(
        _ada_kernel,
        grid=(depth, n // tn),
        in_specs=[
            pl.BlockSpec((rows, d), lambda l, j: (0, 0)),
            pl.BlockSpec((1, d, tn), lambda l, j: (l, 0, j)),
            pl.BlockSpec((1, 1, tn), lambda l, j: (l, 0, j)),
        ],
        out_specs=pl.BlockSpec((1, rows, tn), lambda l, j: (l, 0, j)),
        out_shape=jax.ShapeDtypeStruct((depth, rows, n), F32),
        compiler_params=_params(("parallel", "parallel")),
        name="ada_mods",
    )(cond, ada_w, ada_b.reshape(depth, 1, n))


class Rows:
    def __init__(self, batch, seq, ctx_len):
        self.batch, self.seq, self.ctx_len = batch, seq, ctx_len
        self.n_lat = batch * seq
        self.n_ctx = batch * ctx_len
        self.n_all = self.n_lat + self.n_ctx

    def mod_index(self, layer, k, tm):
        lat_blocks, per_batch = self.n_lat // tm, self.seq // tm

        def index(i, *_):
            b = jnp.where(i < lat_blocks, i // per_batch, self.batch)
            return (layer * 8 + b, 0, k)

        return index


def _linear_kernel(*refs, prologue, rope, rope_pattern, n_lat_blocks, dual, single):
    it = iter(refs)
    x_ref = next(it)
    x2_ref = next(it) if dual else None
    g_ref = next(it) if prologue in ("norm", "norm_mod") else None
    sh_ref = next(it) if prologue == "norm_mod" else None
    sc_ref = next(it) if prologue == "norm_mod" else None
    w_ref = next(it)
    cc_ref = next(it) if rope else None
    ss_ref = next(it) if rope else None
    o_ref = next(it)
    u_ref = next(it)
    inv_ref = next(it)
    first = pl.program_id(1) == 0

    def prologue_from(src_ref):
        if prologue in ("norm", "norm_mod"):
            x = _rms_ref(src_ref, inv_ref, g_ref[...])
        else:
            x = src_ref[...].astype(F32)
        if prologue == "norm_mod":
            x = x * (1.0 + sc_ref[0]) + sh_ref[0]
        u_ref[...] = x.astype(BF16)

    if dual:
        is_lat = pl.program_id(0) < n_lat_blocks
        pl.when(jnp.logical_and(first, is_lat))(functools.partial(prologue_from, x_ref))
        pl.when(jnp.logical_and(first, jnp.logical_not(is_lat)))(functools.partial(prologue_from, x2_ref))
    elif single:
        prologue_from(x_ref)
    else:
        pl.when(first)(functools.partial(prologue_from, x_ref))

    acc = _dot(u_ref[...], w_ref[...])

    def plain():
        o_ref[...] = acc.astype(o_ref.dtype)

    def roped():
        cc, ss = cc_ref[...], ss_ref[...]
        segs = []
        for gi, on in enumerate(rope_pattern):
            seg = acc[:, gi * LANE:(gi + 1) * LANE]
            if on:
                seg = seg * cc + pltpu.roll(seg, LANE // 2, 1) * ss
            segs.append(seg)
        o_ref[...] = jnp.concatenate(segs, axis=1).astype(o_ref.dtype)

    if rope:
        roped()
    else:
        plain()


def linear(x, w, *, rows, n_rows, tm, tn, out_dtype, x_cols=None, prologue="none", gain=None,
           mods=None, layer=0, mod_k=(0, 1), rope_tabs=None, rope_pattern=(), rope_jmax=1 << 30,
           n_out=None, x2=None, name="linear"):
    k, n = w.shape
    n = n if n_out is None else n_out
    xc = 0 if x_cols is None else x_cols
    rope = rope_tabs is not None
    dual = x2 is not None
    n_lat_blocks = rows.n_lat // tm
    if dual:
        in_specs = [pl.BlockSpec((tm, k), lambda i, j: (jnp.minimum(i, n_lat_blocks - 1), 0)),
                    pl.BlockSpec((tm, k), lambda i, j: (jnp.maximum(i - n_lat_blocks, 0), 0),
                                 pipeline_mode=pl.Buffered(1))]
        args = [x, x2]
    else:
        in_specs = [pl.BlockSpec((tm, k), lambda i, j: (i, xc))]
        args = [x]
    if prologue in ("norm", "norm_mod"):
        in_specs.append(pl.BlockSpec((1, k), lambda i, j: (0, 0)))
        args.append(gain.reshape(1, k))
    if prologue == "norm_mod":
        for mk in mod_k:
            in_specs.append(pl.BlockSpec((1, 1, k), rows.mod_index(layer, mk, tm)))
            args.append(mods)
    in_specs.append(pl.BlockSpec((k, tn), lambda i, j: (0, j)))
    args.append(w)
    if rope:
        per_seq = rows.seq // tm

        def tab_index(i, j):
            rotate = jnp.logical_and(i < n_lat_blocks, j < rope_jmax)
            return (jnp.where(rotate, 0, 1), i % per_seq, 0)

        for t, fill in zip(rope_tabs, (1.0, 0.0)):
            in_specs.append(pl.BlockSpec((None, tm, LANE), tab_index))
            args.append(jnp.stack([t, jnp.full_like(t, fill)]))
    kern = functools.partial(_linear_kernel, prologue=prologue, rope=rope, rope_pattern=tuple(rope_pattern),
                             n_lat_blocks=n_lat_blocks, dual=dual, single=(n == tn))
    return pl.pallas_call(
        kern,
        grid=(n_rows // tm, n // tn),
        in_specs=in_specs,
        out_specs=pl.BlockSpec((tm, tn), lambda i, j: (i, j)),
        out_shape=jax.ShapeDtypeStruct((n_rows, n), out_dtype),
        scratch_shapes=[pltpu.VMEM((tm, k), BF16), pltpu.VMEM((tm, 1), F32)],
        compiler_params=_params(("parallel", "arbitrary")),
        name=name,
    )(*args)


def _outproj_kernel(*refs, mode, n_lat_blocks):
    it = iter(refs)
    if mode == "mlstm":
        hf_ref, hb_ref, og_ref, hg_ref = next(it), next(it), next(it), next(it)
    else:
        o_ref_in = next(it)
    w_ref, h_ref = next(it), next(it)
    h2_ref = next(it) if n_lat_blocks is not None else None
    gate_ref, g_ref, out_ref = next(it), next(it), next(it)
    y_ref, inv_ref = next(it), next(it)
    if mode == "mlstm":
        hg = hg_ref[...]
        y = None
        for h in range(ML_HEADS):
            sl = slice(h * ML_DV, (h + 1) * ML_DV)
            hs = hf_ref[:, sl].astype(F32) + hb_ref[:, sl].astype(F32)
            oh = (_rms(hs, hg[:, sl]) * jax.nn.sigmoid(og_ref[:, sl].astype(F32))).astype(BF16)
            part = _dot(oh, w_ref[sl, :])
            y = part if y is None else y + part
        y_ref[...] = y
    else:
        y_ref[...] = _dot(o_ref_in[...], w_ref[...])
    upd = gate_ref[0] * _rms_ref(y_ref, inv_ref, g_ref[...])
    if h2_ref is None:
        out_ref[...] = h_ref[...] + upd
    else:
        out_ref[...] = jnp.where(pl.program_id(0) < n_lat_blocks, h_ref[...], h2_ref[...]) + upd


def outproj_residual(o_args, w, h, mods, gain, *, rows, n_rows, tm, layer, mode="plain", h2=None, name="outproj"):
    k, d = w.shape
    n_lat_blocks = rows.n_lat // tm if h2 is not None else None
    if mode == "mlstm":
        h_fwd, h_bwd, z, head_g = o_args
        in_specs = [
            pl.BlockSpec((tm, k), lambda i: (i, 0)),
            pl.BlockSpec((tm, k), lambda i: (i, 0)),
            pl.BlockSpec((tm, k), lambda i: (i, 2)),
            pl.BlockSpec((1, k), lambda i: (0, 0)),
        ]
        args = [h_fwd, h_bwd, z, head_g.reshape(1, k)]
    else:
        in_specs = [pl.BlockSpec((tm, k), lambda i: (i, 0))]
        args = [o_args]
    in_specs.append(pl.BlockSpec((k, d), lambda i: (0, 0)))
    args.append(w)
    if h2 is None:
        in_specs.append(pl.BlockSpec((tm, d), lambda i: (i, 0)))
        args.append(h)
    else:
        in_specs += [pl.BlockSpec((tm, d), lambda i: (jnp.minimum(i, n_lat_blocks - 1), 0)),
                     pl.BlockSpec((tm, d), lambda i: (jnp.maximum(i - n_lat_blocks, 0), 0))]
        args += [h, h2]
    in_specs += [
        pl.BlockSpec((1, 1, d), rows.mod_index(layer, 2, tm)),
        pl.BlockSpec((1, d), lambda i: (0, 0)),
    ]
    args += [mods, gain.reshape(1, d)]
    return pl.pallas_call(
        functools.partial(_outproj_kernel, mode=mode, n_lat_blocks=n_lat_blocks),
        grid=(n_rows // tm,),
        in_specs=in_specs,
        out_specs=pl.BlockSpec((tm, d), lambda i: (i, 0)),
        out_shape=jax.ShapeDtypeStruct((n_rows, d), F32),
        scratch_shapes=[pltpu.VMEM((tm, d), F32), pltpu.VMEM((tm, 1), F32)],
        compiler_params=_params(("parallel",)),
        name=name,
    )(*args)


def _mlp_kernel(h_ref, g2_ref, sh_ref, sc_ref, w1_ref, w2_ref, gate_ref, g3_ref, out_ref, v_ref, acc_ref,
                inv_ref):
    f = pl.program_id(1)

    @pl.when(f == 0)
    def _():
        v = _rms_ref(h_ref, inv_ref, g2_ref[...]) * (1.0 + sc_ref[0]) + sh_ref[0]
        v_ref[...] = v.astype(BF16)
        acc_ref[...] = jnp.zeros_like(acc_ref)

    a = jnp.maximum(_dot(v_ref[...], w1_ref[...]), 0.0)
    acc_ref[...] += _dot((a * a).astype(BF16), w2_ref[...])

    @pl.when(f == pl.num_programs(1) - 1)
    def _():
        out_ref[...] = h_ref[...] + gate_ref[0] * _rms_ref(acc_ref, inv_ref, g3_ref[...])


def mlp_residual(h, w1, w2, mods, g2, g3, *, rows, n_rows, tm, tf, layer):
    _, d, ff = w1.shape
    return pl.pallas_call(
        _mlp_kernel,
        grid=(n_rows // tm, ff // tf),
        in_specs=[
            pl.BlockSpec((tm, d), lambda i, f: (i, 0)),
            pl.BlockSpec((1, d), lambda i, f: (0, 0)),
            pl.BlockSpec((1, 1, d), rows.mod_index(layer, 3, tm)),
            pl.BlockSpec((1, 1, d), rows.mod_index(layer, 4, tm)),
            pl.BlockSpec((None, d, tf), lambda i, f: (layer, 0, f)),
            pl.BlockSpec((None, tf, d), lambda i, f: (layer, f, 0)),
            pl.BlockSpec((1, 1, d), rows.mod_index(layer, 5, tm)),
            pl.BlockSpec((1, d), lambda i, f: (0, 0)),
        ],
        out_specs=pl.BlockSpec((tm, d), lambda i, f: (i, 0)),
        out_shape=jax.ShapeDtypeStruct((n_rows, d), F32),
        scratch_shapes=[pltpu.VMEM((tm, d), BF16), pltpu.VMEM((tm, d), F32), pltpu.VMEM((tm, 1), F32)],
        compiler_params=_params(("parallel", "arbitrary")),
        name="mlp",
    )(h, g2.reshape(1, d), mods, mods, w1, w2, mods, g3.reshape(1, d))


def _rope_tables(seq, d_rot):
    pos = np.arange(seq)
    row, col = pos // GRID_W, pos % GRID_W
    n = d_rot // 4
    inv = ROPE_THETA ** (-np.arange(n, dtype=np.float64) / n)
    ang = np.concatenate([row[:, None] * inv, col[:, None] * inv], axis=-1)
    cos, sin = np.cos(ang), np.sin(ang)
    half = d_rot // 2
    cc = np.zeros((seq, LANE))
    ss = np.zeros((seq, LANE))
    cc[:, :half] = cos
    cc[:, LANE // 2:LANE // 2 + half] = cos
    ss[:, :half] = -sin
    ss[:, LANE // 2:LANE // 2 + half] = sin
    return jnp.asarray(cc, F32), jnp.asarray(ss, F32)


def _swa_kernel(sink_ref, q_ref, kp_ref, kc_ref, kn_ref, kx_ref, vp_ref, vc_ref, vn_ref, vx_ref, o_ref,
                *, nb, seq, ctx_len):
    n = pl.program_id(1)
    hd, grp, lb = SWA_HEAD_DIM, SWA_HEADS // SWA_KV_HEADS, SWA_BLOCK
    m_rows, n_loc = grp * lb, 3 * lb
    r = lax.broadcasted_iota(jnp.int32, (m_rows, n_loc + ctx_len), 0)
    c = lax.broadcasted_iota(jnp.int32, (m_rows, n_loc + ctx_len), 1)
    qpos = n * lb + (r & (lb - 1))
    kpos = (n - 1) * lb + c
    local_ok = (jnp.abs(kpos - qpos) <= SWA_WINDOW) & (kpos >= 0) & (kpos < seq) & (n < nb)
    valid = local_ok | (c >= n_loc)
    ones = jnp.ones((n_loc + ctx_len, hd), BF16)
    def scores(kv):
        ks = slice(kv * hd, (kv + 1) * hd)
        qg = jnp.concatenate([q_ref[:, (kv * grp + g) * hd:(kv * grp + g + 1) * hd] for g in range(grp)], axis=0)
        keys = jnp.concatenate([kp_ref[:, ks], kc_ref[:, ks], kn_ref[:, ks], kx_ref[:, ks]], axis=0)
        return jnp.where(valid, _dot_nt(qg, keys), -1e30)

    outs = []
    s_next = scores(0)
    for kv in range(SWA_KV_HEADS):
        s = s_next
        if kv + 1 < SWA_KV_HEADS:
            s_next = scores(kv + 1)
        ks = slice(kv * hd, (kv + 1) * hd)
        vals = jnp.concatenate([vp_ref[:, ks], vc_ref[:, ks], vn_ref[:, ks], vx_ref[:, ks]], axis=0)
        v_aug = jnp.concatenate([vals, ones], axis=1)
        for g in range(grp):
            sg = s[g * lb:(g + 1) * lb]
            snk = sink_ref[kv * grp + g] * math.log2(math.e)
            m = jnp.maximum(jnp.max(sg, axis=1, keepdims=True), snk)
            ov = _dot(jnp.exp2(sg - m).astype(BF16), v_aug)
            outs.append((ov[:, :hd] / (ov[:, hd:hd + 1] + jnp.exp2(snk - m))).astype(BF16))
    o_ref[...] = jnp.concatenate(outs, axis=1)


def swa_attention(qkv, sink, *, rows):
    b, seq, ctx_len = rows.batch, rows.seq, rows.ctx_len
    lb = SWA_BLOCK
    nb, ncb = seq // lb, ctx_len // lb
    lat_blocks = rows.n_lat // lb
    qw = SWA_HEADS * SWA_HEAD_DIM
    kw = SWA_KV_HEADS * SWA_HEAD_DIM
    kcol, vcol = qw // kw, qw // kw + 1

    def qidx(bi, n):
        return (jnp.where(n < nb, bi * nb + n, lat_blocks + bi * ncb + (n - nb)), 0)

    def kidx(off, col):
        return lambda bi, n: (bi * nb + jnp.clip(n + off, 0, nb - 1), col)

    def xidx(col):
        return lambda bi, n: (rows.n_lat // ctx_len + bi, col)

    kern = functools.partial(_swa_kernel, nb=nb, seq=seq, ctx_len=ctx_len)
    return pl.pallas_call(
        kern,
        grid=(b, nb + ncb),
        in_specs=[
            pl.BlockSpec(memory_space=pltpu.SMEM),
            pl.BlockSpec((lb, qw), qidx),
            pl.BlockSpec((lb, kw), kidx(-1, kcol)),
            pl.BlockSpec((lb, kw), kidx(0, kcol)),
            pl.BlockSpec((lb, kw), kidx(1, kcol)),
            pl.BlockSpec((ctx_len, kw), xidx(kcol)),
            pl.BlockSpec((lb, kw), kidx(-1, vcol)),
            pl.BlockSpec((lb, kw), kidx(0, vcol)),
            pl.BlockSpec((lb, kw), kidx(1, vcol)),
            pl.BlockSpec((ctx_len, kw), xidx(vcol)),
        ],
        out_specs=pl.BlockSpec((lb, qw), qidx),
        out_shape=jax.ShapeDtypeStruct((rows.n_all, qw), BF16),
        compiler_params=_params(("parallel", "parallel")),
        name="swa_attention",
    )(sink, qkv, qkv, qkv, qkv, qkv, qkv, qkv, qkv, qkv)


def _mlstm_kernel(*refs):
    ins, (of_ref, ob_ref, c_ref, m_ref) = refs[:14], refs[14:]
    c = pl.program_id(1)
    L, H, dqk, dv = ML_CHUNK, ML_HEADS, ML_DQK, ML_DV
    scale = dqk ** -0.5

    @pl.when(c == 0)
    def _():
        c_ref[...] = jnp.zeros_like(c_ref)
        m_ref[...] = jnp.zeros_like(m_ref)

    row = lax.broadcasted_iota(jnp.int32, (L, L), 0)
    col = lax.broadcasted_iota(jnp.int32, (L, L), 1)
    ones = jnp.ones((L, LANE), BF16)
    for d, o_ref in enumerate((of_ref, ob_ref)):
        q_ref, k_ref, v_ref, gi_ref, gf_ref, bi_ref, bf_ref = ins[7 * d:7 * d + 7]
        mask = (col <= row) if d == 0 else (col >= row)
        ones_mask = jnp.where(mask, 1.0, 0.0).astype(BF16)
        i_blk = gi_ref[...] + bi_ref[...]
        f_blk = jax.nn.log_sigmoid(gf_ref[...] + bf_ref[...])
        f_hi = f_blk.astype(BF16)
        r1 = f_blk - f_hi.astype(F32)
        f_mid = r1.astype(BF16)
        f_lo = (r1 - f_mid.astype(F32)).astype(BF16)
        b_blk = _dot(jnp.concatenate([ones_mask] * 3, axis=1), jnp.concatenate([f_hi, f_mid, f_lo], axis=0))
        b_end = jnp.sum(f_blk, axis=0, keepdims=True)
        e_rows = (i_blk - b_blk).T
        m_prev_blk = m_ref[d]
        dec = b_end - b_blk + i_blk
        m_new_blk = jnp.maximum(b_end + m_prev_blk, jnp.max(dec, axis=0, keepdims=True))
        ws_blk = jnp.exp(dec - m_new_blk)
        gs_blk = jnp.exp(b_end + m_prev_blk - m_new_blk)
        for h in range(H):
            qh = q_ref[:, h * dqk:(h + 1) * dqk]
            kh = k_ref[:, h * dqk:(h + 1) * dqk]
            v_aug = jnp.concatenate([v_ref[:, h * dv:(h + 1) * dv], ones], axis=1)
            e = jnp.where(mask, e_rows[h:h + 1, :], -jnp.inf)
            m_prev = m_prev_blk[:, h:h + 1]
            mm = jnp.maximum(m_prev, jnp.max(e, axis=1, keepdims=True))
            s = (_dot_nt(qh, kh) * jnp.exp(e - (mm - math.log(scale)))).astype(BF16)
            g = jnp.exp(m_prev - mm) * scale
            ct = c_ref[d, h]
            lhs = jnp.concatenate([s, (qh.astype(F32) * g).astype(BF16)], axis=1)
            res = _dot(lhs, jnp.concatenate([v_aug, ct.astype(BF16)], axis=0))
            floor = jnp.exp(-(b_blk[:, h:h + 1] + mm))
            hout = res[:, :dv] / jnp.maximum(jnp.abs(res[:, dv:dv + 1]), floor)
            o_ref[:, h * dv:(h + 1) * dv] = hout.astype(o_ref.dtype)
            kw = (kh.astype(F32) * ws_blk[:, h:h + 1]).astype(BF16)
            c_ref[d, h] = gs_blk[:, h:h + 1] * ct + _dot_tn(kw, v_aug)
        m_ref[d] = m_new_blk


def mlstm_scan(z, gates, gate_b, *, rows):
    b, seq, ctx_len = rows.batch, rows.seq, rows.ctx_len
    L = ML_CHUNK
    assert ctx_len == L and seq % L == 0
    nlc = seq // L
    hq, hv = ML_HEADS * ML_DQK, ML_HEADS * ML_DV

    def rb(d):
        def index(bi, c):
            lat = bi * nlc + (c - 1 if d == 0 else nlc - c)
            return jnp.where(c == 0, rows.n_lat // L + bi, lat)
        return index

    in_specs, args = [], []
    for d in range(2):
        r = rb(d)
        in_specs += [
            pl.BlockSpec((L, hq), lambda bi, c, r=r: (r(bi, c), 0)),
            pl.BlockSpec((L, hq), lambda bi, c, r=r: (r(bi, c), 1)),
            pl.BlockSpec((L, hv), lambda bi, c, r=r: (r(bi, c), 1)),
            pl.BlockSpec((L, LANE), lambda bi, c, r=r, d=d: (r(bi, c), 2 * d)),
            pl.BlockSpec((L, LANE), lambda bi, c, r=r, d=d: (r(bi, c), 2 * d + 1)),
            pl.BlockSpec((1, LANE), lambda bi, c, d=d: (0, 2 * d)),
            pl.BlockSpec((1, LANE), lambda bi, c, d=d: (0, 2 * d + 1)),
        ]
        args += [z, z, z, gates, gates, gate_b, gate_b]
    return pl.pallas_call(
        _mlstm_kernel,
        grid=(b, nlc + 1),
        in_specs=in_specs,
        out_specs=[pl.BlockSpec((L, hv), lambda bi, c, r=rb(d): (r(bi, c), 0)) for d in range(2)],
        out_shape=[jax.ShapeDtypeStruct((rows.n_all, hv), BF16)] * 2,
        scratch_shapes=[
            pltpu.VMEM((2, ML_HEADS, ML_DQK, ML_DV + LANE), F32),
            pltpu.VMEM((2, 1, LANE), F32),
        ],
        compiler_params=_params(("parallel", "arbitrary")),
        name="mlstm_scan",
    )(*args)


def _mla_kernel(q_ref, kvl_ref, kvx_ref, krl_ref, krx_ref, o_ref, kk_ref, vt_ref, *, seq, sub, ck):
    @pl.when(pl.program_id(2) == 0)
    def _():
        kk_ref[:seq, :LANE] = kvl_ref[:, :LANE]
        kk_ref[:seq, LANE:] = krl_ref[...]
        kk_ref[seq:, :LANE] = kvx_ref[:, :LANE]
        kk_ref[seq:, LANE:] = krx_ref[...]
        vt_ref[:LANE, :seq] = kvl_ref[:, LANE:].astype(F32).T.astype(BF16)
        vt_ref[:LANE, seq:] = kvx_ref[:, LANE:].astype(F32).T.astype(BF16)
        vt_ref[LANE:, :] = jnp.ones((vt_ref.shape[0] - LANE, vt_ref.shape[1]), BF16)

    nsub = q_ref.shape[0] // sub
    n_keys = kk_ref.shape[0]
    bounds = [(k0, min(k0 + ck, n_keys)) for k0 in range(0, n_keys, ck)]
    nck = len(bounds)
    qs = [q_ref[s * sub:(s + 1) * sub, :] for s in range(nsub)]

    def scores(s, c):
        return _dot_nt(kk_ref[bounds[c][0]:bounds[c][1], :], qs[s])

    m = [None] * nsub
    acc = [None] * nsub
    st = [None] * nsub

    def absorb(s, c):
        cm = jnp.max(st[s], axis=0, keepdims=True)
        m_new = cm if c == 0 else jnp.maximum(m[s], cm)
        pv = _dot(vt_ref[:, bounds[c][0]:bounds[c][1]], jnp.exp2(st[s] - m_new).astype(BF16))
        acc[s] = pv if c == 0 else acc[s] * jnp.exp2(m[s] - m_new) + pv
        m[s] = m_new

    for k in range(-1, nck + nsub - 1):
        nxt = [scores(s, k - s + 1) if 0 <= k - s + 1 < nck else None for s in range(nsub)]
        for s in range(nsub):
            if 0 <= k - s < nck:
                absorb(s, k - s)
        for s in range(nsub):
            if nxt[s] is not None:
                st[s] = nxt[s]
    for s in range(nsub):
        o = acc[s][:LANE] / acc[s][LANE:LANE + 1]
        o_ref[s * sub:(s + 1) * sub, :] = o.T.astype(o_ref.dtype)


def mla_attention(q, kv, z, *, rows, tq=4096, sub=256, ck=1024):
    b, seq, ctx_len = rows.batch, rows.seq, rows.ctx_len
    tq = min(tq, seq)
    nq = seq // tq
    kr_col = z.shape[1] // LANE - 1
    ctx0 = rows.n_lat // ctx_len
    n_keys = seq + ctx_len
    ones_rows = 16
    return pl.pallas_call(
        functools.partial(_mla_kernel, seq=seq, sub=sub, ck=ck),
        grid=(b, MLA_HEADS, nq),
        in_specs=[
            pl.BlockSpec((tq, 2 * LANE), lambda bi, h, i: (bi * nq + i, h)),
            pl.BlockSpec((seq, 2 * LANE), lambda bi, h, i: (bi, h)),
            pl.BlockSpec((ctx_len, 2 * LANE), lambda bi, h, i: (ctx0 + bi, h)),
            pl.BlockSpec((seq, LANE), lambda bi, h, i: (bi, kr_col)),
            pl.BlockSpec((ctx_len, LANE), lambda bi, h, i: (ctx0 + bi, kr_col)),
        ],
        out_specs=pl.BlockSpec((tq, LANE), lambda bi, h, i: (bi * nq + i, h)),
        out_shape=jax.ShapeDtypeStruct((rows.n_lat, MLA_HEADS * MLA_V), BF16),
        scratch_shapes=[pltpu.VMEM((n_keys, 2 * LANE), BF16), pltpu.VMEM((LANE + ones_rows, n_keys), BF16)],
        compiler_params=_params(("parallel", "parallel", "arbitrary")),
        name="mla_attention",
    )(q, kv, kv, z, z)


def _hy_filter_kernel(ft_ref, w1_ref, b1_ref, w2_ref, b2_ref, w3_ref, b3_ref, fr_ref, fr3_ref, w4_ref, b4_ref,
                      dec_ref, o_ref, a_ref, *, back_from):
    j = pl.program_id(0)
    fw = HY_FILTER_W
    hdot = functools.partial(jnp.dot, preferred_element_type=F32, precision=HIGHEST)

    @pl.when(j == 0)
    def _():
        fr = fr_ref[...]
        a = jnp.sin(fr * (hdot(ft_ref[...], w1_ref[...]) + b1_ref[...]))
        a = jnp.sin(fr * (hdot(a, w2_ref[...]) + b2_ref[...]))
        a3 = jnp.sin(fr3_ref[...] * (hdot(a, w3_ref[...]) + b3_ref[...]))
        hi = a3.astype(BF16)
        lo = (a3 - hi.astype(F32)).astype(BF16)
        lane = lax.broadcasted_iota(jnp.int32, a3.shape, 1)
        a_ref[...] = jnp.where((lane >= fw) & (lane < 2 * fw), lo, hi)

    w4 = w4_ref[...]
    w_hi = w4.astype(BF16)
    w_lo = (w4 - w_hi.astype(F32)).astype(BF16)
    rhs = jnp.concatenate([w_hi, w_hi, w_lo, jnp.zeros_like(w_hi)], axis=0)
    t = ft_ref[:, 0:1]
    filt = (_dot(a_ref[...], rhs) + b4_ref[...]) * jnp.exp(-t * jnp.abs(dec_ref[...]))
    row = lax.broadcasted_iota(jnp.int32, filt.shape, 0)
    o_ref[...] = jnp.where(jnp.logical_and(row == 0, j >= back_from), 0.0, filt).astype(o_ref.dtype)


def hyena_filters(seq, w1, b1, w2, b2, w3, b3, w4, b4, freq, decay, tn=512):
    t = np.linspace(0.0, 1.0, seq)[:, None]
    w = (2.0 * math.pi / seq) * np.arange(seq)[:, None]
    bands = np.linspace(1e-4, HY_BANDS - 1, HY_BANDS)[None, :]
    feats = np.zeros((seq, LANE))
    feats[:, :1 + 2 * HY_BANDS] = np.concatenate([t, np.cos(bands * w), -np.sin(bands * w)], axis=-1)
    n = w4.shape[1]
    fw = HY_FILTER_W
    w1p = jnp.zeros((LANE, fw), F32).at[:w1.shape[0]].set(w1)
    row = lambda a: a.reshape(1, -1)
    tile3 = lambda a: jnp.concatenate([a, a, a, jnp.zeros_like(a)], axis=-1)
    full = lambda shape: pl.BlockSpec(shape, lambda j: (0, 0))
    return pl.pallas_call(
        functools.partial(_hy_filter_kernel, back_from=(n // 2) // tn),
        grid=(n // tn,),
        in_specs=[full((seq, LANE)), full((LANE, fw)), full((1, fw)), full((fw, fw)), full((1, fw)),
                  full((fw, 4 * fw)), full((1, 4 * fw)), full((1, fw)), full((1, 4 * fw)),
                  pl.BlockSpec((fw, tn), lambda j: (0, j)),
                  pl.BlockSpec((1, tn), lambda j: (0, j)),
                  pl.BlockSpec((1, tn), lambda j: (0, j))],
        out_specs=pl.BlockSpec((seq, tn), lambda j: (0, j)),
        out_shape=jax.ShapeDtypeStruct((seq, n), BF16),
        scratch_shapes=[pltpu.VMEM((seq, 4 * fw), BF16)],
        compiler_params=_params(("arbitrary",)),
        name="hyena_filters",
    )(jnp.asarray(feats, F32), w1p, row(b1), w2, row(b2), tile3(w3), tile3(row(b3)), row(freq), tile3(row(freq)),
      w4, row(b4), row(decay))


def _dft_mats(seq, p):
    n = 2 * seq
    q = n // p
    k1n = q // 2 + 1
    hi, k1 = np.arange(q // 2), np.arange(k1n)
    th = 2 * np.pi * np.outer(k1, hi) / q
    ma = np.zeros((2 * k1n, q // 2))
    ma[0::2], ma[1::2] = np.cos(th), -np.sin(th)
    lo = np.arange(p)
    wf = np.zeros((k1n, 2 * p, 2 * p))
    wi = np.zeros((k1n, 2 * p, 2 * p))
    for k in k1:
        ph = -2 * np.pi * (np.outer(lo, lo) / p + k * lo[None, :] / n)
        er, ei = np.cos(ph), np.sin(ph)
        wf[k] = np.block([[er, -ei], [ei, er]])
        wi[k] = np.block([[er.T, ei.T], [-ei.T, er.T]])
    c = np.full(k1n, 2.0)
    c[0] = c[-1] = 1.0
    th2 = 2 * np.pi * np.outer(hi, k1) / q
    md = np.zeros((q // 2, 2 * k1n))
    md[:, 0::2], md[:, 1::2] = c * np.cos(th2) / n, -c * np.sin(th2) / n
    as_bf16 = lambda a: jnp.asarray(a, F32).astype(BF16)
    return ma, as_bf16(wf), as_bf16(wi), md


HY_HALO = 16


def _hy_in_kernel(h_ref, hp_ref, hn_ref, g_ref, sh_ref, sc_ref, w_ref, cw_ref, cb_ref, o_ref, u_ref, inv_ref,
                  invh_ref, *, per_seq):
    i = pl.program_id(0)
    tm = h_ref.shape[0]

    @pl.when(pl.program_id(1) == 0)
    def _():
        mod = lambda x: x * (1.0 + sc_ref[0]) + sh_ref[0]
        u_ref[HY_HALO:HY_HALO + tm, :] = mod(_rms_ref(h_ref, inv_ref, g_ref[...])).astype(BF16)
        keep_prev = (i % per_seq != 0).astype(F32)
        keep_next = (i % per_seq != per_seq - 1).astype(F32)
        u_ref[:HY_HALO, :] = (mod(_rms_ref(hp_ref, invh_ref, g_ref[...])) * keep_prev).astype(BF16)
        u_ref[HY_HALO + tm:, :] = (mod(_rms_ref(hn_ref, invh_ref, g_ref[...])) * keep_next).astype(BF16)

    acc = _dot(u_ref[...], w_ref[...])
    n = acc.shape[0]
    mid = slice(HY_HALO, HY_HALO + tm)
    prev = pltpu.roll(acc, 1, 0)[mid]
    nxt = pltpu.roll(acc, n - 1, 0)[mid]
    cw = cw_ref[...]
    o_ref[...] = (prev * cw[0:1] + acc[mid] * cw[1:2] + nxt * cw[2:3] + cb_ref[...]).astype(o_ref.dtype)


def hyena_in_conv(h, w, conv_w, conv_b, mods, gain, *, rows, tm, tn, layer):
    k, n = w.shape
    n_rows = rows.n_lat
    per_seq = rows.seq // tm
    halo_per_blk, n_halo = tm // HY_HALO, n_rows // HY_HALO
    return pl.pallas_call(
        functools.partial(_hy_in_kernel, per_seq=per_seq),
        grid=(n_rows // tm, n // tn),
        in_specs=[
            pl.BlockSpec((tm, k), lambda i, j: (i, 0)),
            pl.BlockSpec((HY_HALO, k), lambda i, j: (jnp.maximum(i * halo_per_blk - 1, 0), 0)),
            pl.BlockSpec((HY_HALO, k), lambda i, j: (jnp.minimum((i + 1) * halo_per_blk, n_halo - 1), 0)),
            pl.BlockSpec((1, k), lambda i, j: (0, 0)),
            pl.BlockSpec((1, 1, k), rows.mod_index(layer, 0, tm)),
            pl.BlockSpec((1, 1, k), rows.mod_index(layer, 1, tm)),
            pl.BlockSpec((k, tn), lambda i, j: (0, j)),
            pl.BlockSpec((HY_SHORT, tn), lambda i, j: (0, j)),
            pl.BlockSpec((1, tn), lambda i, j: (0, j)),
        ],
        out_specs=pl.BlockSpec((tm, tn), lambda i, j: (i, j)),
        out_shape=jax.ShapeDtypeStruct((n_rows, n), BF16),
        scratch_shapes=[pltpu.VMEM((tm + 2 * HY_HALO, k), BF16), pltpu.VMEM((tm, 1), F32),
                        pltpu.VMEM((HY_HALO, 1), F32)],
        compiler_params=_params(("parallel", "arbitrary")),
        name="hyena_in",
    )(h, h, h, gain.reshape(1, k), mods, mods, w, conv_w, conv_b.reshape(1, n))


def _hy_stage_a_kernel(x_ref, m_ref, o_ref):
    qh, lg, g, wc = x_ref.shape
    mo = o_ref.shape[0]
    for l in range(lg):
        x = x_ref[:, l].reshape(qh * g, wc).astype(BF16)
        y = _dot(m_ref[...], x).reshape(mo, g, wc)
        o_ref[:, l * g:(l + 1) * g, :] = y.astype(o_ref.dtype)


def hyena_stage_a(x, mak, *, p, width=2048, col_blk=0, channels=None):
    bx, seq, c_all = x.shape
    c = c_all if channels is None else channels
    g = HY_G
    qh = seq // p
    mo = mak.shape[0] // g
    x5 = x.reshape(bx, qh, p // g, g, c_all)
    lg = HY_LG
    return pl.pallas_call(
        _hy_stage_a_kernel,
        grid=(bx, p // (g * lg), c // width),
        in_specs=[pl.BlockSpec((None, qh, lg, g, width), lambda bi, l, j: (bi, 0, l, 0, col_blk + j)),
                  pl.BlockSpec(mak.shape, lambda bi, l, j: (0, 0))],
        out_specs=pl.BlockSpec((None, mo, lg * g, width), lambda bi, l, j: (bi, 0, l, j)),
        out_shape=jax.ShapeDtypeStruct((bx, mo, p, c), BF16),
        compiler_params=_params(("parallel", "parallel", "parallel")),
        name="hyena_stage_a",
    )(x5, mak)


def _hy_spec_kernel(a0_ref, a1_ref, wf_ref, o_ref):
    p = o_ref.shape[2]
    cb = o_ref.shape[3]
    wf = wf_ref[0]
    x0 = _dot(wf, a0_ref[0].reshape(2 * p, cb))
    x1 = _dot(wf, a1_ref[0].reshape(2 * p, cb))
    o_ref[0, 0] = (x0[:p] + x1[:p]).astype(o_ref.dtype)
    o_ref[0, 1] = (x0[p:] - x1[p:]).astype(o_ref.dtype)


def hyena_spectrum(af, wf, *, d, cb=2048):
    k1n, _, p, _ = af.shape
    nblk = (HY_ORDER * d) // cb
    return pl.pallas_call(
        _hy_spec_kernel,
        grid=(k1n, nblk),
        in_specs=[pl.BlockSpec((1, 2, p, cb), lambda k, j: (k, 0, 0, j)),
                  pl.BlockSpec((1, 2, p, cb), lambda k, j: (k, 0, 0, nblk + j)),
                  pl.BlockSpec((1, 2 * p, 2 * p), lambda k, j: (k, 0, 0))],
        out_specs=pl.BlockSpec((1, 2, p, cb), lambda k, j: (k, 0, 0, j)),
        out_shape=jax.ShapeDtypeStruct((k1n, 2, p, HY_ORDER * d), BF16),
        compiler_params=_params(("parallel", "parallel")),
        name="hyena_spectrum",
    )(af, af, wf)


def _hy_mid_kernel(a_ref, wf_ref, wi_ref, h_ref, o_ref):
    b, _, p, cb = a_ref.shape
    hr, hi = h_ref[0].astype(F32), h_ref[1].astype(F32)
    for bi in range(b):
        x = _dot(wf_ref[...], a_ref[bi].reshape(2 * p, cb))
        xr, xi = x[:p], x[p:]
        y = jnp.concatenate([xr * hr - xi * hi, xr * hi + xi * hr], axis=0).astype(BF16)
        o_ref[bi] = _dot(wi_ref[...], y).reshape(2, p, cb).astype(o_ref.dtype)


def hyena_mid(a, wf, wi, spec, *, order, d, cb=1024):
    b, k1n, _, p, _ = a.shape
    nblk = d // cb
    data = pl.BlockSpec((b, None, 2, p, cb), lambda k, j: (0, k, 0, 0, j))
    mat = pl.BlockSpec((None, 2 * p, 2 * p), lambda k, j: (k, 0, 0))
    return pl.pallas_call(
        _hy_mid_kernel,
        grid=(k1n, nblk),
        in_specs=[data, mat, mat,
                  pl.BlockSpec((None, 2, p, cb), lambda k, j: (k, 0, 0, order * nblk + j))],
        out_specs=data,
        out_shape=jax.ShapeDtypeStruct(a.shape, BF16),
        compiler_params=_params(("parallel", "parallel")),
        name="hyena_mid",
    )(a, wf, wi, spec)


def _hy_stage_d_kernel(c_ref, m_ref, x_ref, z_ref, skip_ref, *rest, fused):
    ma_ref, o_ref, a_ref = rest if fused else (None, rest[0], None)
    mo, rows_c, wc = c_ref.shape
    qh, lg, g, _ = x_ref.shape
    for l in range(lg):
        conv = _dot(m_ref[...], c_ref[:, l * g:(l + 1) * g, :].reshape(mo * g, wc))
        z = z_ref[:, l].reshape(qh * g, wc).astype(F32)
        x = x_ref[:, l].reshape(qh * g, wc).astype(F32)
        y = (x * (conv + z * skip_ref[...])).astype(o_ref.dtype)
        o_ref[:, l] = y.reshape(qh, g, wc)
        if fused:
            a_ref[:, l * g:(l + 1) * g, :] = _dot(ma_ref[...], y).reshape(mo, g, wc).astype(a_ref.dtype)


def hyena_stage_d(c, mdk, xg, z, skip, *, p, mak=None, width=2048, xg_blk=0, z_blk=0):
    b, mo, _, d = c.shape
    g = HY_G
    seq = xg.shape[1]
    qh = seq // p
    fused = mak is not None
    v5 = lambda a: a.reshape(b, qh, p // g, g, a.shape[2])
    lg = HY_LG
    tblk = lambda off: pl.BlockSpec((None, qh, lg, g, width), lambda bi, l, j: (bi, 0, l, 0, off + j))
    tspec = tblk(0)
    fspec = pl.BlockSpec((None, mo, lg * g, width), lambda bi, l, j: (bi, 0, l, j))
    in_specs = [fspec, pl.BlockSpec(mdk.shape, lambda bi, l, j: (0, 0)), tblk(xg_blk), tblk(z_blk),
                pl.BlockSpec((1, width), lambda bi, l, j: (0, j))]
    args = [c, mdk, v5(xg), v5(z), skip.reshape(1, d)]
    out_specs, out_shape = [tspec], [jax.ShapeDtypeStruct((b, qh, p // g, g, d), BF16)]
    if fused:
        in_specs.append(pl.BlockSpec(mak.shape, lambda bi, l, j: (0, 0)))
        args.append(mak)
        out_specs.append(fspec)
        out_shape.append(jax.ShapeDtypeStruct(c.shape, BF16))
    outs = pl.pallas_call(
        functools.partial(_hy_stage_d_kernel, fused=fused),
        grid=(b, p // (g * lg), d // width),
        in_specs=in_specs,
        out_specs=out_specs,
        out_shape=out_shape,
        compiler_params=_params(("parallel", "parallel", "parallel")),
        name="hyena_stage_d",
    )(*args)
    y = outs[0].reshape(b, seq, d)
    return (y, outs[1]) if fused else y


def hyena_long_convs(x3, filt, skip, *, rows, width=2048):
    b, seq = rows.batch, rows.seq
    d = x3.shape[1] // 3
    p = HY_P
    k1n = seq // p + 1
    ma, wf, wi, md = _dft_mats(seq, p)
    eye = np.eye(HY_G)
    as_bf16 = lambda a: jnp.asarray(a, F32).astype(BF16)
    mak, mdk = as_bf16(np.kron(ma, eye)), as_bf16(np.kron(md, eye))
    af = hyena_stage_a(filt.reshape(1, seq, filt.shape[1]), mak, p=p)
    spec = hyena_spectrum(af.reshape(k1n, 2, p, filt.shape[1]), wf, d=d)
    x3 = x3.reshape(b, seq, 3 * d)
    blk = d // width
    a = hyena_stage_a(x3, mak, p=p, width=width, col_blk=2 * blk, channels=d)
    c = hyena_mid(a.reshape(b, k1n, 2, p, d), wf, wi, spec, order=0, d=d)
    y, a = hyena_stage_d(c.reshape(a.shape), mdk, x3, x3, skip[0], p=p, mak=mak, width=width, xg_blk=0, z_blk=2 * blk)
    c = hyena_mid(a.reshape(b, k1n, 2, p, d), wf, wi, spec, order=1, d=d)
    y = hyena_stage_d(c.reshape(a.shape), mdk, x3, y, skip[1], p=p, width=width, xg_blk=blk, z_blk=0)
    return y.reshape(rows.n_lat, d)


def _mlstm_gate_weights(w_in, gate_b):
    h = ML_HEADS
    o4 = 2 * h * ML_DQK + 2 * h * ML_DV
    wg = w_in[:, o4:].reshape(-1, 2, 2, h)
    bg = gate_b.reshape(2, 2, h)
    w_out = jnp.zeros((w_in.shape[0], 4 * LANE), F32)
    b_out = jnp.zeros((1, 4 * LANE), F32)
    for d in range(2):
        for gate in range(2):
            lo = (2 * d + gate) * LANE
            w_out = w_out.at[:, lo:lo + h].set(wg[:, gate, d])
            b_out = b_out.at[0, lo:lo + h].set(bg[gate, d])
    return w_out, b_out


def _mla_weights(w_in, w_uq):
    r2 = MLA_Q_RANK + MLA_KV_RANK
    half = MLA_ROPE // 2

    def spread(w):
        z = jnp.zeros(w.shape[:-1] + (half,), w.dtype)
        return jnp.concatenate([w[..., :half], z, w[..., half:], z], axis=-1)

    w_in_p = jnp.concatenate([w_in[:, :r2], spread(w_in[:, r2:])], axis=1)
    wq = w_uq.reshape(w_uq.shape[0], MLA_HEADS, MLA_NOPE + MLA_ROPE)
    wq_p = jnp.concatenate([wq[..., :MLA_NOPE], spread(wq[..., MLA_NOPE:])], axis=-1)
    wq_p = wq_p * ((MLA_NOPE + MLA_ROPE) ** -0.5 * math.log2(math.e))
    return w_in_p, wq_p.reshape(w_uq.shape[0], MLA_HEADS * 2 * LANE)


def kernel(x, c, ctx, c_ctx, ada_w, ada_b, norm_g, mlp_w1, mlp_w2, swa_w_qkv, swa_sink, swa_w_o, ml_w_in, ml_gate_b, ml_head_g, ml_w_o, mla_w_in, mla_q_g, mla_kv_g, mla_w_uq, mla_w_ukv, mla_w_o, hy_w_in, hy_conv_w, hy_conv_b, hy_f_w1, hy_f_b1, hy_f_w2, hy_f_b2, hy_f_w3, hy_f_b3, hy_f_w4, hy_f_b4, hy_f_freq, hy_decay, hy_skip, hy_w_o):
    b, seq, d = x.shape
    ctx_len = ctx.shape[1]
    depth = ada_w.shape[0]
    assert depth == 4 and b < 8
    rows = Rows(b, seq, ctx_len)
    tm = 512
    tm_lin = 1024 if rows.n_lat % 1024 == 0 and rows.n_ctx % 1024 == 0 else 512
    bf = lambda w: w.astype(BF16)
    mlp_w1b, mlp_w2b = bf(mlp_w1), bf(mlp_w2)

    cond = jnp.zeros((8, d), F32).at[:b].set(c).at[b].set(c_ctx)
    mods = ada_mods(cond, ada_w, ada_b).reshape(depth * 8, 1, N_MOD * d)

    common = dict(rows=rows, tm=tm_lin, mods=mods)

    def finish(h, o_args, w_o, layer, n_rows, mode="plain", h2=None):
        h = outproj_residual(o_args, bf(w_o), h, mods, norm_g[layer, 1], rows=rows, n_rows=n_rows, tm=tm,
                             layer=layer, mode=mode, h2=h2, name=f"outproj{layer}")
        return mlp_residual(h, mlp_w1b, mlp_w2b, mods, norm_g[layer, 2], norm_g[layer, 3],
                            rows=rows, n_rows=n_rows, tm=tm, tf=1024, layer=layer)

    n_qk_groups = SWA_HEADS + SWA_KV_HEADS
    q_cols = SWA_HEADS * SWA_HEAD_DIM
    q_scale = SWA_HEAD_DIM ** -0.5 * math.log2(math.e)
    w_qkv = jnp.concatenate([swa_w_qkv[0][:, :q_cols] * q_scale, swa_w_qkv[0][:, q_cols:]], axis=1)
    x_rows, ctx_rows = x.reshape(rows.n_lat, d), ctx.reshape(rows.n_ctx, d)
    qkv = linear(x_rows, bf(w_qkv), x2=ctx_rows, n_rows=rows.n_all, tn=512, out_dtype=BF16, prologue="norm_mod",
                 gain=norm_g[0, 0], layer=0, rope_tabs=_rope_tables(seq, SWA_HEAD_DIM), rope_pattern=(True,) * 4,
                 rope_jmax=n_qk_groups // 4, name="swa_qkv", **common)
    o = swa_attention(qkv, swa_sink[0], rows=rows)
    h = finish(x_rows, o, swa_w_o[0], 0, rows.n_all, h2=ctx_rows)

    w_gate, b_gate = _mlstm_gate_weights(ml_w_in[0], ml_gate_b[0])
    z = linear(h, bf(ml_w_in[0]), n_rows=rows.n_all, tn=1536, out_dtype=BF16, prologue="norm_mod",
               gain=norm_g[1, 0], layer=1, n_out=2 * ML_HEADS * (ML_DQK + ML_DV), name="mlstm_in", **common)
    gates = linear(h, bf(w_gate), n_rows=rows.n_all, tn=4 * LANE, out_dtype=F32, prologue="norm_mod",
                   gain=norm_g[1, 0], layer=1, name="mlstm_gates", **common)
    h_fwd, h_bwd = mlstm_scan(z, gates, b_gate, rows=rows)
    h = finish(h, (h_fwd, h_bwd, z, ml_head_g[0]), ml_w_o[0], 1, rows.n_all, mode="mlstm")

    w_in_p, w_uq_p = _mla_weights(mla_w_in[0], mla_w_uq[0])
    rope_mla = _rope_tables(seq, MLA_ROPE)
    zc = linear(h, bf(w_in_p), n_rows=rows.n_all, tn=w_in_p.shape[1], out_dtype=BF16, prologue="norm_mod",
                gain=norm_g[2, 0], layer=2, rope_tabs=rope_mla, rope_pattern=(False,) * 8 + (True,),
                name="mla_in", **common)
    q = linear(zc, bf(w_uq_p), n_rows=rows.n_lat, tn=2048, out_dtype=BF16, x_cols=0, prologue="norm",
               gain=mla_q_g[0], rope_tabs=rope_mla, rope_pattern=(False, True) * 8, name="mla_q",
               rows=rows, tm=tm_lin)
    kv = linear(zc, bf(mla_w_ukv[0]), n_rows=rows.n_all, tn=2048, out_dtype=BF16, x_cols=1, prologue="norm",
                gain=mla_kv_g[0], name="mla_kv", rows=rows, tm=tm_lin)
    o = mla_attention(q, kv, zc, rows=rows)
    h = finish(h, o, mla_w_o[0], 2, rows.n_lat)

    x3 = hyena_in_conv(h, bf(hy_w_in[0]), hy_conv_w[0], hy_conv_b[0], mods, norm_g[3, 0], rows=rows, tm=tm_lin,
                       tn=1536, layer=3)
    filt = hyena_filters(seq, hy_f_w1[0], hy_f_b1[0], hy_f_w2[0], hy_f_b2[0], hy_f_w3[0], hy_f_b3[0],
                         hy_f_w4[0], hy_f_b4[0], hy_f_freq[0], hy_decay[0])
    y = hyena_long_convs(x3, filt, hy_skip[0], rows=rows)
    h = finish(h, y, hy_w_o[0], 3, rows.n_lat)
    return h.reshape(b, seq, d)
```

```python
import functools
import math

import numpy as np
import jax
import jax.numpy as jnp
from jax import lax
from jax.experimental import pallas as pl
from jax.experimental.pallas import tpu as pltpu

F32, BF16 = jnp.float32, jnp.bfloat16
HIGHEST = lax.Precision.HIGHEST

RMS_EPS = 1e-6
ROPE_THETA = 10000.0
GRID_W = 64
N_MOD = 6
LANE = 128

SWA_HEADS, SWA_KV_HEADS, SWA_HEAD_DIM, SWA_WINDOW, SWA_BLOCK = 16, 4, 128, 128, 128
ML_HEADS, ML_DQK, ML_DV = 8, 128, 256
ML_CHUNK = 256
MLA_HEADS, MLA_Q_RANK, MLA_KV_RANK, MLA_NOPE, MLA_ROPE, MLA_V = 16, 512, 512, 128, 64, 128
HY_ORDER, HY_BANDS, HY_FILTER_W, HY_SHORT = 2, 16, 64, 3
HY_P = 256
HY_G = 16
HY_LG = 2

VMEM_LIMIT = 48 * 1024 * 1024


def _params(sem):
    return pltpu.CompilerParams(dimension_semantics=sem, vmem_limit_bytes=VMEM_LIMIT)


def _dot(a, b):
    return jnp.dot(a, b, preferred_element_type=F32)


def _dot_nt(a, b):
    return lax.dot_general(a, b, (((1,), (1,)), ((), ())), preferred_element_type=F32)


def _dot_tn(a, b):
    return lax.dot_general(a, b, (((0,), (0,)), ((), ())), preferred_element_type=F32)


def _rms(x, g):
    return x * lax.rsqrt(jnp.mean(x * x, axis=-1, keepdims=True) + RMS_EPS) * g


def _rms_ref(x_ref, inv_ref, g):
    x = x_ref[...].astype(F32)
    inv_ref[...] = lax.rsqrt(jnp.mean(x * x, axis=-1, keepdims=True) + RMS_EPS)
    return x_ref[...].astype(F32) * inv_ref[...] * g


def _ada_kernel(s_ref, w_ref, b_ref, o_ref):
    s = s_ref[...]
    s = s * jax.nn.sigmoid(s)
    hi = s.astype(BF16)
    lo = (s - hi.astype(F32)).astype(BF16)
    w = w_ref[0]
    w_hi = w.astype(BF16)
    w_lo = (w - w_hi.astype(F32)).astype(BF16)
    r = _dot(jnp.concatenate([hi, lo], axis=0), w_hi)
    n = s.shape[0]
    o_ref[0] = r[:n] + r[n:] + _dot(hi, w_lo) + b_ref[0]


def ada_mods(cond, ada_w, ada_b, tn=1536):
    depth, d, n = ada_w.shape
    rows = cond.shape[0]
    return pl.pallas_call(
        _ada_kernel,
        grid=(depth, n // tn),
        in_specs=[
            pl.BlockSpec((rows, d), lambda l, j: (0, 0)),
            pl.BlockSpec((1, d, tn), lambda l, j: (l, 0, j)),
            pl.BlockSpec((1, 1, tn), lambda l, j: (l, 0, j)),
        ],
        out_specs=pl.BlockSpec((1, rows, tn), lambda l, j: (l, 0, j)),
        out_shape=jax.ShapeDtypeStruct((depth, rows, n), F32),
        compiler_params=_params(("parallel", "parallel")),
        name="ada_mods",
    )(cond, ada_w, ada_b.reshape(depth, 1, n))


class Rows:
    def __init__(self, batch, seq, ctx_len):
        self.batch, self.seq, self.ctx_len = batch, seq, ctx_len
        self.n_lat = batch * seq
        self.n_ctx = batch * ctx_len
        self.n_all = self.n_lat + self.n_ctx

    def mod_index(self, layer, k, tm):
        lat_blocks, per_batch = self.n_lat // tm, self.seq // tm

        def index(i, *_):
            b = jnp.where(i < lat_blocks, i // per_batch, self.batch)
            return (layer * 8 + b, 0, k)

        return index


def _linear_kernel(*refs, prologue, rope, rope_pattern, n_lat_blocks, dual, single):
    it = iter(refs)
    x_ref = next(it)
    x2_ref = next(it) if dual else None
    g_ref = next(it) if prologue in ("norm", "norm_mod") else None
    sh_ref = next(it) if prologue == "norm_mod" else None
    sc_ref = next(it) if prologue == "norm_mod" else None
    w_ref = next(it)
    cc_ref = next(it) if rope else None
    ss_ref = next(it) if rope else None
    o_ref = next(it)
    u_ref = next(it)
    inv_ref = next(it)
    first = pl.program_id(1) == 0

    def prologue_from(src_ref):
        if prologue in ("norm", "norm_mod"):
            x = _rms_ref(src_ref, inv_ref, g_ref[...])
        else:
            x = src_ref[...].astype(F32)
        if prologue == "norm_mod":
            x = x * (1.0 + sc_ref[0]) + sh_ref[0]
        u_ref[...] = x.astype(BF16)

    if dual:
        is_lat = pl.program_id(0) < n_lat_blocks
        pl.when(jnp.logical_and(first, is_lat))(functools.partial(prologue_from, x_ref))
        pl.when(jnp.logical_and(first, jnp.logical_not(is_lat)))(functools.partial(prologue_from, x2_ref))
    elif single:
        prologue_from(x_ref)
    else:
        pl.when(first)(functools.partial(prologue_from, x_ref))

    acc = _dot(u_ref[...], w_ref[...])

    def plain():
        o_ref[...] = acc.astype(o_ref.dtype)

    def roped():
        cc, ss = cc_ref[...], ss_ref[...]
        segs = []
        for gi, on in enumerate(rope_pattern):
            seg = acc[:, gi * LANE:(gi + 1) * LANE]
            if on:
                seg = seg * cc + pltpu.roll(seg, LANE // 2, 1) * ss
            segs.append(seg)
        o_ref[...] = jnp.concatenate(segs, axis=1).astype(o_ref.dtype)

    if rope:
        roped()
    else:
        plain()


def linear(x, w, *, rows, n_rows, tm, tn, out_dtype, x_cols=None, prologue="none", gain=None,
           mods=None, layer=0, mod_k=(0, 1), rope_tabs=None, rope_pattern=(), rope_jmax=1 << 30,
           n_out=None, x2=None, name="linear"):
    k, n = w.shape
    n = n if n_out is None else n_out
    xc = 0 if x_cols is None else x_cols
    rope = rope_tabs is not None
    dual = x2 is not None
    n_lat_blocks = rows.n_lat // tm
    if dual:
        in_specs = [pl.BlockSpec((tm, k), lambda i, j: (jnp.minimum(i, n_lat_blocks - 1), 0)),
                    pl.BlockSpec((tm, k), lambda i, j: (jnp.maximum(i - n_lat_blocks, 0), 0),
                                 pipeline_mode=pl.Buffered(1))]
        args = [x, x2]
    else:
        in_specs = [pl.BlockSpec((tm, k), lambda i, j: (i, xc))]
        args = [x]
    if prologue in ("norm", "norm_mod"):
        in_specs.append(pl.BlockSpec((1, k), lambda i, j: (0, 0)))
        args.append(gain.reshape(1, k))
    if prologue == "norm_mod":
        for mk in mod_k:
            in_specs.append(pl.BlockSpec((1, 1, k), rows.mod_index(layer, mk, tm)))
            args.append(mods)
    in_specs.append(pl.BlockSpec((k, tn), lambda i, j: (0, j)))
    args.append(w)
    if rope:
        per_seq = rows.seq // tm

        def tab_index(i, j):
            rotate = jnp.logical_and(i < n_lat_blocks, j < rope_jmax)
            return (jnp.where(rotate, 0, 1), i % per_seq, 0)

        for t, fill in zip(rope_tabs, (1.0, 0.0)):
            in_specs.append(pl.BlockSpec((None, tm, LANE), tab_index))
            args.append(jnp.stack([t, jnp.full_like(t, fill)]))
    kern = functools.partial(_linear_kernel, prologue=prologue, rope=rope, rope_pattern=tuple(rope_pattern),
                             n_lat_blocks=n_lat_blocks, dual=dual, single=(n == tn))
    return pl.pallas_call(
        kern,
        grid=(n_rows // tm, n // tn),
        in_specs=in_specs,
        out_specs=pl.BlockSpec((tm, tn), lambda i, j: (i, j)),
        out_shape=jax.ShapeDtypeStruct((n_rows, n), out_dtype),
        scratch_shapes=[pltpu.VMEM((tm, k), BF16), pltpu.VMEM((tm, 1), F32)],
        compiler_params=_params(("parallel", "arbitrary")),
        name=name,
    )(*args)


def _outproj_kernel(*refs, mode, n_lat_blocks):
    it = iter(refs)
    if mode == "mlstm":
        hf_ref, hb_ref, og_ref, hg_ref = next(it), next(it), next(it), next(it)
    else:
        o_ref_in = next(it)
    w_ref, h_ref = next(it), next(it)
    h2_ref = next(it) if n_lat_blocks is not None else None
    gate_ref, g_ref, out_ref = next(it), next(it), next(it)
    y_ref, inv_ref = next(it), next(it)
    if mode == "mlstm":
        hs = hf_ref[...].astype(F32) + hb_ref[...].astype(F32)
        og = jax.nn.sigmoid(og_ref[...].astype(F32))
        hg = hg_ref[...]
        parts = []
        for h in range(ML_HEADS):
            sl = slice(h * ML_DV, (h + 1) * ML_DV)
            parts.append((_rms(hs[:, sl], hg[:, sl]) * og[:, sl]).astype(BF16))
        o = jnp.concatenate(parts, axis=1)
    else:
        o = o_ref_in[...]
    y_ref[...] = _dot(o, w_ref[...])
    upd = gate_ref[0] * _rms_ref(y_ref, inv_ref, g_ref[...])
    if h2_ref is None:
        out_ref[...] = h_ref[...] + upd
    else:
        out_ref[...] = jnp.where(pl.program_id(0) < n_lat_blocks, h_ref[...], h2_ref[...]) + upd


def outproj_residual(o_args, w, h, mods, gain, *, rows, n_rows, tm, layer, mode="plain", h2=None, name="outproj"):
    k, d = w.shape
    n_lat_blocks = rows.n_lat // tm if h2 is not None else None
    if mode == "mlstm":
        h_fwd, h_bwd, z, head_g = o_args
        in_specs = [
            pl.BlockSpec((tm, k), lambda i: (i, 0)),
            pl.BlockSpec((tm, k), lambda i: (i, 0)),
            pl.BlockSpec((tm, k), lambda i: (i, 2)),
            pl.BlockSpec((1, k), lambda i: (0, 0)),
        ]
        args = [h_fwd, h_bwd, z, head_g.reshape(1, k)]
    else:
        in_specs = [pl.BlockSpec((tm, k), lambda i: (i, 0))]
        args = [o_args]
    in_specs.append(pl.BlockSpec((k, d), lambda i: (0, 0)))
    args.append(w)
    if h2 is None:
        in_specs.append(pl.BlockSpec((tm, d), lambda i: (i, 0)))
        args.append(h)
    else:
        in_specs += [pl.BlockSpec((tm, d), lambda i: (jnp.minimum(i, n_lat_blocks - 1), 0)),
                     pl.BlockSpec((tm, d), lambda i: (jnp.maximum(i - n_lat_blocks, 0), 0))]
        args += [h, h2]
    in_specs += [
        pl.BlockSpec((1, 1, d), rows.mod_index(layer, 2, tm)),
        pl.BlockSpec((1, d), lambda i: (0, 0)),
    ]
    args += [mods, gain.reshape(1, d)]
    return pl.pallas_call(
        functools.partial(_outproj_kernel, mode=mode, n_lat_blocks=n_lat_blocks),
        grid=(n_rows // tm,),
        in_specs=in_specs,
        out_specs=pl.BlockSpec((tm, d), lambda i: (i, 0)),
        out_shape=jax.ShapeDtypeStruct((n_rows, d), F32),
        scratch_shapes=[pltpu.VMEM((tm, d), F32), pltpu.VMEM((tm, 1), F32)],
        compiler_params=_params(("parallel",)),
        name=name,
    )(*args)


def _mlp_kernel(h_ref, g2_ref, sh_ref, sc_ref, w1_ref, w2_ref, gate_ref, g3_ref, out_ref, v_ref, acc_ref,
                inv_ref):
    f = pl.program_id(1)

    @pl.when(f == 0)
    def _():
        v = _rms_ref(h_ref, inv_ref, g2_ref[...]) * (1.0 + sc_ref[0]) + sh_ref[0]
        v_ref[...] = v.astype(BF16)
        acc_ref[...] = jnp.zeros_like(acc_ref)

    a = jnp.maximum(_dot(v_ref[...], w1_ref[...]), 0.0)
    acc_ref[...] += _dot((a * a).astype(BF16), w2_ref[...])

    @pl.when(f == pl.num_programs(1) - 1)
    def _():
        out_ref[...] = h_ref[...] + gate_ref[0] * _rms_ref(acc_ref, inv_ref, g3_ref[...])


def mlp_residual(h, w1, w2, mods, g2, g3, *, rows, n_rows, tm, tf, layer):
    _, d, ff = w1.shape
    return pl.pallas_call(
        _mlp_kernel,
        grid=(n_rows // tm, ff // tf),
        in_specs=[
            pl.BlockSpec((tm, d), lambda i, f: (i, 0)),
            pl.BlockSpec((1, d), lambda i, f: (0, 0)),
            pl.BlockSpec((1, 1, d), rows.mod_index(layer, 3, tm)),
            pl.BlockSpec((1, 1, d), rows.mod_index(layer, 4, tm)),
            pl.BlockSpec((None, d, tf), lambda i, f: (layer, 0, f)),
            pl.BlockSpec((None, tf, d), lambda i, f: (layer, f, 0)),
            pl.BlockSpec((1, 1, d), rows.mod_index(layer, 5, tm)),
            pl.BlockSpec((1, d), lambda i, f: (0, 0)),
        ],
        out_specs=pl.BlockSpec((tm, d), lambda i, f: (i, 0)),
        out_shape=jax.ShapeDtypeStruct((n_rows, d), F32),
        scratch_shapes=[pltpu.VMEM((tm, d), BF16), pltpu.VMEM((tm, d), F32), pltpu.VMEM((tm, 1), F32)],
        compiler_params=_params(("parallel", "arbitrary")),
        name="mlp",
    )(h, g2.reshape(1, d), mods, mods, w1, w2, mods, g3.reshape(1, d))


def _rope_tables(seq, d_rot):
    pos = np.arange(seq)
    row, col = pos // GRID_W, pos % GRID_W
    n = d_rot // 4
    inv = ROPE_THETA ** (-np.arange(n, dtype=np.float64) / n)
    ang = np.concatenate([row[:, None] * inv, col[:, None] * inv], axis=-1)
    cos, sin = np.cos(ang), np.sin(ang)
    half = d_rot // 2
    cc = np.zeros((seq, LANE))
    ss = np.zeros((seq, LANE))
    cc[:, :half] = cos
    cc[:, LANE // 2:LANE // 2 + half] = cos
    ss[:, :half] = -sin
    ss[:, LANE // 2:LANE // 2 + half] = sin
    return jnp.asarray(cc, F32), jnp.asarray(ss, F32)


def _swa_kernel(sink_ref, q_ref, kp_ref, kc_ref, kn_ref, kx_ref, vp_ref, vc_ref, vn_ref, vx_ref, o_ref,
                *, nb, seq, ctx_len):
    n = pl.program_id(1)
    hd, grp, lb = SWA_HEAD_DIM, SWA_HEADS // SWA_KV_HEADS, SWA_BLOCK
    m_rows, n_loc = grp * lb, 3 * lb
    r = lax.broadcasted_iota(jnp.int32, (m_rows, n_loc + ctx_len), 0)
    c = lax.broadcasted_iota(jnp.int32, (m_rows, n_loc + ctx_len), 1)
    qpos = n * lb + (r & (lb - 1))
    kpos = (n - 1) * lb + c
    local_ok = (jnp.abs(kpos - qpos) <= SWA_WINDOW) & (kpos >= 0) & (kpos < seq) & (n < nb)
    valid = local_ok | (c >= n_loc)
    ones = jnp.ones((n_loc + ctx_len, hd), BF16)
    def scores(kv):
        ks = slice(kv * hd, (kv + 1) * hd)
        qg = jnp.concatenate([q_ref[:, (kv * grp + g) * hd:(kv * grp + g + 1) * hd] for g in range(grp)], axis=0)
        keys = jnp.concatenate([kp_ref[:, ks], kc_ref[:, ks], kn_ref[:, ks], kx_ref[:, ks]], axis=0)
        return jnp.where(valid, _dot_nt(qg, keys), -1e30)

    outs = []
    s_next = scores(0)
    for kv in range(SWA_KV_HEADS):
        s = s_next
        if kv + 1 < SWA_KV_HEADS:
            s_next = scores(kv + 1)
        ks = slice(kv * hd, (kv + 1) * hd)
        vals = jnp.concatenate([vp_ref[:, ks], vc_ref[:, ks], vn_ref[:, ks], vx_ref[:, ks]], axis=0)
        v_aug = jnp.concatenate([vals, ones], axis=1)
        for g in range(grp):
            sg = s[g * lb:(g + 1) * lb]
            snk = sink_ref[kv * grp + g] * math.log2(math.e)
            m = jnp.maximum(jnp.max(sg, axis=1, keepdims=True), snk)
            ov = _dot(jnp.exp2(sg - m).astype(BF16), v_aug)
            outs.append((ov[:, :hd] / (ov[:, hd:hd + 1] + jnp.exp2(snk - m))).astype(BF16))
    o_ref[...] = jnp.concatenate(outs, axis=1)


def swa_attention(qkv, sink, *, rows):
    b, seq, ctx_len = rows.batch, rows.seq, rows.ctx_len
    lb = SWA_BLOCK
    nb, ncb = seq // lb, ctx_len // lb
    lat_blocks = rows.n_lat // lb
    qw = SWA_HEADS * SWA_HEAD_DIM
    kw = SWA_KV_HEADS * SWA_HEAD_DIM
    kcol, vcol = qw // kw, qw // kw + 1

    def qidx(bi, n):
        return (jnp.where(n < nb, bi * nb + n, lat_blocks + bi * ncb + (n - nb)), 0)

    def kidx(off, col):
        return lambda bi, n: (bi * nb + jnp.clip(n + off, 0, nb - 1), col)

    def xidx(col):
        return lambda bi, n: (rows.n_lat // ctx_len + bi, col)

    kern = functools.partial(_swa_kernel, nb=nb, seq=seq, ctx_len=ctx_len)
    return pl.pallas_call(
        kern,
        grid=(b, nb + ncb),
        in_specs=[
            pl.BlockSpec(memory_space=pltpu.SMEM),
            pl.BlockSpec((lb, qw), qidx),
            pl.BlockSpec((lb, kw), kidx(-1, kcol)),
            pl.BlockSpec((lb, kw), kidx(0, kcol)),
            pl.BlockSpec((lb, kw), kidx(1, kcol)),
            pl.BlockSpec((ctx_len, kw), xidx(kcol)),
            pl.BlockSpec((lb, kw), kidx(-1, vcol)),
            pl.BlockSpec((lb, kw), kidx(0, vcol)),
            pl.BlockSpec((lb, kw), kidx(1, vcol)),
            pl.BlockSpec((ctx_len, kw), xidx(vcol)),
        ],
        out_specs=pl.BlockSpec((lb, qw), qidx),
        out_shape=jax.ShapeDtypeStruct((rows.n_all, qw), BF16),
        compiler_params=_params(("parallel", "parallel")),
        name="swa_attention",
    )(sink, qkv, qkv, qkv, qkv, qkv, qkv, qkv, qkv, qkv)


def _mlstm_kernel(*refs):
    ins, (of_ref, ob_ref, c_ref, m_ref) = refs[:14], refs[14:]
    c = pl.program_id(1)
    L, H, dqk, dv = ML_CHUNK, ML_HEADS, ML_DQK, ML_DV
    scale = dqk ** -0.5

    @pl.when(c == 0)
    def _():
        c_ref[...] = jnp.zeros_like(c_ref)
        m_ref[...] = jnp.zeros_like(m_ref)

    row = lax.broadcasted_iota(jnp.int32, (L, L), 0)
    col = lax.broadcasted_iota(jnp.int32, (L, L), 1)
    ones = jnp.ones((L, LANE), BF16)
    for d, o_ref in enumerate((of_ref, ob_ref)):
        q_ref, k_ref, v_ref, gi_ref, gf_ref, bi_ref, bf_ref = ins[7 * d:7 * d + 7]
        mask = (col <= row) if d == 0 else (col >= row)
        ones_mask = jnp.where(mask, 1.0, 0.0).astype(BF16)
        i_blk = gi_ref[...] + bi_ref[...]
        f_blk = jax.nn.log_sigmoid(gf_ref[...] + bf_ref[...])
        f_hi = f_blk.astype(BF16)
        r1 = f_blk - f_hi.astype(F32)
        f_mid = r1.astype(BF16)
        f_lo = (r1 - f_mid.astype(F32)).astype(BF16)
        b_blk = _dot(jnp.concatenate([ones_mask] * 3, axis=1), jnp.concatenate([f_hi, f_mid, f_lo], axis=0))
        b_end = jnp.sum(f_blk, axis=0, keepdims=True)
        e_rows = (i_blk - b_blk).T
        m_prev_blk = m_ref[d]
        dec = b_end - b_blk + i_blk
        m_new_blk = jnp.maximum(b_end + m_prev_blk, jnp.max(dec, axis=0, keepdims=True))
        ws_blk = jnp.exp(dec - m_new_blk)
        gs_blk = jnp.exp(b_end + m_prev_blk - m_new_blk)
        for h in range(H):
            qh = q_ref[:, h * dqk:(h + 1) * dqk]
            kh = k_ref[:, h * dqk:(h + 1) * dqk]
            v_aug = jnp.concatenate([v_ref[:, h * dv:(h + 1) * dv], ones], axis=1)
            e = jnp.where(mask, e_rows[h:h + 1, :], -jnp.inf)
            m_prev = m_prev_blk[:, h:h + 1]
            mm = jnp.maximum(m_prev, jnp.max(e, axis=1, keepdims=True))
            s = (_dot_nt(qh, kh) * jnp.exp(e - (mm - math.log(scale)))).astype(BF16)
            g = jnp.exp(m_prev - mm) * scale
            ct = c_ref[d, h]
            lhs = jnp.concatenate([s, (qh.astype(F32) * g).astype(BF16)], axis=1)
            res = _dot(lhs, jnp.concatenate([v_aug, ct.astype(BF16)], axis=0))
            floor = jnp.exp(-(b_blk[:, h:h + 1] + mm))
            hout = res[:, :dv] / jnp.maximum(jnp.abs(res[:, dv:dv + 1]), floor)
            o_ref[:, h * dv:(h + 1) * dv] = hout.astype(o_ref.dtype)
            kw = (kh.astype(F32) * ws_blk[:, h:h + 1]).astype(BF16)
            c_ref[d, h] = gs_blk[:, h:h + 1] * ct + _dot_tn(kw, v_aug)
        m_ref[d] = m_new_blk


def mlstm_scan(z, gates, gate_b, *, rows):
    b, seq, ctx_len = rows.batch, rows.seq, rows.ctx_len
    L = ML_CHUNK
    assert ctx_len == L and seq % L == 0
    nlc = seq // L
    hq, hv = ML_HEADS * ML_DQK, ML_HEADS * ML_DV

    def rb(d):
        def index(bi, c):
            lat = bi * nlc + (c - 1 if d == 0 else nlc - c)
            return jnp.where(c == 0, rows.n_lat // L + bi, lat)
        return index

    in_specs, args = [], []
    for d in range(2):
        r = rb(d)
        in_specs += [
            pl.BlockSpec((L, hq), lambda bi, c, r=r: (r(bi, c), 0)),
            pl.BlockSpec((L, hq), lambda bi, c, r=r: (r(bi, c), 1)),
            pl.BlockSpec((L, hv), lambda bi, c, r=r: (r(bi, c), 1)),
            pl.BlockSpec((L, LANE), lambda bi, c, r=r, d=d: (r(bi, c), 2 * d)),
            pl.BlockSpec((L, LANE), lambda bi, c, r=r, d=d: (r(bi, c), 2 * d + 1)),
            pl.BlockSpec((1, LANE), lambda bi, c, d=d: (0, 2 * d)),
            pl.BlockSpec((1, LANE), lambda bi, c, d=d: (0, 2 * d + 1)),
        ]
        args += [z, z, z, gates, gates, gate_b, gate_b]
    return pl.pallas_call(
        _mlstm_kernel,
        grid=(b, nlc + 1),
        in_specs=in_specs,
        out_specs=[pl.BlockSpec((L, hv), lambda bi, c, r=rb(d): (r(bi, c), 0)) for d in range(2)],
        out_shape=[jax.ShapeDtypeStruct((rows.n_all, hv), BF16)] * 2,
        scratch_shapes=[
            pltpu.VMEM((2, ML_HEADS, ML_DQK, ML_DV + LANE), F32),
            pltpu.VMEM((2, 1, LANE), F32),
        ],
        compiler_params=_params(("parallel", "arbitrary")),
        name="mlstm_scan",
    )(*args)


def _mla_kernel(q_ref, kvl_ref, kvx_ref, krl_ref, krx_ref, o_ref, kk_ref, vt_ref, *, seq, sub, ck):
    @pl.when(pl.program_id(2) == 0)
    def _():
        kk_ref[:seq, :LANE] = kvl_ref[:, :LANE]
        kk_ref[:seq, LANE:] = krl_ref[...]
        kk_ref[seq:, :LANE] = kvx_ref[:, :LANE]
        kk_ref[seq:, LANE:] = krx_ref[...]
        vt_ref[:LANE, :seq] = kvl_ref[:, LANE:].astype(F32).T.astype(BF16)
        vt_ref[:LANE, seq:] = kvx_ref[:, LANE:].astype(F32).T.astype(BF16)
        vt_ref[LANE:, :] = jnp.ones((vt_ref.shape[0] - LANE, vt_ref.shape[1]), BF16)

    nsub = q_ref.shape[0] // sub
    n_keys = kk_ref.shape[0]
    bounds = [(k0, min(k0 + ck, n_keys)) for k0 in range(0, n_keys, ck)]
    nck = len(bounds)
    qs = [q_ref[s * sub:(s + 1) * sub, :] for s in range(nsub)]

    def scores(s, c):
        return _dot_nt(kk_ref[bounds[c][0]:bounds[c][1], :], qs[s])

    m = [None] * nsub
    acc = [None] * nsub
    st = [None] * nsub

    def absorb(s, c):
        cm = jnp.max(st[s], axis=0, keepdims=True)
        m_new = cm if c == 0 else jnp.maximum(m[s], cm)
        pv = _dot(vt_ref[:, bounds[c][0]:bounds[c][1]], jnp.exp2(st[s] - m_new).astype(BF16))
        acc[s] = pv if c == 0 else acc[s] * jnp.exp2(m[s] - m_new) + pv
        m[s] = m_new

    for k in range(-1, nck + nsub - 1):
        nxt = [scores(s, k - s + 1) if 0 <= k - s + 1 < nck else None for s in range(nsub)]
        for s in range(nsub):
            if 0 <= k - s < nck:
                absorb(s, k - s)
        for s in range(nsub):
            if nxt[s] is not None:
                st[s] = nxt[s]
    for s in range(nsub):
        o = acc[s][:LANE] / acc[s][LANE:LANE + 1]
        o_ref[s * sub:(s + 1) * sub, :] = o.T.astype(o_ref.dtype)


def mla_attention(q, kv, z, *, rows, tq=4096, sub=256, ck=1024):
    b, seq, ctx_len = rows.batch, rows.seq, rows.ctx_len
    tq = min(tq, seq)
    nq = seq // tq
    kr_col = z.shape[1] // LANE - 1
    ctx0 = rows.n_lat // ctx_len
    n_keys = seq + ctx_len
    ones_rows = 16
    return pl.pallas_call(
        functools.partial(_mla_kernel, seq=seq, sub=sub, ck=ck),
        grid=(b, MLA_HEADS, nq),
        in_specs=[
            pl.BlockSpec((tq, 2 * LANE), lambda bi, h, i: (bi * nq + i, h)),
            pl.BlockSpec((seq, 2 * LANE), lambda bi, h, i: (bi, h)),
            pl.BlockSpec((ctx_len, 2 * LANE), lambda bi, h, i: (ctx0 + bi, h)),
            pl.BlockSpec((seq, LANE), lambda bi, h, i: (bi, kr_col)),
            pl.BlockSpec((ctx_len, LANE), lambda bi, h, i: (ctx0 + bi, kr_col)),
        ],
        out_specs=pl.BlockSpec((tq, LANE), lambda bi, h, i: (bi * nq + i, h)),
        out_shape=jax.ShapeDtypeStruct((rows.n_lat, MLA_HEADS * MLA_V), BF16),
        scratch_shapes=[pltpu.VMEM((n_keys, 2 * LANE), BF16), pltpu.VMEM((LANE + ones_rows, n_keys), BF16)],
        compiler_params=_params(("parallel", "parallel", "arbitrary")),
        name="mla_attention",
    )(q, kv, kv, z, z)


def _hy_filter_kernel(ft_ref, w1_ref, b1_ref, w2_ref, b2_ref, w3_ref, b3_ref, fr_ref, fr3_ref, w4_ref, b4_ref,
                      dec_ref, o_ref, a_ref, *, back_from):
    j = pl.program_id(0)
    fw = HY_FILTER_W
    hdot = functools.partial(jnp.dot, preferred_element_type=F32, precision=HIGHEST)

    @pl.when(j == 0)
    def _():
        fr = fr_ref[...]
        a = jnp.sin(fr * (hdot(ft_ref[...], w1_ref[...]) + b1_ref[...]))
        a = jnp.sin(fr * (hdot(a, w2_ref[...]) + b2_ref[...]))
        a3 = jnp.sin(fr3_ref[...] * (hdot(a, w3_ref[...]) + b3_ref[...]))
        hi = a3.astype(BF16)
        lo = (a3 - hi.astype(F32)).astype(BF16)
        lane = lax.broadcasted_iota(jnp.int32, a3.shape, 1)
        a_ref[...] = jnp.where((lane >= fw) & (lane < 2 * fw), lo, hi)

    w4 = w4_ref[...]
    w_hi = w4.astype(BF16)
    w_lo = (w4 - w_hi.astype(F32)).astype(BF16)
    rhs = jnp.concatenate([w_hi, w_hi, w_lo, jnp.zeros_like(w_hi)], axis=0)
    t = ft_ref[:, 0:1]
    filt = (_dot(a_ref[...], rhs) + b4_ref[...]) * jnp.exp(-t * jnp.abs(dec_ref[...]))
    row = lax.broadcasted_iota(jnp.int32, filt.shape, 0)
    o_ref[...] = jnp.where(jnp.logical_and(row == 0, j >= back_from), 0.0, filt).astype(o_ref.dtype)


def hyena_filters(seq, w1, b1, w2, b2, w3, b3, w4, b4, freq, decay, tn=512):
    t = np.linspace(0.0, 1.0, seq)[:, None]
    w = (2.0 * math.pi / seq) * np.arange(seq)[:, None]
    bands = np.linspace(1e-4, HY_BANDS - 1, HY_BANDS)[None, :]
    feats = np.zeros((seq, LANE))
    feats[:, :1 + 2 * HY_BANDS] = np.concatenate([t, np.cos(bands * w), -np.sin(bands * w)], axis=-1)
    n = w4.shape[1]
    fw = HY_FILTER_W
    w1p = jnp.zeros((LANE, fw), F32).at[:w1.shape[0]].set(w1)
    row = lambda a: a.reshape(1, -1)
    tile3 = lambda a: jnp.concatenate([a, a, a, jnp.zeros_like(a)], axis=-1)
    full = lambda shape: pl.BlockSpec(shape, lambda j: (0, 0))
    return pl.pallas_call(
        functools.partial(_hy_filter_kernel, back_from=(n // 2) // tn),
        grid=(n // tn,),
        in_specs=[full((seq, LANE)), full((LANE, fw)), full((1, fw)), full((fw, fw)), full((1, fw)),
                  full((fw, 4 * fw)), full((1, 4 * fw)), full((1, fw)), full((1, 4 * fw)),
                  pl.BlockSpec((fw, tn), lambda j: (0, j)),
                  pl.BlockSpec((1, tn), lambda j: (0, j)),
                  pl.BlockSpec((1, tn), lambda j: (0, j))],
        out_specs=pl.BlockSpec((seq, tn), lambda j: (0, j)),
        out_shape=jax.ShapeDtypeStruct((seq, n), BF16),
        scratch_shapes=[pltpu.VMEM((seq, 4 * fw), BF16)],
        compiler_params=_params(("arbitrary",)),
        name="hyena_filters",
    )(jnp.asarray(feats, F32), w1p, row(b1), w2, row(b2), tile3(w3), tile3(row(b3)), row(freq), tile3(row(freq)),
      w4, row(b4), row(decay))


def _dft_mats(seq, p):
    n = 2 * seq
    q = n // p
    k1n = q // 2 + 1
    hi, k1 = np.arange(q // 2), np.arange(k1n)
    th = 2 * np.pi * np.outer(k1, hi) / q
    ma = np.zeros((2 * k1n, q // 2))
    ma[0::2], ma[1::2] = np.cos(th), -np.sin(th)
    lo = np.arange(p)
    wf = np.zeros((k1n, 2 * p, 2 * p))
    wi = np.zeros((k1n, 2 * p, 2 * p))
    for k in k1:
        ph = -2 * np.pi * (np.outer(lo, lo) / p + k * lo[None, :] / n)
        er, ei = np.cos(ph), np.sin(ph)
        wf[k] = np.block([[er, -ei], [ei, er]])
        wi[k] = np.block([[er.T, ei.T], [-ei.T, er.T]])
    c = np.full(k1n, 2.0)
    c[0] = c[-1] = 1.0
    th2 = 2 * np.pi * np.outer(hi, k1) / q
    md = np.zeros((q // 2, 2 * k1n))
    md[:, 0::2], md[:, 1::2] = c * np.cos(th2) / n, -c * np.sin(th2) / n
    as_bf16 = lambda a: jnp.asarray(a, F32).astype(BF16)
    return ma, as_bf16(wf), as_bf16(wi), md


HY_HALO = 16


def _hy_in_kernel(h_ref, hp_ref, hn_ref, g_ref, sh_ref, sc_ref, w_ref, cw_ref, cb_ref, o_ref, u_ref, inv_ref,
                  invh_ref, *, per_seq):
    i = pl.program_id(0)
    tm = h_ref.shape[0]

    @pl.when(pl.program_id(1) == 0)
    def _():
        mod = lambda x: x * (1.0 + sc_ref[0]) + sh_ref[0]
        u_ref[HY_HALO:HY_HALO + tm, :] = mod(_rms_ref(h_ref, inv_ref, g_ref[...])).astype(BF16)
        keep_prev = (i % per_seq != 0).astype(F32)
        keep_next = (i % per_seq != per_seq - 1).astype(F32)
        u_ref[:HY_HALO, :] = (mod(_rms_ref(hp_ref, invh_ref, g_ref[...])) * keep_prev).astype(BF16)
        u_ref[HY_HALO + tm:, :] = (mod(_rms_ref(hn_ref, invh_ref, g_ref[...])) * keep_next).astype(BF16)

    acc = _dot(u_ref[...], w_ref[...])
    n = acc.shape[0]
    mid = slice(HY_HALO, HY_HALO + tm)
    prev = pltpu.roll(acc, 1, 0)[mid]
    nxt = pltpu.roll(acc, n - 1, 0)[mid]
    cw = cw_ref[...]
    o_ref[...] = (prev * cw[0:1] + acc[mid] * cw[1:2] + nxt * cw[2:3] + cb_ref[...]).astype(o_ref.dtype)


def hyena_in_conv(h, w, conv_w, conv_b, mods, gain, *, rows, tm, tn, layer):
    k, n = w.shape
    n_rows = rows.n_lat
    per_seq = rows.seq // tm
    halo_per_blk, n_halo = tm // HY_HALO, n_rows // HY_HALO
    return pl.pallas_call(
        functools.partial(_hy_in_kernel, per_seq=per_seq),
        grid=(n_rows // tm, n // tn),
        in_specs=[
            pl.BlockSpec((tm, k), lambda i, j: (i, 0)),
            pl.BlockSpec((HY_HALO, k), lambda i, j: (jnp.maximum(i * halo_per_blk - 1, 0), 0)),
            pl.BlockSpec((HY_HALO, k), lambda i, j: (jnp.minimum((i + 1) * halo_per_blk, n_halo - 1), 0)),
            pl.BlockSpec((1, k), lambda i, j: (0, 0)),
            pl.BlockSpec((1, 1, k), rows.mod_index(layer, 0, tm)),
            pl.BlockSpec((1, 1, k), rows.mod_index(layer, 1, tm)),
            pl.BlockSpec((k, tn), lambda i, j: (0, j)),
            pl.BlockSpec((HY_SHORT, tn), lambda i, j: (0, j)),
            pl.BlockSpec((1, tn), lambda i, j: (0, j)),
        ],
        out_specs=pl.BlockSpec((tm, tn), lambda i, j: (i, j)),
        out_shape=jax.ShapeDtypeStruct((n_rows, n), BF16),
        scratch_shapes=[pltpu.VMEM((tm + 2 * HY_HALO, k), BF16), pltpu.VMEM((tm, 1), F32),
                        pltpu.VMEM((HY_HALO, 1), F32)],
        compiler_params=_params(("parallel", "arbitrary")),
        name="hyena_in",
    )(h, h, h, gain.reshape(1, k), mods, mods, w, conv_w, conv_b.reshape(1, n))


def _hy_stage_a_kernel(x_ref, m_ref, o_ref):
    qh, lg, g, wc = x_ref.shape
    mo = o_ref.shape[0]
    for l in range(lg):
        x = x_ref[:, l].reshape(qh * g, wc).astype(BF16)
        y = _dot(m_ref[...], x).reshape(mo, g, wc)
        o_ref[:, l * g:(l + 1) * g, :] = y.astype(o_ref.dtype)


def hyena_stage_a(x, mak, *, p, width=2048, col_blk=0, channels=None):
    bx, seq, c_all = x.shape
    c = c_all if channels is None else channels
    g = HY_G
    qh = seq // p
    mo = mak.shape[0] // g
    x5 = x.reshape(bx, qh, p // g, g, c_all)
    lg = HY_LG
    grid = (bx, p // (g * lg), c // width)
    x_spec = pl.BlockSpec((None, qh, lg, g, width), lambda bi, l, j: (bi, 0, l, 0, col_blk + j),
                          pipeline_mode=pl.Buffered(3))
    o_spec = pl.BlockSpec((None, mo, lg * g, width), lambda bi, l, j: (bi, 0, l, j))

    def outer(x_hbm, m_ref, o_hbm):
        pltpu.emit_pipeline(lambda x_ref, o_ref: _hy_stage_a_kernel(x_ref, m_ref, o_ref), grid=grid,
                            in_specs=[x_spec], out_specs=[o_spec])(x_hbm, o_hbm)

    return pl.pallas_call(
        outer,
        in_specs=[pl.BlockSpec(memory_space=pl.ANY), pl.BlockSpec(memory_space=pltpu.VMEM)],
        out_specs=pl.BlockSpec(memory_space=pl.ANY),
        out_shape=jax.ShapeDtypeStruct((bx, mo, p, c), BF16),
        compiler_params=pltpu.CompilerParams(vmem_limit_bytes=VMEM_LIMIT),
        name="hyena_stage_a",
    )(x5, mak)


def _hy_spec_kernel(a0_ref, a1_ref, wf_ref, o_ref):
    p = o_ref.shape[2]
    cb = o_ref.shape[3]
    wf = wf_ref[0]
    x0 = _dot(wf, a0_ref[0].reshape(2 * p, cb))
    x1 = _dot(wf, a1_ref[0].reshape(2 * p, cb))
    o_ref[0, 0] = x0[:p] + x1[:p]
    o_ref[0, 1] = x0[p:] - x1[p:]


def hyena_spectrum(af, wf, *, d, cb=2048):
    k1n, _, p, _ = af.shape
    nblk = (HY_ORDER * d) // cb
    return pl.pallas_call(
        _hy_spec_kernel,
        grid=(k1n, nblk),
        in_specs=[pl.BlockSpec((1, 2, p, cb), lambda k, j: (k, 0, 0, j)),
                  pl.BlockSpec((1, 2, p, cb), lambda k, j: (k, 0, 0, nblk + j)),
                  pl.BlockSpec((1, 2 * p, 2 * p), lambda k, j: (k, 0, 0))],
        out_specs=pl.BlockSpec((1, 2, p, cb), lambda k, j: (k, 0, 0, j)),
        out_shape=jax.ShapeDtypeStruct((k1n, 2, p, HY_ORDER * d), F32),
        compiler_params=_params(("parallel", "parallel")),
        name="hyena_spectrum",
    )(af, af, wf)


def _hy_mid_kernel(a_ref, wf_ref, wi_ref, h_ref, o_ref):
    b, _, p, cb = a_ref.shape
    hr, hi = h_ref[0], h_ref[1]
    for bi in range(b):
        x = _dot(wf_ref[...], a_ref[bi].reshape(2 * p, cb))
        xr, xi = x[:p], x[p:]
        y = jnp.concatenate([xr * hr - xi * hi, xr * hi + xi * hr], axis=0).astype(BF16)
        o_ref[bi] = _dot(wi_ref[...], y).reshape(2, p, cb).astype(o_ref.dtype)


def hyena_mid(a, wf, wi, spec, *, order, d, cb=1024):
    b, k1n, _, p, _ = a.shape
    nblk = d // cb
    data = pl.BlockSpec((b, None, 2, p, cb), lambda k, j: (0, k, 0, 0, j))
    mat = pl.BlockSpec((None, 2 * p, 2 * p), lambda k, j: (k, 0, 0))
    return pl.pallas_call(
        _hy_mid_kernel,
        grid=(k1n, nblk),
        in_specs=[data, mat, mat,
                  pl.BlockSpec((None, 2, p, cb), lambda k, j: (k, 0, 0, order * nblk + j))],
        out_specs=data,
        out_shape=jax.ShapeDtypeStruct(a.shape, BF16),
        compiler_params=_params(("parallel", "parallel")),
        name="hyena_mid",
    )(a, wf, wi, spec)


def _hy_stage_d_kernel(c_ref, m_ref, x_ref, z_ref, skip_ref, *rest, fused):
    ma_ref, o_ref, a_ref = rest if fused else (None, rest[0], None)
    mo, rows_c, wc = c_ref.shape
    qh, lg, g, _ = x_ref.shape
    for l in range(lg):
        conv = _dot(m_ref[...], c_ref[:, l * g:(l + 1) * g, :].reshape(mo * g, wc))
        z = z_ref[:, l].reshape(qh * g, wc).astype(F32)
        x = x_ref[:, l].reshape(qh * g, wc).astype(F32)
        y = (x * (conv + z * skip_ref[...])).astype(o_ref.dtype)
        o_ref[:, l] = y.reshape(qh, g, wc)
        if fused:
            a_ref[:, l * g:(l + 1) * g, :] = _dot(ma_ref[...], y).reshape(mo, g, wc).astype(a_ref.dtype)


def hyena_stage_d(c, mdk, xg, z, skip, *, p, mak=None, width=2048, xg_blk=0, z_blk=0):
    b, mo, _, d = c.shape
    g = HY_G
    seq = xg.shape[1]
    qh = seq // p
    fused = mak is not None
    v5 = lambda a: a.reshape(b, qh, p // g, g, a.shape[2])
    lg = HY_LG
    tblk = lambda off: pl.BlockSpec((None, qh, lg, g, width), lambda bi, l, j: (bi, 0, l, 0, off + j))
    tspec = tblk(0)
    fspec = pl.BlockSpec((None, mo, lg * g, width), lambda bi, l, j: (bi, 0, l, j))
    in_specs = [fspec, pl.BlockSpec(mdk.shape, lambda bi, l, j: (0, 0)), tblk(xg_blk), tblk(z_blk),
                pl.BlockSpec((1, width), lambda bi, l, j: (0, j))]
    args = [c, mdk, v5(xg), v5(z), skip.reshape(1, d)]
    out_specs, out_shape = [tspec], [jax.ShapeDtypeStruct((b, qh, p // g, g, d), BF16)]
    if fused:
        in_specs.append(pl.BlockSpec(mak.shape, lambda bi, l, j: (0, 0)))
        args.append(mak)
        out_specs.append(fspec)
        out_shape.append(jax.ShapeDtypeStruct(c.shape, BF16))
    outs = pl.pallas_call(
        functools.partial(_hy_stage_d_kernel, fused=fused),
        grid=(b, p // (g * lg), d // width),
        in_specs=in_specs,
        out_specs=out_specs,
        out_shape=out_shape,
        compiler_params=_params(("parallel", "parallel", "parallel")),
        name="hyena_stage_d",
    )(*args)
    y = outs[0].reshape(b, seq, d)
    return (y, outs[1]) if fused else y


def hyena_long_convs(x3, filt, skip, *, rows, width=2048):
    b, seq = rows.batch, rows.seq
    d = x3.shape[1] // 3
    p = HY_P
    k1n = seq // p + 1
    ma, wf, wi, md = _dft_mats(seq, p)
    eye = np.eye(HY_G)
    as_bf16 = lambda a: jnp.asarray(a, F32).astype(BF16)
    mak, mdk = as_bf16(np.kron(ma, eye)), as_bf16(np.kron(md, eye))
    af = hyena_stage_a(filt.reshape(1, seq, filt.shape[1]), mak, p=p)
    spec = hyena_spectrum(af.reshape(k1n, 2, p, filt.shape[1]), wf, d=d)
    x3 = x3.reshape(b, seq, 3 * d)
    blk = d // width
    a = hyena_stage_a(x3, mak, p=p, width=width, col_blk=2 * blk, channels=d)
    c = hyena_mid(a.reshape(b, k1n, 2, p, d), wf, wi, spec, order=0, d=d)
    y, a = hyena_stage_d(c.reshape(a.shape), mdk, x3, x3, skip[0], p=p, mak=mak, width=width, xg_blk=0, z_blk=2 * blk)
    c = hyena_mid(a.reshape(b, k1n, 2, p, d), wf, wi, spec, order=1, d=d)
    y = hyena_stage_d(c.reshape(a.shape), mdk, x3, y, skip[1], p=p, width=width, xg_blk=blk, z_blk=0)
    return y.reshape(rows.n_lat, d)


def _mlstm_gate_weights(w_in, gate_b):
    h = ML_HEADS
    o4 = 2 * h * ML_DQK + 2 * h * ML_DV
    wg = w_in[:, o4:].reshape(-1, 2, 2, h)
    bg = gate_b.reshape(2, 2, h)
    w_out = jnp.zeros((w_in.shape[0], 4 * LANE), F32)
    b_out = jnp.zeros((1, 4 * LANE), F32)
    for d in range(2):
        for gate in range(2):
            lo = (2 * d + gate) * LANE
            w_out = w_out.at[:, lo:lo + h].set(wg[:, gate, d])
            b_out = b_out.at[0, lo:lo + h].set(bg[gate, d])
    return w_out, b_out


def _mla_weights(w_in, w_uq):
    r2 = MLA_Q_RANK + MLA_KV_RANK
    half = MLA_ROPE // 2

    def spread(w):
        z = jnp.zeros(w.shape[:-1] + (half,), w.dtype)
        return jnp.concatenate([w[..., :half], z, w[..., half:], z], axis=-1)

    w_in_p = jnp.concatenate([w_in[:, :r2], spread(w_in[:, r2:])], axis=1)
    wq = w_uq.reshape(w_uq.shape[0], MLA_HEADS, MLA_NOPE + MLA_ROPE)
    wq_p = jnp.concatenate([wq[..., :MLA_NOPE], spread(wq[..., MLA_NOPE:])], axis=-1)
    wq_p = wq_p * ((MLA_NOPE + MLA_ROPE) ** -0.5 * math.log2(math.e))
    return w_in_p, wq_p.reshape(w_uq.shape[0], MLA_HEADS * 2 * LANE)


def kernel(x, c, ctx, c_ctx, ada_w, ada_b, norm_g, mlp_w1, mlp_w2, swa_w_qkv, swa_sink, swa_w_o, ml_w_in, ml_gate_b, ml_head_g, ml_w_o, mla_w_in, mla_q_g, mla_kv_g, mla_w_uq, mla_w_ukv, mla_w_o, hy_w_in, hy_conv_w, hy_conv_b, hy_f_w1, hy_f_b1, hy_f_w2, hy_f_b2, hy_f_w3, hy_f_b3, hy_f_w4, hy_f_b4, hy_f_freq, hy_decay, hy_skip, hy_w_o):
    b, seq, d = x.shape
    ctx_len = ctx.shape[1]
    depth = ada_w.shape[0]
    assert depth == 4 and b < 8
    rows = Rows(b, seq, ctx_len)
    tm = 512
    tm_lin = 1024 if rows.n_lat % 1024 == 0 and rows.n_ctx % 1024 == 0 else 512
    bf = lambda w: w.astype(BF16)
    mlp_w1b, mlp_w2b = bf(mlp_w1), bf(mlp_w2)

    cond = jnp.zeros((8, d), F32).at[:b].set(c).at[b].set(c_ctx)
    mods = ada_mods(cond, ada_w, ada_b).reshape(depth * 8, 1, N_MOD * d)

    common = dict(rows=rows, tm=tm_lin, mods=mods)

    def finish(h, o_args, w_o, layer, n_rows, mode="plain", h2=None):
        h = outproj_residual(o_args, bf(w_o), h, mods, norm_g[layer, 1], rows=rows, n_rows=n_rows, tm=tm,
                             layer=layer, mode=mode, h2=h2, name=f"outproj{layer}")
        return mlp_residual(h, mlp_w1b, mlp_w2b, mods, norm_g[layer, 2], norm_g[layer, 3],
                            rows=rows, n_rows=n_rows, tm=tm, tf=1024, layer=layer)

    n_qk_groups = SWA_HEADS + SWA_KV_HEADS
    q_cols = SWA_HEADS * SWA_HEAD_DIM
    q_scale = SWA_HEAD_DIM ** -0.5 * math.log2(math.e)
    w_qkv = jnp.concatenate([swa_w_qkv[0][:, :q_cols] * q_scale, swa_w_qkv[0][:, q_cols:]], axis=1)
    x_rows, ctx_rows = x.reshape(rows.n_lat, d), ctx.reshape(rows.n_ctx, d)
    qkv = linear(x_rows, bf(w_qkv), x2=ctx_rows, n_rows=rows.n_all, tn=512, out_dtype=BF16, prologue="norm_mod",
                 gain=norm_g[0, 0], layer=0, rope_tabs=_rope_tables(seq, SWA_HEAD_DIM), rope_pattern=(True,) * 4,
                 rope_jmax=n_qk_groups // 4, name="swa_qkv", **common)
    o = swa_attention(qkv, swa_sink[0], rows=rows)
    h = finish(x_rows, o, swa_w_o[0], 0, rows.n_all, h2=ctx_rows)

    w_gate, b_gate = _mlstm_gate_weights(ml_w_in[0], ml_gate_b[0])
    z = linear(h, bf(ml_w_in[0]), n_rows=rows.n_all, tn=1536, out_dtype=BF16, prologue="norm_mod",
               gain=norm_g[1, 0], layer=1, n_out=2 * ML_HEADS * (ML_DQK + ML_DV), name="mlstm_in", **common)
    gates = linear(h, bf(w_gate), n_rows=rows.n_all, tn=4 * LANE, out_dtype=F32, prologue="norm_mod",
                   gain=norm_g[1, 0], layer=1, name="mlstm_gates", **common)
    h_fwd, h_bwd = mlstm_scan(z, gates, b_gate, rows=rows)
    h = finish(h, (h_fwd, h_bwd, z, ml_head_g[0]), ml_w_o[0], 1, rows.n_all, mode="mlstm")

    w_in_p, w_uq_p = _mla_weights(mla_w_in[0], mla_w_uq[0])
    rope_mla = _rope_tables(seq, MLA_ROPE)
    zc = linear(h, bf(w_in_p), n_rows=rows.n_all, tn=w_in_p.shape[1], out_dtype=BF16, prologue="norm_mod",
                gain=norm_g[2, 0], layer=2, rope_tabs=rope_mla, rope_pattern=(False,) * 8 + (True,),
                name="mla_in", **common)
    q = linear(zc, bf(w_uq_p), n_rows=rows.n_lat, tn=2048, out_dtype=BF16, x_cols=0, prologue="norm",
               gain=mla_q_g[0], rope_tabs=rope_mla, rope_pattern=(False, True) * 8, name="mla_q",
               rows=rows, tm=tm_lin)
    kv = linear(zc, bf(mla_w_ukv[0]), n_rows=rows.n_all, tn=2048, out_dtype=BF16, x_cols=1, prologue="norm",
                gain=mla_kv_g[0], name="mla_kv", rows=rows, tm=tm_lin)
    o = mla_attention(q, kv, zc, rows=rows)
    h = finish(h, o, mla_w_o[0], 2, rows.n_lat)

    x3 = hyena_in_conv(h, bf(hy_w_in[0]), hy_conv_w[0], hy_conv_b[0], mods, norm_g[3, 0], rows=rows, tm=tm_lin,
                       tn=1536, layer=3)
    filt = hyena_filters(seq, hy_f_w1[0], hy_f_b1[0], hy_f_w2[0], hy_f_b2[0], hy_f_w3[0], hy_f_b3[0],
                         hy_f_w4[0], hy_f_b4[0], hy_f_freq[0], hy_decay[0])
    y = hyena_long_convs(x3, filt, hy_skip[0], rows=rows)
    h = finish(h, y, hy_w_o[0], 3, rows.n_lat)
    return h.reshape(b, seq, d)
```
